```python
import jax, jax.numpy as jnp
from jax import lax
import numpy as np

D_MODEL = 1024
BATCH = 32
SEQ = 2048
DEPTH = 2
DEC_BATCH = 8
DEC_SEQ = 64
PAST_LEN = 2048

CHUNK = 64
N_A = DEPTH // 2
N_B = DEPTH - N_A
N_DENSE = (DEPTH + 1) // 2
N_MOE = DEPTH // 2
HK = 8
HV = 16
DK = 128
DV = 128
QK_DIM = HK * DK
V_DIM = HV * DV
CONV_CH = 2 * QK_DIM + V_DIM
CONV_W = 4
A_PROJ = CONV_CH + V_DIM + 2 * HV
HQ = 16
HKV = 4
HD = 64
GQ = HQ // HKV
WINDOW = 128
ROT_DIM = HD // 4
ROPE_THETA = 500000.0
D_FF = 2816
N_EXPERTS = 8
TOP_K = 2
EPS = 1e-6

kernel_name = 'streaming_yoco_gdn_swa_moe_step'


def _rmsnorm(x, g):
    xf = x.astype(jnp.float32)
    y = xf * lax.rsqrt(jnp.mean(xf * xf, axis=-1, keepdims=True) + EPS) * g.astype(jnp.float32)
    return y.astype(x.dtype)


def _l2norm(x):
    return x * lax.rsqrt(jnp.sum(x * x, axis=-1, keepdims=True) + EPS)


def _rope_partial(x, pos):
    half = ROT_DIM // 2
    inv = jnp.power(ROPE_THETA, -jnp.arange(half, dtype=jnp.float32) * 2.0 / ROT_DIM)
    ang = pos.astype(jnp.float32)[:, None] * inv[None, :]
    cos = jnp.cos(ang)[None, :, None, :]
    sin = jnp.sin(ang)[None, :, None, :]
    x1 = x[..., :half].astype(jnp.float32)
    x2 = x[..., half:ROT_DIM].astype(jnp.float32)
    rot = jnp.concatenate([x1 * cos - x2 * sin, x2 * cos + x1 * sin], axis=-1).astype(x.dtype)
    return jnp.concatenate([rot, x[..., ROT_DIM:]], axis=-1)


def _gated_delta(q, k, v, g, beta, s0, chunk):
    B, T, H, _ = q.shape
    n = T // chunk

    def blk(t):
        t = t.reshape((B, n, chunk, H) + t.shape[3:])
        return jnp.moveaxis(t, (1, 3), (0, 2))

    qb, kb, vb, gb, bb = blk(q), blk(k), blk(v), blk(g), blk(beta)
    gb = jnp.cumsum(gb, axis=-1)
    idx = jnp.arange(chunk)
    incl = idx[:, None] >= idx[None, :]
    strict = idx[:, None] > idx[None, :]
    eye = jnp.eye(chunk, dtype=jnp.float32)

    def step(S, inp):
        qc, kc, vc, gc, bc = inp
        decay = jnp.exp(jnp.where(incl, gc[..., :, None] - gc[..., None, :], -jnp.inf))
        kbeta = kc * bc[..., None]
        a = jnp.where(strict, jnp.einsum('bhid,bhjd->bhij', kbeta, kc) * decay, 0.0) + eye
        rhs = jnp.concatenate([vc * bc[..., None], kbeta * jnp.exp(gc)[..., None]], axis=-1)
        sol = lax.linalg.triangular_solve(a, rhs, left_side=True, lower=True, unit_diagonal=True)
        u, w = sol[..., :DV], sol[..., DV:]
        v_new = u - jnp.einsum('bhck,bhkv->bhcv', w, S)
        attn = jnp.einsum('bhid,bhjd->bhij', qc, kc) * decay
        o = (jnp.einsum('bhck,bhkv->bhcv', qc * jnp.exp(gc)[..., None], S)
             + jnp.einsum('bhij,bhjv->bhiv', attn, v_new))
        g_last = gc[..., -1]
        S = (S * jnp.exp(g_last)[..., None, None]
             + jnp.einsum('bhck,bhcv->bhkv', kc * jnp.exp(g_last[..., None] - gc)[..., None], v_new))
        return S, o

    S, o = lax.scan(step, s0, (qb, kb, vb, gb, bb))
    o = jnp.transpose(o, (1, 0, 3, 2, 4)).reshape(B, T, H, DV)
    return o, S


def _gdn_layer(x, norm_g, w_in, w_conv, a_log, dt_bias, o_norm, w_out, conv_state, s0):
    B, T, _ = x.shape
    h = _rmsnorm(x, norm_g)
    proj = h @ w_in
    qkv, z, b, a = jnp.split(proj, [CONV_CH, CONV_CH + V_DIM, CONV_CH + V_DIM + HV], axis=-1)
    xpad = jnp.concatenate([conv_state.astype(qkv.dtype), qkv], axis=1)
    conv = xpad[:, 0:T] * w_conv[0]
    for j in range(1, CONV_W):
        conv = conv + xpad[:, j:j + T] * w_conv[j]
    conv = jax.nn.silu(conv)
    new_conv = xpad[:, T:]
    q, k, v = jnp.split(conv.astype(jnp.float32), [QK_DIM, 2 * QK_DIM], axis=-1)
    q = _l2norm(q.reshape(B, T, HK, DK)) * (DK ** -0.5)
    k = _l2norm(k.reshape(B, T, HK, DK))
    q = jnp.repeat(q, HV // HK, axis=2)
    k = jnp.repeat(k, HV // HK, axis=2)
    v = v.reshape(B, T, HV, DV)
    beta = jax.nn.sigmoid(b.astype(jnp.float32))
    g = -jnp.exp(a_log.astype(jnp.float32)) * jax.nn.softplus(a.astype(jnp.float32) + dt_bias.astype(jnp.float32))
    o, S = _gated_delta(q, k, v, g, beta, s0.astype(jnp.float32), min(CHUNK, T))
    zf = z.reshape(B, T, HV, DV).astype(jnp.float32)
    o = (_rmsnorm(o, o_norm) * jax.nn.silu(zf)).astype(x.dtype)
    return x + o.reshape(B, T, V_DIM) @ w_out, new_conv, S


def _sink_attend(q, k, v, sinks, valid):
    s = jnp.einsum('bqhgd,bkhd->bhgqk', q, k).astype(jnp.float32) * (HD ** -0.5)
    s = jnp.where(valid, s, -jnp.inf)
    sink = sinks.astype(jnp.float32)[None, :, :, None, None]
    m = jnp.maximum(jnp.max(s, axis=-1, keepdims=True), sink)
    p = jnp.exp(s - m)
    p = p / (jnp.sum(p, axis=-1, keepdims=True) + jnp.exp(sink - m))
    return jnp.einsum('bhgqk,bkhd->bqhgd', p.astype(v.dtype), v)


def _swa_prompt(q, k, v, sinks):
    B, T = q.shape[:2]
    n = T // CHUNK
    span = WINDOW + CHUNK
    kp = jnp.pad(k, ((0, 0), (WINDOW, 0), (0, 0), (0, 0)))
    vp = jnp.pad(v, ((0, 0), (WINDOW, 0), (0, 0), (0, 0)))
    qb = q.reshape(B, n, CHUNK, HKV, GQ, HD)
    offs = jnp.arange(span)

    def one(c):
        qc = lax.dynamic_index_in_dim(qb, c, axis=1, keepdims=False)
        kc = lax.dynamic_slice_in_dim(kp, c * CHUNK, span, axis=1)
        vc = lax.dynamic_slice_in_dim(vp, c * CHUNK, span, axis=1)
        valid = c * CHUNK + offs >= WINDOW
        return _sink_attend(qc, kc, vc, sinks, valid)

    o = lax.map(one, jnp.arange(n))
    return jnp.moveaxis(o, 0, 1).reshape(B, T, HQ * HD)


def _dense_ffn(x, g, w1, w3, w2):
    h = _rmsnorm(x, g)
    return x + (jax.nn.silu(h @ w1) * (h @ w3)) @ w2


def _moe_ffn(x, g, router, w1, w3, w2):
    B, T, D = x.shape
    h = _rmsnorm(x, g).reshape(-1, D)
    n = h.shape[0]
    logits = (h @ router).astype(jnp.float32)
    top_v, top_i = lax.top_k(logits, TOP_K)
    gates = jax.nn.softmax(top_v, axis=-1)
    e_flat = top_i.reshape(-1)
    order = jnp.argsort(e_flat)
    tok = order // TOP_K
    xs = h[tok]
    sizes = jnp.bincount(e_flat, length=N_EXPERTS).astype(jnp.int32)
    a = lax.ragged_dot(xs, w1, sizes)
    b = lax.ragged_dot(xs, w3, sizes)
    y = lax.ragged_dot(jax.nn.silu(a) * b, w2, sizes)
    y = y * gates.reshape(-1)[order][:, None].astype(y.dtype)
    out = jax.ops.segment_sum(y, tok, num_segments=n)
    return x + out.reshape(B, T, D).astype(x.dtype)


def _trunk(x, pos, conv0, s0, win_k, win_v, p):
    B, T, _ = x.shape
    new_conv, new_s = [], []
    k = v = None
    for layer in range(DEPTH):
        if layer < N_A:
            x, c_st, S = _gdn_layer(x, p['norm_attn'][layer], p['gdn_w_in'][layer], p['gdn_w_conv'][layer],
                                    p['gdn_a_log'][layer], p['gdn_dt_bias'][layer], p['gdn_o_norm'][layer],
                                    p['gdn_w_out'][layer], conv0[layer], s0[layer])
            new_conv.append(c_st)
            new_s.append(S)
        else:
            ib = layer - N_A
            if k is None:
                hk = _rmsnorm(x, p['kv_norm'])
                kv = hk @ p['w_kv']
                k = _rope_partial(kv[..., :HKV * HD].reshape(B, T, HKV, HD), pos)
                v = kv[..., HKV * HD:].reshape(B, T, HKV, HD)
            h = _rmsnorm(x, p['norm_attn'][layer])
            q = _rope_partial((h @ p['swa_w_q'][ib]).reshape(B, T, HQ, HD), pos)
            sinks = p['swa_sinks'][ib].reshape(HKV, GQ)
            if win_k is None:
                o = _swa_prompt(q, k, v, sinks)
            else:
                kc = jnp.concatenate([win_k.astype(k.dtype), k], axis=1)
                vc = jnp.concatenate([win_v.astype(v.dtype), v], axis=1)
                valid = jnp.ones((kc.shape[1],), dtype=bool)
                o = _sink_attend(q.reshape(B, T, HKV, GQ, HD), kc, vc, sinks, valid).reshape(B, T, HQ * HD)
            x = x + o @ p['swa_w_o'][ib]
        if layer % 2 == 0:
            i = layer // 2
            x = _dense_ffn(x, p['norm_ffn'][layer], p['ffn_w1'][i], p['ffn_w3'][i], p['ffn_w2'][i])
        else:
            i = layer // 2
            x = _moe_ffn(x, p['norm_ffn'][layer], p['moe_router'][i], p['moe_w1'][i], p['moe_w3'][i], p['moe_w2'][i])
    y = _rmsnorm(x, p['final_norm'])
    return y, jnp.stack(new_conv), jnp.stack(new_s), k, v


def setup_inputs(seed: int = 0) -> dict:
    key = jax.random.key(seed)
    ks = jax.random.split(key, 32)
    f32 = jnp.float32
    nrm = lambda k, s, sc: jax.random.normal(k, s, f32) * sc
    win_c = min(WINDOW, PAST_LEN)
    dt = jnp.exp(jax.random.uniform(ks[10], (N_A, HV), f32, np.log(1e-3), np.log(1e-1)))
    return {
        'x_prompt': nrm(ks[0], (BATCH, SEQ, D_MODEL), 1.0),
        'x_sample': nrm(ks[1], (DEC_BATCH, DEC_SEQ, D_MODEL), 1.0),
        'cache_conv': nrm(ks[2], (N_A, DEC_BATCH, CONV_W - 1, CONV_CH), 1.0),
        'state_delta': nrm(ks[3], (N_A, DEC_BATCH, HV, DK, DV), DK ** -0.5),
        'cache_k': nrm(ks[4], (DEC_BATCH, win_c, HKV, HD), 1.0),
        'cache_v': nrm(ks[5], (DEC_BATCH, win_c, HKV, HD), 1.0),
        'norm_attn': 1.0 + nrm(ks[6], (DEPTH, D_MODEL), 0.05),
        'norm_ffn': 1.0 + nrm(ks[7], (DEPTH, D_MODEL), 0.05),
        'gdn_w_in': nrm(ks[8], (N_A, D_MODEL, A_PROJ), D_MODEL ** -0.5),
        'gdn_w_conv': nrm(ks[9], (N_A, CONV_W, CONV_CH), CONV_W ** -0.5),
        'gdn_a_log': jnp.log(jax.random.uniform(ks[11], (N_A, HV), f32, 1.0, 16.0)),
        'gdn_dt_bias': dt + jnp.log(-jnp.expm1(-dt)),
        'gdn_o_norm': 1.0 + nrm(ks[12], (N_A, DV), 0.05),
        'gdn_w_out': nrm(ks[13], (N_A, V_DIM, D_MODEL), V_DIM ** -0.5),
        'kv_norm': 1.0 + nrm(ks[14], (D_MODEL,), 0.05),
        'w_kv': nrm(ks[15], (D_MODEL, 2 * HKV * HD), D_MODEL ** -0.5),
        'swa_w_q': nrm(ks[16], (N_B, D_MODEL, HQ * HD), D_MODEL ** -0.5),
        'swa_sinks': nrm(ks[17], (N_B, HQ), 1.0),
        'swa_w_o': nrm(ks[18], (N_B, HQ * HD, D_MODEL), (HQ * HD) ** -0.5),
        'ffn_w1': nrm(ks[19], (N_DENSE, D_MODEL, D_FF), D_MODEL ** -0.5),
        'ffn_w3': nrm(ks[20], (N_DENSE, D_MODEL, D_FF), D_MODEL ** -0.5),
        'ffn_w2': nrm(ks[21], (N_DENSE, D_FF, D_MODEL), D_FF ** -0.5),
        'moe_router': nrm(ks[22], (N_MOE, D_MODEL, N_EXPERTS), D_MODEL ** -0.5),
        'moe_w1': nrm(ks[23], (N_MOE, N_EXPERTS, D_MODEL, D_FF), D_MODEL ** -0.5),
        'moe_w3': nrm(ks[24], (N_MOE, N_EXPERTS, D_MODEL, D_FF), D_MODEL ** -0.5),
        'moe_w2': nrm(ks[25], (N_MOE, N_EXPERTS, D_FF, D_MODEL), D_FF ** -0.5),
        'final_norm': 1.0 + nrm(ks[26], (D_MODEL,), 0.05),
    }


def reference(x_prompt, x_sample, cache_conv, state_delta, cache_k, cache_v,
              norm_attn, norm_ffn, gdn_w_in, gdn_w_conv, gdn_a_log, gdn_dt_bias, gdn_o_norm, gdn_w_out,
              kv_norm, w_kv, swa_w_q, swa_sinks, swa_w_o,
              ffn_w1, ffn_w3, ffn_w2, moe_router, moe_w1, moe_w3, moe_w2, final_norm):
    p = dict(norm_attn=norm_attn, norm_ffn=norm_ffn, gdn_w_in=gdn_w_in, gdn_w_conv=gdn_w_conv,
             gdn_a_log=gdn_a_log, gdn_dt_bias=gdn_dt_bias, gdn_o_norm=gdn_o_norm, gdn_w_out=gdn_w_out,
             kv_norm=kv_norm, w_kv=w_kv, swa_w_q=swa_w_q, swa_sinks=swa_sinks, swa_w_o=swa_w_o,
             ffn_w1=ffn_w1, ffn_w3=ffn_w3, ffn_w2=ffn_w2, moe_router=moe_router,
             moe_w1=moe_w1, moe_w3=moe_w3, moe_w2=moe_w2, final_norm=final_norm)
    Bp, Tp, _ = x_prompt.shape
    pos_p = jnp.arange(Tp, dtype=jnp.int32)
    conv0 = jnp.zeros((N_A, Bp, CONV_W - 1, CONV_CH), x_prompt.dtype)
    s0 = jnp.zeros((N_A, Bp, HV, DK, DV), jnp.float32)
    y_prompt, conv_p, delta_p, k_full, v_full = _trunk(x_prompt, pos_p, conv0, s0, None, None, p)
    win_p = min(WINDOW, Tp)
    k_p = k_full[:, Tp - win_p:]
    v_p = v_full[:, Tp - win_p:]
    pos_s = PAST_LEN + jnp.arange(x_sample.shape[1], dtype=jnp.int32)
    y_sample, conv_s, delta_s, k_s, v_s = _trunk(x_sample, pos_s, cache_conv, state_delta, cache_k, cache_v, p)
    return (y_prompt, y_sample, conv_p, delta_p, k_p, v_p, conv_s, delta_s, k_s, v_s)
```

```python
import functools

import jax
import jax.numpy as jnp
import numpy as np
from jax import lax
from jax.experimental import pallas as pl
from jax.experimental.pallas import tpu as pltpu

F32 = jnp.float32
BF16 = jnp.bfloat16

D_MODEL = 1024
CHUNK = 64
HK = 8
HV = 16
DK = 128
DV = 128
QK_DIM = HK * DK
V_DIM = HV * DV
CONV_CH = 2 * QK_DIM + V_DIM
CONV_W = 4
HQ = 16
HKV = 4
HD = 64
WINDOW = 128
ROT_DIM = HD // 4
ROPE_THETA = 500000.0
D_FF = 2816
N_EXPERTS = 8
TOP_K = 2
EPS = 1e-6
PAST_LEN = 2048

LANES = 128
ROW_TILE = 512
VMEM_LIMIT = 48 * 1024 * 1024


def _params(sem, vmem=VMEM_LIMIT):
    return pltpu.CompilerParams(dimension_semantics=sem, vmem_limit_bytes=vmem)


def _rms_scale(x):
    return lax.rsqrt(jnp.mean(x * x, axis=-1, keepdims=True) + EPS)


def _bdot(a, b):
    return jnp.dot(a.astype(BF16), b.astype(BF16), preferred_element_type=F32)


def _bdot_nt(a, b):
    return lax.dot_general(a.astype(BF16), b.astype(BF16), (((1,), (1,)), ((), ())),
                           preferred_element_type=F32)


def _bdot_tn(a, b):
    return lax.dot_general(a.astype(BF16), b.astype(BF16), (((0,), (0,)), ((), ())),
                           preferred_element_type=F32)


def _split3(x):
    x1 = x.astype(BF16)
    r1 = x - x1.astype(F32)
    x2 = r1.astype(BF16)
    x3 = (r1 - x2.astype(F32)).astype(BF16)
    return x1, x2, x3


def _split2(x):
    x1 = x.astype(BF16)
    x2 = (x - x1.astype(F32)).astype(BF16)
    return x1, x2


def _dot_hi(a, b):
    a1, a2 = _split2(a)
    b1, b2 = _split2(b)
    d = functools.partial(jnp.dot, preferred_element_type=F32)
    return d(a1, b1) + (d(a1, b2) + d(a2, b1))


def _sigmoid(x):
    return 1.0 / (1.0 + jnp.exp(-x))


def _silu(x):
    return x * _sigmoid(x)


def _qkv_conv_kernel(x_ref, g_ref, w_ref, wc_ref, cs_ref, out_ref, cst_ref,
                     h_scr, pad_scr, carry_scr, *, tm, tn):
    ti = pl.program_id(1)
    j = pl.program_id(2)

    @pl.when(j == 0)
    def _():
        x = x_ref[...]
        h_scr[...] = (x * _rms_scale(x) * g_ref[...]).astype(BF16)

    proj = jnp.dot(h_scr[...], w_ref[...], preferred_element_type=F32)

    @pl.when(ti == 0)
    def _():
        carry_scr[j, 5:8, :] = cs_ref[0]

    pad_scr[0:8, :] = carry_scr[j]
    pad_scr[8:, :] = proj
    wc = wc_ref[...]
    acc = proj * wc[3:4, :]
    for s in range(1, CONV_W):
        acc = acc + pad_scr[8 - s:8 - s + tm, :] * wc[3 - s:4 - s, :]
    carry_scr[j] = pad_scr[tm:tm + 8, :]
    cst_ref[0] = pad_scr[tm + 5:tm + 8, :]
    y = _silu(acc)

    @pl.when(j >= 2)
    def _():
        out_ref[...] = y.astype(out_ref.dtype)

    @pl.when(j < 2)
    def _():
        scale = jnp.where(j == 0, DK ** -0.5, 1.0).astype(F32)
        for hh in range(tn // DK):
            seg = y[:, hh * DK:(hh + 1) * DK]
            inv = lax.rsqrt(jnp.sum(seg * seg, axis=-1, keepdims=True) + EPS) * scale
            out_ref[:, hh * DK:(hh + 1) * DK] = (seg * inv).astype(out_ref.dtype)


def _qkv_conv(x, g, w_qkv, w_conv, conv_state, B, T):
    R = B * T
    tm = min(ROW_TILE, T)
    tn = QK_DIM
    nt = T // tm
    nj = CONV_CH // tn
    kern = functools.partial(_qkv_conv_kernel, tm=tm, tn=tn)
    return pl.pallas_call(
        kern,
        grid=(B, nt, nj),
        in_specs=[
            pl.BlockSpec((tm, D_MODEL), lambda b, t, j: (b * nt + t, 0)),
            pl.BlockSpec((1, D_MODEL), lambda b, t, j: (0, 0)),
            pl.BlockSpec((D_MODEL, tn), lambda b, t, j: (0, j)),
            pl.BlockSpec((CONV_W, tn), lambda b, t, j: (0, j)),
            pl.BlockSpec((1, CONV_W - 1, tn), lambda b, t, j: (b, 0, j)),
        ],
        out_specs=[
            pl.BlockSpec((tm, tn), lambda b, t, j: (b * nt + t, j)),
            pl.BlockSpec((1, CONV_W - 1, tn), lambda b, t, j: (b * nt + t, 0, j)),
        ],
        out_shape=[
            jax.ShapeDtypeStruct((R, CONV_CH), BF16),
            jax.ShapeDtypeStruct((B * nt, CONV_W - 1, CONV_CH), F32),
        ],
        scratch_shapes=[
            pltpu.VMEM((tm, D_MODEL), BF16),
            pltpu.VMEM((tm + 8, tn), F32),
            pltpu.VMEM((nj, 8, tn), F32),
        ],
        compiler_params=_params(("arbitrary", "arbitrary", "arbitrary")),
        name="gdn_qkv_conv",
    )(x, g, w_qkv, w_conv, conv_state)


def _softplus(x):
    return jnp.maximum(x, 0.0) + jnp.log(1.0 + jnp.exp(-jnp.abs(x)))


def _zbg_kernel(x_ref, g_ref, wz_ref, wba_ref, wbat_ref, alog_ref, dtb_ref, alogt_ref, dtbt_ref, tri_ref,
                z_ref, beta_ref, gc_ref, gct_ref, *, tm):
    x = x_ref[...]
    h = (x * _rms_scale(x) * g_ref[...]).astype(BF16)
    z_ref[...] = jnp.dot(h, wz_ref[...], preferred_element_type=F32).astype(z_ref.dtype)
    ba = jnp.dot(h, wba_ref[...], preferred_element_type=F32)
    beta_ref[...] = _sigmoid(ba[:, :HV])
    gcol = -jnp.exp(alog_ref[...]) * _softplus(ba[:, HV:] + dtb_ref[...])
    tri = tri_ref[...]
    g1, g2, g3 = _split3(gcol)
    d = functools.partial(jnp.dot, preferred_element_type=F32)
    gc_ref[...] = d(tri, g1) + (d(tri, g2) + d(tri, g3))
    at = lax.dot_general(wbat_ref[...], h, (((1,), (1,)), ((), ())), preferred_element_type=F32)
    grow = -jnp.exp(alogt_ref[...]) * _softplus(at + dtbt_ref[...])
    r1, r2, r3 = _split3(grow)
    dn = lambda a: lax.dot_general(a, tri, (((1,), (1,)), ((), ())), preferred_element_type=F32)
    gct = dn(r1) + (dn(r2) + dn(r3))
    for c in range(tm // CHUNK):
        gct_ref[0, c] = gct[:, c * CHUNK:(c + 1) * CHUNK]


def _zbg(x, g, w_z, w_ba, a_log, dt_bias, B, T):
    R = B * T
    tm = min(ROW_TILE, T)
    nt = T // tm
    nc = T // CHUNK
    idx = np.arange(tm)
    tri = jnp.asarray(((idx[:, None] // CHUNK == idx[None, :] // CHUNK) & (idx[:, None] >= idx[None, :]))
                      .astype(np.float32), dtype=BF16)
    w_a_t = jnp.transpose(w_ba[:, HV:])
    full = lambda shape: pl.BlockSpec(shape, lambda b, t: (0,) * len(shape))
    kern = functools.partial(_zbg_kernel, tm=tm)
    return pl.pallas_call(
        kern,
        grid=(B, nt),
        in_specs=[
            pl.BlockSpec((tm, D_MODEL), lambda b, t: (b * nt + t, 0)),
            full((1, D_MODEL)),
            full((D_MODEL, V_DIM)),
            full((D_MODEL, 2 * HV)),
            full((HV, D_MODEL)),
            full((1, HV)), full((1, HV)), full((HV, 1)), full((HV, 1)),
            full((tm, tm)),
        ],
        out_specs=[
            pl.BlockSpec((tm, V_DIM), lambda b, t: (b * nt + t, 0)),
            pl.BlockSpec((tm, HV), lambda b, t: (b * nt + t, 0)),
            pl.BlockSpec((tm, HV), lambda b, t: (b * nt + t, 0)),
            pl.BlockSpec((1, tm // CHUNK, HV, CHUNK), lambda b, t: (b, t, 0, 0)),
        ],
        out_shape=[
            jax.ShapeDtypeStruct((R, V_DIM), BF16),
            jax.ShapeDtypeStruct((R, HV), F32),
            jax.ShapeDtypeStruct((R, HV), F32),
            jax.ShapeDtypeStruct((B, nc, HV, CHUNK), F32),
        ],
        compiler_params=_params(("arbitrary", "arbitrary")),
        name="gdn_z_beta_decay",
    )(x, g, w_z, w_ba, w_a_t, a_log.reshape(1, HV), dt_bias.reshape(1, HV),
      a_log.reshape(HV, 1), dt_bias.reshape(HV, 1), tri)


def _unit_lower_inverse(low):
    c = low.shape[0]
    eye = (lax.broadcasted_iota(jnp.int32, (c, c), 0) == lax.broadcasted_iota(jnp.int32, (c, c), 1)).astype(F32)
    acc = eye - low
    pw = low
    n = 2
    while n < c:
        pw = _dot_hi(pw, pw)
        acc = acc + _dot_hi(acc, pw)
        n *= 2
    return acc


def _delta_kernel(q_ref, k_ref, v_ref, z_ref, beta_ref, gc_ref, gct_ref, s0_ref, on_ref,
                  og_ref, s_ref):
    c = pl.program_id(1)

    @pl.when(c == 0)
    def _():
        s_ref[...] = s0_ref[...]

    C = CHUNK
    ri = lax.broadcasted_iota(jnp.int32, (C, C), 0)
    ci = lax.broadcasted_iota(jnp.int32, (C, C), 1)
    incl = ri >= ci
    strict = ri > ci
    beta = beta_ref[...]
    gc = gc_ref[...]
    gct = gct_ref[0, 0]
    onorm = on_ref[...]

    for hk in range(HK):
        kb = k_ref[:, hk * DK:(hk + 1) * DK]
        qb = q_ref[:, hk * DK:(hk + 1) * DK]
        kf = kb.astype(F32)
        qf = qb.astype(F32)
        kq = _bdot_nt(jnp.concatenate([kb, qb], axis=0), kb)
        kk = kq[:C]
        qk = kq[C:]
        for hv in range(HV // HK):
            h = hk * (HV // HK) + hv
            bcol = beta[:, h:h + 1]
            gcol = gc[:, h:h + 1]
            grow = gct[h:h + 1, :]
            decay = jnp.exp(jnp.where(incl, gcol - grow, -jnp.inf))
            low = jnp.where(strict, bcol * kk * decay, 0.0)
            tinv = _unit_lower_inverse(low)
            eg = jnp.exp(gcol)
            vf = v_ref[:, h * DV:(h + 1) * DV].astype(F32)
            rhs = jnp.concatenate([vf * bcol, kf * (bcol * eg)], axis=1)
            uw = _bdot(tinv, rhs)
            u = uw[:, :DV]
            w = uw[:, DV:]
            S = s_ref[0, h]
            wq = _bdot(jnp.concatenate([w, qf * eg], axis=0), S)
            v_new = u - wq[:C]
            o = wq[C:] + _bdot(qk * decay, v_new)
            g_last = grow[:, C - 1:C]
            s_ref[0, h] = S * jnp.exp(g_last) + _bdot_tn(kf * jnp.exp(g_last - gcol), v_new)
            on = o * _rms_scale(o) * onorm
            zf = z_ref[:, h * DV:(h + 1) * DV].astype(F32)
            og_ref[:, h * DV:(h + 1) * DV] = (on * _silu(zf)).astype(og_ref.dtype)


def _delta(qkv, z, beta, gc, gct, s0, o_norm, B, T):
    R = B * T
    nc = T // CHUNK
    row = lambda b, c: (b * nc + c, 0)
    return pl.pallas_call(
        _delta_kernel,
        grid=(B, nc),
        in_specs=[
            pl.BlockSpec((CHUNK, QK_DIM), lambda b, c: (b * nc + c, 0)),
            pl.BlockSpec((CHUNK, QK_DIM), lambda b, c: (b * nc + c, 1)),
            pl.BlockSpec((CHUNK, V_DIM), lambda b, c: (b * nc + c, 1)),
            pl.BlockSpec((CHUNK, V_DIM), row),
            pl.BlockSpec((CHUNK, HV), row),
            pl.BlockSpec((CHUNK, HV), row),
            pl.BlockSpec((1, 1, HV, CHUNK), lambda b, c: (b, c, 0, 0)),
            pl.BlockSpec((1, HV, DK, DV), lambda b, c: (b, 0, 0, 0)),
            pl.BlockSpec((1, DV), lambda b, c: (0, 0)),
        ],
        out_specs=[
            pl.BlockSpec((CHUNK, V_DIM), row),
            pl.BlockSpec((1, HV, DK, DV), lambda b, c: (b, 0, 0, 0)),
        ],
        out_shape=[
            jax.ShapeDtypeStruct((R, V_DIM), BF16),
            jax.ShapeDtypeStruct((B, HV, DK, DV), F32),
        ],
        compiler_params=_params(("arbitrary", "arbitrary")),
        name="gdn_delta_rule",
    )(qkv, qkv, qkv, z, beta, gc, gct, s0, o_norm)


def _proj_res_kernel(x_ref, a_ref, w_ref, o_ref):
    o_ref[...] = x_ref[...] + jnp.dot(a_ref[...], w_ref[...], preferred_element_type=F32)


def _proj_res(x, a, w):
    R = x.shape[0]
    K = a.shape[1]
    tm = min(ROW_TILE, R)
    return pl.pallas_call(
        _proj_res_kernel,
        grid=(R // tm,),
        in_specs=[
            pl.BlockSpec((tm, D_MODEL), lambda i: (i, 0)),
            pl.BlockSpec((tm, K), lambda i: (i, 0)),
            pl.BlockSpec((K, D_MODEL), lambda i: (0, 0)),
        ],
        out_specs=pl.BlockSpec((tm, D_MODEL), lambda i: (i, 0)),
        out_shape=jax.ShapeDtypeStruct((R, D_MODEL), F32),
        compiler_params=_params(("arbitrary",)),
        name="proj_residual",
    )(x, a, w)


FF_TILE = 1408


def _dense_ffn_kernel(x_ref, g_ref, w1_ref, w3_ref, w2_ref, o_ref, h_scr, acc_scr):
    f = pl.program_id(1)

    @pl.when(f == 0)
    def _():
        x = x_ref[...]
        h_scr[...] = (x * _rms_scale(x) * g_ref[...]).astype(BF16)
        acc_scr[...] = x

    h = h_scr[...]
    a = jnp.dot(h, w1_ref[...], preferred_element_type=F32)
    b = jnp.dot(h, w3_ref[...], preferred_element_type=F32)
    mid = (_silu(a) * b).astype(BF16)
    acc_scr[...] += jnp.dot(mid, w2_ref[...], preferred_element_type=F32)

    @pl.when(f == pl.num_programs(1) - 1)
    def _():
        o_ref[...] = acc_scr[...]


def _dense_ffn(x, g, w1, w3, w2):
    R = x.shape[0]
    tm = min(ROW_TILE, R)
    nf = D_FF // FF_TILE
    return pl.pallas_call(
        _dense_ffn_kernel,
        grid=(R // tm, nf),
        in_specs=[
            pl.BlockSpec((tm, D_MODEL), lambda i, f: (i, 0)),
            pl.BlockSpec((1, D_MODEL), lambda i, f: (0, 0)),
            pl.BlockSpec((D_MODEL, FF_TILE), lambda i, f: (0, f)),
            pl.BlockSpec((D_MODEL, FF_TILE), lambda i, f: (0, f)),
            pl.BlockSpec((FF_TILE, D_MODEL), lambda i, f: (f, 0)),
        ],
        out_specs=pl.BlockSpec((tm, D_MODEL), lambda i, f: (i, 0)),
        out_shape=jax.ShapeDtypeStruct((R, D_MODEL), F32),
        scratch_shapes=[pltpu.VMEM((tm, D_MODEL), BF16), pltpu.VMEM((tm, D_MODEL), F32)],
        compiler_params=_params(("arbitrary", "arbitrary")),
        name="dense_swiglu",
    )(x, g, w1, w3, w2)


KV_DUP = 2 * HKV * LANES
KV_STD = 2 * HKV * HD


def _rope(x, cos, sm, sp):
    up = pltpu.roll(x, LANES - ROT_DIM // 2, 1)
    dn = pltpu.roll(x, ROT_DIM // 2, 1)
    return x * cos + up * sm + dn * sp


def _qkv_rope_kernel(x_ref, gq_ref, gkv_ref, wq_ref, wkv_ref, cos_ref, sm_ref, sp_ref,
                     q_ref, kvd_ref, kvs_ref):
    x = x_ref[...]
    xr = x * _rms_scale(x)
    hq = (xr * gq_ref[...]).astype(BF16)
    hkv = (xr * gkv_ref[...]).astype(BF16)
    cos = cos_ref[...]
    sm = sm_ref[...]
    sp = sp_ref[...]
    q = jnp.dot(hq, wq_ref[...], preferred_element_type=F32)
    for t in range(HQ * HD // LANES):
        q_ref[:, t * LANES:(t + 1) * LANES] = _rope(q[:, t * LANES:(t + 1) * LANES], cos, sm, sp).astype(q_ref.dtype)
    kv = jnp.dot(hkv, wkv_ref[...], preferred_element_type=F32)
    half = KV_DUP // 2
    for t in range(HKV):
        sl = slice(t * LANES, (t + 1) * LANES)
        kvd_ref[:, sl] = _rope(kv[:, sl], cos, sm, sp).astype(kvd_ref.dtype)
    kvd_ref[:, half:] = kv[:, half:KV_DUP].astype(kvd_ref.dtype)
    for t in range(HKV * HD // LANES):
        sl = slice(KV_DUP + t * LANES, KV_DUP + (t + 1) * LANES)
        kvs_ref[:, t * LANES:(t + 1) * LANES] = _rope(kv[:, sl], cos, sm, sp)
    kvs_ref[:, HKV * HD:] = kv[:, KV_DUP + HKV * HD:]


def _rope_tables(pos):
    half = ROT_DIM // 2
    inv = jnp.power(ROPE_THETA, -jnp.arange(half, dtype=F32) * 2.0 / ROT_DIM)
    ang = pos.astype(F32)[:, None] * inv[None, :]
    cos = jnp.cos(ang)
    sin = jnp.sin(ang)
    T = pos.shape[0]
    one = jnp.ones((T, HD - ROT_DIM), F32)
    zero = jnp.zeros((T, HD - ROT_DIM), F32)
    zh = jnp.zeros((T, half), F32)
    c64 = jnp.concatenate([cos, cos, one], axis=1)
    sm64 = jnp.concatenate([-sin, zh, zero], axis=1)
    sp64 = jnp.concatenate([zh, sin, zero], axis=1)
    dup = lambda a: jnp.concatenate([a, a], axis=1)
    return dup(c64), dup(sm64), dup(sp64)


def _qkv_rope(x, gq, gkv, w_q, w_kvx, tables, B, T):
    R = B * T
    tm = min(ROW_TILE, T)
    nt = T // tm
    nkv = KV_DUP + KV_STD
    tab = pl.BlockSpec((tm, LANES), lambda i: (i % nt, 0))
    return pl.pallas_call(
        _qkv_rope_kernel,
        grid=(R // tm,),
        in_specs=[
            pl.BlockSpec((tm, D_MODEL), lambda i: (i, 0)),
            pl.BlockSpec((1, D_MODEL), lambda i: (0, 0)),
            pl.BlockSpec((1, D_MODEL), lambda i: (0, 0)),
            pl.BlockSpec((D_MODEL, HQ * HD), lambda i: (0, 0)),
            pl.BlockSpec((D_MODEL, nkv), lambda i: (0, 0)),
            tab, tab, tab,
        ],
        out_specs=[
            pl.BlockSpec((tm, HQ * HD), lambda i: (i, 0)),
            pl.BlockSpec((tm, KV_DUP), lambda i: (i, 0)),
            pl.BlockSpec((tm, KV_STD), lambda i: (i, 0)),
        ],
        out_shape=[
            jax.ShapeDtypeStruct((R, HQ * HD), BF16),
            jax.ShapeDtypeStruct((R, KV_DUP), BF16),
            jax.ShapeDtypeStruct((R, KV_STD), F32),
        ],
        compiler_params=_params(("arbitrary",)),
        name="swa_qkv_rope",
    )(x, gq, gkv, w_q, w_kvx, *tables)


KEY_CHUNKS = WINDOW // CHUNK + 1
KEY_PAD = 256


def _attn_kernel(q_ref, k0_ref, k1_ref, k2_ref, sink_ref, o_ref, *, off):
    c = pl.program_id(1)
    krefs = (k0_ref, k1_ref, k2_ref)
    lane = lax.broadcasted_iota(jnp.int32, (KEY_CHUNKS * CHUNK, LANES), 1)
    lo = lane < HD
    col = lax.broadcasted_iota(jnp.int32, (CHUNK, KEY_PAD), 1)
    first = c + off - (KEY_CHUNKS - 1)
    valid = (col < KEY_CHUNKS * CHUNK) & (col // CHUNK + first >= 0)
    zpad = jnp.zeros((KEY_PAD - KEY_CHUNKS * CHUNK, LANES), BF16)
    zero = jnp.zeros((), BF16)
    for g in range(HKV):
        kg = jnp.concatenate([r[:, g * LANES:(g + 1) * LANES] for r in krefs], axis=0)
        vg = jnp.concatenate([r[:, KV_DUP // 2 + g * LANES:KV_DUP // 2 + (g + 1) * LANES] for r in krefs], axis=0)
        kblk = jnp.concatenate([jnp.where(lo, kg, zero), zpad, jnp.where(lo, zero, kg), zpad], axis=0)
        vblk = jnp.concatenate([jnp.where(lo, vg, zero), zpad, jnp.where(lo, zero, vg), zpad], axis=0)
        for p in range(HQ // HKV // 2):
            t = g * (HQ // HKV // 2) + p
            qp = q_ref[:, t * LANES:(t + 1) * LANES]
            s = lax.dot_general(qp, kblk, (((1,), (1,)), ((), ())), preferred_element_type=F32) * (HD ** -0.5)
            ps = []
            for hh in range(2):
                sh = jnp.where(valid, s[:, hh * KEY_PAD:(hh + 1) * KEY_PAD], -jnp.inf)
                sink = sink_ref[:, 2 * t + hh:2 * t + hh + 1]
                m = jnp.maximum(jnp.max(sh, axis=-1, keepdims=True), sink)
                e = jnp.exp(sh - m)
                den = jnp.sum(e, axis=-1, keepdims=True) + jnp.exp(sink - m)
                ps.append((e / den).astype(BF16))
            pcat = jnp.concatenate(ps, axis=1)
            o_ref[:, t * LANES:(t + 1) * LANES] = jnp.dot(pcat, vblk, preferred_element_type=F32).astype(o_ref.dtype)


def _attention(q, kvd, sinks, B, Tq, Tk):
    ncq = Tq // CHUNK
    nck = Tk // CHUNK
    off = nck - ncq
    kern = functools.partial(_attn_kernel, off=off)

    def kspec(i):
        return pl.BlockSpec((CHUNK, KV_DUP),
                            lambda b, c: (b * nck + jnp.maximum(c + off - (KEY_CHUNKS - 1) + i, 0), 0))

    return pl.pallas_call(
        kern,
        grid=(B, ncq),
        in_specs=[
            pl.BlockSpec((CHUNK, HQ * HD), lambda b, c: (b * ncq + c, 0)),
            kspec(0), kspec(1), kspec(2),
            pl.BlockSpec((1, HQ), lambda b, c: (0, 0)),
        ],
        out_specs=pl.BlockSpec((CHUNK, HQ * HD), lambda b, c: (b * ncq + c, 0)),
        out_shape=jax.ShapeDtypeStruct((B * Tq, HQ * HD), BF16),
        compiler_params=_params(("arbitrary", "arbitrary")),
        name="swa_attention",
    )(q, kvd, kvd, kvd, sinks)


def _oproj_router_kernel(x_ref, a_ref, w_ref, g_ref, wr_ref, x3_ref, h_ref, e_ref, gate_ref, cnt_ref):
    i = pl.program_id(0)
    x3 = x_ref[...] + jnp.dot(a_ref[...], w_ref[...], preferred_element_type=F32)
    x3_ref[...] = x3
    h = x3 * _rms_scale(x3) * g_ref[...]
    h_ref[...] = h
    logits = _dot_hi(h, wr_ref[...])
    eidx = lax.broadcasted_iota(jnp.int32, logits.shape, 1)
    m1 = jnp.max(logits, axis=-1, keepdims=True)
    i1 = jnp.min(jnp.where(logits == m1, eidx, N_EXPERTS), axis=-1, keepdims=True)
    rest = jnp.where(eidx == i1, -jnp.inf, logits)
    m2 = jnp.max(rest, axis=-1, keepdims=True)
    i2 = jnp.min(jnp.where(rest == m2, eidx, N_EXPERTS), axis=-1, keepdims=True)
    e2 = jnp.exp(m2 - m1)
    den = 1.0 + e2
    e_ref[...] = jnp.concatenate([i1, i2], axis=1)
    gate_ref[...] = jnp.concatenate([1.0 / den, e2 / den], axis=1)
    hot = ((eidx == i1) | (eidx == i2)).astype(F32)

    @pl.when(i == 0)
    def _():
        cnt_ref[...] = jnp.zeros_like(cnt_ref)

    cnt_ref[...] += jnp.sum(hot, axis=0, keepdims=True)


def _oproj_router(x, a, w_o, g, w_r):
    R = x.shape[0]
    tm = min(ROW_TILE, R)
    row = lambda n: pl.BlockSpec((tm, n), lambda i: (i, 0))
    return pl.pallas_call(
        _oproj_router_kernel,
        grid=(R // tm,),
        in_specs=[
            row(D_MODEL), row(HQ * HD),
            pl.BlockSpec((HQ * HD, D_MODEL), lambda i: (0, 0)),
            pl.BlockSpec((1, D_MODEL), lambda i: (0, 0)),
            pl.BlockSpec((D_MODEL, N_EXPERTS), lambda i: (0, 0)),
        ],
        out_specs=[row(D_MODEL), row(D_MODEL), row(TOP_K), row(TOP_K),
                   pl.BlockSpec((1, N_EXPERTS), lambda i: (0, 0))],
        out_shape=[
            jax.ShapeDtypeStruct((R, D_MODEL), F32),
            jax.ShapeDtypeStruct((R, D_MODEL), F32),
            jax.ShapeDtypeStruct((R, TOP_K), jnp.int32),
            jax.ShapeDtypeStruct((R, TOP_K), F32),
            jax.ShapeDtypeStruct((1, N_EXPERTS), F32),
        ],
        compiler_params=_params(("arbitrary",)),
        name="swa_oproj_router",
    )(x, a, w_o, g, w_r)


def _slot_kernel(e_ref, base_ref, tri_ref, pos_ref, run_scr):
    i = pl.program_id(0)

    @pl.when(i == 0)
    def _():
        run_scr[...] = jnp.zeros_like(run_scr)

    e = e_ref[...]
    tm = e.shape[0]
    eidx = lax.broadcasted_iota(jnp.int32, (tm, N_EXPERTS), 1)
    hot0 = eidx == e[:, 0:1]
    hot1 = eidx == e[:, 1:2]
    hot = (hot0 | hot1).astype(BF16)
    before = jnp.dot(tri_ref[...], hot, preferred_element_type=F32)
    dest = before + run_scr[...] + base_ref[...]
    p0 = jnp.sum(jnp.where(hot0, dest, 0.0), axis=-1, keepdims=True)
    p1 = jnp.sum(jnp.where(hot1, dest, 0.0), axis=-1, keepdims=True)
    pos_ref[...] = jnp.concatenate([p0, p1], axis=1).astype(jnp.int32)
    run_scr[...] += jnp.sum(hot.astype(F32), axis=0, keepdims=True)


def _slots(eidx, base):
    R = eidx.shape[0]
    tm = min(ROW_TILE, R)
    idx = np.arange(tm)
    tri = jnp.asarray((idx[:, None] > idx[None, :]).astype(np.float32), dtype=BF16)
    return pl.pallas_call(
        _slot_kernel,
        grid=(R // tm,),
        in_specs=[
            pl.BlockSpec((tm, TOP_K), lambda i: (i, 0)),
            pl.BlockSpec((1, N_EXPERTS), lambda i: (0, 0)),
            pl.BlockSpec((tm, tm), lambda i: (0, 0)),
        ],
        out_specs=pl.BlockSpec((tm, TOP_K), lambda i: (i, 0)),
        out_shape=jax.ShapeDtypeStruct((R, TOP_K), jnp.int32),
        scratch_shapes=[pltpu.VMEM((1, N_EXPERTS), F32)],
        compiler_params=_params(("arbitrary",)),
        name="moe_slots",
    )(eidx, base, tri)


def _row_copy(src, i, dst, j, sem):
    return pltpu.make_async_copy(src.at[pl.ds(i, 1)], dst.at[pl.ds(j, 1)], sem)


def _dispatch_kernel(pos_ref, cnt_ref, base_ref, nused_ref, h_ref, xs_ref, zero_scr, sem, zsem, *, tm, te, n_tiles):
    i = pl.program_id(0)

    @pl.when(i == 0)
    def _():
        zero_scr[...] = jnp.zeros_like(zero_scr)

        def tile_copy(t):
            dst = xs_ref.at[pl.ds(pl.multiple_of(t * te, te), te)]
            return pltpu.make_async_copy(zero_scr, dst, zsem)

        def issue_tile(t, carry):
            tile_copy(t).start()
            return carry

        def drain_tile(t, carry):
            tile_copy(t).wait()
            return carry

        lax.fori_loop(nused_ref[0], n_tiles, issue_tile, 0)
        lax.fori_loop(nused_ref[0], n_tiles, drain_tile, 0)
        for e in range(N_EXPERTS):
            n = cnt_ref[e]
            start = base_ref[e] + n
            npad = (te - n % te) % te

            def issue(r, carry):
                _row_copy(zero_scr, 0, xs_ref, start + r, zsem).start()
                return carry

            def drain(r, carry):
                _row_copy(zero_scr, 0, xs_ref, start + r, zsem).wait()
                return carry

            lax.fori_loop(0, npad, issue, 0)
            lax.fori_loop(0, npad, drain, 0)

    def issue(r, carry):
        _row_copy(h_ref, r, xs_ref, pos_ref[2 * r], sem).start()
        _row_copy(h_ref, r, xs_ref, pos_ref[2 * r + 1], sem).start()
        return carry

    def drain(r, carry):
        _row_copy(h_ref, r, xs_ref, pos_ref[2 * r], sem).wait()
        _row_copy(h_ref, r, xs_ref, pos_ref[2 * r + 1], sem).wait()
        return carry

    lax.fori_loop(0, tm, issue, 0)
    lax.fori_loop(0, tm, drain, 0)


def _dispatch(h, pos_flat, counts, base, n_used, n_tiles, te):
    R = h.shape[0]
    tm = min(ROW_TILE, R)
    kern = functools.partial(_dispatch_kernel, tm=tm, te=te, n_tiles=n_tiles)
    smem = pl.BlockSpec(memory_space=pltpu.SMEM)
    return pl.pallas_call(
        kern,
        grid=(R // tm,),
        in_specs=[
            pl.BlockSpec((TOP_K * tm,), lambda i: (i,), memory_space=pltpu.SMEM),
            smem, smem, smem,
            pl.BlockSpec((tm, D_MODEL), lambda i: (i, 0)),
        ],
        out_specs=pl.BlockSpec(memory_space=pl.ANY),
        out_shape=jax.ShapeDtypeStruct((n_tiles * te, D_MODEL), F32),
        scratch_shapes=[pltpu.VMEM((te, D_MODEL), F32), pltpu.SemaphoreType.DMA(()), pltpu.SemaphoreType.DMA(())],
        compiler_params=pltpu.CompilerParams(dimension_semantics=("arbitrary",), vmem_limit_bytes=VMEM_LIMIT,
                                             has_side_effects=True),
        name="moe_dispatch",
    )(pos_flat, counts, base, n_used, h)


def _expert_ffn_kernel(te_ref, nused_ref, x_ref, w1_ref, w3_ref, w2_ref, y_ref, acc_scr):
    i = pl.program_id(0)
    f = pl.program_id(1)

    @pl.when(i < nused_ref[0])
    def _():
        h = x_ref[...].astype(BF16)
        a = jnp.dot(h, w1_ref[0], preferred_element_type=F32)
        b = jnp.dot(h, w3_ref[0], preferred_element_type=F32)
        mid = (_silu(a) * b).astype(BF16)
        part = jnp.dot(mid, w2_ref[0], preferred_element_type=F32)

        @pl.when(f == 0)
        def _():
            acc_scr[...] = part

        @pl.when(f > 0)
        def _():
            acc_scr[...] += part

        @pl.when(f == pl.num_programs(1) - 1)
        def _():
            y_ref[...] = acc_scr[...]

    @pl.when((i >= nused_ref[0]) & (f == pl.num_programs(1) - 1))
    def _():
        y_ref[...] = jnp.zeros_like(y_ref)


def _expert_ffn(xs, tile_expert, n_used, w1, w3, w2, te):
    n_rows = xs.shape[0]
    nt = n_rows // te
    nf = D_FF // FF_TILE
    grid_spec = pltpu.PrefetchScalarGridSpec(
        num_scalar_prefetch=2,
        grid=(nt, nf),
        in_specs=[
            pl.BlockSpec((te, D_MODEL), lambda i, f, te_r, nu_r: (i, 0)),
            pl.BlockSpec((1, D_MODEL, FF_TILE), lambda i, f, te_r, nu_r: (te_r[i], 0, f)),
            pl.BlockSpec((1, D_MODEL, FF_TILE), lambda i, f, te_r, nu_r: (te_r[i], 0, f)),
            pl.BlockSpec((1, FF_TILE, D_MODEL), lambda i, f, te_r, nu_r: (te_r[i], f, 0)),
        ],
        out_specs=pl.BlockSpec((te, D_MODEL), lambda i, f, te_r, nu_r: (i, 0)),
        scratch_shapes=[pltpu.VMEM((te, D_MODEL), F32)],
    )
    return pl.pallas_call(
        _expert_ffn_kernel,
        grid_spec=grid_spec,
        out_shape=jax.ShapeDtypeStruct((n_rows, D_MODEL), F32),
        compiler_params=_params(("arbitrary", "arbitrary")),
        name="moe_expert_swiglu",
    )(tile_expert, n_used, xs, w1, w3, w2)


def _combine_kernel(pos_ref, x_ref, gate_ref, g_ref, y_ref, o_ref, buf0, buf1, sem, *, tm):
    def issue(r, carry):
        _row_copy(y_ref, pos_ref[2 * r], buf0, r, sem).start()
        _row_copy(y_ref, pos_ref[2 * r + 1], buf1, r, sem).start()
        return carry

    def drain(r, carry):
        _row_copy(y_ref, pos_ref[2 * r], buf0, r, sem).wait()
        _row_copy(y_ref, pos_ref[2 * r + 1], buf1, r, sem).wait()
        return carry

    lax.fori_loop(0, tm, issue, 0)
    lax.fori_loop(0, tm, drain, 0)
    gate = gate_ref[...]
    x = x_ref[...] + (buf0[...] * gate[:, 0:1] + buf1[...] * gate[:, 1:2])
    o_ref[...] = x * _rms_scale(x) * g_ref[...]


def _combine(x3, gates, pos_flat, y, g):
    R = x3.shape[0]
    tm = min(ROW_TILE, R)
    kern = functools.partial(_combine_kernel, tm=tm)
    return pl.pallas_call(
        kern,
        grid=(R // tm,),
        in_specs=[
            pl.BlockSpec((TOP_K * tm,), lambda i: (i,), memory_space=pltpu.SMEM),
            pl.BlockSpec((tm, D_MODEL), lambda i: (i, 0)),
            pl.BlockSpec((tm, TOP_K), lambda i: (i, 0)),
            pl.BlockSpec((1, D_MODEL), lambda i: (0, 0)),
            pl.BlockSpec(memory_space=pl.ANY),
        ],
        out_specs=pl.BlockSpec((tm, D_MODEL), lambda i: (i, 0)),
        out_shape=jax.ShapeDtypeStruct((R, D_MODEL), F32),
        scratch_shapes=[pltpu.VMEM((tm, D_MODEL), F32), pltpu.VMEM((tm, D_MODEL), F32),
                        pltpu.SemaphoreType.DMA(())],
        compiler_params=_params(("arbitrary",)),
        name="moe_combine_norm",
    )(pos_flat, x3, gates, g, y)


def _prep_weights(p):
    w_in = p["gdn_w_in"][0]
    w_kv = p["w_kv"]
    kcols = w_kv[:, :HKV * HD].reshape(D_MODEL, HKV, HD)
    vcols = w_kv[:, HKV * HD:].reshape(D_MODEL, HKV, HD)
    dup = lambda a: jnp.concatenate([a, a], axis=2).reshape(D_MODEL, HKV * LANES)
    w_kvx = jnp.concatenate([dup(kcols), dup(vcols), w_kv], axis=1)
    return dict(
        w_qkv=w_in[:, :CONV_CH].astype(BF16),
        w_z=w_in[:, CONV_CH:CONV_CH + V_DIM].astype(BF16),
        w_ba=w_in[:, CONV_CH + V_DIM:].astype(BF16),
        w_out=p["gdn_w_out"][0].astype(BF16),
        ffn_w1=p["ffn_w1"][0].astype(BF16),
        ffn_w3=p["ffn_w3"][0].astype(BF16),
        ffn_w2=p["ffn_w2"][0].astype(BF16),
        w_kvx=w_kvx.astype(BF16),
        w_q=p["swa_w_q"][0].astype(BF16),
        w_o=p["swa_w_o"][0].astype(BF16),
        moe_w1=p["moe_w1"][0].astype(BF16),
        moe_w3=p["moe_w3"][0].astype(BF16),
        moe_w2=p["moe_w2"][0].astype(BF16),
    )


def _dup_heads(a):
    B, T = a.shape[:2]
    return jnp.concatenate([a, a], axis=3).reshape(B * T, HKV * LANES)


def _trunk(x, pos, conv0, s0, win_k, win_v, p, w):
    B, T, _ = x.shape
    R = B * T
    xf = x.reshape(R, D_MODEL)
    row = lambda a: a.reshape(1, -1)

    qkv, conv_tiles = _qkv_conv(xf, row(p["norm_attn"][0]), w["w_qkv"], p["gdn_w_conv"][0], conv0, B, T)
    conv_new = conv_tiles.reshape(B, -1, CONV_W - 1, CONV_CH)[:, -1]
    z, beta, gc, gct = _zbg(xf, row(p["norm_attn"][0]), w["w_z"], w["w_ba"], p["gdn_a_log"][0],
                            p["gdn_dt_bias"][0], B, T)
    og, s_new = _delta(qkv, z, beta, gc, gct, s0, row(p["gdn_o_norm"][0]), B, T)
    x1 = _proj_res(xf, og, w["w_out"])
    x2 = _dense_ffn(x1, row(p["norm_ffn"][0]), w["ffn_w1"], w["ffn_w3"], w["ffn_w2"])

    q, kvd, kvs = _qkv_rope(x2, row(p["norm_attn"][1]), row(p["kv_norm"]), w["w_q"], w["w_kvx"],
                            _rope_tables(pos), B, T)
    k_new = kvs[:, :HKV * HD].reshape(B, T, HKV, HD)
    v_new = kvs[:, HKV * HD:].reshape(B, T, HKV, HD)
    if win_k is None:
        keys, Tk = kvd, T
    else:
        hist = jnp.concatenate([_dup_heads(win_k), _dup_heads(win_v)], axis=1).astype(BF16)
        Tw = win_k.shape[1]
        keys = jnp.concatenate([hist.reshape(B, Tw, KV_DUP), kvd.reshape(B, T, KV_DUP)], axis=1)
        Tk = Tw + T
        keys = keys.reshape(B * Tk, KV_DUP)
    att = _attention(q, keys, row(p["swa_sinks"][0]), B, T, Tk)
    x3, h, eidx, gates, counts = _oproj_router(x2, att, w["w_o"], row(p["norm_ffn"][1]), p["moe_router"][0])

    te = min(ROW_TILE, R)
    n_tiles = TOP_K * R // te + N_EXPERTS
    cnt = counts[0].astype(jnp.int32)
    tiles = (cnt + te - 1) // te
    tile_end = jnp.cumsum(tiles)
    base = (tile_end - tiles) * te
    n_used = tile_end[-1:]
    tile_expert = jnp.minimum(jnp.searchsorted(tile_end, jnp.arange(n_tiles, dtype=jnp.int32), side="right"),
                              N_EXPERTS - 1).astype(jnp.int32)
    pos_slot = _slots(eidx, base.astype(F32).reshape(1, N_EXPERTS)).reshape(-1)
    n_used = n_used.astype(jnp.int32)
    xs = _dispatch(h, pos_slot, cnt, base.astype(jnp.int32), n_used, n_tiles, te)
    ys = _expert_ffn(xs, tile_expert, n_used, w["moe_w1"], w["moe_w3"], w["moe_w2"], te)
    y = _combine(x3, gates, pos_slot, ys, row(p["final_norm"]))
    return (y.reshape(B, T, D_MODEL), conv_new[None], s_new[None], k_new, v_new)


def kernel(x_prompt, x_sample, cache_conv, state_delta, cache_k, cache_v, norm_attn, norm_ffn, gdn_w_in,
           gdn_w_conv, gdn_a_log, gdn_dt_bias, gdn_o_norm, gdn_w_out, kv_norm, w_kv, swa_w_q, swa_sinks, swa_w_o,
           ffn_w1, ffn_w3, ffn_w2, moe_router, moe_w1, moe_w3, moe_w2, final_norm):
    p = dict(norm_attn=norm_attn, norm_ffn=norm_ffn, gdn_w_in=gdn_w_in, gdn_w_conv=gdn_w_conv,
             gdn_a_log=gdn_a_log, gdn_dt_bias=gdn_dt_bias, gdn_o_norm=gdn_o_norm, gdn_w_out=gdn_w_out,
             kv_norm=kv_norm, w_kv=w_kv, swa_w_q=swa_w_q, swa_sinks=swa_sinks, swa_w_o=swa_w_o,
             ffn_w1=ffn_w1, ffn_w3=ffn_w3, ffn_w2=ffn_w2, moe_router=moe_router,
             moe_w1=moe_w1, moe_w3=moe_w3, moe_w2=moe_w2, final_norm=final_norm)
    w = _prep_weights(p)
    Bp, Tp, _ = x_prompt.shape
    Bs, Ts, _ = x_sample.shape
    conv0 = jnp.zeros((Bp, CONV_W - 1, CONV_CH), F32)
    s0 = jnp.zeros((Bp, HV, DK, DV), F32)
    y_p, conv_p, delta_p, k_full, v_full = _trunk(x_prompt, jnp.arange(Tp, dtype=jnp.int32), conv0, s0,
                                                  None, None, p, w)
    win = min(WINDOW, Tp)
    k_p = k_full[:, Tp - win:]
    v_p = v_full[:, Tp - win:]
    pos_s = PAST_LEN + jnp.arange(Ts, dtype=jnp.int32)
    y_s, conv_s, delta_s, k_s, v_s = _trunk(x_sample, pos_s, cache_conv[0], state_delta[0],
                                            cache_k, cache_v, p, w)
    return (y_p, y_s, conv_p, delta_p, k_p, v_p, conv_s, delta_s, k_s, v_s)
```

```python
import functools

import jax
import jax.numpy as jnp
import numpy as np
from jax import lax
from jax.experimental import pallas as pl
from jax.experimental.pallas import tpu as pltpu

F32 = jnp.float32
BF16 = jnp.bfloat16

D_MODEL = 1024
CHUNK = 64
HK = 8
HV = 16
DK = 128
DV = 128
QK_DIM = HK * DK
V_DIM = HV * DV
CONV_CH = 2 * QK_DIM + V_DIM
CONV_W = 4
HQ = 16
HKV = 4
HD = 64
WINDOW = 128
ROT_DIM = HD // 4
ROPE_THETA = 500000.0
D_FF = 2816
N_EXPERTS = 8
TOP_K = 2
EPS = 1e-6
PAST_LEN = 2048

LANES = 128
ROW_TILE = 512
VMEM_LIMIT = 48 * 1024 * 1024


def _params(sem, vmem=VMEM_LIMIT):
    return pltpu.CompilerParams(dimension_semantics=sem, vmem_limit_bytes=vmem)


def _rms_scale(x):
    return lax.rsqrt(jnp.mean(x * x, axis=-1, keepdims=True) + EPS)


def _bdot(a, b):
    return jnp.dot(a.astype(BF16), b.astype(BF16), preferred_element_type=F32)


def _bdot_nt(a, b):
    return lax.dot_general(a.astype(BF16), b.astype(BF16), (((1,), (1,)), ((), ())),
                           preferred_element_type=F32)


def _bdot_tn(a, b):
    return lax.dot_general(a.astype(BF16), b.astype(BF16), (((0,), (0,)), ((), ())),
                           preferred_element_type=F32)


def _split3(x):
    x1 = x.astype(BF16)
    r1 = x - x1.astype(F32)
    x2 = r1.astype(BF16)
    x3 = (r1 - x2.astype(F32)).astype(BF16)
    return x1, x2, x3


def _split2(x):
    x1 = x.astype(BF16)
    x2 = (x - x1.astype(F32)).astype(BF16)
    return x1, x2


def _dot_hi(a, b):
    a1, a2 = _split2(a)
    b1, b2 = _split2(b)
    d = functools.partial(jnp.dot, preferred_element_type=F32)
    return d(a1, b1) + (d(a1, b2) + d(a2, b1))


def _spread(x, emat, pieces):
    parts = _split3(x)[:pieces]
    out = jnp.dot(parts[0], emat, preferred_element_type=F32)
    for part in parts[1:]:
        out = out + jnp.dot(part, emat, preferred_element_type=F32)
    return out


def _head_spread_matrices():
    h = np.arange(HV)[:, None]
    full = (np.arange(HV * DV)[None, :] // DV == h)
    pair = (np.arange(HV * CHUNK)[None, :] // CHUNK == h)
    return jnp.asarray(full.astype(np.float32), dtype=BF16), jnp.asarray(pair.astype(np.float32), dtype=BF16)


def _sigmoid(x):
    return 1.0 / (1.0 + jnp.exp(-x))


def _silu(x):
    return x * _sigmoid(x)


def _qkv_conv_kernel(x_ref, g_ref, w_ref, wc_ref, cs_ref, out_ref, cst_ref,
                     h_scr, pad_scr, carry_scr, *, tm, tn):
    ti = pl.program_id(1)
    j = pl.program_id(2)

    @pl.when(j == 0)
    def _():
        x = x_ref[...]
        h_scr[...] = (x * _rms_scale(x) * g_ref[...]).astype(BF16)

    proj = jnp.dot(h_scr[...], w_ref[...], preferred_element_type=F32)

    @pl.when(ti == 0)
    def _():
        carry_scr[j, 5:8, :] = cs_ref[0]

    pad_scr[0:8, :] = carry_scr[j]
    pad_scr[8:, :] = proj
    wc = wc_ref[...]
    acc = proj * wc[3:4, :]
    for s in range(1, CONV_W):
        acc = acc + pad_scr[8 - s:8 - s + tm, :] * wc[3 - s:4 - s, :]
    carry_scr[j] = pad_scr[tm:tm + 8, :]
    cst_ref[0] = pad_scr[tm + 5:tm + 8, :]
    y = _silu(acc)

    @pl.when(j >= 2)
    def _():
        out_ref[...] = y.astype(out_ref.dtype)

    @pl.when(j < 2)
    def _():
        scale = jnp.where(j == 0, DK ** -0.5, 1.0).astype(F32)
        for hh in range(tn // DK):
            seg = y[:, hh * DK:(hh + 1) * DK]
            inv = lax.rsqrt(jnp.sum(seg * seg, axis=-1, keepdims=True) + EPS) * scale
            out_ref[:, hh * DK:(hh + 1) * DK] = (seg * inv).astype(out_ref.dtype)


def _qkv_conv(x, g, w_qkv, w_conv, conv_state, B, T):
    R = B * T
    tm = min(ROW_TILE, T)
    tn = QK_DIM
    nt = T // tm
    nj = CONV_CH // tn
    kern = functools.partial(_qkv_conv_kernel, tm=tm, tn=tn)
    return pl.pallas_call(
        kern,
        grid=(B, nt, nj),
        in_specs=[
            pl.BlockSpec((tm, D_MODEL), lambda b, t, j: (b * nt + t, 0)),
            pl.BlockSpec((1, D_MODEL), lambda b, t, j: (0, 0)),
            pl.BlockSpec((D_MODEL, tn), lambda b, t, j: (0, j)),
            pl.BlockSpec((CONV_W, tn), lambda b, t, j: (0, j)),
            pl.BlockSpec((1, CONV_W - 1, tn), lambda b, t, j: (b, 0, j)),
        ],
        out_specs=[
            pl.BlockSpec((tm, tn), lambda b, t, j: (b * nt + t, j)),
            pl.BlockSpec((1, CONV_W - 1, tn), lambda b, t, j: (b * nt + t, 0, j)),
        ],
        out_shape=[
            jax.ShapeDtypeStruct((R, CONV_CH), BF16),
            jax.ShapeDtypeStruct((B * nt, CONV_W - 1, CONV_CH), F32),
        ],
        scratch_shapes=[
            pltpu.VMEM((tm, D_MODEL), BF16),
            pltpu.VMEM((tm + 8, tn), F32),
            pltpu.VMEM((nj, 8, tn), F32),
        ],
        compiler_params=_params(("arbitrary", "arbitrary", "arbitrary")),
        name="gdn_qkv_conv",
    )(x, g, w_qkv, w_conv, conv_state)


def _softplus(x):
    return jnp.maximum(x, 0.0) + jnp.log(1.0 + jnp.exp(-jnp.abs(x)))


def _zbg_kernel(x_ref, g_ref, wz_ref, wba_ref, wbat_ref, alog_ref, dtb_ref, alogt_ref, dtbt_ref, tri_ref,
                z_ref, beta_ref, gc_ref, gct_ref, *, tm):
    x = x_ref[...]
    h = (x * _rms_scale(x) * g_ref[...]).astype(BF16)
    z_ref[...] = jnp.dot(h, wz_ref[...], preferred_element_type=F32).astype(z_ref.dtype)
    ba = jnp.dot(h, wba_ref[...], preferred_element_type=F32)
    beta_ref[...] = _sigmoid(ba[:, :HV])
    gcol = -jnp.exp(alog_ref[...]) * _softplus(ba[:, HV:] + dtb_ref[...])
    tri = tri_ref[...]
    g1, g2, g3 = _split3(gcol)
    d = functools.partial(jnp.dot, preferred_element_type=F32)
    gc_ref[...] = d(tri, g1) + (d(tri, g2) + d(tri, g3))
    at = lax.dot_general(wbat_ref[...], h, (((1,), (1,)), ((), ())), preferred_element_type=F32)
    grow = -jnp.exp(alogt_ref[...]) * _softplus(at + dtbt_ref[...])
    r1, r2, r3 = _split3(grow)
    dn = lambda a: lax.dot_general(a, tri, (((1,), (1,)), ((), ())), preferred_element_type=F32)
    gct = dn(r1) + (dn(r2) + dn(r3))
    for c in range(tm // CHUNK):
        sl = slice(c * CHUNK, (c + 1) * CHUNK)
        gct_ref[0, c] = jnp.concatenate([gct[:HV // 2, sl], gct[HV // 2:, sl]], axis=1)


def _zbg(x, g, w_z, w_ba, a_log, dt_bias, B, T):
    R = B * T
    tm = min(ROW_TILE, T)
    nt = T // tm
    nc = T // CHUNK
    idx = np.arange(tm)
    tri = jnp.asarray(((idx[:, None] // CHUNK == idx[None, :] // CHUNK) & (idx[:, None] >= idx[None, :]))
                      .astype(np.float32), dtype=BF16)
    pair_order = np.concatenate([np.arange(0, HV, 2), np.arange(1, HV, 2)])
    w_a_t = jnp.transpose(w_ba[:, HV:])[pair_order]
    full = lambda shape: pl.BlockSpec(shape, lambda b, t: (0,) * len(shape))
    kern = functools.partial(_zbg_kernel, tm=tm)
    return pl.pallas_call(
        kern,
        grid=(B, nt),
        in_specs=[
            pl.BlockSpec((tm, D_MODEL), lambda b, t: (b * nt + t, 0)),
            full((1, D_MODEL)),
            full((D_MODEL, V_DIM)),
            full((D_MODEL, 2 * HV)),
            full((HV, D_MODEL)),
            full((1, HV)), full((1, HV)), full((HV, 1)), full((HV, 1)),
            full((tm, tm)),
        ],
        out_specs=[
            pl.BlockSpec((tm, V_DIM), lambda b, t: (b * nt + t, 0)),
            pl.BlockSpec((tm, HV), lambda b, t: (b * nt + t, 0)),
            pl.BlockSpec((tm, HV), lambda b, t: (b * nt + t, 0)),
            pl.BlockSpec((1, tm // CHUNK, HV // 2, 2 * CHUNK), lambda b, t: (b, t, 0, 0)),
        ],
        out_shape=[
            jax.ShapeDtypeStruct((R, V_DIM), BF16),
            jax.ShapeDtypeStruct((R, HV), F32),
            jax.ShapeDtypeStruct((R, HV), F32),
            jax.ShapeDtypeStruct((B, nc, HV // 2, 2 * CHUNK), F32),
        ],
        compiler_params=_params(("arbitrary", "arbitrary")),
        name="gdn_z_beta_decay",
    )(x, g, w_z, w_ba, w_a_t, a_log.reshape(1, HV), dt_bias.reshape(1, HV),
      a_log[pair_order].reshape(HV, 1), dt_bias[pair_order].reshape(HV, 1), tri)


PAIR = HV // HK


def _pair_rows(x, lo):
    zero = jnp.zeros((), x.dtype)
    return jnp.concatenate([jnp.where(lo, x, zero), jnp.where(lo, zero, x)], axis=0)


def _pair_products(lhs_list, rhs_list, lo):
    def split(a):
        hi = a.astype(BF16)
        return hi, (a - hi.astype(F32)).astype(BF16)

    ops = []
    for lhs, rhs in zip(lhs_list, rhs_list):
        lh, ll = split(lhs)
        rh, rl = split(rhs)
        ops.append((jnp.concatenate([lh, ll], axis=0),
                    jnp.concatenate([_pair_rows(rh, lo), _pair_rows(rl, lo)], axis=1)))
    quads = [jnp.dot(a, b, preferred_element_type=F32) for a, b in ops]
    out = []
    for q in quads:
        m, n = q.shape[0] // 2, q.shape[1] // 2
        out.append(q[:m, :n] + (q[:m, n:] + q[m:, :n]))
    return out


def _pair_unit_lower_inverse(low_list, lo, eye2):
    C = low_list[0].shape[0]
    xs = [-low for low in low_list]
    accs = [eye2 + x for x in xs]
    pws = _pair_products(xs, xs, lo)
    n = 2
    while 2 * n < C:
        prods = _pair_products([jnp.concatenate([a, p], axis=0) for a, p in zip(accs, pws)], pws, lo)
        accs = [a + pr[:C] for a, pr in zip(accs, prods)]
        pws = [pr[C:] for pr in prods]
        n *= 2
    prods = _pair_products(accs, pws, lo)
    return [a + pr for a, pr in zip(accs, prods)]


def _delta_prep_kernel(q_ref, k_ref, v_ref, beta_ref, gc_ref, gct_ref, epair_ref, u_ref, w_ref, a_ref, *, cb):
    C = CHUNK
    ri = lax.broadcasted_iota(jnp.int32, (C, 2 * C), 0)
    li = lax.broadcasted_iota(jnp.int32, (C, 2 * C), 1)
    lm = li & (C - 1)
    lo = li < C
    incl = ri >= lm
    strict = ri > lm
    eye2 = (ri == lm).astype(F32)
    units = [(c, p) for c in range(cb) for p in range(HK)]
    rows = lambda c: slice(c * C, (c + 1) * C)

    grams = []
    for c, p in units:
        kb = k_ref[rows(c), p * DK:(p + 1) * DK]
        qb = q_ref[rows(c), p * DK:(p + 1) * DK]
        grams.append(lax.dot_general(jnp.concatenate([kb, qb], axis=0), jnp.concatenate([kb, kb], axis=0),
                                     (((1,), (1,)), ((), ())), preferred_element_type=F32))

    epair = epair_ref[...]
    g2s, b2s, betas, begs = [], [], [], []
    for c in range(cb):
        beta = beta_ref[rows(c), :]
        gc = gc_ref[rows(c), :]
        g2s.append(_spread(gc, epair, 3))
        b2s.append(_spread(beta, epair, 2))
        betas.append(beta)
        begs.append(beta * jnp.exp(gc))

    lows = []
    for (c, p), gram in zip(units, grams):
        pt = slice(p * 2 * C, (p + 1) * 2 * C)
        grow2 = gct_ref[c, p:p + 1, :]
        decay2 = jnp.exp(jnp.where(incl, g2s[c][:, pt] - grow2, -jnp.inf))
        a_ref[rows(c), pt] = (gram[C:] * decay2).astype(a_ref.dtype)
        lows.append(jnp.where(strict, b2s[c][:, pt] * gram[:C] * decay2, 0.0))

    tinvs = _pair_unit_lower_inverse(lows, lo, eye2)

    ops = []
    for (c, p), tinv in zip(units, tinvs):
        kf = k_ref[rows(c), p * DK:(p + 1) * DK].astype(F32)
        rhs = []
        for h in (PAIR * p, PAIR * p + 1):
            ht = slice(h * DV, (h + 1) * DV)
            vf = v_ref[rows(c), ht].astype(F32)
            rhs.append(jnp.concatenate([vf * betas[c][:, h:h + 1], kf * begs[c][:, h:h + 1]], axis=1))
        ops.append((_pair_rows(tinv.astype(BF16), lo), jnp.concatenate(rhs, axis=0).astype(BF16)))
    for (c, p), (lhs, rhs) in zip(units, ops):
        uw = jnp.dot(lhs, rhs, preferred_element_type=F32)
        for i, h in enumerate((PAIR * p, PAIR * p + 1)):
            u_ref[rows(c), h * DV:(h + 1) * DV] = uw[i * C:(i + 1) * C, :DV].astype(u_ref.dtype)
            w_ref[rows(c), h * DV:(h + 1) * DV] = uw[i * C:(i + 1) * C, DV:].astype(w_ref.dtype)


def _delta_prep(qkv, beta, gc, gct, B, T):
    R = B * T
    cb = 2 if T % (2 * CHUNK) == 0 else 1
    rb = cb * CHUNK
    gct2 = gct.reshape(R // CHUNK, HK, 2 * CHUNK)
    kern = functools.partial(_delta_prep_kernel, cb=cb)
    return pl.pallas_call(
        kern,
        grid=(R // rb,),
        in_specs=[
            pl.BlockSpec((rb, QK_DIM), lambda i: (i, 0)),
            pl.BlockSpec((rb, QK_DIM), lambda i: (i, 1)),
            pl.BlockSpec((rb, V_DIM), lambda i: (i, 1)),
            pl.BlockSpec((rb, HV), lambda i: (i, 0)),
            pl.BlockSpec((rb, HV), lambda i: (i, 0)),
            pl.BlockSpec((cb, HK, 2 * CHUNK), lambda i: (i, 0, 0)),
            pl.BlockSpec((HV, HV * CHUNK), lambda i: (0, 0)),
        ],
        out_specs=[
            pl.BlockSpec((rb, V_DIM), lambda i: (i, 0)),
            pl.BlockSpec((rb, V_DIM), lambda i: (i, 0)),
            pl.BlockSpec((rb, HV * CHUNK), lambda i: (i, 0)),
        ],
        out_shape=[
            jax.ShapeDtypeStruct((R, V_DIM), BF16),
            jax.ShapeDtypeStruct((R, V_DIM), BF16),
            jax.ShapeDtypeStruct((R, HV * CHUNK), BF16),
        ],
        compiler_params=_params(("arbitrary",)),
        name="gdn_delta_prep",
    )(qkv, qkv, qkv, beta, gc, gct2, _head_spread_matrices()[1])


def _delta_scan_kernel(q_ref, k_ref, u_ref, w_ref, a_ref, z_ref, gc_ref, s0_ref, on_ref, efull_ref,
                       og_ref, s_ref, *, bb):
    c = pl.program_id(1)

    @pl.when(c == 0)
    def _():
        s_ref[...] = s0_ref[...]

    C = CHUNK
    lo = lax.broadcasted_iota(jnp.int32, (C, 2 * C), 1) < C
    onorm = on_ref[...]
    units = [(b, h) for b in range(bb) for h in range(HV)]
    pairs = [(b, p) for b in range(bb) for p in range(HK)]

    efull = efull_ref[...]
    egs, kgs, dgs = [], [], []
    for b in range(bb):
        gc = gc_ref[b]
        g_last = gc[C - 1:C, :]
        egs.append(_spread(jnp.exp(gc), efull, 2))
        kgs.append(_spread(jnp.exp(g_last - gc), efull, 2))
        dgs.append(_spread(jnp.exp(gc[C - 8:C, :]), efull, 3)[7:8, :])

    ops = []
    kds = []
    dec = []
    for b, h in units:
        p = h // PAIR
        ht = slice(h * DV, (h + 1) * DV)
        qg = (q_ref[b, :, p * DK:(p + 1) * DK].astype(F32) * egs[b][:, ht]).astype(BF16)
        kds.append((k_ref[b, :, p * DK:(p + 1) * DK].astype(F32) * kgs[b][:, ht]).astype(BF16))
        dec.append(dgs[b][:, ht])
        ops.append((jnp.concatenate([w_ref[b, :, ht], qg], axis=0), s_ref[b, h].astype(BF16)))
    wqs = [jnp.dot(a, s, preferred_element_type=F32) for a, s in ops]

    vns = [(u_ref[b, :, h * DV:(h + 1) * DV].astype(F32) - wq[:C]).astype(BF16) for (b, h), wq in zip(units, wqs)]

    oparts = []
    for i, (b, p) in enumerate(pairs):
        att = _pair_rows(a_ref[b, :, p * 2 * C:(p + 1) * 2 * C], lo)
        vn2 = jnp.concatenate([vns[PAIR * i], vns[PAIR * i + 1]], axis=0)
        oparts.append(jnp.dot(att, vn2, preferred_element_type=F32))
    sds = [lax.dot_general(kd, vn, (((0,), (0,)), ((), ())), preferred_element_type=F32)
           for kd, vn in zip(kds, vns)]

    for i, (b, h) in enumerate(units):
        s_ref[b, h] = s_ref[b, h] * dec[i] + sds[i]
        o = wqs[i][C:] + oparts[i // PAIR][(h % PAIR) * C:(h % PAIR + 1) * C]
        on = o * _rms_scale(o) * onorm
        zf = z_ref[b, :, h * DV:(h + 1) * DV].astype(F32)
        og_ref[b, :, h * DV:(h + 1) * DV] = (on * _silu(zf)).astype(og_ref.dtype)


def _delta_scan(qkv, u, w, att, z, gc, s0, o_norm, B, T):
    nc = T // CHUNK
    bb = 2 if B % 2 == 0 else 1
    v3 = lambda a: a.reshape(B, T, a.shape[-1])
    blk = lambda n, j=0: pl.BlockSpec((bb, CHUNK, n), lambda b, c: (b, c, j))
    kern = functools.partial(_delta_scan_kernel, bb=bb)
    og, s_new = pl.pallas_call(
        kern,
        grid=(B // bb, nc),
        in_specs=[
            blk(QK_DIM, 0), blk(QK_DIM, 1), blk(V_DIM), blk(V_DIM), blk(HV * CHUNK), blk(V_DIM), blk(HV),
            pl.BlockSpec((bb, HV, DK, DV), lambda b, c: (b, 0, 0, 0)),
            pl.BlockSpec((1, DV), lambda b, c: (0, 0)),
            pl.BlockSpec((HV, HV * DV), lambda b, c: (0, 0)),
        ],
        out_specs=[
            blk(V_DIM),
            pl.BlockSpec((bb, HV, DK, DV), lambda b, c: (b, 0, 0, 0)),
        ],
        out_shape=[
            jax.ShapeDtypeStruct((B, T, V_DIM), BF16),
            jax.ShapeDtypeStruct((B, HV, DK, DV), F32),
        ],
        compiler_params=_params(("arbitrary", "arbitrary")),
        name="gdn_delta_scan",
    )(v3(qkv), v3(qkv), v3(u), v3(w), v3(att), v3(z), v3(gc), s0, o_norm, _head_spread_matrices()[0])
    return og.reshape(B * T, V_DIM), s_new


def _proj_res_kernel(x_ref, a_ref, w_ref, o_ref):
    o_ref[...] = x_ref[...] + jnp.dot(a_ref[...], w_ref[...], preferred_element_type=F32)


def _proj_res(x, a, w):
    R = x.shape[0]
    K = a.shape[1]
    tm = min(ROW_TILE, R)
    return pl.pallas_call(
        _proj_res_kernel,
        grid=(R // tm,),
        in_specs=[
            pl.BlockSpec((tm, D_MODEL), lambda i: (i, 0)),
            pl.BlockSpec((tm, K), lambda i: (i, 0)),
            pl.BlockSpec((K, D_MODEL), lambda i: (0, 0)),
        ],
        out_specs=pl.BlockSpec((tm, D_MODEL), lambda i: (i, 0)),
        out_shape=jax.ShapeDtypeStruct((R, D_MODEL), F32),
        compiler_params=_params(("arbitrary",)),
        name="proj_residual",
    )(x, a, w)


FF_TILE = 1408


def _dense_ffn_kernel(x_ref, g_ref, w1_ref, w3_ref, w2_ref, o_ref, h_scr, acc_scr):
    f = pl.program_id(1)

    @pl.when(f == 0)
    def _():
        x = x_ref[...]
        h_scr[...] = (x * _rms_scale(x) * g_ref[...]).astype(BF16)
        acc_scr[...] = x

    h = h_scr[...]
    a = jnp.dot(h, w1_ref[...], preferred_element_type=F32)
    b = jnp.dot(h, w3_ref[...], preferred_element_type=F32)
    mid = (_silu(a) * b).astype(BF16)
    acc_scr[...] += jnp.dot(mid, w2_ref[...], preferred_element_type=F32)

    @pl.when(f == pl.num_programs(1) - 1)
    def _():
        o_ref[...] = acc_scr[...]


def _dense_ffn(x, g, w1, w3, w2):
    R = x.shape[0]
    tm = min(ROW_TILE, R)
    nf = D_FF // FF_TILE
    return pl.pallas_call(
        _dense_ffn_kernel,
        grid=(R // tm, nf),
        in_specs=[
            pl.BlockSpec((tm, D_MODEL), lambda i, f: (i, 0)),
            pl.BlockSpec((1, D_MODEL), lambda i, f: (0, 0)),
            pl.BlockSpec((D_MODEL, FF_TILE), lambda i, f: (0, f)),
            pl.BlockSpec((D_MODEL, FF_TILE), lambda i, f: (0, f)),
            pl.BlockSpec((FF_TILE, D_MODEL), lambda i, f: (f, 0)),
        ],
        out_specs=pl.BlockSpec((tm, D_MODEL), lambda i, f: (i, 0)),
        out_shape=jax.ShapeDtypeStruct((R, D_MODEL), F32),
        scratch_shapes=[pltpu.VMEM((tm, D_MODEL), BF16), pltpu.VMEM((tm, D_MODEL), F32)],
        compiler_params=_params(("arbitrary", "arbitrary")),
        name="dense_swiglu",
    )(x, g, w1, w3, w2)


KV_DUP = 2 * HKV * LANES
KV_STD = 2 * HKV * HD


def _rope(x, cos, sm, sp):
    up = pltpu.roll(x, LANES - ROT_DIM // 2, 1)
    dn = pltpu.roll(x, ROT_DIM // 2, 1)
    return x * cos + up * sm + dn * sp


def _qkv_rope_kernel(x_ref, gq_ref, gkv_ref, wq_ref, wkv_ref, cos_ref, sm_ref, sp_ref,
                     q_ref, kvd_ref, kvs_ref):
    x = x_ref[...]
    xr = x * _rms_scale(x)
    hq = (xr * gq_ref[...]).astype(BF16)
    hkv = (xr * gkv_ref[...]).astype(BF16)
    cos = cos_ref[...]
    sm = sm_ref[...]
    sp = sp_ref[...]
    q = jnp.dot(hq, wq_ref[...], preferred_element_type=F32)
    for t in range(HQ * HD // LANES):
        q_ref[:, t * LANES:(t + 1) * LANES] = _rope(q[:, t * LANES:(t + 1) * LANES], cos, sm, sp).astype(q_ref.dtype)
    kv = jnp.dot(hkv, wkv_ref[...], preferred_element_type=F32)
    half = KV_DUP // 2
    for t in range(HKV):
        sl = slice(t * LANES, (t + 1) * LANES)
        kvd_ref[:, sl] = _rope(kv[:, sl], cos, sm, sp).astype(kvd_ref.dtype)
    kvd_ref[:, half:] = kv[:, half:KV_DUP].astype(kvd_ref.dtype)
    for t in range(HKV * HD // LANES):
        sl = slice(KV_DUP + t * LANES, KV_DUP + (t + 1) * LANES)
        kvs_ref[:, t * LANES:(t + 1) * LANES] = _rope(kv[:, sl], cos, sm, sp)
    kvs_ref[:, HKV * HD:] = kv[:, KV_DUP + HKV * HD:]


def _rope_tables(pos):
    half = ROT_DIM // 2
    inv = jnp.power(ROPE_THETA, -jnp.arange(half, dtype=F32) * 2.0 / ROT_DIM)
    ang = pos.astype(F32)[:, None] * inv[None, :]
    cos = jnp.cos(ang)
    sin = jnp.sin(ang)
    T = pos.shape[0]
    one = jnp.ones((T, HD - ROT_DIM), F32)
    zero = jnp.zeros((T, HD - ROT_DIM), F32)
    zh = jnp.zeros((T, half), F32)
    c64 = jnp.concatenate([cos, cos, one], axis=1)
    sm64 = jnp.concatenate([-sin, zh, zero], axis=1)
    sp64 = jnp.concatenate([zh, sin, zero], axis=1)
    dup = lambda a: jnp.concatenate([a, a], axis=1)
    return dup(c64), dup(sm64), dup(sp64)


def _qkv_rope(x, gq, gkv, w_q, w_kvx, tables, B, T):
    R = B * T
    tm = min(ROW_TILE, T)
    nt = T // tm
    nkv = KV_DUP + KV_STD
    tab = pl.BlockSpec((tm, LANES), lambda i: (i % nt, 0))
    return pl.pallas_call(
        _qkv_rope_kernel,
        grid=(R // tm,),
        in_specs=[
            pl.BlockSpec((tm, D_MODEL), lambda i: (i, 0)),
            pl.BlockSpec((1, D_MODEL), lambda i: (0, 0)),
            pl.BlockSpec((1, D_MODEL), lambda i: (0, 0)),
            pl.BlockSpec((D_MODEL, HQ * HD), lambda i: (0, 0)),
            pl.BlockSpec((D_MODEL, nkv), lambda i: (0, 0)),
            tab, tab, tab,
        ],
        out_specs=[
            pl.BlockSpec((tm, HQ * HD), lambda i: (i, 0)),
            pl.BlockSpec((tm, KV_DUP), lambda i: (i, 0)),
            pl.BlockSpec((tm, KV_STD), lambda i: (i, 0)),
        ],
        out_shape=[
            jax.ShapeDtypeStruct((R, HQ * HD), BF16),
            jax.ShapeDtypeStruct((R, KV_DUP), BF16),
            jax.ShapeDtypeStruct((R, KV_STD), F32),
        ],
        compiler_params=_params(("arbitrary",)),
        name="swa_qkv_rope",
    )(x, gq, gkv, w_q, w_kvx, *tables)


KEY_CHUNKS = WINDOW // CHUNK + 1
KEY_PAD = 256


def _attn_kernel(q_ref, k0_ref, k1_ref, k2_ref, sink_ref, o_ref, *, off):
    c = pl.program_id(1)
    krefs = (k0_ref, k1_ref, k2_ref)
    lane = lax.broadcasted_iota(jnp.int32, (KEY_CHUNKS * CHUNK, LANES), 1)
    lo = lane < HD
    col = lax.broadcasted_iota(jnp.int32, (CHUNK, KEY_PAD), 1)
    first = c + off - (KEY_CHUNKS - 1)
    valid = (col < KEY_CHUNKS * CHUNK) & (col // CHUNK + first >= 0)
    zpad = jnp.zeros((KEY_PAD - KEY_CHUNKS * CHUNK, LANES), BF16)
    zero = jnp.zeros((), BF16)
    tiles = HQ // HKV // 2

    vblks = []
    scores = []
    for g in range(HKV):
        kg = jnp.concatenate([r[:, g * LANES:(g + 1) * LANES] for r in krefs], axis=0)
        vg = jnp.concatenate([r[:, KV_DUP // 2 + g * LANES:KV_DUP // 2 + (g + 1) * LANES] for r in krefs], axis=0)
        kblk = jnp.concatenate([jnp.where(lo, kg, zero), zpad, jnp.where(lo, zero, kg), zpad], axis=0)
        vblks.append(jnp.concatenate([jnp.where(lo, vg, zero), zpad, jnp.where(lo, zero, vg), zpad], axis=0))
        qg = jnp.concatenate([q_ref[:, (g * tiles + p) * LANES:(g * tiles + p + 1) * LANES] for p in range(tiles)],
                             axis=0)
        scores.append(lax.dot_general(qg, kblk, (((1,), (1,)), ((), ())), preferred_element_type=F32))

    probs = []
    for g in range(HKV):
        rows = []
        for p in range(tiles):
            t = g * tiles + p
            ps = []
            for hh in range(2):
                s = scores[g][p * CHUNK:(p + 1) * CHUNK, hh * KEY_PAD:(hh + 1) * KEY_PAD] * (HD ** -0.5)
                sh = jnp.where(valid, s, -jnp.inf)
                sink = sink_ref[:, 2 * t + hh:2 * t + hh + 1]
                m = jnp.maximum(jnp.max(sh, axis=-1, keepdims=True), sink)
                e = jnp.exp(sh - m)
                den = jnp.sum(e, axis=-1, keepdims=True) + jnp.exp(sink - m)
                ps.append((e / den).astype(BF16))
            rows.append(jnp.concatenate(ps, axis=1))
        probs.append(jnp.concatenate(rows, axis=0))

    outs = [jnp.dot(pr, vb, preferred_element_type=F32) for pr, vb in zip(probs, vblks)]
    for g in range(HKV):
        for p in range(tiles):
            t = g * tiles + p
            o_ref[:, t * LANES:(t + 1) * LANES] = outs[g][p * CHUNK:(p + 1) * CHUNK].astype(o_ref.dtype)


def _attention(q, kvd, sinks, B, Tq, Tk):
    ncq = Tq // CHUNK
    nck = Tk // CHUNK
    off = nck - ncq
    kern = functools.partial(_attn_kernel, off=off)

    def kspec(i):
        return pl.BlockSpec((CHUNK, KV_DUP),
                            lambda b, c: (b * nck + jnp.maximum(c + off - (KEY_CHUNKS - 1) + i, 0), 0))

    return pl.pallas_call(
        kern,
        grid=(B, ncq),
        in_specs=[
            pl.BlockSpec((CHUNK, HQ * HD), lambda b, c: (b * ncq + c, 0)),
            kspec(0), kspec(1), kspec(2),
            pl.BlockSpec((1, HQ), lambda b, c: (0, 0)),
        ],
        out_specs=pl.BlockSpec((CHUNK, HQ * HD), lambda b, c: (b * ncq + c, 0)),
        out_shape=jax.ShapeDtypeStruct((B * Tq, HQ * HD), BF16),
        compiler_params=_params(("arbitrary", "arbitrary")),
        name="swa_attention",
    )(q, kvd, kvd, kvd, sinks)


def _oproj_router_kernel(x_ref, a_ref, w_ref, g_ref, wr_ref, x3_ref, h_ref, e_ref, gate_ref, cnt_ref):
    i = pl.program_id(0)
    x3 = x_ref[...] + jnp.dot(a_ref[...], w_ref[...], preferred_element_type=F32)
    x3_ref[...] = x3
    h = x3 * _rms_scale(x3) * g_ref[...]
    h_ref[...] = h
    logits = _dot_hi(h, wr_ref[...])
    eidx = lax.broadcasted_iota(jnp.int32, logits.shape, 1)
    m1 = jnp.max(logits, axis=-1, keepdims=True)
    i1 = jnp.min(jnp.where(logits == m1, eidx, N_EXPERTS), axis=-1, keepdims=True)
    rest = jnp.where(eidx == i1, -jnp.inf, logits)
    m2 = jnp.max(rest, axis=-1, keepdims=True)
    i2 = jnp.min(jnp.where(rest == m2, eidx, N_EXPERTS), axis=-1, keepdims=True)
    e2 = jnp.exp(m2 - m1)
    den = 1.0 + e2
    e_ref[...] = jnp.concatenate([i1, i2], axis=1)
    gate_ref[...] = jnp.concatenate([1.0 / den, e2 / den], axis=1)
    hot = ((eidx == i1) | (eidx == i2)).astype(F32)

    @pl.when(i == 0)
    def _():
        cnt_ref[...] = jnp.zeros_like(cnt_ref)

    cnt_ref[...] += jnp.sum(hot, axis=0, keepdims=True)


def _oproj_router(x, a, w_o, g, w_r):
    R = x.shape[0]
    tm = min(ROW_TILE, R)
    row = lambda n: pl.BlockSpec((tm, n), lambda i: (i, 0))
    return pl.pallas_call(
        _oproj_router_kernel,
        grid=(R // tm,),
        in_specs=[
            row(D_MODEL), row(HQ * HD),
            pl.BlockSpec((HQ * HD, D_MODEL), lambda i: (0, 0)),
            pl.BlockSpec((1, D_MODEL), lambda i: (0, 0)),
            pl.BlockSpec((D_MODEL, N_EXPERTS), lambda i: (0, 0)),
        ],
        out_specs=[row(D_MODEL), row(D_MODEL), row(TOP_K), row(TOP_K),
                   pl.BlockSpec((1, N_EXPERTS), lambda i: (0, 0))],
        out_shape=[
            jax.ShapeDtypeStruct((R, D_MODEL), F32),
            jax.ShapeDtypeStruct((R, D_MODEL), F32),
            jax.ShapeDtypeStruct((R, TOP_K), jnp.int32),
            jax.ShapeDtypeStruct((R, TOP_K), F32),
            jax.ShapeDtypeStruct((1, N_EXPERTS), F32),
        ],
        compiler_params=_params(("arbitrary",)),
        name="swa_oproj_router",
    )(x, a, w_o, g, w_r)


def _slot_kernel(e_ref, base_ref, tri_ref, pos_ref, run_scr):
    i = pl.program_id(0)

    @pl.when(i == 0)
    def _():
        run_scr[...] = jnp.zeros_like(run_scr)

    e = e_ref[...]
    tm = e.shape[0]
    eidx = lax.broadcasted_iota(jnp.int32, (tm, N_EXPERTS), 1)
    hot0 = eidx == e[:, 0:1]
    hot1 = eidx == e[:, 1:2]
    hot = (hot0 | hot1).astype(BF16)
    before = jnp.dot(tri_ref[...], hot, preferred_element_type=F32)
    dest = before + run_scr[...] + base_ref[...]
    p0 = jnp.sum(jnp.where(hot0, dest, 0.0), axis=-1, keepdims=True)
    p1 = jnp.sum(jnp.where(hot1, dest, 0.0), axis=-1, keepdims=True)
    pos_ref[...] = jnp.concatenate([p0, p1], axis=1).astype(jnp.int32)
    run_scr[...] += jnp.sum(hot.astype(F32), axis=0, keepdims=True)


def _slots(eidx, base):
    R = eidx.shape[0]
    tm = min(ROW_TILE, R)
    idx = np.arange(tm)
    tri = jnp.asarray((idx[:, None] > idx[None, :]).astype(np.float32), dtype=BF16)
    return pl.pallas_call(
        _slot_kernel,
        grid=(R // tm,),
        in_specs=[
            pl.BlockSpec((tm, TOP_K), lambda i: (i, 0)),
            pl.BlockSpec((1, N_EXPERTS), lambda i: (0, 0)),
            pl.BlockSpec((tm, tm), lambda i: (0, 0)),
        ],
        out_specs=pl.BlockSpec((tm, TOP_K), lambda i: (i, 0)),
        out_shape=jax.ShapeDtypeStruct((R, TOP_K), jnp.int32),
        scratch_shapes=[pltpu.VMEM((1, N_EXPERTS), F32)],
        compiler_params=_params(("arbitrary",)),
        name="moe_slots",
    )(eidx, base, tri)


def _row_copy(src, i, dst, j, sem):
    return pltpu.make_async_copy(src.at[pl.ds(i, 1)], dst.at[pl.ds(j, 1)], sem)


def _dispatch_kernel(pos_ref, cnt_ref, base_ref, nused_ref, h_ref, xs_ref, zero_scr, sem, zsem, *, tm, te, n_tiles):
    i = pl.program_id(0)

    @pl.when(i == 0)
    def _():
        zero_scr[...] = jnp.zeros_like(zero_scr)

        def tile_copy(t):
            dst = xs_ref.at[pl.ds(pl.multiple_of(t * te, te), te)]
            return pltpu.make_async_copy(zero_scr, dst, zsem)

        def issue_tile(t, carry):
            tile_copy(t).start()
            return carry

        def drain_tile(t, carry):
            tile_copy(t).wait()
            return carry

        lax.fori_loop(nused_ref[0], n_tiles, issue_tile, 0)
        lax.fori_loop(nused_ref[0], n_tiles, drain_tile, 0)
        for e in range(N_EXPERTS):
            n = cnt_ref[e]
            start = base_ref[e] + n
            npad = (te - n % te) % te

            def issue(r, carry):
                _row_copy(zero_scr, 0, xs_ref, start + r, zsem).start()
                return carry

            def drain(r, carry):
                _row_copy(zero_scr, 0, xs_ref, start + r, zsem).wait()
                return carry

            lax.fori_loop(0, npad, issue, 0)
            lax.fori_loop(0, npad, drain, 0)

    def issue(r, carry):
        _row_copy(h_ref, r, xs_ref, pos_ref[2 * r], sem).start()
        _row_copy(h_ref, r, xs_ref, pos_ref[2 * r + 1], sem).start()
        return carry

    def drain(r, carry):
        _row_copy(h_ref, r, xs_ref, pos_ref[2 * r], sem).wait()
        _row_copy(h_ref, r, xs_ref, pos_ref[2 * r + 1], sem).wait()
        return carry

    lax.fori_loop(0, tm, issue, 0)
    lax.fori_loop(0, tm, drain, 0)


def _dispatch(h, pos_flat, counts, base, n_used, n_tiles, te):
    R = h.shape[0]
    tm = min(ROW_TILE, R)
    kern = functools.partial(_dispatch_kernel, tm=tm, te=te, n_tiles=n_tiles)
    smem = pl.BlockSpec(memory_space=pltpu.SMEM)
    return pl.pallas_call(
        kern,
        grid=(R // tm,),
        in_specs=[
            pl.BlockSpec((TOP_K * tm,), lambda i: (i,), memory_space=pltpu.SMEM),
            smem, smem, smem,
            pl.BlockSpec((tm, D_MODEL), lambda i: (i, 0)),
        ],
        out_specs=pl.BlockSpec(memory_space=pl.ANY),
        out_shape=jax.ShapeDtypeStruct((n_tiles * te, D_MODEL), F32),
        scratch_shapes=[pltpu.VMEM((te, D_MODEL), F32), pltpu.SemaphoreType.DMA(()), pltpu.SemaphoreType.DMA(())],
        compiler_params=pltpu.CompilerParams(dimension_semantics=("arbitrary",), vmem_limit_bytes=VMEM_LIMIT,
                                             has_side_effects=True),
        name="moe_dispatch",
    )(pos_flat, counts, base, n_used, h)


def _expert_ffn_kernel(te_ref, nused_ref, x_ref, w1_ref, w3_ref, w2_ref, y_ref, acc_scr):
    i = pl.program_id(0)
    f = pl.program_id(1)

    @pl.when(i < nused_ref[0])
    def _():
        h = x_ref[...].astype(BF16)
        a = jnp.dot(h, w1_ref[0], preferred_element_type=F32)
        b = jnp.dot(h, w3_ref[0], preferred_element_type=F32)
        mid = (_silu(a) * b).astype(BF16)
        part = jnp.dot(mid, w2_ref[0], preferred_element_type=F32)

        @pl.when(f == 0)
        def _():
            acc_scr[...] = part

        @pl.when(f > 0)
        def _():
            acc_scr[...] += part

        @pl.when(f == pl.num_programs(1) - 1)
        def _():
            y_ref[...] = acc_scr[...]

    @pl.when((i >= nused_ref[0]) & (f == pl.num_programs(1) - 1))
    def _():
        y_ref[...] = jnp.zeros_like(y_ref)


def _expert_ffn(xs, tile_expert, n_used, w1, w3, w2, te):
    n_rows = xs.shape[0]
    nt = n_rows // te
    nf = D_FF // FF_TILE
    grid_spec = pltpu.PrefetchScalarGridSpec(
        num_scalar_prefetch=2,
        grid=(nt, nf),
        in_specs=[
            pl.BlockSpec((te, D_MODEL), lambda i, f, te_r, nu_r: (i, 0)),
            pl.BlockSpec((1, D_MODEL, FF_TILE), lambda i, f, te_r, nu_r: (te_r[i], 0, f)),
            pl.BlockSpec((1, D_MODEL, FF_TILE), lambda i, f, te_r, nu_r: (te_r[i], 0, f)),
            pl.BlockSpec((1, FF_TILE, D_MODEL), lambda i, f, te_r, nu_r: (te_r[i], f, 0)),
        ],
        out_specs=pl.BlockSpec((te, D_MODEL), lambda i, f, te_r, nu_r: (i, 0)),
        scratch_shapes=[pltpu.VMEM((te, D_MODEL), F32)],
    )
    return pl.pallas_call(
        _expert_ffn_kernel,
        grid_spec=grid_spec,
        out_shape=jax.ShapeDtypeStruct((n_rows, D_MODEL), F32),
        compiler_params=_params(("arbitrary", "arbitrary")),
        name="moe_expert_swiglu",
    )(tile_expert, n_used, xs, w1, w3, w2)


def _combine_kernel(pos_ref, x_ref, gate_ref, g_ref, y_ref, o_ref, buf0, buf1, sem, *, tm):
    def issue(r, carry):
        _row_copy(y_ref, pos_ref[2 * r], buf0, r, sem).start()
        _row_copy(y_ref, pos_ref[2 * r + 1], buf1, r, sem).start()
        return carry

    def drain(r, carry):
        _row_copy(y_ref, pos_ref[2 * r], buf0, r, sem).wait()
        _row_copy(y_ref, pos_ref[2 * r + 1], buf1, r, sem).wait()
        return carry

    lax.fori_loop(0, tm, issue, 0)
    lax.fori_loop(0, tm, drain, 0)
    gate = gate_ref[...]
    x = x_ref[...] + (buf0[...] * gate[:, 0:1] + buf1[...] * gate[:, 1:2])
    o_ref[...] = x * _rms_scale(x) * g_ref[...]


def _combine(x3, gates, pos_flat, y, g):
    R = x3.shape[0]
    tm = min(ROW_TILE, R)
    kern = functools.partial(_combine_kernel, tm=tm)
    return pl.pallas_call(
        kern,
        grid=(R // tm,),
        in_specs=[
            pl.BlockSpec((TOP_K * tm,), lambda i: (i,), memory_space=pltpu.SMEM),
            pl.BlockSpec((tm, D_MODEL), lambda i: (i, 0)),
            pl.BlockSpec((tm, TOP_K), lambda i: (i, 0)),
            pl.BlockSpec((1, D_MODEL), lambda i: (0, 0)),
            pl.BlockSpec(memory_space=pl.ANY),
        ],
        out_specs=pl.BlockSpec((tm, D_MODEL), lambda i: (i, 0)),
        out_shape=jax.ShapeDtypeStruct((R, D_MODEL), F32),
        scratch_shapes=[pltpu.VMEM((tm, D_MODEL), F32), pltpu.VMEM((tm, D_MODEL), F32),
                        pltpu.SemaphoreType.DMA(())],
        compiler_params=_params(("arbitrary",)),
        name="moe_combine_norm",
    )(pos_flat, x3, gates, g, y)


def _prep_weights(p):
    w_in = p["gdn_w_in"][0]
    w_kv = p["w_kv"]
    kcols = w_kv[:, :HKV * HD].reshape(D_MODEL, HKV, HD)
    vcols = w_kv[:, HKV * HD:].reshape(D_MODEL, HKV, HD)
    dup = lambda a: jnp.concatenate([a, a], axis=2).reshape(D_MODEL, HKV * LANES)
    w_kvx = jnp.concatenate([dup(kcols), dup(vcols), w_kv], axis=1)
    return dict(
        w_qkv=w_in[:, :CONV_CH].astype(BF16),
        w_z=w_in[:, CONV_CH:CONV_CH + V_DIM].astype(BF16),
        w_ba=w_in[:, CONV_CH + V_DIM:].astype(BF16),
        w_out=p["gdn_w_out"][0].astype(BF16),
        ffn_w1=p["ffn_w1"][0].astype(BF16),
        ffn_w3=p["ffn_w3"][0].astype(BF16),
        ffn_w2=p["ffn_w2"][0].astype(BF16),
        w_kvx=w_kvx.astype(BF16),
        w_q=p["swa_w_q"][0].astype(BF16),
        w_o=p["swa_w_o"][0].astype(BF16),
        moe_w1=p["moe_w1"][0].astype(BF16),
        moe_w3=p["moe_w3"][0].astype(BF16),
        moe_w2=p["moe_w2"][0].astype(BF16),
    )


def _dup_heads(a):
    B, T = a.shape[:2]
    return jnp.concatenate([a, a], axis=3).reshape(B * T, HKV * LANES)


def _trunk(x, pos, conv0, s0, win_k, win_v, p, w):
    B, T, _ = x.shape
    R = B * T
    xf = x.reshape(R, D_MODEL)
    row = lambda a: a.reshape(1, -1)

    qkv, conv_tiles = _qkv_conv(xf, row(p["norm_attn"][0]), w["w_qkv"], p["gdn_w_conv"][0], conv0, B, T)
    conv_new = conv_tiles.reshape(B, -1, CONV_W - 1, CONV_CH)[:, -1]
    z, beta, gc, gct = _zbg(xf, row(p["norm_attn"][0]), w["w_z"], w["w_ba"], p["gdn_a_log"][0],
                            p["gdn_dt_bias"][0], B, T)
    u, wv, att_l = _delta_prep(qkv, beta, gc, gct, B, T)
    og, s_new = _delta_scan(qkv, u, wv, att_l, z, gc, s0, row(p["gdn_o_norm"][0]), B, T)
    x1 = _proj_res(xf, og, w["w_out"])
    x2 = _dense_ffn(x1, row(p["norm_ffn"][0]), w["ffn_w1"], w["ffn_w3"], w["ffn_w2"])

    q, kvd, kvs = _qkv_rope(x2, row(p["norm_attn"][1]), row(p["kv_norm"]), w["w_q"], w["w_kvx"],
                            _rope_tables(pos), B, T)
    k_new = kvs[:, :HKV * HD].reshape(B, T, HKV, HD)
    v_new = kvs[:, HKV * HD:].reshape(B, T, HKV, HD)
    if win_k is None:
        keys, Tk = kvd, T
    else:
        hist = jnp.concatenate([_dup_heads(win_k), _dup_heads(win_v)], axis=1).astype(BF16)
        Tw = win_k.shape[1]
        keys = jnp.concatenate([hist.reshape(B, Tw, KV_DUP), kvd.reshape(B, T, KV_DUP)], axis=1)
        Tk = Tw + T
        keys = keys.reshape(B * Tk, KV_DUP)
    att = _attention(q, keys, row(p["swa_sinks"][0]), B, T, Tk)
    x3, h, eidx, gates, counts = _oproj_router(x2, att, w["w_o"], row(p["norm_ffn"][1]), p["moe_router"][0])

    te = min(ROW_TILE, R)
    n_tiles = TOP_K * R // te + N_EXPERTS
    cnt = counts[0].astype(jnp.int32)
    tiles = (cnt + te - 1) // te
    tile_end = jnp.cumsum(tiles)
    base = (tile_end - tiles) * te
    n_used = tile_end[-1:]
    tile_expert = jnp.minimum(jnp.searchsorted(tile_end, jnp.arange(n_tiles, dtype=jnp.int32), side="right"),
                              N_EXPERTS - 1).astype(jnp.int32)
    pos_slot = _slots(eidx, base.astype(F32).reshape(1, N_EXPERTS)).reshape(-1)
    n_used = n_used.astype(jnp.int32)
    xs = _dispatch(h, pos_slot, cnt, base.astype(jnp.int32), n_used, n_tiles, te)
    ys = _expert_ffn(xs, tile_expert, n_used, w["moe_w1"], w["moe_w3"], w["moe_w2"], te)
    y = _combine(x3, gates, pos_slot, ys, row(p["final_norm"]))
    return (y.reshape(B, T, D_MODEL), conv_new[None], s_new[None], k_new, v_new)


def kernel(x_prompt, x_sample, cache_conv, state_delta, cache_k, cache_v, norm_attn, norm_ffn, gdn_w_in,
           gdn_w_conv, gdn_a_log, gdn_dt_bias, gdn_o_norm, gdn_w_out, kv_norm, w_kv, swa_w_q, swa_sinks, swa_w_o,
           ffn_w1, ffn_w3, ffn_w2, moe_router, moe_w1, moe_w3, moe_w2, final_norm):
    p = dict(norm_attn=norm_attn, norm_ffn=norm_ffn, gdn_w_in=gdn_w_in, gdn_w_conv=gdn_w_conv,
             gdn_a_log=gdn_a_log, gdn_dt_bias=gdn_dt_bias, gdn_o_norm=gdn_o_norm, gdn_w_out=gdn_w_out,
             kv_norm=kv_norm, w_kv=w_kv, swa_w_q=swa_w_q, swa_sinks=swa_sinks, swa_w_o=swa_w_o,
             ffn_w1=ffn_w1, ffn_w3=ffn_w3, ffn_w2=ffn_w2, moe_router=moe_router,
             moe_w1=moe_w1, moe_w3=moe_w3, moe_w2=moe_w2, final_norm=final_norm)
    w = _prep_weights(p)
    Bp, Tp, _ = x_prompt.shape
    Bs, Ts, _ = x_sample.shape
    conv0 = jnp.zeros((Bp, CONV_W - 1, CONV_CH), F32)
    s0 = jnp.zeros((Bp, HV, DK, DV), F32)
    y_p, conv_p, delta_p, k_full, v_full = _trunk(x_prompt, jnp.arange(Tp, dtype=jnp.int32), conv0, s0,
                                                  None, None, p, w)
    win = min(WINDOW, Tp)
    k_p = k_full[:, Tp - win:]
    v_p = v_full[:, Tp - win:]
    pos_s = PAST_LEN + jnp.arange(Ts, dtype=jnp.int32)
    y_s, conv_s, delta_s, k_s, v_s = _trunk(x_sample, pos_s, cache_conv[0], state_delta[0],
                                            cache_k, cache_v, p, w)
    return (y_p, y_s, conv_p, delta_p, k_p, v_p, conv_s, delta_s, k_s, v_s)
```

```python
import functools

import jax
import jax.numpy as jnp
import numpy as np
from jax import lax
from jax.experimental import pallas as pl
from jax.experimental.pallas import tpu as pltpu

F32 = jnp.float32
BF16 = jnp.bfloat16

D_MODEL = 1024
CHUNK = 64
HK = 8
HV = 16
DK = 128
DV = 128
QK_DIM = HK * DK
V_DIM = HV * DV
CONV_CH = 2 * QK_DIM + V_DIM
CONV_W = 4
HQ = 16
HKV = 4
HD = 64
WINDOW = 128
ROT_DIM = HD // 4
ROPE_THETA = 500000.0
D_FF = 2816
N_EXPERTS = 8
TOP_K = 2
EPS = 1e-6
PAST_LEN = 2048

LANES = 128
ROW_TILE = 512
VMEM_LIMIT = 48 * 1024 * 1024
VMEM_LIMIT_BIG = 56 * 1024 * 1024


def _params(sem, vmem=VMEM_LIMIT):
    return pltpu.CompilerParams(dimension_semantics=sem, vmem_limit_bytes=vmem)


def _rms_scale(x):
    return lax.rsqrt(jnp.mean(x * x, axis=-1, keepdims=True) + EPS)


def _bdot(a, b):
    return jnp.dot(a.astype(BF16), b.astype(BF16), preferred_element_type=F32)


def _bdot_nt(a, b):
    return lax.dot_general(a.astype(BF16), b.astype(BF16), (((1,), (1,)), ((), ())),
                           preferred_element_type=F32)


def _bdot_tn(a, b):
    return lax.dot_general(a.astype(BF16), b.astype(BF16), (((0,), (0,)), ((), ())),
                           preferred_element_type=F32)


def _split3(x):
    x1 = x.astype(BF16)
    r1 = x - x1.astype(F32)
    x2 = r1.astype(BF16)
    x3 = (r1 - x2.astype(F32)).astype(BF16)
    return x1, x2, x3


def _split2(x):
    x1 = x.astype(BF16)
    x2 = (x - x1.astype(F32)).astype(BF16)
    return x1, x2


def _dot_hi(a, b):
    a1, a2 = _split2(a)
    b1, b2 = _split2(b)
    d = functools.partial(jnp.dot, preferred_element_type=F32)
    return d(a1, b1) + (d(a1, b2) + d(a2, b1))


def _spread(x, emat, pieces):
    parts = _split3(x)[:pieces]
    out = jnp.dot(parts[0], emat, preferred_element_type=F32)
    for part in parts[1:]:
        out = out + jnp.dot(part, emat, preferred_element_type=F32)
    return out


def _head_spread_matrices():
    h = np.arange(HV)[:, None]
    full = (np.arange(HV * DV)[None, :] // DV == h)
    pair = (np.arange(HV * CHUNK)[None, :] // CHUNK == h)
    return jnp.asarray(full.astype(np.float32), dtype=BF16), jnp.asarray(pair.astype(np.float32), dtype=BF16)


def _sigmoid(x):
    return 0.5 * jnp.tanh(0.5 * x) + 0.5


def _silu(x):
    return x * _sigmoid(x)


def _qkv_conv_kernel(x_ref, g_ref, w_ref, wc_ref, cs_ref, out_ref, cst_ref,
                     h_scr, pad_scr, carry_scr, *, tm, tn):
    ti = pl.program_id(1)
    j = pl.program_id(2)

    @pl.when(j == 0)
    def _():
        x = x_ref[...]
        h_scr[...] = (x * _rms_scale(x) * g_ref[...]).astype(BF16)

    proj = jnp.dot(h_scr[...], w_ref[...], preferred_element_type=F32)

    @pl.when(ti == 0)
    def _():
        carry_scr[j, 5:8, :] = cs_ref[0]

    pad_scr[0:8, :] = carry_scr[j]
    pad_scr[8:, :] = proj
    wc = wc_ref[...]
    acc = proj * wc[3:4, :]
    for s in range(1, CONV_W):
        acc = acc + pad_scr[8 - s:8 - s + tm, :] * wc[3 - s:4 - s, :]
    carry_scr[j] = pad_scr[tm:tm + 8, :]
    cst_ref[0] = pad_scr[tm + 5:tm + 8, :]
    y = _silu(acc)

    @pl.when(j >= 2)
    def _():
        out_ref[...] = y.astype(out_ref.dtype)

    @pl.when(j < 2)
    def _():
        scale = jnp.where(j == 0, DK ** -0.5, 1.0).astype(F32)
        for hh in range(tn // DK):
            seg = y[:, hh * DK:(hh + 1) * DK]
            inv = lax.rsqrt(jnp.sum(seg * seg, axis=-1, keepdims=True) + EPS) * scale
            out_ref[:, hh * DK:(hh + 1) * DK] = (seg * inv).astype(out_ref.dtype)


def _qkv_conv(x, g, w_qkv, w_conv, conv_state, B, T):
    R = B * T
    tm = min(ROW_TILE, T)
    tn = QK_DIM
    nt = T // tm
    nj = CONV_CH // tn
    kern = functools.partial(_qkv_conv_kernel, tm=tm, tn=tn)
    return pl.pallas_call(
        kern,
        grid=(B, nt, nj),
        in_specs=[
            pl.BlockSpec((tm, D_MODEL), lambda b, t, j: (b * nt + t, 0)),
            pl.BlockSpec((1, D_MODEL), lambda b, t, j: (0, 0)),
            pl.BlockSpec((D_MODEL, tn), lambda b, t, j: (0, j)),
            pl.BlockSpec((CONV_W, tn), lambda b, t, j: (0, j)),
            pl.BlockSpec((1, CONV_W - 1, tn), lambda b, t, j: (b, 0, j)),
        ],
        out_specs=[
            pl.BlockSpec((tm, tn), lambda b, t, j: (b * nt + t, j)),
            pl.BlockSpec((1, CONV_W - 1, tn), lambda b, t, j: (b * nt + t, 0, j)),
        ],
        out_shape=[
            jax.ShapeDtypeStruct((R, CONV_CH), BF16),
            jax.ShapeDtypeStruct((B * nt, CONV_W - 1, CONV_CH), F32),
        ],
        scratch_shapes=[
            pltpu.VMEM((tm, D_MODEL), BF16),
            pltpu.VMEM((tm + 8, tn), F32),
            pltpu.VMEM((nj, 8, tn), F32),
        ],
        compiler_params=_params(("arbitrary", "arbitrary", "arbitrary")),
        name="gdn_qkv_conv",
    )(x, g, w_qkv, w_conv, conv_state)


def _softplus(x):
    return jnp.maximum(x, 0.0) + jnp.log(1.0 + jnp.exp(-jnp.abs(x)))


def _zbg_kernel(x_ref, g_ref, wz_ref, wba_ref, wbat_ref, alog_ref, dtb_ref, alogt_ref, dtbt_ref, tri_ref,
                z_ref, beta_ref, gc_ref, gct_ref, *, tm):
    x = x_ref[...]
    h = (x * _rms_scale(x) * g_ref[...]).astype(BF16)
    z_ref[...] = jnp.dot(h, wz_ref[...], preferred_element_type=F32).astype(z_ref.dtype)
    ba = jnp.dot(h, wba_ref[...], preferred_element_type=F32)
    beta_ref[...] = _sigmoid(ba[:, :HV])
    gcol = -jnp.exp(alog_ref[...]) * _softplus(ba[:, HV:] + dtb_ref[...])
    tri = tri_ref[...]
    g1, g2, g3 = _split3(gcol)
    d = functools.partial(jnp.dot, preferred_element_type=F32)
    gc_ref[...] = d(tri, g1) + (d(tri, g2) + d(tri, g3))
    at = lax.dot_general(wbat_ref[...], h, (((1,), (1,)), ((), ())), preferred_element_type=F32)
    grow = -jnp.exp(alogt_ref[...]) * _softplus(at + dtbt_ref[...])
    r1, r2, r3 = _split3(grow)
    dn = lambda a: lax.dot_general(a, tri, (((1,), (1,)), ((), ())), preferred_element_type=F32)
    gct = dn(r1) + (dn(r2) + dn(r3))
    for c in range(tm // CHUNK):
        sl = slice(c * CHUNK, (c + 1) * CHUNK)
        gct_ref[0, c] = jnp.concatenate([gct[:HV // 2, sl], gct[HV // 2:, sl]], axis=1)


def _zbg(x, g, w_z, w_ba, a_log, dt_bias, B, T):
    R = B * T
    tm = min(ROW_TILE, T)
    nt = T // tm
    nc = T // CHUNK
    idx = np.arange(tm)
    tri = jnp.asarray(((idx[:, None] // CHUNK == idx[None, :] // CHUNK) & (idx[:, None] >= idx[None, :]))
                      .astype(np.float32), dtype=BF16)
    pair_order = np.concatenate([np.arange(0, HV, 2), np.arange(1, HV, 2)])
    w_a_t = jnp.transpose(w_ba[:, HV:])[pair_order]
    full = lambda shape: pl.BlockSpec(shape, lambda b, t: (0,) * len(shape))
    kern = functools.partial(_zbg_kernel, tm=tm)
    return pl.pallas_call(
        kern,
        grid=(B, nt),
        in_specs=[
            pl.BlockSpec((tm, D_MODEL), lambda b, t: (b * nt + t, 0)),
            full((1, D_MODEL)),
            full((D_MODEL, V_DIM)),
            full((D_MODEL, 2 * HV)),
            full((HV, D_MODEL)),
            full((1, HV)), full((1, HV)), full((HV, 1)), full((HV, 1)),
            full((tm, tm)),
        ],
        out_specs=[
            pl.BlockSpec((tm, V_DIM), lambda b, t: (b * nt + t, 0)),
            pl.BlockSpec((tm, HV), lambda b, t: (b * nt + t, 0)),
            pl.BlockSpec((tm, HV), lambda b, t: (b * nt + t, 0)),
            pl.BlockSpec((1, tm // CHUNK, HV // 2, 2 * CHUNK), lambda b, t: (b, t, 0, 0)),
        ],
        out_shape=[
            jax.ShapeDtypeStruct((R, V_DIM), BF16),
            jax.ShapeDtypeStruct((R, HV), F32),
            jax.ShapeDtypeStruct((R, HV), F32),
            jax.ShapeDtypeStruct((B, nc, HV // 2, 2 * CHUNK), F32),
        ],
        compiler_params=_params(("arbitrary", "arbitrary")),
        name="gdn_z_beta_decay",
    )(x, g, w_z, w_ba, w_a_t, a_log.reshape(1, HV), dt_bias.reshape(1, HV),
      a_log[pair_order].reshape(HV, 1), dt_bias[pair_order].reshape(HV, 1), tri)


PAIR = HV // HK


def _pair_rows(x, lo):
    zero = jnp.zeros((), x.dtype)
    return jnp.concatenate([jnp.where(lo, x, zero), jnp.where(lo, zero, x)], axis=0)


def _pair_products(lhs_list, rhs_list, lo):
    def split(a):
        hi = a.astype(BF16)
        return hi, (a - hi.astype(F32)).astype(BF16)

    ops = []
    for lhs, rhs in zip(lhs_list, rhs_list):
        lh, ll = split(lhs)
        rh, rl = split(rhs)
        ops.append((jnp.concatenate([lh, ll], axis=0),
                    jnp.concatenate([_pair_rows(rh, lo), _pair_rows(rl, lo)], axis=1)))
    quads = [jnp.dot(a, b, preferred_element_type=F32) for a, b in ops]
    out = []
    for q in quads:
        m, n = q.shape[0] // 2, q.shape[1] // 2
        out.append(q[:m, :n] + (q[:m, n:] + q[m:, :n]))
    return out


def _pair_unit_lower_inverse(low_list, lo, eye2):
    C = low_list[0].shape[0]
    xs = [-low for low in low_list]
    accs = [eye2 + x for x in xs]
    pws = _pair_products(xs, xs, lo)
    n = 2
    while 2 * n < C:
        prods = _pair_products([jnp.concatenate([a, p], axis=0) for a, p in zip(accs, pws)], pws, lo)
        accs = [a + pr[:C] for a, pr in zip(accs, prods)]
        pws = [pr[C:] for pr in prods]
        n *= 2
    prods = _pair_products(accs, pws, lo)
    return [a + pr for a, pr in zip(accs, prods)]


def _delta_prep_kernel(q_ref, k_ref, v_ref, beta_ref, gc_ref, gct_ref, epair_ref, u_ref, w_ref, a_ref, *, cb):
    C = CHUNK
    ri = lax.broadcasted_iota(jnp.int32, (C, 2 * C), 0)
    li = lax.broadcasted_iota(jnp.int32, (C, 2 * C), 1)
    lm = li & (C - 1)
    lo = li < C
    incl = ri >= lm
    strict = ri > lm
    eye2 = (ri == lm).astype(F32)
    units = [(c, p) for c in range(cb) for p in range(HK)]
    rows = lambda c: slice(c * C, (c + 1) * C)

    grams = []
    for c, p in units:
        kb = k_ref[rows(c), p * DK:(p + 1) * DK]
        qb = q_ref[rows(c), p * DK:(p + 1) * DK]
        grams.append(lax.dot_general(jnp.concatenate([kb, qb], axis=0), jnp.concatenate([kb, kb], axis=0),
                                     (((1,), (1,)), ((), ())), preferred_element_type=F32))

    epair = epair_ref[...]
    g2s, b2s, betas, begs = [], [], [], []
    for c in range(cb):
        beta = beta_ref[rows(c), :]
        gc = gc_ref[rows(c), :]
        g2s.append(_spread(gc, epair, 3))
        b2s.append(_spread(beta, epair, 2))
        betas.append(beta)
        begs.append(beta * jnp.exp(gc))

    lows = []
    for (c, p), gram in zip(units, grams):
        pt = slice(p * 2 * C, (p + 1) * 2 * C)
        grow2 = gct_ref[c, p:p + 1, :]
        decay2 = jnp.exp(jnp.where(incl, g2s[c][:, pt] - grow2, -jnp.inf))
        a_ref[rows(c), pt] = (gram[C:] * decay2).astype(a_ref.dtype)
        lows.append(jnp.where(strict, b2s[c][:, pt] * gram[:C] * decay2, 0.0))

    tinvs = _pair_unit_lower_inverse(lows, lo, eye2)

    ops = []
    for (c, p), tinv in zip(units, tinvs):
        kf = k_ref[rows(c), p * DK:(p + 1) * DK].astype(F32)
        rhs = []
        for h in (PAIR * p, PAIR * p + 1):
            ht = slice(h * DV, (h + 1) * DV)
            vf = v_ref[rows(c), ht].astype(F32)
            rhs.append(jnp.concatenate([vf * betas[c][:, h:h + 1], kf * begs[c][:, h:h + 1]], axis=1))
        ops.append((_pair_rows(tinv.astype(BF16), lo), jnp.concatenate(rhs, axis=0).astype(BF16)))
    for (c, p), (lhs, rhs) in zip(units, ops):
        uw = jnp.dot(lhs, rhs, preferred_element_type=F32)
        for i, h in enumerate((PAIR * p, PAIR * p + 1)):
            u_ref[rows(c), h * DV:(h + 1) * DV] = uw[i * C:(i + 1) * C, :DV].astype(u_ref.dtype)
            w_ref[rows(c), h * DV:(h + 1) * DV] = uw[i * C:(i + 1) * C, DV:].astype(w_ref.dtype)


def _delta_prep(qkv, beta, gc, gct, B, T):
    R = B * T
    cb = 2 if T % (2 * CHUNK) == 0 else 1
    rb = cb * CHUNK
    gct2 = gct.reshape(R // CHUNK, HK, 2 * CHUNK)
    kern = functools.partial(_delta_prep_kernel, cb=cb)
    return pl.pallas_call(
        kern,
        grid=(R // rb,),
        in_specs=[
            pl.BlockSpec((rb, QK_DIM), lambda i: (i, 0)),
            pl.BlockSpec((rb, QK_DIM), lambda i: (i, 1)),
            pl.BlockSpec((rb, V_DIM), lambda i: (i, 1)),
            pl.BlockSpec((rb, HV), lambda i: (i, 0)),
            pl.BlockSpec((rb, HV), lambda i: (i, 0)),
            pl.BlockSpec((cb, HK, 2 * CHUNK), lambda i: (i, 0, 0)),
            pl.BlockSpec((HV, HV * CHUNK), lambda i: (0, 0)),
        ],
        out_specs=[
            pl.BlockSpec((rb, V_DIM), lambda i: (i, 0)),
            pl.BlockSpec((rb, V_DIM), lambda i: (i, 0)),
            pl.BlockSpec((rb, HV * CHUNK), lambda i: (i, 0)),
        ],
        out_shape=[
            jax.ShapeDtypeStruct((R, V_DIM), BF16),
            jax.ShapeDtypeStruct((R, V_DIM), BF16),
            jax.ShapeDtypeStruct((R, HV * CHUNK), BF16),
        ],
        compiler_params=_params(("arbitrary",)),
        name="gdn_delta_prep",
    )(qkv, qkv, qkv, beta, gc, gct2, _head_spread_matrices()[1])


def _delta_scan_kernel(q_ref, k_ref, u_ref, w_ref, a_ref, z_ref, gc_ref, s0_ref, on_ref, efull_ref,
                       og_ref, s_ref, *, bb):
    c = pl.program_id(1)

    @pl.when(c == 0)
    def _():
        s_ref[...] = s0_ref[...]

    C = CHUNK
    lo = lax.broadcasted_iota(jnp.int32, (C, 2 * C), 1) < C
    onorm = on_ref[...]
    units = [(b, h) for b in range(bb) for h in range(HV)]
    pairs = [(b, p) for b in range(bb) for p in range(HK)]

    efull = efull_ref[...]
    egs, kgs, dgs = [], [], []
    for b in range(bb):
        gc = gc_ref[b]
        g_last = gc[C - 1:C, :]
        egs.append(_spread(jnp.exp(gc), efull, 2))
        kgs.append(_spread(jnp.exp(g_last - gc), efull, 2))
        dgs.append(_spread(jnp.exp(gc[C - 8:C, :]), efull, 3)[7:8, :])

    ops = []
    kds = []
    dec = []
    for b, h in units:
        p = h // PAIR
        ht = slice(h * DV, (h + 1) * DV)
        qg = (q_ref[b, :, p * DK:(p + 1) * DK].astype(F32) * egs[b][:, ht]).astype(BF16)
        kds.append((k_ref[b, :, p * DK:(p + 1) * DK].astype(F32) * kgs[b][:, ht]).astype(BF16))
        dec.append(dgs[b][:, ht])
        ops.append((jnp.concatenate([w_ref[b, :, ht], qg], axis=0), s_ref[b, h].astype(BF16)))
    wqs = [jnp.dot(a, s, preferred_element_type=F32) for a, s in ops]

    vns = [(u_ref[b, :, h * DV:(h + 1) * DV].astype(F32) - wq[:C]).astype(BF16) for (b, h), wq in zip(units, wqs)]

    oparts = []
    for i, (b, p) in enumerate(pairs):
        att = _pair_rows(a_ref[b, :, p * 2 * C:(p + 1) * 2 * C], lo)
        vn2 = jnp.concatenate([vns[PAIR * i], vns[PAIR * i + 1]], axis=0)
        oparts.append(jnp.dot(att, vn2, preferred_element_type=F32))
    sds = [lax.dot_general(kd, vn, (((0,), (0,)), ((), ())), preferred_element_type=F32)
           for kd, vn in zip(kds, vns)]

    for i, (b, h) in enumerate(units):
        s_ref[b, h] = s_ref[b, h] * dec[i] + sds[i]
        o = wqs[i][C:] + oparts[i // PAIR][(h % PAIR) * C:(h % PAIR + 1) * C]
        on = o * _rms_scale(o) * onorm
        zf = z_ref[b, :, h * DV:(h + 1) * DV].astype(F32)
        og_ref[b, :, h * DV:(h + 1) * DV] = (on * _silu(zf)).astype(og_ref.dtype)


def _delta_scan(qkv, u, w, att, z, gc, s0, o_norm, B, T):
    nc = T // CHUNK
    bb = 2 if B % 2 == 0 else 1
    v3 = lambda a: a.reshape(B, T, a.shape[-1])
    blk = lambda n, j=0: pl.BlockSpec((bb, CHUNK, n), lambda b, c: (b, c, j))
    kern = functools.partial(_delta_scan_kernel, bb=bb)
    og, s_new = pl.pallas_call(
        kern,
        grid=(B // bb, nc),
        in_specs=[
            blk(QK_DIM, 0), blk(QK_DIM, 1), blk(V_DIM), blk(V_DIM), blk(HV * CHUNK), blk(V_DIM), blk(HV),
            pl.BlockSpec((bb, HV, DK, DV), lambda b, c: (b, 0, 0, 0)),
            pl.BlockSpec((1, DV), lambda b, c: (0, 0)),
            pl.BlockSpec((HV, HV * DV), lambda b, c: (0, 0)),
        ],
        out_specs=[
            blk(V_DIM),
            pl.BlockSpec((bb, HV, DK, DV), lambda b, c: (b, 0, 0, 0)),
        ],
        out_shape=[
            jax.ShapeDtypeStruct((B, T, V_DIM), BF16),
            jax.ShapeDtypeStruct((B, HV, DK, DV), F32),
        ],
        compiler_params=_params(("arbitrary", "arbitrary")),
        name="gdn_delta_scan",
    )(v3(qkv), v3(qkv), v3(u), v3(w), v3(att), v3(z), v3(gc), s0, o_norm, _head_spread_matrices()[0])
    return og.reshape(B * T, V_DIM), s_new


FF_TILE = 1408
RESIDENT = pl.Buffered(1)


def _swiglu(h, w1_ref, w3_ref, w2_ref, lead=()):
    out = None
    for f in range(D_FF // FF_TILE):
        sl = slice(f * FF_TILE, (f + 1) * FF_TILE)
        a = jnp.dot(h, w1_ref[lead + (slice(None), sl)], preferred_element_type=F32)
        b = jnp.dot(h, w3_ref[lead + (slice(None), sl)], preferred_element_type=F32)
        mid = (_silu(a) * b).astype(BF16)
        part = jnp.dot(mid, w2_ref[lead + (sl, slice(None))], preferred_element_type=F32)
        out = part if out is None else out + part
    return out


def _mixer_out_ffn_kernel(x_ref, a_ref, wo_ref, g_ref, w1_ref, w3_ref, w2_ref, o_ref):
    x1 = x_ref[...] + jnp.dot(a_ref[...], wo_ref[...], preferred_element_type=F32)
    h = (x1 * _rms_scale(x1) * g_ref[...]).astype(BF16)
    o_ref[...] = x1 + _swiglu(h, w1_ref, w3_ref, w2_ref)


def _mixer_out_ffn(x, a, w_out, g, w1, w3, w2):
    R = x.shape[0]
    K = a.shape[1]
    tm = min(ROW_TILE, R)
    held = lambda shape: pl.BlockSpec(shape, lambda i: (0,) * len(shape), pipeline_mode=RESIDENT)
    return pl.pallas_call(
        _mixer_out_ffn_kernel,
        grid=(R // tm,),
        in_specs=[
            pl.BlockSpec((tm, D_MODEL), lambda i: (i, 0)),
            pl.BlockSpec((tm, K), lambda i: (i, 0)),
            held((K, D_MODEL)),
            held((1, D_MODEL)),
            held((D_MODEL, D_FF)), held((D_MODEL, D_FF)), held((D_FF, D_MODEL)),
        ],
        out_specs=pl.BlockSpec((tm, D_MODEL), lambda i: (i, 0)),
        out_shape=jax.ShapeDtypeStruct((R, D_MODEL), F32),
        compiler_params=_params(("arbitrary",), VMEM_LIMIT_BIG),
        name="gdn_out_dense_swiglu",
    )(x, a, w_out, g, w1, w3, w2)


KV_DUP = 2 * HKV * LANES
KV_STD = 2 * HKV * HD


def _rope(x, cos, sm, sp):
    up = pltpu.roll(x, LANES - ROT_DIM // 2, 1)
    dn = pltpu.roll(x, ROT_DIM // 2, 1)
    return x * cos + up * sm + dn * sp


def _qkv_rope_kernel(x_ref, gq_ref, gkv_ref, wq_ref, wkv_ref, cos_ref, sm_ref, sp_ref,
                     q_ref, kvd_ref, kvs_ref):
    x = x_ref[...]
    xr = x * _rms_scale(x)
    hq = (xr * gq_ref[...]).astype(BF16)
    hkv = (xr * gkv_ref[...]).astype(BF16)
    cos = cos_ref[...]
    sm = sm_ref[...]
    sp = sp_ref[...]
    q = jnp.dot(hq, wq_ref[...], preferred_element_type=F32)
    for t in range(HQ * HD // LANES):
        sl = slice(t * LANES, (t + 1) * LANES)
        q_ref[:, sl] = (_rope(q[:, sl], cos, sm, sp) * (HD ** -0.5)).astype(q_ref.dtype)
    kv = jnp.dot(hkv, wkv_ref[...], preferred_element_type=F32)
    half = KV_DUP // 2
    for t in range(HKV):
        sl = slice(t * LANES, (t + 1) * LANES)
        kvd_ref[:, sl] = _rope(kv[:, sl], cos, sm, sp).astype(kvd_ref.dtype)
    kvd_ref[:, half:] = kv[:, half:KV_DUP].astype(kvd_ref.dtype)
    for t in range(HKV * HD // LANES):
        sl = slice(KV_DUP + t * LANES, KV_DUP + (t + 1) * LANES)
        kvs_ref[:, t * LANES:(t + 1) * LANES] = _rope(kv[:, sl], cos, sm, sp)
    kvs_ref[:, HKV * HD:] = kv[:, KV_DUP + HKV * HD:]


def _rope_tables(pos):
    half = ROT_DIM // 2
    inv = jnp.power(ROPE_THETA, -jnp.arange(half, dtype=F32) * 2.0 / ROT_DIM)
    ang = pos.astype(F32)[:, None] * inv[None, :]
    cos = jnp.cos(ang)
    sin = jnp.sin(ang)
    T = pos.shape[0]
    one = jnp.ones((T, HD - ROT_DIM), F32)
    zero = jnp.zeros((T, HD - ROT_DIM), F32)
    zh = jnp.zeros((T, half), F32)
    c64 = jnp.concatenate([cos, cos, one], axis=1)
    sm64 = jnp.concatenate([-sin, zh, zero], axis=1)
    sp64 = jnp.concatenate([zh, sin, zero], axis=1)
    dup = lambda a: jnp.concatenate([a, a], axis=1)
    return dup(c64), dup(sm64), dup(sp64)


def _qkv_rope(x, gq, gkv, w_q, w_kvx, tables, B, T):
    R = B * T
    tm = min(ROW_TILE, T)
    nt = T // tm
    nkv = KV_DUP + KV_STD
    tab = pl.BlockSpec((tm, LANES), lambda i: (i % nt, 0))
    return pl.pallas_call(
        _qkv_rope_kernel,
        grid=(R // tm,),
        in_specs=[
            pl.BlockSpec((tm, D_MODEL), lambda i: (i, 0)),
            pl.BlockSpec((1, D_MODEL), lambda i: (0, 0)),
            pl.BlockSpec((1, D_MODEL), lambda i: (0, 0)),
            pl.BlockSpec((D_MODEL, HQ * HD), lambda i: (0, 0)),
            pl.BlockSpec((D_MODEL, nkv), lambda i: (0, 0)),
            tab, tab, tab,
        ],
        out_specs=[
            pl.BlockSpec((tm, HQ * HD), lambda i: (i, 0)),
            pl.BlockSpec((tm, KV_DUP), lambda i: (i, 0)),
            pl.BlockSpec((tm, KV_STD), lambda i: (i, 0)),
        ],
        out_shape=[
            jax.ShapeDtypeStruct((R, HQ * HD), BF16),
            jax.ShapeDtypeStruct((R, KV_DUP), BF16),
            jax.ShapeDtypeStruct((R, KV_STD), F32),
        ],
        compiler_params=_params(("arbitrary",)),
        name="swa_qkv_rope",
    )(x, gq, gkv, w_q, w_kvx, *tables)


KEY_CHUNKS = WINDOW // CHUNK + 1
KEY_PAD = 256


def _attn_kernel(q_ref, k0_ref, k1_ref, k2_ref, sink_ref, ones_ref, o_ref, *, off):
    c = pl.program_id(1)
    krefs = (k0_ref, k1_ref, k2_ref)
    lane = lax.broadcasted_iota(jnp.int32, (KEY_CHUNKS * CHUNK, LANES), 1)
    lo = lane < HD
    col = lax.broadcasted_iota(jnp.int32, (1, KEY_PAD), 1)
    first = c + off - (KEY_CHUNKS - 1)
    valid = (col < KEY_CHUNKS * CHUNK) & (col // CHUNK + first >= 0)
    base_bias = jnp.where(valid, 0.0, -jnp.inf).astype(F32)
    sink_col = col == KEY_CHUNKS * CHUNK
    zpad = jnp.zeros((KEY_PAD - KEY_CHUNKS * CHUNK, LANES), BF16)
    zero = jnp.zeros((), BF16)
    tiles = HQ // HKV // 2
    ones_blk = ones_ref[...]

    vblks = []
    scores = []
    for g in range(HKV):
        kg = jnp.concatenate([r[:, g * LANES:(g + 1) * LANES] for r in krefs], axis=0)
        vg = jnp.concatenate([r[:, KV_DUP // 2 + g * LANES:KV_DUP // 2 + (g + 1) * LANES] for r in krefs], axis=0)
        kblk = jnp.concatenate([jnp.where(lo, kg, zero), zpad, jnp.where(lo, zero, kg), zpad], axis=0)
        vblk = jnp.concatenate([jnp.where(lo, vg, zero), zpad, jnp.where(lo, zero, vg), zpad], axis=0)
        vblks.append(jnp.concatenate([vblk, ones_blk], axis=1))
        qg = jnp.concatenate([q_ref[:, (g * tiles + p) * LANES:(g * tiles + p + 1) * LANES] for p in range(tiles)],
                             axis=0)
        scores.append(lax.dot_general(qg, kblk, (((1,), (1,)), ((), ())), preferred_element_type=F32))

    probs = []
    for g in range(HKV):
        rows = []
        for p in range(tiles):
            t = g * tiles + p
            ps = []
            for hh in range(2):
                bias = jnp.where(sink_col, sink_ref[:, 2 * t + hh:2 * t + hh + 1], base_bias)
                sh = scores[g][p * CHUNK:(p + 1) * CHUNK, hh * KEY_PAD:(hh + 1) * KEY_PAD] + bias
                m = jnp.max(sh, axis=-1, keepdims=True)
                ps.append(jnp.exp(sh - m).astype(BF16))
            rows.append(jnp.concatenate(ps, axis=1))
        probs.append(jnp.concatenate(rows, axis=0))

    outs = [jnp.dot(pr, vb, preferred_element_type=F32) for pr, vb in zip(probs, vblks)]
    for g in range(HKV):
        for p in range(tiles):
            t = g * tiles + p
            rows = slice(p * CHUNK, (p + 1) * CHUNK)
            o_ref[:, t * LANES:(t + 1) * LANES] = (outs[g][rows, :LANES] / outs[g][rows, LANES:]).astype(o_ref.dtype)


def _attention(q, kvd, sinks, B, Tq, Tk):
    ncq = Tq // CHUNK
    nck = Tk // CHUNK
    off = nck - ncq
    kern = functools.partial(_attn_kernel, off=off)
    blk_row = np.arange(2 * KEY_PAD)[:, None] // KEY_PAD
    blk_lane = np.arange(LANES)[None, :] // HD
    ones_blk = jnp.asarray((blk_row == blk_lane).astype(np.float32), dtype=BF16)

    def kspec(i):
        return pl.BlockSpec((CHUNK, KV_DUP),
                            lambda b, c: (b * nck + jnp.maximum(c + off - (KEY_CHUNKS - 1) + i, 0), 0))

    return pl.pallas_call(
        kern,
        grid=(B, ncq),
        in_specs=[
            pl.BlockSpec((CHUNK, HQ * HD), lambda b, c: (b * ncq + c, 0)),
            kspec(0), kspec(1), kspec(2),
            pl.BlockSpec((1, HQ), lambda b, c: (0, 0)),
            pl.BlockSpec((2 * KEY_PAD, LANES), lambda b, c: (0, 0)),
        ],
        out_specs=pl.BlockSpec((CHUNK, HQ * HD), lambda b, c: (b * ncq + c, 0)),
        out_shape=jax.ShapeDtypeStruct((B * Tq, HQ * HD), BF16),
        compiler_params=_params(("arbitrary", "arbitrary")),
        name="swa_attention",
    )(q, kvd, kvd, kvd, sinks, ones_blk)


def _oproj_router_kernel(x_ref, a_ref, w_ref, g_ref, wr_ref, x3_ref, h_ref, e_ref, gate_ref, cnt_ref):
    i = pl.program_id(0)
    x3 = x_ref[...] + jnp.dot(a_ref[...], w_ref[...], preferred_element_type=F32)
    x3_ref[...] = x3
    h = x3 * _rms_scale(x3) * g_ref[...]
    h_ref[...] = h
    logits = _dot_hi(h, wr_ref[...])
    eidx = lax.broadcasted_iota(jnp.int32, logits.shape, 1)
    m1 = jnp.max(logits, axis=-1, keepdims=True)
    i1 = jnp.min(jnp.where(logits == m1, eidx, N_EXPERTS), axis=-1, keepdims=True)
    rest = jnp.where(eidx == i1, -jnp.inf, logits)
    m2 = jnp.max(rest, axis=-1, keepdims=True)
    i2 = jnp.min(jnp.where(rest == m2, eidx, N_EXPERTS), axis=-1, keepdims=True)
    e2 = jnp.exp(m2 - m1)
    den = 1.0 + e2
    e_ref[...] = jnp.concatenate([i1, i2], axis=1)
    gate_ref[...] = jnp.concatenate([1.0 / den, e2 / den], axis=1)
    hot = ((eidx == i1) | (eidx == i2)).astype(F32)

    @pl.when(i == 0)
    def _():
        cnt_ref[...] = jnp.zeros_like(cnt_ref)

    cnt_ref[...] += jnp.sum(hot, axis=0, keepdims=True)


def _oproj_router(x, a, w_o, g, w_r):
    R = x.shape[0]
    tm = min(ROW_TILE, R)
    row = lambda n: pl.BlockSpec((tm, n), lambda i: (i, 0))
    return pl.pallas_call(
        _oproj_router_kernel,
        grid=(R // tm,),
        in_specs=[
            row(D_MODEL), row(HQ * HD),
            pl.BlockSpec((HQ * HD, D_MODEL), lambda i: (0, 0)),
            pl.BlockSpec((1, D_MODEL), lambda i: (0, 0)),
            pl.BlockSpec((D_MODEL, N_EXPERTS), lambda i: (0, 0)),
        ],
        out_specs=[row(D_MODEL), row(D_MODEL), row(TOP_K), row(TOP_K),
                   pl.BlockSpec((1, N_EXPERTS), lambda i: (0, 0))],
        out_shape=[
            jax.ShapeDtypeStruct((R, D_MODEL), F32),
            jax.ShapeDtypeStruct((R, D_MODEL), F32),
            jax.ShapeDtypeStruct((R, TOP_K), jnp.int32),
            jax.ShapeDtypeStruct((R, TOP_K), F32),
            jax.ShapeDtypeStruct((1, N_EXPERTS), F32),
        ],
        compiler_params=_params(("arbitrary",)),
        name="swa_oproj_router",
    )(x, a, w_o, g, w_r)


def _slot_kernel(e_ref, base_ref, tri_ref, pos_ref, run_scr):
    i = pl.program_id(0)

    @pl.when(i == 0)
    def _():
        run_scr[...] = jnp.zeros_like(run_scr)

    e = e_ref[...]
    tm = e.shape[0]
    eidx = lax.broadcasted_iota(jnp.int32, (tm, N_EXPERTS), 1)
    hot0 = eidx == e[:, 0:1]
    hot1 = eidx == e[:, 1:2]
    hot = (hot0 | hot1).astype(BF16)
    before = jnp.dot(tri_ref[...], hot, preferred_element_type=F32)
    dest = before + run_scr[...] + base_ref[...]
    p0 = jnp.sum(jnp.where(hot0, dest, 0.0), axis=-1, keepdims=True)
    p1 = jnp.sum(jnp.where(hot1, dest, 0.0), axis=-1, keepdims=True)
    pos_ref[...] = jnp.concatenate([p0, p1], axis=1).astype(jnp.int32)
    run_scr[...] += jnp.sum(hot.astype(F32), axis=0, keepdims=True)


def _slots(eidx, base):
    R = eidx.shape[0]
    tm = min(ROW_TILE, R)
    idx = np.arange(tm)
    tri = jnp.asarray((idx[:, None] > idx[None, :]).astype(np.float32), dtype=BF16)
    return pl.pallas_call(
        _slot_kernel,
        grid=(R // tm,),
        in_specs=[
            pl.BlockSpec((tm, TOP_K), lambda i: (i, 0)),
            pl.BlockSpec((1, N_EXPERTS), lambda i: (0, 0)),
            pl.BlockSpec((tm, tm), lambda i: (0, 0)),
        ],
        out_specs=pl.BlockSpec((tm, TOP_K), lambda i: (i, 0)),
        out_shape=jax.ShapeDtypeStruct((R, TOP_K), jnp.int32),
        scratch_shapes=[pltpu.VMEM((1, N_EXPERTS), F32)],
        compiler_params=_params(("arbitrary",)),
        name="moe_slots",
    )(eidx, base, tri)


def _row_copy(src, i, dst, j, sem):
    return pltpu.make_async_copy(src.at[pl.ds(i, 1)], dst.at[pl.ds(j, 1)], sem)


def _dispatch_kernel(pos_ref, cnt_ref, base_ref, nused_ref, h_ref, xs_ref, zero_scr, sem, zsem, *, tm, te, n_tiles):
    i = pl.program_id(0)

    @pl.when(i == 0)
    def _():
        zero_scr[...] = jnp.zeros_like(zero_scr)

        def tile_copy(t):
            dst = xs_ref.at[pl.ds(pl.multiple_of(t * te, te), te)]
            return pltpu.make_async_copy(zero_scr, dst, zsem)

        def issue_tile(t, carry):
            tile_copy(t).start()
            return carry

        def drain_tile(t, carry):
            tile_copy(t).wait()
            return carry

        lax.fori_loop(nused_ref[0], n_tiles, issue_tile, 0)
        lax.fori_loop(nused_ref[0], n_tiles, drain_tile, 0)
        for e in range(N_EXPERTS):
            n = cnt_ref[e]
            start = base_ref[e] + n
            npad = (te - n % te) % te

            def issue(r, carry):
                _row_copy(zero_scr, 0, xs_ref, start + r, zsem).start()
                return carry

            def drain(r, carry):
                _row_copy(zero_scr, 0, xs_ref, start + r, zsem).wait()
                return carry

            lax.fori_loop(0, npad, issue, 0)
            lax.fori_loop(0, npad, drain, 0)

    def issue(r, carry):
        _row_copy(h_ref, r, xs_ref, pos_ref[2 * r], sem).start()
        _row_copy(h_ref, r, xs_ref, pos_ref[2 * r + 1], sem).start()
        return carry

    lax.fori_loop(0, tm, issue, 0)
    for _ in range(TOP_K):
        pltpu.make_async_copy(h_ref, xs_ref.at[pl.ds(0, tm)], sem).wait()


def _dispatch(h, pos_flat, counts, base, n_used, n_tiles, te):
    R = h.shape[0]
    tm = min(ROW_TILE, R)
    kern = functools.partial(_dispatch_kernel, tm=tm, te=te, n_tiles=n_tiles)
    smem = pl.BlockSpec(memory_space=pltpu.SMEM)
    return pl.pallas_call(
        kern,
        grid=(R // tm,),
        in_specs=[
            pl.BlockSpec((TOP_K * tm,), lambda i: (i,), memory_space=pltpu.SMEM),
            smem, smem, smem,
            pl.BlockSpec((tm, D_MODEL), lambda i: (i, 0)),
        ],
        out_specs=pl.BlockSpec(memory_space=pl.ANY),
        out_shape=jax.ShapeDtypeStruct((n_tiles * te, D_MODEL), F32),
        scratch_shapes=[pltpu.VMEM((te, D_MODEL), F32), pltpu.SemaphoreType.DMA(()), pltpu.SemaphoreType.DMA(())],
        compiler_params=pltpu.CompilerParams(dimension_semantics=("arbitrary",), vmem_limit_bytes=VMEM_LIMIT,
                                             has_side_effects=True),
        name="moe_dispatch",
    )(pos_flat, counts, base, n_used, h)


def _expert_ffn_kernel(te_ref, nused_ref, x_ref, w1_ref, w3_ref, w2_ref, y_ref):
    i = pl.program_id(0)

    @pl.when(i < nused_ref[0])
    def _():
        y_ref[...] = _swiglu(x_ref[...].astype(BF16), w1_ref, w3_ref, w2_ref, lead=(0,))

    @pl.when(i >= nused_ref[0])
    def _():
        y_ref[...] = jnp.zeros_like(y_ref)


def _expert_ffn(xs, tile_expert, n_used, w1, w3, w2, te):
    n_rows = xs.shape[0]
    nt = n_rows // te
    wspec = lambda shape: pl.BlockSpec((1,) + shape, lambda i, te_r, nu_r: (te_r[i], 0, 0), pipeline_mode=RESIDENT)
    grid_spec = pltpu.PrefetchScalarGridSpec(
        num_scalar_prefetch=2,
        grid=(nt,),
        in_specs=[
            pl.BlockSpec((te, D_MODEL), lambda i, te_r, nu_r: (i, 0)),
            wspec((D_MODEL, D_FF)), wspec((D_MODEL, D_FF)), wspec((D_FF, D_MODEL)),
        ],
        out_specs=pl.BlockSpec((te, D_MODEL), lambda i, te_r, nu_r: (i, 0)),
    )
    return pl.pallas_call(
        _expert_ffn_kernel,
        grid_spec=grid_spec,
        out_shape=jax.ShapeDtypeStruct((n_rows, D_MODEL), F32),
        compiler_params=_params(("arbitrary",), VMEM_LIMIT_BIG),
        name="moe_expert_swiglu",
    )(tile_expert, n_used, xs, w1, w3, w2)


def _combine_kernel(pos_ref, x_ref, gate_ref, g_ref, y_ref, o_ref, buf0, buf1, sem, *, tm):
    def issue(r, carry):
        _row_copy(y_ref, pos_ref[2 * r], buf0, r, sem).start()
        _row_copy(y_ref, pos_ref[2 * r + 1], buf1, r, sem).start()
        return carry

    lax.fori_loop(0, tm, issue, 0)
    for buf in (buf0, buf1):
        pltpu.make_async_copy(y_ref.at[pl.ds(0, tm)], buf, sem).wait()
    gate = gate_ref[...]
    x = x_ref[...] + (buf0[...] * gate[:, 0:1] + buf1[...] * gate[:, 1:2])
    o_ref[...] = x * _rms_scale(x) * g_ref[...]


def _combine(x3, gates, pos_flat, y, g):
    R = x3.shape[0]
    tm = min(ROW_TILE, R)
    kern = functools.partial(_combine_kernel, tm=tm)
    return pl.pallas_call(
        kern,
        grid=(R // tm,),
        in_specs=[
            pl.BlockSpec((TOP_K * tm,), lambda i: (i,), memory_space=pltpu.SMEM),
            pl.BlockSpec((tm, D_MODEL), lambda i: (i, 0)),
            pl.BlockSpec((tm, TOP_K), lambda i: (i, 0)),
            pl.BlockSpec((1, D_MODEL), lambda i: (0, 0)),
            pl.BlockSpec(memory_space=pl.ANY),
        ],
        out_specs=pl.BlockSpec((tm, D_MODEL), lambda i: (i, 0)),
        out_shape=jax.ShapeDtypeStruct((R, D_MODEL), F32),
        scratch_shapes=[pltpu.VMEM((tm, D_MODEL), F32), pltpu.VMEM((tm, D_MODEL), F32),
                        pltpu.SemaphoreType.DMA(())],
        compiler_params=_params(("arbitrary",)),
        name="moe_combine_norm",
    )(pos_flat, x3, gates, g, y)


def _prep_weights(p):
    w_in = p["gdn_w_in"][0]
    w_kv = p["w_kv"]
    kcols = w_kv[:, :HKV * HD].reshape(D_MODEL, HKV, HD)
    vcols = w_kv[:, HKV * HD:].reshape(D_MODEL, HKV, HD)
    dup = lambda a: jnp.concatenate([a, a], axis=2).reshape(D_MODEL, HKV * LANES)
    w_kvx = jnp.concatenate([dup(kcols), dup(vcols), w_kv], axis=1)
    return dict(
        w_qkv=w_in[:, :CONV_CH].astype(BF16),
        w_z=w_in[:, CONV_CH:CONV_CH + V_DIM].astype(BF16),
        w_ba=w_in[:, CONV_CH + V_DIM:].astype(BF16),
        w_out=p["gdn_w_out"][0].astype(BF16),
        ffn_w1=p["ffn_w1"][0].astype(BF16),
        ffn_w3=p["ffn_w3"][0].astype(BF16),
        ffn_w2=p["ffn_w2"][0].astype(BF16),
        w_kvx=w_kvx.astype(BF16),
        w_q=p["swa_w_q"][0].astype(BF16),
        w_o=p["swa_w_o"][0].astype(BF16),
        moe_w1=p["moe_w1"][0].astype(BF16),
        moe_w3=p["moe_w3"][0].astype(BF16),
        moe_w2=p["moe_w2"][0].astype(BF16),
    )


def _dup_heads(a):
    B, T = a.shape[:2]
    return jnp.concatenate([a, a], axis=3).reshape(B * T, HKV * LANES)


def _trunk(x, pos, conv0, s0, win_k, win_v, p, w):
    B, T, _ = x.shape
    R = B * T
    xf = x.reshape(R, D_MODEL)
    row = lambda a: a.reshape(1, -1)

    qkv, conv_tiles = _qkv_conv(xf, row(p["norm_attn"][0]), w["w_qkv"], p["gdn_w_conv"][0], conv0, B, T)
    conv_new = conv_tiles.reshape(B, -1, CONV_W - 1, CONV_CH)[:, -1]
    z, beta, gc, gct = _zbg(xf, row(p["norm_attn"][0]), w["w_z"], w["w_ba"], p["gdn_a_log"][0],
                            p["gdn_dt_bias"][0], B, T)
    u, wv, att_l = _delta_prep(qkv, beta, gc, gct, B, T)
    og, s_new = _delta_scan(qkv, u, wv, att_l, z, gc, s0, row(p["gdn_o_norm"][0]), B, T)
    x2 = _mixer_out_ffn(xf, og, w["w_out"], row(p["norm_ffn"][0]), w["ffn_w1"], w["ffn_w3"], w["ffn_w2"])

    q, kvd, kvs = _qkv_rope(x2, row(p["norm_attn"][1]), row(p["kv_norm"]), w["w_q"], w["w_kvx"],
                            _rope_tables(pos), B, T)
    k_new = kvs[:, :HKV * HD].reshape(B, T, HKV, HD)
    v_new = kvs[:, HKV * HD:].reshape(B, T, HKV, HD)
    if win_k is None:
        keys, Tk = kvd, T
    else:
        hist = jnp.concatenate([_dup_heads(win_k), _dup_heads(win_v)], axis=1).astype(BF16)
        Tw = win_k.shape[1]
        keys = jnp.concatenate([hist.reshape(B, Tw, KV_DUP), kvd.reshape(B, T, KV_DUP)], axis=1)
        Tk = Tw + T
        keys = keys.reshape(B * Tk, KV_DUP)
    att = _attention(q, keys, row(p["swa_sinks"][0]), B, T, Tk)
    x3, h, eidx, gates, counts = _oproj_router(x2, att, w["w_o"], row(p["norm_ffn"][1]), p["moe_router"][0])

    te = min(ROW_TILE, R)
    n_tiles = TOP_K * R // te + N_EXPERTS
    cnt = counts[0].astype(jnp.int32)
    tiles = (cnt + te - 1) // te
    tile_end = jnp.cumsum(tiles)
    base = (tile_end - tiles) * te
    n_used = tile_end[-1:]
    tile_expert = jnp.minimum(jnp.searchsorted(tile_end, jnp.arange(n_tiles, dtype=jnp.int32), side="right"),
                              N_EXPERTS - 1).astype(jnp.int32)
    pos_slot = _slots(eidx, base.astype(F32).reshape(1, N_EXPERTS)).reshape(-1)
    n_used = n_used.astype(jnp.int32)
    xs = _dispatch(h, pos_slot, cnt, base.astype(jnp.int32), n_used, n_tiles, te)
    ys = _expert_ffn(xs, tile_expert, n_used, w["moe_w1"], w["moe_w3"], w["moe_w2"], te)
    y = _combine(x3, gates, pos_slot, ys, row(p["final_norm"]))
    return (y.reshape(B, T, D_MODEL), conv_new[None], s_new[None], k_new, v_new)


def kernel(x_prompt, x_sample, cache_conv, state_delta, cache_k, cache_v, norm_attn, norm_ffn, gdn_w_in,
           gdn_w_conv, gdn_a_log, gdn_dt_bias, gdn_o_norm, gdn_w_out, kv_norm, w_kv, swa_w_q, swa_sinks, swa_w_o,
           ffn_w1, ffn_w3, ffn_w2, moe_router, moe_w1, moe_w3, moe_w2, final_norm):
    p = dict(norm_attn=norm_attn, norm_ffn=norm_ffn, gdn_w_in=gdn_w_in, gdn_w_conv=gdn_w_conv,
             gdn_a_log=gdn_a_log, gdn_dt_bias=gdn_dt_bias, gdn_o_norm=gdn_o_norm, gdn_w_out=gdn_w_out,
             kv_norm=kv_norm, w_kv=w_kv, swa_w_q=swa_w_q, swa_sinks=swa_sinks, swa_w_o=swa_w_o,
             ffn_w1=ffn_w1, ffn_w3=ffn_w3, ffn_w2=ffn_w2, moe_router=moe_router,
             moe_w1=moe_w1, moe_w3=moe_w3, moe_w2=moe_w2, final_norm=final_norm)
    w = _prep_weights(p)
    Bp, Tp, _ = x_prompt.shape
    Bs, Ts, _ = x_sample.shape
    conv0 = jnp.zeros((Bp, CONV_W - 1, CONV_CH), F32)
    s0 = jnp.zeros((Bp, HV, DK, DV), F32)
    y_p, conv_p, delta_p, k_full, v_full = _trunk(x_prompt, jnp.arange(Tp, dtype=jnp.int32), conv0, s0,
                                                  None, None, p, w)
    win = min(WINDOW, Tp)
    k_p = k_full[:, Tp - win:]
    v_p = v_full[:, Tp - win:]
    pos_s = PAST_LEN + jnp.arange(Ts, dtype=jnp.int32)
    y_s, conv_s, delta_s, k_s, v_s = _trunk(x_sample, pos_s, cache_conv[0], state_delta[0],
                                            cache_k, cache_v, p, w)
    return (y_p, y_s, conv_p, delta_p, k_p, v_p, conv_s, delta_s, k_s, v_s)
```

```python
import functools

import jax
import jax.numpy as jnp
import numpy as np
from jax import lax
from jax.experimental import pallas as pl
from jax.experimental.pallas import tpu as pltpu

F32 = jnp.float32
BF16 = jnp.bfloat16

D_MODEL = 1024
CHUNK = 64
HK = 8
HV = 16
DK = 128
DV = 128
QK_DIM = HK * DK
V_DIM = HV * DV
CONV_CH = 2 * QK_DIM + V_DIM
CONV_W = 4
HQ = 16
HKV = 4
HD = 64
WINDOW = 128
ROT_DIM = HD // 4
ROPE_THETA = 500000.0
D_FF = 2816
N_EXPERTS = 8
TOP_K = 2
EPS = 1e-6
PAST_LEN = 2048

LANES = 128
ROW_TILE = 512
VMEM_LIMIT = 48 * 1024 * 1024
VMEM_LIMIT_BIG = 56 * 1024 * 1024


def _params(sem, vmem=VMEM_LIMIT):
    return pltpu.CompilerParams(dimension_semantics=sem, vmem_limit_bytes=vmem)


def _rms_scale(x):
    return lax.rsqrt(jnp.mean(x * x, axis=-1, keepdims=True) + EPS)


def _bdot(a, b):
    return jnp.dot(a.astype(BF16), b.astype(BF16), preferred_element_type=F32)


def _bdot_nt(a, b):
    return lax.dot_general(a.astype(BF16), b.astype(BF16), (((1,), (1,)), ((), ())),
                           preferred_element_type=F32)


def _bdot_tn(a, b):
    return lax.dot_general(a.astype(BF16), b.astype(BF16), (((0,), (0,)), ((), ())),
                           preferred_element_type=F32)


def _split3(x):
    x1 = x.astype(BF16)
    r1 = x - x1.astype(F32)
    x2 = r1.astype(BF16)
    x3 = (r1 - x2.astype(F32)).astype(BF16)
    return x1, x2, x3


def _split2(x):
    x1 = x.astype(BF16)
    x2 = (x - x1.astype(F32)).astype(BF16)
    return x1, x2


def _dot_hi(a, b):
    a1, a2 = _split2(a)
    b1, b2 = _split2(b)
    d = functools.partial(jnp.dot, preferred_element_type=F32)
    return d(a1, b1) + (d(a1, b2) + d(a2, b1))


def _spread(x, emat, pieces):
    parts = _split3(x)[:pieces]
    out = jnp.dot(parts[0], emat, preferred_element_type=F32)
    for part in parts[1:]:
        out = out + jnp.dot(part, emat, preferred_element_type=F32)
    return out


def _head_spread_matrices():
    h = np.arange(HV)[:, None]
    full = (np.arange(HV * DV)[None, :] // DV == h)
    pair = (np.arange(HV * CHUNK)[None, :] // CHUNK == h)
    return jnp.asarray(full.astype(np.float32), dtype=BF16), jnp.asarray(pair.astype(np.float32), dtype=BF16)


def _sigmoid(x):
    return 0.5 * jnp.tanh(0.5 * x) + 0.5


def _silu(x):
    half = 0.5 * x
    return half * jnp.tanh(half) + half


CONV_SUB = 256


def _qkv_conv_kernel(x_ref, g_ref, w_ref, wc_ref, cs_ref, out_ref, cst_ref,
                     h_scr, pad_scr, carry_scr, *, tm, tn, l2):
    ti = pl.program_id(1)
    j = pl.program_id(2)

    @pl.when(j == 0)
    def _():
        x = x_ref[...]
        h_scr[...] = (x * _rms_scale(x) * g_ref[...]).astype(BF16)

    @pl.when(ti == 0)
    def _():
        carry_scr[j, 5:8, :] = cs_ref[0]

    h = h_scr[...]
    wc = wc_ref[...]
    scale = jnp.where(j == 0, DK ** -0.5, 1.0).astype(F32)
    cols = lambda k: slice(k * CONV_SUB, (k + 1) * CONV_SUB)

    def project(k):
        return jnp.dot(h, w_ref[:, cols(k)], preferred_element_type=F32)

    def finish(k, proj):
        c = cols(k)
        pad_scr[0:8, c] = carry_scr[j, :, c]
        pad_scr[8:, c] = proj
        acc = proj * wc[3:4, c]
        for s in range(1, CONV_W):
            acc = acc + pad_scr[8 - s:8 - s + tm, c] * wc[3 - s:4 - s, c]
        carry_scr[j, :, c] = pad_scr[tm:tm + 8, c]
        cst_ref[0, :, c] = pad_scr[tm + 5:tm + 8, c]
        y = _silu(acc)
        if not l2:
            out_ref[:, c] = y.astype(out_ref.dtype)
            return
        for hh in range(CONV_SUB // DK):
            seg = y[:, hh * DK:(hh + 1) * DK]
            inv = lax.rsqrt(jnp.sum(seg * seg, axis=-1, keepdims=True) + EPS) * scale
            lanes = slice(k * CONV_SUB + hh * DK, k * CONV_SUB + (hh + 1) * DK)
            out_ref[:, lanes] = (seg * inv).astype(out_ref.dtype)

    pending = project(0)
    for k in range(tn // CONV_SUB):
        upcoming = project(k + 1) if k + 1 < tn // CONV_SUB else None
        finish(k, pending)
        pending = upcoming


def _qkv_conv(x, g, w, w_conv, conv_state, B, T, l2):
    R = B * T
    n = w.shape[1]
    tm = min(ROW_TILE, T)
    tn = QK_DIM
    nt = T // tm
    nj = n // tn
    kern = functools.partial(_qkv_conv_kernel, tm=tm, tn=tn, l2=l2)
    return pl.pallas_call(
        kern,
        grid=(B, nt, nj),
        in_specs=[
            pl.BlockSpec((tm, D_MODEL), lambda b, t, j: (b * nt + t, 0)),
            pl.BlockSpec((1, D_MODEL), lambda b, t, j: (0, 0)),
            pl.BlockSpec((D_MODEL, tn), lambda b, t, j: (0, j)),
            pl.BlockSpec((CONV_W, tn), lambda b, t, j: (0, j)),
            pl.BlockSpec((1, CONV_W - 1, tn), lambda b, t, j: (b, 0, j)),
        ],
        out_specs=[
            pl.BlockSpec((tm, tn), lambda b, t, j: (b * nt + t, j)),
            pl.BlockSpec((1, CONV_W - 1, tn), lambda b, t, j: (b * nt + t, 0, j)),
        ],
        out_shape=[
            jax.ShapeDtypeStruct((R, n), BF16),
            jax.ShapeDtypeStruct((B * nt, CONV_W - 1, n), F32),
        ],
        scratch_shapes=[
            pltpu.VMEM((tm, D_MODEL), BF16),
            pltpu.VMEM((tm + 8, tn), F32),
            pltpu.VMEM((nj, 8, tn), F32),
        ],
        compiler_params=_params(("arbitrary", "arbitrary", "arbitrary")),
        name="gdn_qk_conv" if l2 else "gdn_v_conv",
    )(x, g, w, w_conv, conv_state)


def _softplus(x):
    return jnp.maximum(x, 0.0) + jnp.log(1.0 + jnp.exp(-jnp.abs(x)))


def _zbg_kernel(x_ref, g_ref, w_ref, alog_ref, dtb_ref, tri_ref, z_ref, beta_ref, gc_ref, gct_ref, *, tm):
    x = x_ref[...]
    h = (x * _rms_scale(x) * g_ref[...]).astype(BF16)
    zba = jnp.dot(h, w_ref[...], preferred_element_type=F32)
    z_ref[...] = zba[:, :V_DIM].astype(z_ref.dtype)
    beta_ref[...] = _sigmoid(zba[:, V_DIM:V_DIM + HV])
    gcol = -jnp.exp(alog_ref[...]) * _softplus(zba[:, V_DIM + HV:V_DIM + 2 * HV] + dtb_ref[...])
    tri = tri_ref[...]
    g1, g2, g3 = _split3(gcol)
    d = functools.partial(jnp.dot, preferred_element_type=F32)
    gc = d(tri, g1) + (d(tri, g2) + d(tri, g3))
    gc_ref[...] = gc
    r = lax.broadcasted_iota(jnp.int32, (HV, HV), 0)
    c = lax.broadcasted_iota(jnp.int32, (HV, HV), 1)
    sel = (c == jnp.where(r < HV // 2, 2 * r, 2 * r - (HV - 1))).astype(BF16)
    tn = lambda a: lax.dot_general(sel, a, (((1,), (1,)), ((), ())), preferred_element_type=F32)
    c1, c2, c3 = _split3(gc)
    gct = tn(c1) + (tn(c2) + tn(c3))
    for ch in range(tm // CHUNK):
        sl = slice(ch * CHUNK, (ch + 1) * CHUNK)
        gct_ref[0, ch] = jnp.concatenate([gct[:HV // 2, sl], gct[HV // 2:, sl]], axis=1)


def _zbg(x, g, w_z, w_ba, a_log, dt_bias, B, T):
    R = B * T
    tm = min(ROW_TILE, T)
    nt = T // tm
    nc = T // CHUNK
    idx = np.arange(tm)
    tri = jnp.asarray(((idx[:, None] // CHUNK == idx[None, :] // CHUNK) & (idx[:, None] >= idx[None, :]))
                      .astype(np.float32), dtype=BF16)
    w_zba = jnp.concatenate([w_z, w_ba, jnp.zeros((D_MODEL, LANES - 2 * HV), BF16)], axis=1)
    full = lambda shape: pl.BlockSpec(shape, lambda b, t: (0,) * len(shape))
    kern = functools.partial(_zbg_kernel, tm=tm)
    return pl.pallas_call(
        kern,
        grid=(B, nt),
        in_specs=[
            pl.BlockSpec((tm, D_MODEL), lambda b, t: (b * nt + t, 0)),
            full((1, D_MODEL)),
            full((D_MODEL, V_DIM + LANES)),
            full((1, HV)), full((1, HV)),
            full((tm, tm)),
        ],
        out_specs=[
            pl.BlockSpec((tm, V_DIM), lambda b, t: (b * nt + t, 0)),
            pl.BlockSpec((tm, HV), lambda b, t: (b * nt + t, 0)),
            pl.BlockSpec((tm, HV), lambda b, t: (b * nt + t, 0)),
            pl.BlockSpec((1, tm // CHUNK, HV // 2, 2 * CHUNK), lambda b, t: (b, t, 0, 0)),
        ],
        out_shape=[
            jax.ShapeDtypeStruct((R, V_DIM), BF16),
            jax.ShapeDtypeStruct((R, HV), F32),
            jax.ShapeDtypeStruct((R, HV), F32),
            jax.ShapeDtypeStruct((B, nc, HV // 2, 2 * CHUNK), F32),
        ],
        compiler_params=_params(("arbitrary", "arbitrary")),
        name="gdn_z_beta_decay",
    )(x, g, w_zba, a_log.reshape(1, HV), dt_bias.reshape(1, HV), tri)


PAIR = HV // HK


def _pair_rows(x, lo):
    zero = jnp.zeros((), x.dtype)
    return jnp.concatenate([jnp.where(lo, x, zero), jnp.where(lo, zero, x)], axis=0)


def _pair_products(lhs_list, rhs_list, lo):
    def split(a):
        hi = a.astype(BF16)
        return hi, (a - hi.astype(F32)).astype(BF16)

    ops = []
    for lhs, rhs in zip(lhs_list, rhs_list):
        lh, ll = split(lhs)
        rh, rl = split(rhs)
        ops.append((jnp.concatenate([lh, ll], axis=0),
                    jnp.concatenate([_pair_rows(rh, lo), _pair_rows(rl, lo)], axis=1)))
    quads = [jnp.dot(a, b, preferred_element_type=F32) for a, b in ops]
    out = []
    for q in quads:
        m, n = q.shape[0] // 2, q.shape[1] // 2
        out.append(q[:m, :n] + (q[:m, n:] + q[m:, :n]))
    return out


def _pair_unit_lower_inverse(low_list, lo, eye2):
    C = low_list[0].shape[0]
    xs = [-low for low in low_list]
    accs = [eye2 + x for x in xs]
    pws = _pair_products(xs, xs, lo)
    n = 2
    while 2 * n < C:
        prods = _pair_products([jnp.concatenate([a, p], axis=0) for a, p in zip(accs, pws)], pws, lo)
        accs = [a + pr[:C] for a, pr in zip(accs, prods)]
        pws = [pr[C:] for pr in prods]
        n *= 2
    prods = _pair_products(accs, pws, lo)
    return [a + pr for a, pr in zip(accs, prods)]


def _delta_prep_kernel(q_ref, k_ref, v_ref, beta_ref, gc_ref, gct_ref, epair_ref, u_ref, w_ref, a_ref, *, cb):
    C = CHUNK
    ri = lax.broadcasted_iota(jnp.int32, (C, 2 * C), 0)
    li = lax.broadcasted_iota(jnp.int32, (C, 2 * C), 1)
    lm = li & (C - 1)
    lo = li < C
    incl = ri >= lm
    strict = ri > lm
    eye2 = (ri == lm).astype(F32)
    units = [(c, p) for c in range(cb) for p in range(HK)]
    rows = lambda c: slice(c * C, (c + 1) * C)

    grams = []
    for c, p in units:
        kb = k_ref[rows(c), p * DK:(p + 1) * DK]
        qb = q_ref[rows(c), p * DK:(p + 1) * DK]
        grams.append(lax.dot_general(jnp.concatenate([kb, qb], axis=0), jnp.concatenate([kb, kb], axis=0),
                                     (((1,), (1,)), ((), ())), preferred_element_type=F32))

    epair = epair_ref[...]
    g2s, b2s, betas, begs = [], [], [], []
    for c in range(cb):
        beta = beta_ref[rows(c), :]
        gc = gc_ref[rows(c), :]
        g2s.append(_spread(gc, epair, 3))
        b2s.append(_spread(beta, epair, 2))
        betas.append(beta)
        begs.append(beta * jnp.exp(gc))

    lows = []
    for (c, p), gram in zip(units, grams):
        pt = slice(p * 2 * C, (p + 1) * 2 * C)
        grow2 = gct_ref[c, p:p + 1, :]
        decay2 = jnp.exp(jnp.where(incl, g2s[c][:, pt] - grow2, -jnp.inf))
        a_ref[rows(c), pt] = (gram[C:] * decay2).astype(a_ref.dtype)
        lows.append(jnp.where(strict, b2s[c][:, pt] * gram[:C] * decay2, 0.0))

    tinvs = _pair_unit_lower_inverse(lows, lo, eye2)

    ops = []
    for (c, p), tinv in zip(units, tinvs):
        kf = k_ref[rows(c), p * DK:(p + 1) * DK].astype(F32)
        rhs = []
        for h in (PAIR * p, PAIR * p + 1):
            ht = slice(h * DV, (h + 1) * DV)
            vf = v_ref[rows(c), ht].astype(F32)
            rhs.append(jnp.concatenate([vf * betas[c][:, h:h + 1], kf * begs[c][:, h:h + 1]], axis=1))
        ops.append((_pair_rows(tinv.astype(BF16), lo), jnp.concatenate(rhs, axis=0).astype(BF16)))
    for (c, p), (lhs, rhs) in zip(units, ops):
        uw = jnp.dot(lhs, rhs, preferred_element_type=F32)
        for i, h in enumerate((PAIR * p, PAIR * p + 1)):
            u_ref[rows(c), h * DV:(h + 1) * DV] = uw[i * C:(i + 1) * C, :DV].astype(u_ref.dtype)
            w_ref[rows(c), h * DV:(h + 1) * DV] = uw[i * C:(i + 1) * C, DV:].astype(w_ref.dtype)


def _delta_prep(qk, v, beta, gc, gct, B, T):
    R = B * T
    cb = 2 if T % (2 * CHUNK) == 0 else 1
    rb = cb * CHUNK
    gct2 = gct.reshape(R // CHUNK, HK, 2 * CHUNK)
    kern = functools.partial(_delta_prep_kernel, cb=cb)
    return pl.pallas_call(
        kern,
        grid=(R // rb,),
        in_specs=[
            pl.BlockSpec((rb, QK_DIM), lambda i: (i, 0)),
            pl.BlockSpec((rb, QK_DIM), lambda i: (i, 1)),
            pl.BlockSpec((rb, V_DIM), lambda i: (i, 0)),
            pl.BlockSpec((rb, HV), lambda i: (i, 0)),
            pl.BlockSpec((rb, HV), lambda i: (i, 0)),
            pl.BlockSpec((cb, HK, 2 * CHUNK), lambda i: (i, 0, 0)),
            pl.BlockSpec((HV, HV * CHUNK), lambda i: (0, 0)),
        ],
        out_specs=[
            pl.BlockSpec((rb, V_DIM), lambda i: (i, 0)),
            pl.BlockSpec((rb, V_DIM), lambda i: (i, 0)),
            pl.BlockSpec((rb, HV * CHUNK), lambda i: (i, 0)),
        ],
        out_shape=[
            jax.ShapeDtypeStruct((R, V_DIM), BF16),
            jax.ShapeDtypeStruct((R, V_DIM), BF16),
            jax.ShapeDtypeStruct((R, HV * CHUNK), BF16),
        ],
        compiler_params=_params(("arbitrary",)),
        name="gdn_delta_prep",
    )(qk, qk, v, beta, gc, gct2, _head_spread_matrices()[1])


def _delta_scan_kernel(q_ref, k_ref, u_ref, w_ref, a_ref, z_ref, gc_ref, s0_ref, on_ref, efull_ref,
                       og_ref, s_ref, *, bb):
    c = pl.program_id(1)

    @pl.when(c == 0)
    def _():
        s_ref[...] = s0_ref[...]

    C = CHUNK
    lo = lax.broadcasted_iota(jnp.int32, (C, 2 * C), 1) < C
    onorm = on_ref[...]
    units = [(b, h) for b in range(bb) for h in range(HV)]
    pairs = [(b, p) for b in range(bb) for p in range(HK)]

    efull = efull_ref[...]
    egs, kgs, dgs = [], [], []
    for b in range(bb):
        gc = gc_ref[b]
        g_last = gc[C - 1:C, :]
        egs.append(_spread(jnp.exp(gc), efull, 2))
        kgs.append(_spread(jnp.exp(g_last - gc), efull, 2))
        dgs.append(_spread(jnp.exp(gc[C - 8:C, :]), efull, 3)[7:8, :])

    ops = []
    kds = []
    dec = []
    for b, h in units:
        p = h // PAIR
        ht = slice(h * DV, (h + 1) * DV)
        qg = (q_ref[b, :, p * DK:(p + 1) * DK].astype(F32) * egs[b][:, ht]).astype(BF16)
        kds.append((k_ref[b, :, p * DK:(p + 1) * DK].astype(F32) * kgs[b][:, ht]).astype(BF16))
        dec.append(dgs[b][:, ht])
        ops.append((jnp.concatenate([w_ref[b, :, ht], qg], axis=0), s_ref[b, h].astype(BF16)))
    wqs = [jnp.dot(a, s, preferred_element_type=F32) for a, s in ops]

    vns = [(u_ref[b, :, h * DV:(h + 1) * DV].astype(F32) - wq[:C]).astype(BF16) for (b, h), wq in zip(units, wqs)]

    oparts = []
    for i, (b, p) in enumerate(pairs):
        att = _pair_rows(a_ref[b, :, p * 2 * C:(p + 1) * 2 * C], lo)
        vn2 = jnp.concatenate([vns[PAIR * i], vns[PAIR * i + 1]], axis=0)
        oparts.append(jnp.dot(att, vn2, preferred_element_type=F32))
    sds = [lax.dot_general(kd, vn, (((0,), (0,)), ((), ())), preferred_element_type=F32)
           for kd, vn in zip(kds, vns)]

    for i, (b, h) in enumerate(units):
        s_ref[b, h] = s_ref[b, h] * dec[i] + sds[i]
        o = wqs[i][C:] + oparts[i // PAIR][(h % PAIR) * C:(h % PAIR + 1) * C]
        on = o * _rms_scale(o) * onorm
        zf = z_ref[b, :, h * DV:(h + 1) * DV].astype(F32)
        og_ref[b, :, h * DV:(h + 1) * DV] = (on * _silu(zf)).astype(og_ref.dtype)


def _delta_scan(qk, u, w, att, z, gc, s0, o_norm, B, T):
    nc = T // CHUNK
    bb = 2 if B % 2 == 0 else 1
    v3 = lambda a: a.reshape(B, T, a.shape[-1])
    blk = lambda n, j=0: pl.BlockSpec((bb, CHUNK, n), lambda b, c: (b, c, j))
    kern = functools.partial(_delta_scan_kernel, bb=bb)
    og, s_new = pl.pallas_call(
        kern,
        grid=(B // bb, nc),
        in_specs=[
            blk(QK_DIM, 0), blk(QK_DIM, 1), blk(V_DIM), blk(V_DIM), blk(HV * CHUNK), blk(V_DIM), blk(HV),
            pl.BlockSpec((bb, HV, DK, DV), lambda b, c: (b, 0, 0, 0)),
            pl.BlockSpec((1, DV), lambda b, c: (0, 0)),
            pl.BlockSpec((HV, HV * DV), lambda b, c: (0, 0)),
        ],
        out_specs=[
            blk(V_DIM),
            pl.BlockSpec((bb, HV, DK, DV), lambda b, c: (b, 0, 0, 0)),
        ],
        out_shape=[
            jax.ShapeDtypeStruct((B, T, V_DIM), BF16),
            jax.ShapeDtypeStruct((B, HV, DK, DV), F32),
        ],
        compiler_params=_params(("arbitrary", "arbitrary")),
        name="gdn_delta_scan",
    )(v3(qk), v3(qk), v3(u), v3(w), v3(att), v3(z), v3(gc), s0, o_norm, _head_spread_matrices()[0])
    return og.reshape(B * T, V_DIM), s_new


FF_TILE = 1408
RESIDENT = pl.Buffered(1)


def _swiglu(h, w1_ref, w3_ref, w2_ref, lead=()):
    out = None
    for f in range(D_FF // FF_TILE):
        sl = slice(f * FF_TILE, (f + 1) * FF_TILE)
        a = jnp.dot(h, w1_ref[lead + (slice(None), sl)], preferred_element_type=F32)
        b = jnp.dot(h, w3_ref[lead + (slice(None), sl)], preferred_element_type=F32)
        mid = (_silu(a) * b).astype(BF16)
        part = jnp.dot(mid, w2_ref[lead + (sl, slice(None))], preferred_element_type=F32)
        out = part if out is None else out + part
    return out


def _mixer_out_ffn_kernel(x_ref, a_ref, wo_ref, g_ref, w1_ref, w3_ref, w2_ref, o_ref):
    x1 = x_ref[...] + jnp.dot(a_ref[...], wo_ref[...], preferred_element_type=F32)
    h = (x1 * _rms_scale(x1) * g_ref[...]).astype(BF16)
    o_ref[...] = x1 + _swiglu(h, w1_ref, w3_ref, w2_ref)


def _mixer_out_ffn(x, a, w_out, g, w1, w3, w2):
    R = x.shape[0]
    K = a.shape[1]
    tm = min(ROW_TILE, R)
    held = lambda shape: pl.BlockSpec(shape, lambda i: (0,) * len(shape), pipeline_mode=RESIDENT)
    return pl.pallas_call(
        _mixer_out_ffn_kernel,
        grid=(R // tm,),
        in_specs=[
            pl.BlockSpec((tm, D_MODEL), lambda i: (i, 0)),
            pl.BlockSpec((tm, K), lambda i: (i, 0)),
            held((K, D_MODEL)),
            held((1, D_MODEL)),
            held((D_MODEL, D_FF)), held((D_MODEL, D_FF)), held((D_FF, D_MODEL)),
        ],
        out_specs=pl.BlockSpec((tm, D_MODEL), lambda i: (i, 0)),
        out_shape=jax.ShapeDtypeStruct((R, D_MODEL), F32),
        compiler_params=_params(("arbitrary",), VMEM_LIMIT_BIG),
        name="gdn_out_dense_swiglu",
    )(x, a, w_out, g, w1, w3, w2)


KV_DUP = 2 * HKV * LANES
KV_STD = 2 * HKV * HD


def _rope(x, cos, sm, sp):
    up = pltpu.roll(x, LANES - ROT_DIM // 2, 1)
    dn = pltpu.roll(x, ROT_DIM // 2, 1)
    return x * cos + up * sm + dn * sp


def _qkv_rope_kernel(x_ref, gq_ref, gkv_ref, wq_ref, wkv_ref, cos_ref, sm_ref, sp_ref,
                     q_ref, kvd_ref, kvs_ref):
    x = x_ref[...]
    xr = x * _rms_scale(x)
    hq = (xr * gq_ref[...]).astype(BF16)
    hkv = (xr * gkv_ref[...]).astype(BF16)
    cos = cos_ref[...]
    sm = sm_ref[...]
    sp = sp_ref[...]
    q = jnp.dot(hq, wq_ref[...], preferred_element_type=F32)
    for t in range(HQ * HD // LANES):
        sl = slice(t * LANES, (t + 1) * LANES)
        q_ref[:, sl] = (_rope(q[:, sl], cos, sm, sp) * (HD ** -0.5)).astype(q_ref.dtype)
    kv = jnp.dot(hkv, wkv_ref[...], preferred_element_type=F32)
    half = KV_DUP // 2
    for t in range(HKV):
        sl = slice(t * LANES, (t + 1) * LANES)
        kvd_ref[:, sl] = _rope(kv[:, sl], cos, sm, sp).astype(kvd_ref.dtype)
    kvd_ref[:, half:] = kv[:, half:KV_DUP].astype(kvd_ref.dtype)
    for t in range(HKV * HD // LANES):
        sl = slice(KV_DUP + t * LANES, KV_DUP + (t + 1) * LANES)
        kvs_ref[:, t * LANES:(t + 1) * LANES] = _rope(kv[:, sl], cos, sm, sp)
    kvs_ref[:, HKV * HD:] = kv[:, KV_DUP + HKV * HD:]


def _rope_tables(pos):
    half = ROT_DIM // 2
    inv = jnp.power(ROPE_THETA, -jnp.arange(half, dtype=F32) * 2.0 / ROT_DIM)
    ang = pos.astype(F32)[:, None] * inv[None, :]
    cos = jnp.cos(ang)
    sin = jnp.sin(ang)
    T = pos.shape[0]
    one = jnp.ones((T, HD - ROT_DIM), F32)
    zero = jnp.zeros((T, HD - ROT_DIM), F32)
    zh = jnp.zeros((T, half), F32)
    c64 = jnp.concatenate([cos, cos, one], axis=1)
    sm64 = jnp.concatenate([-sin, zh, zero], axis=1)
    sp64 = jnp.concatenate([zh, sin, zero], axis=1)
    dup = lambda a: jnp.concatenate([a, a], axis=1)
    return dup(c64), dup(sm64), dup(sp64)


def _qkv_rope(x, gq, gkv, w_q, w_kvx, tables, B, T):
    R = B * T
    tm = min(ROW_TILE, T)
    nt = T // tm
    nkv = KV_DUP + KV_STD
    tab = pl.BlockSpec((tm, LANES), lambda i: (i % nt, 0))
    return pl.pallas_call(
        _qkv_rope_kernel,
        grid=(R // tm,),
        in_specs=[
            pl.BlockSpec((tm, D_MODEL), lambda i: (i, 0)),
            pl.BlockSpec((1, D_MODEL), lambda i: (0, 0)),
            pl.BlockSpec((1, D_MODEL), lambda i: (0, 0)),
            pl.BlockSpec((D_MODEL, HQ * HD), lambda i: (0, 0)),
            pl.BlockSpec((D_MODEL, nkv), lambda i: (0, 0)),
            tab, tab, tab,
        ],
        out_specs=[
            pl.BlockSpec((tm, HQ * HD), lambda i: (i, 0)),
            pl.BlockSpec((tm, KV_DUP), lambda i: (i, 0)),
            pl.BlockSpec((tm, KV_STD), lambda i: (i, 0)),
        ],
        out_shape=[
            jax.ShapeDtypeStruct((R, HQ * HD), BF16),
            jax.ShapeDtypeStruct((R, KV_DUP), BF16),
            jax.ShapeDtypeStruct((R, KV_STD), F32),
        ],
        compiler_params=_params(("arbitrary",)),
        name="swa_qkv_rope",
    )(x, gq, gkv, w_q, w_kvx, *tables)


KEY_CHUNKS = WINDOW // CHUNK + 1
KEY_PAD = 256


def _attn_kernel(*refs, off, cq):
    q_ref = refs[0]
    krefs = refs[1:KEY_CHUNKS + cq]
    sink_ref, ones_ref, o_ref = refs[KEY_CHUNKS + cq:]
    c = pl.program_id(1)
    lane = lax.broadcasted_iota(jnp.int32, (KEY_CHUNKS * CHUNK, LANES), 1)
    lo = lane < HD
    col = lax.broadcasted_iota(jnp.int32, (1, KEY_PAD), 1)
    sink_col = col == KEY_CHUNKS * CHUNK
    zpad = jnp.zeros((KEY_PAD - KEY_CHUNKS * CHUNK, LANES), BF16)
    zero = jnp.zeros((), BF16)
    tiles = HQ // HKV // 2
    ones_blk = ones_ref[...]
    units = [(u, g) for u in range(cq) for g in range(HKV)]

    vblks = []
    scores = []
    for u, g in units:
        kr = krefs[u:u + KEY_CHUNKS]
        kg = jnp.concatenate([r[:, g * LANES:(g + 1) * LANES] for r in kr], axis=0)
        vg = jnp.concatenate([r[:, KV_DUP // 2 + g * LANES:KV_DUP // 2 + (g + 1) * LANES] for r in kr], axis=0)
        kblk = jnp.concatenate([jnp.where(lo, kg, zero), zpad, jnp.where(lo, zero, kg), zpad], axis=0)
        vblk = jnp.concatenate([jnp.where(lo, vg, zero), zpad, jnp.where(lo, zero, vg), zpad], axis=0)
        vblks.append(jnp.concatenate([vblk, ones_blk], axis=1))
        qg = jnp.concatenate([q_ref[u * CHUNK:(u + 1) * CHUNK, (g * tiles + p) * LANES:(g * tiles + p + 1) * LANES]
                              for p in range(tiles)], axis=0)
        scores.append(lax.dot_general(qg, kblk, (((1,), (1,)), ((), ())), preferred_element_type=F32))

    probs = []
    for (u, g), sc in zip(units, scores):
        first = c * cq + u + off - (KEY_CHUNKS - 1)
        valid = (col < KEY_CHUNKS * CHUNK) & (col // CHUNK + first >= 0)
        base_bias = jnp.where(valid, 0.0, -jnp.inf).astype(F32)
        rows = []
        for p in range(tiles):
            t = g * tiles + p
            ps = []
            for hh in range(2):
                bias = jnp.where(sink_col, sink_ref[:, 2 * t + hh:2 * t + hh + 1], base_bias)
                sh = sc[p * CHUNK:(p + 1) * CHUNK, hh * KEY_PAD:(hh + 1) * KEY_PAD] + bias
                m = jnp.max(sh, axis=-1, keepdims=True)
                ps.append(jnp.exp(sh - m).astype(BF16))
            rows.append(jnp.concatenate(ps, axis=1))
        probs.append(jnp.concatenate(rows, axis=0))

    outs = [jnp.dot(pr, vb, preferred_element_type=F32) for pr, vb in zip(probs, vblks)]
    for (u, g), out in zip(units, outs):
        for p in range(tiles):
            t = g * tiles + p
            rows = slice(p * CHUNK, (p + 1) * CHUNK)
            o_ref[u * CHUNK:(u + 1) * CHUNK, t * LANES:(t + 1) * LANES] = (
                out[rows, :LANES] / out[rows, LANES:]).astype(o_ref.dtype)


def _attention(q, kvd, sinks, B, Tq, Tk):
    ncq = Tq // CHUNK
    nck = Tk // CHUNK
    off = nck - ncq
    cq = 2 if ncq % 2 == 0 else 1
    kern = functools.partial(_attn_kernel, off=off, cq=cq)
    blk_row = np.arange(2 * KEY_PAD)[:, None] // KEY_PAD
    blk_lane = np.arange(LANES)[None, :] // HD
    ones_blk = jnp.asarray((blk_row == blk_lane).astype(np.float32), dtype=BF16)

    def kspec(i):
        return pl.BlockSpec((CHUNK, KV_DUP),
                            lambda b, c: (b * nck + jnp.maximum(c * cq + off - (KEY_CHUNKS - 1) + i, 0), 0))

    qspec = pl.BlockSpec((cq * CHUNK, HQ * HD), lambda b, c: (b * (ncq // cq) + c, 0))
    return pl.pallas_call(
        kern,
        grid=(B, ncq // cq),
        in_specs=[qspec] + [kspec(i) for i in range(KEY_CHUNKS + cq - 1)] + [
            pl.BlockSpec((1, HQ), lambda b, c: (0, 0)),
            pl.BlockSpec((2 * KEY_PAD, LANES), lambda b, c: (0, 0)),
        ],
        out_specs=qspec,
        out_shape=jax.ShapeDtypeStruct((B * Tq, HQ * HD), BF16),
        compiler_params=_params(("arbitrary", "arbitrary")),
        name="swa_attention",
    )(q, *([kvd] * (KEY_CHUNKS + cq - 1)), sinks, ones_blk)


def _oproj_router_kernel(x_ref, a_ref, w_ref, g_ref, wr_ref, x3_ref, h_ref, e_ref, gate_ref, cnt_ref):
    i = pl.program_id(0)
    x3 = x_ref[...] + jnp.dot(a_ref[...], w_ref[...], preferred_element_type=F32)
    x3_ref[...] = x3
    h = x3 * _rms_scale(x3) * g_ref[...]
    h_ref[...] = h
    h1, h2 = _split2(h)
    r1, r2 = _split2(wr_ref[...])
    part = jnp.dot(h1, jnp.concatenate([r1, r2], axis=1), preferred_element_type=F32)
    logits = part[:, :N_EXPERTS] + (part[:, N_EXPERTS:] + jnp.dot(h2, r1, preferred_element_type=F32))
    eidx = lax.broadcasted_iota(jnp.int32, logits.shape, 1)
    m1 = jnp.max(logits, axis=-1, keepdims=True)
    i1 = jnp.min(jnp.where(logits == m1, eidx, N_EXPERTS), axis=-1, keepdims=True)
    rest = jnp.where(eidx == i1, -jnp.inf, logits)
    m2 = jnp.max(rest, axis=-1, keepdims=True)
    i2 = jnp.min(jnp.where(rest == m2, eidx, N_EXPERTS), axis=-1, keepdims=True)
    e2 = jnp.exp(m2 - m1)
    den = 1.0 + e2
    e_ref[...] = jnp.concatenate([i1, i2], axis=1)
    gate_ref[...] = jnp.concatenate([1.0 / den, e2 / den], axis=1)
    hot = ((eidx == i1) | (eidx == i2)).astype(F32)

    @pl.when(i == 0)
    def _():
        cnt_ref[...] = jnp.zeros_like(cnt_ref)

    cnt_ref[...] += jnp.sum(hot, axis=0, keepdims=True)


def _oproj_router(x, a, w_o, g, w_r):
    R = x.shape[0]
    tm = min(ROW_TILE, R)
    row = lambda n: pl.BlockSpec((tm, n), lambda i: (i, 0))
    return pl.pallas_call(
        _oproj_router_kernel,
        grid=(R // tm,),
        in_specs=[
            row(D_MODEL), row(HQ * HD),
            pl.BlockSpec((HQ * HD, D_MODEL), lambda i: (0, 0)),
            pl.BlockSpec((1, D_MODEL), lambda i: (0, 0)),
            pl.BlockSpec((D_MODEL, N_EXPERTS), lambda i: (0, 0)),
        ],
        out_specs=[row(D_MODEL), row(D_MODEL), row(TOP_K), row(TOP_K),
                   pl.BlockSpec((1, N_EXPERTS), lambda i: (0, 0))],
        out_shape=[
            jax.ShapeDtypeStruct((R, D_MODEL), F32),
            jax.ShapeDtypeStruct((R, D_MODEL), F32),
            jax.ShapeDtypeStruct((R, TOP_K), jnp.int32),
            jax.ShapeDtypeStruct((R, TOP_K), F32),
            jax.ShapeDtypeStruct((1, N_EXPERTS), F32),
        ],
        compiler_params=_params(("arbitrary",)),
        name="swa_oproj_router",
    )(x, a, w_o, g, w_r)


def _slot_kernel(e_ref, base_ref, tri_ref, pos_ref, run_scr):
    i = pl.program_id(0)

    @pl.when(i == 0)
    def _():
        run_scr[...] = jnp.zeros_like(run_scr)

    e = e_ref[...]
    tm = e.shape[0]
    eidx = lax.broadcasted_iota(jnp.int32, (tm, N_EXPERTS), 1)
    hot0 = eidx == e[:, 0:1]
    hot1 = eidx == e[:, 1:2]
    hot = (hot0 | hot1).astype(BF16)
    before = jnp.dot(tri_ref[...], hot, preferred_element_type=F32)
    dest = before + run_scr[...] + base_ref[...]
    p0 = jnp.sum(jnp.where(hot0, dest, 0.0), axis=-1, keepdims=True)
    p1 = jnp.sum(jnp.where(hot1, dest, 0.0), axis=-1, keepdims=True)
    pos_ref[...] = jnp.concatenate([p0, p1], axis=1).astype(jnp.int32)
    run_scr[...] += jnp.sum(hot.astype(F32), axis=0, keepdims=True)


def _slots(eidx, base):
    R = eidx.shape[0]
    tm = min(ROW_TILE, R)
    idx = np.arange(tm)
    tri = jnp.asarray((idx[:, None] > idx[None, :]).astype(np.float32), dtype=BF16)
    return pl.pallas_call(
        _slot_kernel,
        grid=(R // tm,),
        in_specs=[
            pl.BlockSpec((tm, TOP_K), lambda i: (i, 0)),
            pl.BlockSpec((1, N_EXPERTS), lambda i: (0, 0)),
            pl.BlockSpec((tm, tm), lambda i: (0, 0)),
        ],
        out_specs=pl.BlockSpec((tm, TOP_K), lambda i: (i, 0)),
        out_shape=jax.ShapeDtypeStruct((R, TOP_K), jnp.int32),
        scratch_shapes=[pltpu.VMEM((1, N_EXPERTS), F32)],
        compiler_params=_params(("arbitrary",)),
        name="moe_slots",
    )(eidx, base, tri)


DMA_UNROLL = 8


def _row_copy(src, i, dst, j, sem):
    return pltpu.make_async_copy(src.at[pl.ds(i, 1)], dst.at[pl.ds(j, 1)], sem)


def _dispatch_kernel(pos_ref, cnt_ref, base_ref, nused_ref, h_ref, xs_ref, zero_scr, sem, zsem, *, tm, te, n_tiles):
    i = pl.program_id(0)

    @pl.when(i == 0)
    def _():
        zero_scr[...] = jnp.zeros_like(zero_scr)

        def tile_copy(t):
            dst = xs_ref.at[pl.ds(pl.multiple_of(t * te, te), te)]
            return pltpu.make_async_copy(zero_scr, dst, zsem)

        def issue_tile(t, carry):
            tile_copy(t).start()
            return carry

        def drain_tile(t, carry):
            tile_copy(t).wait()
            return carry

        lax.fori_loop(nused_ref[0], n_tiles, issue_tile, 0)
        lax.fori_loop(nused_ref[0], n_tiles, drain_tile, 0)
        for e in range(N_EXPERTS):
            n = cnt_ref[e]
            start = base_ref[e] + n
            npad = (te - n % te) % te

            def issue(r, carry):
                _row_copy(zero_scr, 0, xs_ref, start + r, zsem).start()
                return carry

            def drain(r, carry):
                _row_copy(zero_scr, 0, xs_ref, start + r, zsem).wait()
                return carry

            lax.fori_loop(0, npad, issue, 0)
            lax.fori_loop(0, npad, drain, 0)

    def issue(r, carry):
        _row_copy(h_ref, r, xs_ref, pos_ref[2 * r], sem).start()
        _row_copy(h_ref, r, xs_ref, pos_ref[2 * r + 1], sem).start()
        return carry

    lax.fori_loop(0, tm, issue, 0, unroll=DMA_UNROLL)
    for _ in range(TOP_K):
        pltpu.make_async_copy(h_ref, xs_ref.at[pl.ds(0, tm)], sem).wait()


def _dispatch(h, pos_flat, counts, base, n_used, n_tiles, te):
    R = h.shape[0]
    tm = min(ROW_TILE, R)
    kern = functools.partial(_dispatch_kernel, tm=tm, te=te, n_tiles=n_tiles)
    smem = pl.BlockSpec(memory_space=pltpu.SMEM)
    return pl.pallas_call(
        kern,
        grid=(R // tm,),
        in_specs=[
            pl.BlockSpec((TOP_K * tm,), lambda i: (i,), memory_space=pltpu.SMEM),
            smem, smem, smem,
            pl.BlockSpec((tm, D_MODEL), lambda i: (i, 0)),
        ],
        out_specs=pl.BlockSpec(memory_space=pl.ANY),
        out_shape=jax.ShapeDtypeStruct((n_tiles * te, D_MODEL), F32),
        scratch_shapes=[pltpu.VMEM((te, D_MODEL), F32), pltpu.SemaphoreType.DMA(()), pltpu.SemaphoreType.DMA(())],
        compiler_params=pltpu.CompilerParams(dimension_semantics=("arbitrary",), vmem_limit_bytes=VMEM_LIMIT,
                                             has_side_effects=True),
        name="moe_dispatch",
    )(pos_flat, counts, base, n_used, h)


def _expert_ffn_kernel(te_ref, nused_ref, x_ref, w1_ref, w3_ref, w2_ref, y_ref):
    i = pl.program_id(0)

    @pl.when(i < nused_ref[0])
    def _():
        y_ref[...] = _swiglu(x_ref[...].astype(BF16), w1_ref, w3_ref, w2_ref, lead=(0,))

    @pl.when(i >= nused_ref[0])
    def _():
        y_ref[...] = jnp.zeros_like(y_ref)


def _expert_ffn(xs, tile_expert, n_used, w1, w3, w2, te):
    n_rows = xs.shape[0]
    nt = n_rows // te
    wspec = lambda shape: pl.BlockSpec((1,) + shape, lambda i, te_r, nu_r: (te_r[i], 0, 0), pipeline_mode=RESIDENT)
    grid_spec = pltpu.PrefetchScalarGridSpec(
        num_scalar_prefetch=2,
        grid=(nt,),
        in_specs=[
            pl.BlockSpec((te, D_MODEL), lambda i, te_r, nu_r: (i, 0)),
            wspec((D_MODEL, D_FF)), wspec((D_MODEL, D_FF)), wspec((D_FF, D_MODEL)),
        ],
        out_specs=pl.BlockSpec((te, D_MODEL), lambda i, te_r, nu_r: (i, 0)),
    )
    return pl.pallas_call(
        _expert_ffn_kernel,
        grid_spec=grid_spec,
        out_shape=jax.ShapeDtypeStruct((n_rows, D_MODEL), F32),
        compiler_params=_params(("arbitrary",), VMEM_LIMIT_BIG),
        name="moe_expert_swiglu",
    )(tile_expert, n_used, xs, w1, w3, w2)


def _combine_kernel(pos_ref, x_ref, gate_ref, g_ref, y_ref, o_ref, buf0, buf1, sem, *, tm):
    def issue(r, carry):
        _row_copy(y_ref, pos_ref[2 * r], buf0, r, sem).start()
        _row_copy(y_ref, pos_ref[2 * r + 1], buf1, r, sem).start()
        return carry

    lax.fori_loop(0, tm, issue, 0, unroll=DMA_UNROLL)
    for buf in (buf0, buf1):
        pltpu.make_async_copy(y_ref.at[pl.ds(0, tm)], buf, sem).wait()
    gate = gate_ref[...]
    x = x_ref[...] + (buf0[...] * gate[:, 0:1] + buf1[...] * gate[:, 1:2])
    o_ref[...] = x * _rms_scale(x) * g_ref[...]


def _combine(x3, gates, pos_flat, y, g):
    R = x3.shape[0]
    tm = min(ROW_TILE, R)
    kern = functools.partial(_combine_kernel, tm=tm)
    return pl.pallas_call(
        kern,
        grid=(R // tm,),
        in_specs=[
            pl.BlockSpec((TOP_K * tm,), lambda i: (i,), memory_space=pltpu.SMEM),
            pl.BlockSpec((tm, D_MODEL), lambda i: (i, 0)),
            pl.BlockSpec((tm, TOP_K), lambda i: (i, 0)),
            pl.BlockSpec((1, D_MODEL), lambda i: (0, 0)),
            pl.BlockSpec(memory_space=pl.ANY),
        ],
        out_specs=pl.BlockSpec((tm, D_MODEL), lambda i: (i, 0)),
        out_shape=jax.ShapeDtypeStruct((R, D_MODEL), F32),
        scratch_shapes=[pltpu.VMEM((tm, D_MODEL), F32), pltpu.VMEM((tm, D_MODEL), F32),
                        pltpu.SemaphoreType.DMA(())],
        compiler_params=_params(("arbitrary",)),
        name="moe_combine_norm",
    )(pos_flat, x3, gates, g, y)


def _prep_weights(p):
    w_in = p["gdn_w_in"][0]
    w_kv = p["w_kv"]
    kcols = w_kv[:, :HKV * HD].reshape(D_MODEL, HKV, HD)
    vcols = w_kv[:, HKV * HD:].reshape(D_MODEL, HKV, HD)
    dup = lambda a: jnp.concatenate([a, a], axis=2).reshape(D_MODEL, HKV * LANES)
    w_kvx = jnp.concatenate([dup(kcols), dup(vcols), w_kv], axis=1)
    return dict(
        w_qk=w_in[:, :2 * QK_DIM].astype(BF16),
        w_v=w_in[:, 2 * QK_DIM:CONV_CH].astype(BF16),
        w_z=w_in[:, CONV_CH:CONV_CH + V_DIM].astype(BF16),
        w_ba=w_in[:, CONV_CH + V_DIM:].astype(BF16),
        w_out=p["gdn_w_out"][0].astype(BF16),
        ffn_w1=p["ffn_w1"][0].astype(BF16),
        ffn_w3=p["ffn_w3"][0].astype(BF16),
        ffn_w2=p["ffn_w2"][0].astype(BF16),
        w_kvx=w_kvx.astype(BF16),
        w_q=p["swa_w_q"][0].astype(BF16),
        w_o=p["swa_w_o"][0].astype(BF16),
        moe_w1=p["moe_w1"][0].astype(BF16),
        moe_w3=p["moe_w3"][0].astype(BF16),
        moe_w2=p["moe_w2"][0].astype(BF16),
    )


def _dup_heads(a):
    B, T = a.shape[:2]
    return jnp.concatenate([a, a], axis=3).reshape(B * T, HKV * LANES)


def _trunk(x, pos, conv0, s0, win_k, win_v, p, w):
    B, T, _ = x.shape
    R = B * T
    xf = x.reshape(R, D_MODEL)
    row = lambda a: a.reshape(1, -1)

    g_attn = row(p["norm_attn"][0])
    w_conv = p["gdn_w_conv"][0]
    nqk = 2 * QK_DIM
    qk, conv_qk = _qkv_conv(xf, g_attn, w["w_qk"], w_conv[:, :nqk], conv0[:, :, :nqk], B, T, True)
    v, conv_v = _qkv_conv(xf, g_attn, w["w_v"], w_conv[:, nqk:], conv0[:, :, nqk:], B, T, False)
    conv_new = jnp.concatenate([conv_qk, conv_v], axis=-1).reshape(B, -1, CONV_W - 1, CONV_CH)[:, -1]
    z, beta, gc, gct = _zbg(xf, g_attn, w["w_z"], w["w_ba"], p["gdn_a_log"][0], p["gdn_dt_bias"][0], B, T)
    u, wv, att_l = _delta_prep(qk, v, beta, gc, gct, B, T)
    og, s_new = _delta_scan(qk, u, wv, att_l, z, gc, s0, row(p["gdn_o_norm"][0]), B, T)
    x2 = _mixer_out_ffn(xf, og, w["w_out"], row(p["norm_ffn"][0]), w["ffn_w1"], w["ffn_w3"], w["ffn_w2"])

    q, kvd, kvs = _qkv_rope(x2, row(p["norm_attn"][1]), row(p["kv_norm"]), w["w_q"], w["w_kvx"],
                            _rope_tables(pos), B, T)
    k_new = kvs[:, :HKV * HD].reshape(B, T, HKV, HD)
    v_new = kvs[:, HKV * HD:].reshape(B, T, HKV, HD)
    if win_k is None:
        keys, Tk = kvd, T
    else:
        hist = jnp.concatenate([_dup_heads(win_k), _dup_heads(win_v)], axis=1).astype(BF16)
        Tw = win_k.shape[1]
        keys = jnp.concatenate([hist.reshape(B, Tw, KV_DUP), kvd.reshape(B, T, KV_DUP)], axis=1)
        Tk = Tw + T
        keys = keys.reshape(B * Tk, KV_DUP)
    att = _attention(q, keys, row(p["swa_sinks"][0]), B, T, Tk)
    x3, h, eidx, gates, counts = _oproj_router(x2, att, w["w_o"], row(p["norm_ffn"][1]), p["moe_router"][0])

    te = min(ROW_TILE, R)
    n_tiles = TOP_K * R // te + N_EXPERTS
    cnt = counts[0].astype(jnp.int32)
    tiles = (cnt + te - 1) // te
    tile_end = jnp.cumsum(tiles)
    base = (tile_end - tiles) * te
    n_used = tile_end[-1:]
    tile_expert = jnp.minimum(jnp.searchsorted(tile_end, jnp.arange(n_tiles, dtype=jnp.int32), side="right"),
                              N_EXPERTS - 1).astype(jnp.int32)
    pos_slot = _slots(eidx, base.astype(F32).reshape(1, N_EXPERTS)).reshape(-1)
    n_used = n_used.astype(jnp.int32)
    xs = _dispatch(h, pos_slot, cnt, base.astype(jnp.int32), n_used, n_tiles, te)
    ys = _expert_ffn(xs, tile_expert, n_used, w["moe_w1"], w["moe_w3"], w["moe_w2"], te)
    y = _combine(x3, gates, pos_slot, ys, row(p["final_norm"]))
    return (y.reshape(B, T, D_MODEL), conv_new[None], s_new[None], k_new, v_new)


def kernel(x_prompt, x_sample, cache_conv, state_delta, cache_k, cache_v, norm_attn, norm_ffn, gdn_w_in,
           gdn_w_conv, gdn_a_log, gdn_dt_bias, gdn_o_norm, gdn_w_out, kv_norm, w_kv, swa_w_q, swa_sinks, swa_w_o,
           ffn_w1, ffn_w3, ffn_w2, moe_router, moe_w1, moe_w3, moe_w2, final_norm):
    p = dict(norm_attn=norm_attn, norm_ffn=norm_ffn, gdn_w_in=gdn_w_in, gdn_w_conv=gdn_w_conv,
             gdn_a_log=gdn_a_log, gdn_dt_bias=gdn_dt_bias, gdn_o_norm=gdn_o_norm, gdn_w_out=gdn_w_out,
             kv_norm=kv_norm, w_kv=w_kv, swa_w_q=swa_w_q, swa_sinks=swa_sinks, swa_w_o=swa_w_o,
             ffn_w1=ffn_w1, ffn_w3=ffn_w3, ffn_w2=ffn_w2, moe_router=moe_router,
             moe_w1=moe_w1, moe_w3=moe_w3, moe_w2=moe_w2, final_norm=final_norm)
    w = _prep_weights(p)
    Bp, Tp, _ = x_prompt.shape
    Bs, Ts, _ = x_sample.shape
    conv0 = jnp.zeros((Bp, CONV_W - 1, CONV_CH), F32)
    s0 = jnp.zeros((Bp, HV, DK, DV), F32)
    y_p, conv_p, delta_p, k_full, v_full = _trunk(x_prompt, jnp.arange(Tp, dtype=jnp.int32), conv0, s0,
                                                  None, None, p, w)
    win = min(WINDOW, Tp)
    k_p = k_full[:, Tp - win:]
    v_p = v_full[:, Tp - win:]
    pos_s = PAST_LEN + jnp.arange(Ts, dtype=jnp.int32)
    y_s, conv_s, delta_s, k_s, v_s = _trunk(x_sample, pos_s, cache_conv[0], state_delta[0],
                                            cache_k, cache_v, p, w)
    return (y_p, y_s, conv_p, delta_p, k_p, v_p, conv_s, delta_s, k_s, v_s)
```

```python
import functools

import jax
import jax.numpy as jnp
import numpy as np
from jax import lax
from jax.experimental import pallas as pl
from jax.experimental.pallas import tpu as pltpu

F32 = jnp.float32
BF16 = jnp.bfloat16

D_MODEL = 1024
CHUNK = 64
HK = 8
HV = 16
DK = 128
DV = 128
QK_DIM = HK * DK
V_DIM = HV * DV
CONV_CH = 2 * QK_DIM + V_DIM
CONV_W = 4
HQ = 16
HKV = 4
HD = 64
WINDOW = 128
ROT_DIM = HD // 4
ROPE_THETA = 500000.0
D_FF = 2816
N_EXPERTS = 8
TOP_K = 2
EPS = 1e-6
PAST_LEN = 2048

LANES = 128
ROW_TILE = 512
EXPERT_TILE = 512
VMEM_LIMIT = 48 * 1024 * 1024
VMEM_LIMIT_BIG = 56 * 1024 * 1024


def _params(sem, vmem=VMEM_LIMIT):
    return pltpu.CompilerParams(dimension_semantics=sem, vmem_limit_bytes=vmem)


def _rms_scale(x):
    return lax.rsqrt(jnp.mean(x * x, axis=-1, keepdims=True) + EPS)


def _bdot(a, b):
    return jnp.dot(a.astype(BF16), b.astype(BF16), preferred_element_type=F32)


def _bdot_nt(a, b):
    return lax.dot_general(a.astype(BF16), b.astype(BF16), (((1,), (1,)), ((), ())),
                           preferred_element_type=F32)


def _bdot_tn(a, b):
    return lax.dot_general(a.astype(BF16), b.astype(BF16), (((0,), (0,)), ((), ())),
                           preferred_element_type=F32)


def _split3(x):
    x1 = x.astype(BF16)
    r1 = x - x1.astype(F32)
    x2 = r1.astype(BF16)
    x3 = (r1 - x2.astype(F32)).astype(BF16)
    return x1, x2, x3


def _split2(x):
    x1 = x.astype(BF16)
    x2 = (x - x1.astype(F32)).astype(BF16)
    return x1, x2


def _dot_hi(a, b):
    a1, a2 = _split2(a)
    b1, b2 = _split2(b)
    d = functools.partial(jnp.dot, preferred_element_type=F32)
    return d(a1, b1) + (d(a1, b2) + d(a2, b1))


def _spread(x, emat, pieces):
    parts = _split3(x)[:pieces]
    out = jnp.dot(parts[0], emat, preferred_element_type=F32)
    for part in parts[1:]:
        out = out + jnp.dot(part, emat, preferred_element_type=F32)
    return out


def _head_spread_matrices():
    h = np.arange(HV)[:, None]
    full = (np.arange(HV * DV)[None, :] // DV == h)
    pair = (np.arange(HV * CHUNK)[None, :] // CHUNK == h)
    return jnp.asarray(full.astype(np.float32), dtype=BF16), jnp.asarray(pair.astype(np.float32), dtype=BF16)


def _sigmoid(x):
    return 0.5 * jnp.tanh(0.5 * x) + 0.5


def _silu(x):
    half = 0.5 * x
    return half * jnp.tanh(half) + half


CONV_SUB = 256


def _qkv_conv_kernel(x_ref, g_ref, w_ref, wc_ref, cs_ref, out_ref, cst_ref,
                     h_scr, pad_scr, carry_scr, *, tm, tn, l2):
    ti = pl.program_id(1)
    j = pl.program_id(2)

    @pl.when(j == 0)
    def _():
        x = x_ref[...]
        h_scr[...] = (x * _rms_scale(x) * g_ref[...]).astype(BF16)

    @pl.when(ti == 0)
    def _():
        carry_scr[j, 5:8, :] = cs_ref[0]

    h = h_scr[...]
    wc = wc_ref[...]
    scale = jnp.where(j == 0, DK ** -0.5, 1.0).astype(F32)
    cols = lambda k: slice(k * CONV_SUB, (k + 1) * CONV_SUB)

    def project(k):
        return jnp.dot(h, w_ref[:, cols(k)], preferred_element_type=F32)

    def finish(k, proj):
        c = cols(k)
        pad_scr[0:8, c] = carry_scr[j, :, c]
        pad_scr[8:, c] = proj
        acc = proj * wc[3:4, c]
        for s in range(1, CONV_W):
            acc = acc + pad_scr[8 - s:8 - s + tm, c] * wc[3 - s:4 - s, c]
        carry_scr[j, :, c] = pad_scr[tm:tm + 8, c]
        cst_ref[0, :, c] = pad_scr[tm + 5:tm + 8, c]
        y = _silu(acc)
        if not l2:
            out_ref[:, c] = y.astype(out_ref.dtype)
            return
        for hh in range(CONV_SUB // DK):
            seg = y[:, hh * DK:(hh + 1) * DK]
            inv = lax.rsqrt(jnp.sum(seg * seg, axis=-1, keepdims=True) + EPS) * scale
            lanes = slice(k * CONV_SUB + hh * DK, k * CONV_SUB + (hh + 1) * DK)
            out_ref[:, lanes] = (seg * inv).astype(out_ref.dtype)

    pending = project(0)
    for k in range(tn // CONV_SUB):
        upcoming = project(k + 1) if k + 1 < tn // CONV_SUB else None
        finish(k, pending)
        pending = upcoming


def _qkv_conv(x, g, w, w_conv, conv_state, B, T, l2):
    R = B * T
    n = w.shape[1]
    tm = min(ROW_TILE, T)
    tn = QK_DIM
    nt = T // tm
    nj = n // tn
    kern = functools.partial(_qkv_conv_kernel, tm=tm, tn=tn, l2=l2)
    return pl.pallas_call(
        kern,
        grid=(B, nt, nj),
        in_specs=[
            pl.BlockSpec((tm, D_MODEL), lambda b, t, j: (b * nt + t, 0)),
            pl.BlockSpec((1, D_MODEL), lambda b, t, j: (0, 0)),
            pl.BlockSpec((D_MODEL, tn), lambda b, t, j: (0, j)),
            pl.BlockSpec((CONV_W, tn), lambda b, t, j: (0, j)),
            pl.BlockSpec((1, CONV_W - 1, tn), lambda b, t, j: (b, 0, j)),
        ],
        out_specs=[
            pl.BlockSpec((tm, tn), lambda b, t, j: (b * nt + t, j)),
            pl.BlockSpec((1, CONV_W - 1, tn), lambda b, t, j: (b * nt + t, 0, j)),
        ],
        out_shape=[
            jax.ShapeDtypeStruct((R, n), BF16),
            jax.ShapeDtypeStruct((B * nt, CONV_W - 1, n), F32),
        ],
        scratch_shapes=[
            pltpu.VMEM((tm, D_MODEL), BF16),
            pltpu.VMEM((tm + 8, tn), F32),
            pltpu.VMEM((nj, 8, tn), F32),
        ],
        compiler_params=_params(("arbitrary", "arbitrary", "arbitrary")),
        name="gdn_qk_conv" if l2 else "gdn_v_conv",
    )(x, g, w, w_conv, conv_state)


def _softplus(x):
    return jnp.maximum(x, 0.0) + jnp.log(1.0 + jnp.exp(-jnp.abs(x)))


def _zbg_kernel(x_ref, g_ref, w_ref, alog_ref, dtb_ref, tri_ref, z_ref, beta_ref, gc_ref, gct_ref, *, tm):
    x = x_ref[...]
    h = (x * _rms_scale(x) * g_ref[...]).astype(BF16)
    ba = jnp.dot(h, w_ref[:, V_DIM:], preferred_element_type=F32)
    z_ref[...] = jnp.dot(h, w_ref[:, :V_DIM], preferred_element_type=F32).astype(z_ref.dtype)
    beta_ref[...] = _sigmoid(ba[:, :HV])
    gcol = -jnp.exp(alog_ref[...]) * _softplus(ba[:, HV:2 * HV] + dtb_ref[...])
    tri = tri_ref[...]
    g1, g2, g3 = _split3(gcol)
    d = functools.partial(jnp.dot, preferred_element_type=F32)
    gc = d(tri, g1) + (d(tri, g2) + d(tri, g3))
    gc_ref[...] = gc
    r = lax.broadcasted_iota(jnp.int32, (HV, HV), 0)
    c = lax.broadcasted_iota(jnp.int32, (HV, HV), 1)
    sel = (c == jnp.where(r < HV // 2, 2 * r, 2 * r - (HV - 1))).astype(BF16)
    tn = lambda a: lax.dot_general(sel, a, (((1,), (1,)), ((), ())), preferred_element_type=F32)
    c1, c2, c3 = _split3(gc)
    gct = tn(c1) + (tn(c2) + tn(c3))
    for ch in range(tm // CHUNK):
        sl = slice(ch * CHUNK, (ch + 1) * CHUNK)
        gct_ref[0, ch] = jnp.concatenate([gct[:HV // 2, sl], gct[HV // 2:, sl]], axis=1)


def _zbg(x, g, w_z, w_ba, a_log, dt_bias, B, T):
    R = B * T
    tm = min(ROW_TILE, T)
    nt = T // tm
    nc = T // CHUNK
    idx = np.arange(tm)
    tri = jnp.asarray(((idx[:, None] // CHUNK == idx[None, :] // CHUNK) & (idx[:, None] >= idx[None, :]))
                      .astype(np.float32), dtype=BF16)
    w_zba = jnp.concatenate([w_z, w_ba, jnp.zeros((D_MODEL, LANES - 2 * HV), BF16)], axis=1)
    full = lambda shape: pl.BlockSpec(shape, lambda b, t: (0,) * len(shape))
    kern = functools.partial(_zbg_kernel, tm=tm)
    return pl.pallas_call(
        kern,
        grid=(B, nt),
        in_specs=[
            pl.BlockSpec((tm, D_MODEL), lambda b, t: (b * nt + t, 0)),
            full((1, D_MODEL)),
            full((D_MODEL, V_DIM + LANES)),
            full((1, HV)), full((1, HV)),
            full((tm, tm)),
        ],
        out_specs=[
            pl.BlockSpec((tm, V_DIM), lambda b, t: (b * nt + t, 0)),
            pl.BlockSpec((tm, HV), lambda b, t: (b * nt + t, 0)),
            pl.BlockSpec((tm, HV), lambda b, t: (b * nt + t, 0)),
            pl.BlockSpec((1, tm // CHUNK, HV // 2, 2 * CHUNK), lambda b, t: (b, t, 0, 0)),
        ],
        out_shape=[
            jax.ShapeDtypeStruct((R, V_DIM), BF16),
            jax.ShapeDtypeStruct((R, HV), F32),
            jax.ShapeDtypeStruct((R, HV), F32),
            jax.ShapeDtypeStruct((B, nc, HV // 2, 2 * CHUNK), F32),
        ],
        compiler_params=_params(("arbitrary", "arbitrary")),
        name="gdn_z_beta_decay",
    )(x, g, w_zba, a_log.reshape(1, HV), dt_bias.reshape(1, HV), tri)


PAIR = HV // HK


def _pair_rows(x, lo):
    zero = jnp.zeros((), x.dtype)
    return jnp.concatenate([jnp.where(lo, x, zero), jnp.where(lo, zero, x)], axis=0)


def _pair_products(lhs_list, rhs_list, lo):
    def split(a):
        hi = a.astype(BF16)
        return hi, (a - hi.astype(F32)).astype(BF16)

    ops = []
    for lhs, rhs in zip(lhs_list, rhs_list):
        lh, ll = split(lhs)
        rh, rl = split(rhs)
        rh2 = _pair_rows(rh, lo)
        ops.append((jnp.concatenate([lh, lh, ll], axis=1), jnp.concatenate([rh2, _pair_rows(rl, lo), rh2], axis=0)))
    return [jnp.dot(a, b, preferred_element_type=F32) for a, b in ops]


def _pair_unit_lower_inverse(low_list, lo, eye2):
    C = low_list[0].shape[0]
    xs = [-low for low in low_list]
    accs = [eye2 + x for x in xs]
    pws = _pair_products(xs, xs, lo)
    n = 2
    while 2 * n < C:
        prods = _pair_products([jnp.concatenate([a, p], axis=0) for a, p in zip(accs, pws)], pws, lo)
        accs = [a + pr[:C] for a, pr in zip(accs, prods)]
        pws = [pr[C:] for pr in prods]
        n *= 2
    prods = _pair_products(accs, pws, lo)
    return [a + pr for a, pr in zip(accs, prods)]


def _delta_prep_kernel(q_ref, k_ref, v_ref, beta_ref, gc_ref, gct_ref, epair_ref, u_ref, w_ref, a_ref, *, cb):
    C = CHUNK
    ri = lax.broadcasted_iota(jnp.int32, (C, 2 * C), 0)
    li = lax.broadcasted_iota(jnp.int32, (C, 2 * C), 1)
    lm = li & (C - 1)
    lo = li < C
    incl = ri >= lm
    strict = ri > lm
    eye2 = (ri == lm).astype(F32)
    units = [(c, p) for c in range(cb) for p in range(HK)]
    rows = lambda c: slice(c * C, (c + 1) * C)

    grams = []
    for c, p in units:
        kb = k_ref[rows(c), p * DK:(p + 1) * DK]
        qb = q_ref[rows(c), p * DK:(p + 1) * DK]
        grams.append(lax.dot_general(jnp.concatenate([kb, qb], axis=0), jnp.concatenate([kb, kb], axis=0),
                                     (((1,), (1,)), ((), ())), preferred_element_type=F32))

    epair = epair_ref[...]
    g2s, b2s, betas, begs = [], [], [], []
    for c in range(cb):
        beta = beta_ref[rows(c), :]
        gc = gc_ref[rows(c), :]
        g2s.append(_spread(gc, epair, 3))
        b2s.append(_spread(beta, epair, 2))
        betas.append(beta)
        begs.append(beta * jnp.exp(gc))

    lows = []
    for (c, p), gram in zip(units, grams):
        pt = slice(p * 2 * C, (p + 1) * 2 * C)
        grow2 = gct_ref[c, p:p + 1, :]
        decay2 = jnp.exp(jnp.where(incl, g2s[c][:, pt] - grow2, -jnp.inf))
        a_ref[rows(c), pt] = (gram[C:] * decay2).astype(a_ref.dtype)
        lows.append(jnp.where(strict, b2s[c][:, pt] * gram[:C] * decay2, 0.0))

    tinvs = _pair_unit_lower_inverse(lows, lo, eye2)

    ops = []
    for (c, p), tinv in zip(units, tinvs):
        kf = k_ref[rows(c), p * DK:(p + 1) * DK].astype(F32)
        rhs = []
        for h in (PAIR * p, PAIR * p + 1):
            ht = slice(h * DV, (h + 1) * DV)
            vf = v_ref[rows(c), ht].astype(F32)
            rhs.append(jnp.concatenate([vf * betas[c][:, h:h + 1], kf * begs[c][:, h:h + 1]], axis=1))
        ops.append((_pair_rows(tinv.astype(BF16), lo), jnp.concatenate(rhs, axis=0).astype(BF16)))
    for (c, p), (lhs, rhs) in zip(units, ops):
        uw = jnp.dot(lhs, rhs, preferred_element_type=F32)
        for i, h in enumerate((PAIR * p, PAIR * p + 1)):
            u_ref[rows(c), h * DV:(h + 1) * DV] = uw[i * C:(i + 1) * C, :DV].astype(u_ref.dtype)
            w_ref[rows(c), h * DV:(h + 1) * DV] = uw[i * C:(i + 1) * C, DV:].astype(w_ref.dtype)


def _delta_prep(qk, v, beta, gc, gct, B, T):
    R = B * T
    cb = 2 if T % (2 * CHUNK) == 0 else 1
    rb = cb * CHUNK
    gct2 = gct.reshape(R // CHUNK, HK, 2 * CHUNK)
    kern = functools.partial(_delta_prep_kernel, cb=cb)
    return pl.pallas_call(
        kern,
        grid=(R // rb,),
        in_specs=[
            pl.BlockSpec((rb, QK_DIM), lambda i: (i, 0)),
            pl.BlockSpec((rb, QK_DIM), lambda i: (i, 1)),
            pl.BlockSpec((rb, V_DIM), lambda i: (i, 0)),
            pl.BlockSpec((rb, HV), lambda i: (i, 0)),
            pl.BlockSpec((rb, HV), lambda i: (i, 0)),
            pl.BlockSpec((cb, HK, 2 * CHUNK), lambda i: (i, 0, 0)),
            pl.BlockSpec((HV, HV * CHUNK), lambda i: (0, 0)),
        ],
        out_specs=[
            pl.BlockSpec((rb, V_DIM), lambda i: (i, 0)),
            pl.BlockSpec((rb, V_DIM), lambda i: (i, 0)),
            pl.BlockSpec((rb, HV * CHUNK), lambda i: (i, 0)),
        ],
        out_shape=[
            jax.ShapeDtypeStruct((R, V_DIM), BF16),
            jax.ShapeDtypeStruct((R, V_DIM), BF16),
            jax.ShapeDtypeStruct((R, HV * CHUNK), BF16),
        ],
        compiler_params=_params(("arbitrary",)),
        name="gdn_delta_prep",
    )(qk, qk, v, beta, gc, gct2, _head_spread_matrices()[1])


def _delta_scan_kernel(q_ref, k_ref, u_ref, w_ref, a_ref, z_ref, gc_ref, s0_ref, on_ref, efull_ref,
                       og_ref, s_ref, *, bb):
    c = pl.program_id(1)

    @pl.when(c == 0)
    def _():
        s_ref[...] = s0_ref[...]

    C = CHUNK
    lo = lax.broadcasted_iota(jnp.int32, (C, 2 * C), 1) < C
    onorm = on_ref[...]
    units = [(b, h) for b in range(bb) for h in range(HV)]
    pairs = [(b, p) for b in range(bb) for p in range(HK)]

    efull = efull_ref[...]
    egs, kgs, dgs = [], [], []
    for b in range(bb):
        gc = gc_ref[b]
        g_last = gc[C - 1:C, :]
        egs.append(_spread(jnp.exp(gc), efull, 2))
        kgs.append(_spread(jnp.exp(g_last - gc), efull, 2))
        dgs.append(_spread(jnp.exp(gc[C - 8:C, :]), efull, 3)[7:8, :])

    ops = []
    kds = []
    dec = []
    for b, h in units:
        p = h // PAIR
        ht = slice(h * DV, (h + 1) * DV)
        qg = (q_ref[b, :, p * DK:(p + 1) * DK].astype(F32) * egs[b][:, ht]).astype(BF16)
        kds.append((k_ref[b, :, p * DK:(p + 1) * DK].astype(F32) * kgs[b][:, ht]).astype(BF16))
        dec.append(dgs[b][:, ht])
        ops.append((jnp.concatenate([w_ref[b, :, ht], qg], axis=0), s_ref[b, h].astype(BF16)))
    wqs = [jnp.dot(a, s, preferred_element_type=F32) for a, s in ops]

    vns = [(u_ref[b, :, h * DV:(h + 1) * DV].astype(F32) - wq[:C]).astype(BF16) for (b, h), wq in zip(units, wqs)]

    oparts = []
    for i, (b, p) in enumerate(pairs):
        att = _pair_rows(a_ref[b, :, p * 2 * C:(p + 1) * 2 * C], lo)
        vn2 = jnp.concatenate([vns[PAIR * i], vns[PAIR * i + 1]], axis=0)
        oparts.append(jnp.dot(att, vn2, preferred_element_type=F32))
    sds = [lax.dot_general(kd, vn, (((0,), (0,)), ((), ())), preferred_element_type=F32)
           for kd, vn in zip(kds, vns)]

    for i, (b, h) in enumerate(units):
        s_ref[b, h] = s_ref[b, h] * dec[i] + sds[i]
        o = wqs[i][C:] + oparts[i // PAIR][(h % PAIR) * C:(h % PAIR + 1) * C]
        on = o * _rms_scale(o) * onorm
        zf = z_ref[b, :, h * DV:(h + 1) * DV].astype(F32)
        og_ref[b, :, h * DV:(h + 1) * DV] = (on * _silu(zf)).astype(og_ref.dtype)


def _delta_scan(qk, u, w, att, z, gc, s0, o_norm, B, T):
    nc = T // CHUNK
    bb = 2 if B % 2 == 0 else 1
    v3 = lambda a: a.reshape(B, T, a.shape[-1])
    blk = lambda n, j=0: pl.BlockSpec((bb, CHUNK, n), lambda b, c: (b, c, j))
    kern = functools.partial(_delta_scan_kernel, bb=bb)
    og, s_new = pl.pallas_call(
        kern,
        grid=(B // bb, nc),
        in_specs=[
            blk(QK_DIM, 0), blk(QK_DIM, 1), blk(V_DIM), blk(V_DIM), blk(HV * CHUNK), blk(V_DIM), blk(HV),
            pl.BlockSpec((bb, HV, DK, DV), lambda b, c: (b, 0, 0, 0)),
            pl.BlockSpec((1, DV), lambda b, c: (0, 0)),
            pl.BlockSpec((HV, HV * DV), lambda b, c: (0, 0)),
        ],
        out_specs=[
            blk(V_DIM),
            pl.BlockSpec((bb, HV, DK, DV), lambda b, c: (b, 0, 0, 0)),
        ],
        out_shape=[
            jax.ShapeDtypeStruct((B, T, V_DIM), BF16),
            jax.ShapeDtypeStruct((B, HV, DK, DV), F32),
        ],
        compiler_params=_params(("arbitrary", "arbitrary")),
        name="gdn_delta_scan",
    )(v3(qk), v3(qk), v3(u), v3(w), v3(att), v3(z), v3(gc), s0, o_norm, _head_spread_matrices()[0])
    return og.reshape(B * T, V_DIM), s_new


FF_TILE = 1408
RESIDENT = pl.Buffered(1)


def _swiglu(h, w1_ref, w3_ref, w2_ref, lead=()):
    out = None
    for f in range(D_FF // FF_TILE):
        sl = slice(f * FF_TILE, (f + 1) * FF_TILE)
        a = jnp.dot(h, w1_ref[lead + (slice(None), sl)], preferred_element_type=F32)
        b = jnp.dot(h, w3_ref[lead + (slice(None), sl)], preferred_element_type=F32)
        mid = (_silu(a) * b).astype(BF16)
        part = jnp.dot(mid, w2_ref[lead + (sl, slice(None))], preferred_element_type=F32)
        out = part if out is None else out + part
    return out


def _mixer_out_ffn_kernel(x_ref, a_ref, wo_ref, g_ref, w1_ref, w3_ref, w2_ref, o_ref):
    x1 = x_ref[...] + jnp.dot(a_ref[...], wo_ref[...], preferred_element_type=F32)
    h = (x1 * _rms_scale(x1) * g_ref[...]).astype(BF16)
    o_ref[...] = x1 + _swiglu(h, w1_ref, w3_ref, w2_ref)


def _mixer_out_ffn(x, a, w_out, g, w1, w3, w2):
    R = x.shape[0]
    K = a.shape[1]
    tm = min(ROW_TILE, R)
    held = lambda shape: pl.BlockSpec(shape, lambda i: (0,) * len(shape), pipeline_mode=RESIDENT)
    return pl.pallas_call(
        _mixer_out_ffn_kernel,
        grid=(R // tm,),
        in_specs=[
            pl.BlockSpec((tm, D_MODEL), lambda i: (i, 0)),
            pl.BlockSpec((tm, K), lambda i: (i, 0)),
            held((K, D_MODEL)),
            held((1, D_MODEL)),
            held((D_MODEL, D_FF)), held((D_MODEL, D_FF)), held((D_FF, D_MODEL)),
        ],
        out_specs=pl.BlockSpec((tm, D_MODEL), lambda i: (i, 0)),
        out_shape=jax.ShapeDtypeStruct((R, D_MODEL), F32),
        compiler_params=_params(("arbitrary",), VMEM_LIMIT_BIG),
        name="gdn_out_dense_swiglu",
    )(x, a, w_out, g, w1, w3, w2)


KV_DUP = 2 * HKV * LANES
KV_STD = 2 * HKV * HD


def _rope(x, cos, sm, sp):
    up = pltpu.roll(x, LANES - ROT_DIM // 2, 1)
    dn = pltpu.roll(x, ROT_DIM // 2, 1)
    return x * cos + up * sm + dn * sp


def _qkv_rope_kernel(x_ref, gq_ref, gkv_ref, wq_ref, wkv_ref, wkvs_ref, cos_ref, sm_ref, sp_ref,
                     q_ref, kvd_ref, kvs_ref, *, nt, keep):
    x = x_ref[...]
    xr = x * _rms_scale(x)
    hq = (xr * gq_ref[...]).astype(BF16)
    hkv = (xr * gkv_ref[...]).astype(BF16)
    cos = cos_ref[...]
    sm = sm_ref[...]
    sp = sp_ref[...]
    q = jnp.dot(hq, wq_ref[...], preferred_element_type=F32)
    for t in range(HQ * HD // LANES):
        sl = slice(t * LANES, (t + 1) * LANES)
        q_ref[:, sl] = (_rope(q[:, sl], cos, sm, sp) * (HD ** -0.5)).astype(q_ref.dtype)
    kv = jnp.dot(hkv, wkv_ref[...], preferred_element_type=F32)
    half = KV_DUP // 2
    for t in range(HKV):
        sl = slice(t * LANES, (t + 1) * LANES)
        kvd_ref[:, sl] = _rope(kv[:, sl], cos, sm, sp).astype(kvd_ref.dtype)
    kvd_ref[:, half:] = kv[:, half:].astype(kvd_ref.dtype)

    @pl.when(pl.program_id(0) % nt == nt - 1)
    def _():
        tm = x.shape[0]
        rows = slice(tm - keep, tm)
        kvs = jnp.dot(hkv[rows], wkvs_ref[...], preferred_element_type=F32)
        for t in range(HKV * HD // LANES):
            sl = slice(t * LANES, (t + 1) * LANES)
            kvs_ref[:, sl] = _rope(kvs[:, sl], cos[rows], sm[rows], sp[rows])
        kvs_ref[:, HKV * HD:] = kvs[:, HKV * HD:]


def _rope_tables(pos):
    half = ROT_DIM // 2
    inv = jnp.power(ROPE_THETA, -jnp.arange(half, dtype=F32) * 2.0 / ROT_DIM)
    ang = pos.astype(F32)[:, None] * inv[None, :]
    cos = jnp.cos(ang)
    sin = jnp.sin(ang)
    T = pos.shape[0]
    one = jnp.ones((T, HD - ROT_DIM), F32)
    zero = jnp.zeros((T, HD - ROT_DIM), F32)
    zh = jnp.zeros((T, half), F32)
    c64 = jnp.concatenate([cos, cos, one], axis=1)
    sm64 = jnp.concatenate([-sin, zh, zero], axis=1)
    sp64 = jnp.concatenate([zh, sin, zero], axis=1)
    dup = lambda a: jnp.concatenate([a, a], axis=1)
    return dup(c64), dup(sm64), dup(sp64)


def _qkv_rope(x, gq, gkv, w_q, w_kvd, w_kvs, tables, B, T, keep):
    R = B * T
    tm = min(ROW_TILE, T)
    nt = T // tm
    assert keep <= tm
    tab = pl.BlockSpec((tm, LANES), lambda i: (i % nt, 0))
    kern = functools.partial(_qkv_rope_kernel, nt=nt, keep=keep)
    return pl.pallas_call(
        kern,
        grid=(R // tm,),
        in_specs=[
            pl.BlockSpec((tm, D_MODEL), lambda i: (i, 0)),
            pl.BlockSpec((1, D_MODEL), lambda i: (0, 0)),
            pl.BlockSpec((1, D_MODEL), lambda i: (0, 0)),
            pl.BlockSpec((D_MODEL, HQ * HD), lambda i: (0, 0)),
            pl.BlockSpec((D_MODEL, KV_DUP), lambda i: (0, 0)),
            pl.BlockSpec((D_MODEL, KV_STD), lambda i: (0, 0)),
            tab, tab, tab,
        ],
        out_specs=[
            pl.BlockSpec((tm, HQ * HD), lambda i: (i, 0)),
            pl.BlockSpec((tm, KV_DUP), lambda i: (i, 0)),
            pl.BlockSpec((keep, KV_STD), lambda i: (i // nt, 0)),
        ],
        out_shape=[
            jax.ShapeDtypeStruct((R, HQ * HD), BF16),
            jax.ShapeDtypeStruct((R, KV_DUP), BF16),
            jax.ShapeDtypeStruct((B * keep, KV_STD), F32),
        ],
        compiler_params=_params(("arbitrary",)),
        name="swa_qkv_rope",
    )(x, gq, gkv, w_q, w_kvd, w_kvs, *tables)


KEY_CHUNKS = WINDOW // CHUNK + 1
KEY_PAD = 256


def _attn_kernel(*refs, off, cq):
    q_ref = refs[0]
    krefs = refs[1:KEY_CHUNKS + cq]
    sink_ref, ones_ref, o_ref = refs[KEY_CHUNKS + cq:]
    c = pl.program_id(1)
    lane = lax.broadcasted_iota(jnp.int32, (KEY_CHUNKS * CHUNK, LANES), 1)
    lo = lane < HD
    col = lax.broadcasted_iota(jnp.int32, (1, KEY_PAD), 1)
    sink_col = col == KEY_CHUNKS * CHUNK
    zpad = jnp.zeros((KEY_PAD - KEY_CHUNKS * CHUNK, LANES), BF16)
    zero = jnp.zeros((), BF16)
    tiles = HQ // HKV // 2
    ones_blk = ones_ref[...]
    units = [(u, g) for u in range(cq) for g in range(HKV)]

    vblks = []
    scores = []
    for u, g in units:
        kr = krefs[u:u + KEY_CHUNKS]
        kg = jnp.concatenate([r[:, g * LANES:(g + 1) * LANES] for r in kr], axis=0)
        vg = jnp.concatenate([r[:, KV_DUP // 2 + g * LANES:KV_DUP // 2 + (g + 1) * LANES] for r in kr], axis=0)
        kblk = jnp.concatenate([jnp.where(lo, kg, zero), zpad, jnp.where(lo, zero, kg), zpad], axis=0)
        vblk = jnp.concatenate([jnp.where(lo, vg, zero), zpad, jnp.where(lo, zero, vg), zpad], axis=0)
        vblks.append(jnp.concatenate([vblk, ones_blk], axis=1))
        qg = jnp.concatenate([q_ref[u * CHUNK:(u + 1) * CHUNK, (g * tiles + p) * LANES:(g * tiles + p + 1) * LANES]
                              for p in range(tiles)], axis=0)
        scores.append(lax.dot_general(qg, kblk, (((1,), (1,)), ((), ())), preferred_element_type=F32))

    probs = []
    for (u, g), sc in zip(units, scores):
        first = c * cq + u + off - (KEY_CHUNKS - 1)
        valid = (col < KEY_CHUNKS * CHUNK) & (col // CHUNK + first >= 0)
        base_bias = jnp.where(valid, 0.0, -jnp.inf).astype(F32)
        rows = []
        for p in range(tiles):
            t = g * tiles + p
            ps = []
            for hh in range(2):
                bias = jnp.where(sink_col, sink_ref[:, 2 * t + hh:2 * t + hh + 1], base_bias)
                sh = sc[p * CHUNK:(p + 1) * CHUNK, hh * KEY_PAD:(hh + 1) * KEY_PAD] + bias
                m = jnp.max(sh, axis=-1, keepdims=True)
                ps.append(jnp.exp(sh - m).astype(BF16))
            rows.append(jnp.concatenate(ps, axis=1))
        probs.append(jnp.concatenate(rows, axis=0))

    outs = [jnp.dot(pr, vb, preferred_element_type=F32) for pr, vb in zip(probs, vblks)]
    for (u, g), out in zip(units, outs):
        for p in range(tiles):
            t = g * tiles + p
            rows = slice(p * CHUNK, (p + 1) * CHUNK)
            o_ref[u * CHUNK:(u + 1) * CHUNK, t * LANES:(t + 1) * LANES] = (
                out[rows, :LANES] / out[rows, LANES:]).astype(o_ref.dtype)


def _attention(q, kvd, sinks, B, Tq, Tk):
    ncq = Tq // CHUNK
    nck = Tk // CHUNK
    off = nck - ncq
    cq = 2 if ncq % 2 == 0 else 1
    kern = functools.partial(_attn_kernel, off=off, cq=cq)
    blk_row = np.arange(2 * KEY_PAD)[:, None] // KEY_PAD
    blk_lane = np.arange(LANES)[None, :] // HD
    ones_blk = jnp.asarray((blk_row == blk_lane).astype(np.float32), dtype=BF16)

    def kspec(i):
        return pl.BlockSpec((CHUNK, KV_DUP),
                            lambda b, c: (b * nck + jnp.maximum(c * cq + off - (KEY_CHUNKS - 1) + i, 0), 0))

    qspec = pl.BlockSpec((cq * CHUNK, HQ * HD), lambda b, c: (b * (ncq // cq) + c, 0))
    return pl.pallas_call(
        kern,
        grid=(B, ncq // cq),
        in_specs=[qspec] + [kspec(i) for i in range(KEY_CHUNKS + cq - 1)] + [
            pl.BlockSpec((1, HQ), lambda b, c: (0, 0)),
            pl.BlockSpec((2 * KEY_PAD, LANES), lambda b, c: (0, 0)),
        ],
        out_specs=qspec,
        out_shape=jax.ShapeDtypeStruct((B * Tq, HQ * HD), BF16),
        compiler_params=_params(("arbitrary", "arbitrary")),
        name="swa_attention",
    )(q, *([kvd] * (KEY_CHUNKS + cq - 1)), sinks, ones_blk)


def _oproj_router_kernel(x_ref, a_ref, w_ref, g_ref, wr_ref, x3_ref, h_ref, e_ref, gate_ref, cnt_ref):
    i = pl.program_id(0)
    x3 = x_ref[...] + jnp.dot(a_ref[...], w_ref[...], preferred_element_type=F32)
    x3_ref[...] = x3
    h = x3 * _rms_scale(x3) * g_ref[...]
    h_ref[...] = h
    h1, h2 = _split2(h)
    r1, r2 = _split2(wr_ref[...])
    part = jnp.dot(h1, jnp.concatenate([r1, r2], axis=1), preferred_element_type=F32)
    logits = part[:, :N_EXPERTS] + (part[:, N_EXPERTS:] + jnp.dot(h2, r1, preferred_element_type=F32))
    eidx = lax.broadcasted_iota(jnp.int32, logits.shape, 1)
    m1 = jnp.max(logits, axis=-1, keepdims=True)
    i1 = jnp.min(jnp.where(logits == m1, eidx, N_EXPERTS), axis=-1, keepdims=True)
    rest = jnp.where(eidx == i1, -jnp.inf, logits)
    m2 = jnp.max(rest, axis=-1, keepdims=True)
    i2 = jnp.min(jnp.where(rest == m2, eidx, N_EXPERTS), axis=-1, keepdims=True)
    e2 = jnp.exp(m2 - m1)
    den = 1.0 + e2
    e_ref[...] = jnp.concatenate([i1, i2], axis=1)
    gate_ref[...] = jnp.concatenate([1.0 / den, e2 / den], axis=1)
    hot = ((eidx == i1) | (eidx == i2)).astype(F32)

    @pl.when(i == 0)
    def _():
        cnt_ref[...] = jnp.zeros_like(cnt_ref)

    cnt_ref[...] += jnp.sum(hot, axis=0, keepdims=True)


def _oproj_router(x, a, w_o, g, w_r):
    R = x.shape[0]
    tm = min(ROW_TILE, R)
    row = lambda n: pl.BlockSpec((tm, n), lambda i: (i, 0))
    return pl.pallas_call(
        _oproj_router_kernel,
        grid=(R // tm,),
        in_specs=[
            row(D_MODEL), row(HQ * HD),
            pl.BlockSpec((HQ * HD, D_MODEL), lambda i: (0, 0)),
            pl.BlockSpec((1, D_MODEL), lambda i: (0, 0)),
            pl.BlockSpec((D_MODEL, N_EXPERTS), lambda i: (0, 0)),
        ],
        out_specs=[row(D_MODEL), row(D_MODEL), row(TOP_K), row(TOP_K),
                   pl.BlockSpec((1, N_EXPERTS), lambda i: (0, 0))],
        out_shape=[
            jax.ShapeDtypeStruct((R, D_MODEL), F32),
            jax.ShapeDtypeStruct((R, D_MODEL), F32),
            jax.ShapeDtypeStruct((R, TOP_K), jnp.int32),
            jax.ShapeDtypeStruct((R, TOP_K), F32),
            jax.ShapeDtypeStruct((1, N_EXPERTS), F32),
        ],
        compiler_params=_params(("arbitrary",)),
        name="swa_oproj_router",
    )(x, a, w_o, g, w_r)


def _slot_kernel(e_ref, base_ref, tri_ref, pos_ref, run_scr):
    i = pl.program_id(0)

    @pl.when(i == 0)
    def _():
        run_scr[...] = jnp.zeros_like(run_scr)

    e = e_ref[...]
    tm = e.shape[0]
    eidx = lax.broadcasted_iota(jnp.int32, (tm, N_EXPERTS), 1)
    hot0 = eidx == e[:, 0:1]
    hot1 = eidx == e[:, 1:2]
    hot = (hot0 | hot1).astype(BF16)
    before = jnp.dot(tri_ref[...], hot, preferred_element_type=F32)
    dest = before + run_scr[...] + base_ref[...]
    p0 = jnp.sum(jnp.where(hot0, dest, 0.0), axis=-1, keepdims=True)
    p1 = jnp.sum(jnp.where(hot1, dest, 0.0), axis=-1, keepdims=True)
    pos_ref[...] = jnp.concatenate([p0, p1], axis=1).astype(jnp.int32)
    run_scr[...] += jnp.sum(hot.astype(F32), axis=0, keepdims=True)


def _slots(eidx, base):
    R = eidx.shape[0]
    tm = min(ROW_TILE, R)
    idx = np.arange(tm)
    tri = jnp.asarray((idx[:, None] > idx[None, :]).astype(np.float32), dtype=BF16)
    return pl.pallas_call(
        _slot_kernel,
        grid=(R // tm,),
        in_specs=[
            pl.BlockSpec((tm, TOP_K), lambda i: (i, 0)),
            pl.BlockSpec((1, N_EXPERTS), lambda i: (0, 0)),
            pl.BlockSpec((tm, tm), lambda i: (0, 0)),
        ],
        out_specs=pl.BlockSpec((tm, TOP_K), lambda i: (i, 0)),
        out_shape=jax.ShapeDtypeStruct((R, TOP_K), jnp.int32),
        scratch_shapes=[pltpu.VMEM((1, N_EXPERTS), F32)],
        compiler_params=_params(("arbitrary",)),
        name="moe_slots",
    )(eidx, base, tri)


DMA_UNROLL = 8


def _row_copy(src, i, dst, j, sem):
    return pltpu.make_async_copy(src.at[pl.ds(i, 1)], dst.at[pl.ds(j, 1)], sem)


def _dispatch_kernel(pos_ref, cnt_ref, base_ref, nused_ref, h_ref, xs_ref, zero_scr, sem, zsem, *, tm, te, n_tiles):
    i = pl.program_id(0)

    @pl.when(i == 0)
    def _():
        zero_scr[...] = jnp.zeros_like(zero_scr)

        def tile_copy(t):
            dst = xs_ref.at[pl.ds(pl.multiple_of(t * te, te), te)]
            return pltpu.make_async_copy(zero_scr, dst, zsem)

        def issue_tile(t, carry):
            tile_copy(t).start()
            return carry

        def drain_tile(t, carry):
            tile_copy(t).wait()
            return carry

        lax.fori_loop(nused_ref[0], n_tiles, issue_tile, 0)
        lax.fori_loop(nused_ref[0], n_tiles, drain_tile, 0)
        for e in range(N_EXPERTS):
            n = cnt_ref[e]
            start = base_ref[e] + n
            npad = (te - n % te) % te

            def issue(r, carry):
                _row_copy(zero_scr, 0, xs_ref, start + r, zsem).start()
                return carry

            def drain(r, carry):
                _row_copy(zero_scr, 0, xs_ref, start + r, zsem).wait()
                return carry

            lax.fori_loop(0, npad, issue, 0)
            lax.fori_loop(0, npad, drain, 0)

    def issue(r, carry):
        _row_copy(h_ref, r, xs_ref, pos_ref[2 * r], sem).start()
        _row_copy(h_ref, r, xs_ref, pos_ref[2 * r + 1], sem).start()
        return carry

    lax.fori_loop(0, tm, issue, 0, unroll=DMA_UNROLL)
    for _ in range(TOP_K):
        pltpu.make_async_copy(h_ref, xs_ref.at[pl.ds(0, tm)], sem).wait()


def _dispatch(h, pos_flat, counts, base, n_used, n_tiles, te):
    R = h.shape[0]
    tm = min(ROW_TILE, R)
    kern = functools.partial(_dispatch_kernel, tm=tm, te=te, n_tiles=n_tiles)
    smem = pl.BlockSpec(memory_space=pltpu.SMEM)
    return pl.pallas_call(
        kern,
        grid=(R // tm,),
        in_specs=[
            pl.BlockSpec((TOP_K * tm,), lambda i: (i,), memory_space=pltpu.SMEM),
            smem, smem, smem,
            pl.BlockSpec((tm, D_MODEL), lambda i: (i, 0)),
        ],
        out_specs=pl.BlockSpec(memory_space=pl.ANY),
        out_shape=jax.ShapeDtypeStruct((n_tiles * te, D_MODEL), F32),
        scratch_shapes=[pltpu.VMEM((te, D_MODEL), F32), pltpu.SemaphoreType.DMA(()), pltpu.SemaphoreType.DMA(())],
        compiler_params=pltpu.CompilerParams(dimension_semantics=("arbitrary",), vmem_limit_bytes=VMEM_LIMIT,
                                             has_side_effects=True),
        name="moe_dispatch",
    )(pos_flat, counts, base, n_used, h)


def _expert_ffn_kernel(te_ref, nused_ref, x_ref, w1_ref, w3_ref, w2_ref, y_ref):
    i = pl.program_id(0)

    @pl.when(i < nused_ref[0])
    def _():
        y_ref[...] = _swiglu(x_ref[...].astype(BF16), w1_ref, w3_ref, w2_ref, lead=(0,))

    @pl.when(i >= nused_ref[0])
    def _():
        y_ref[...] = jnp.zeros_like(y_ref)


def _expert_ffn(xs, tile_expert, n_used, w1, w3, w2, te):
    n_rows = xs.shape[0]
    nt = n_rows // te
    wspec = lambda shape: pl.BlockSpec((1,) + shape, lambda i, te_r, nu_r: (te_r[i], 0, 0), pipeline_mode=RESIDENT)
    grid_spec = pltpu.PrefetchScalarGridSpec(
        num_scalar_prefetch=2,
        grid=(nt,),
        in_specs=[
            pl.BlockSpec((te, D_MODEL), lambda i, te_r, nu_r: (i, 0)),
            wspec((D_MODEL, D_FF)), wspec((D_MODEL, D_FF)), wspec((D_FF, D_MODEL)),
        ],
        out_specs=pl.BlockSpec((te, D_MODEL), lambda i, te_r, nu_r: (i, 0)),
    )
    return pl.pallas_call(
        _expert_ffn_kernel,
        grid_spec=grid_spec,
        out_shape=jax.ShapeDtypeStruct((n_rows, D_MODEL), F32),
        compiler_params=_params(("arbitrary",), VMEM_LIMIT_BIG),
        name="moe_expert_swiglu",
    )(tile_expert, n_used, xs, w1, w3, w2)


def _combine_kernel(pos_ref, x_ref, gate_ref, g_ref, y_ref, o_ref, buf0, buf1, sem, *, tm):
    def issue(r, carry):
        _row_copy(y_ref, pos_ref[2 * r], buf0, r, sem).start()
        _row_copy(y_ref, pos_ref[2 * r + 1], buf1, r, sem).start()
        return carry

    lax.fori_loop(0, tm, issue, 0, unroll=DMA_UNROLL)
    for buf in (buf0, buf1):
        pltpu.make_async_copy(y_ref.at[pl.ds(0, tm)], buf, sem).wait()
    gate = gate_ref[...]
    x = x_ref[...] + (buf0[...] * gate[:, 0:1] + buf1[...] * gate[:, 1:2])
    o_ref[...] = x * _rms_scale(x) * g_ref[...]


def _combine(x3, gates, pos_flat, y, g):
    R = x3.shape[0]
    tm = min(ROW_TILE, R)
    kern = functools.partial(_combine_kernel, tm=tm)
    return pl.pallas_call(
        kern,
        grid=(R // tm,),
        in_specs=[
            pl.BlockSpec((TOP_K * tm,), lambda i: (i,), memory_space=pltpu.SMEM),
            pl.BlockSpec((tm, D_MODEL), lambda i: (i, 0)),
            pl.BlockSpec((tm, TOP_K), lambda i: (i, 0)),
            pl.BlockSpec((1, D_MODEL), lambda i: (0, 0)),
            pl.BlockSpec(memory_space=pl.ANY),
        ],
        out_specs=pl.BlockSpec((tm, D_MODEL), lambda i: (i, 0)),
        out_shape=jax.ShapeDtypeStruct((R, D_MODEL), F32),
        scratch_shapes=[pltpu.VMEM((tm, D_MODEL), F32), pltpu.VMEM((tm, D_MODEL), F32),
                        pltpu.SemaphoreType.DMA(())],
        compiler_params=_params(("arbitrary",)),
        name="moe_combine_norm",
    )(pos_flat, x3, gates, g, y)


def _prep_weights(p):
    w_in = p["gdn_w_in"][0]
    w_kv = p["w_kv"]
    kcols = w_kv[:, :HKV * HD].reshape(D_MODEL, HKV, HD)
    vcols = w_kv[:, HKV * HD:].reshape(D_MODEL, HKV, HD)
    dup = lambda a: jnp.concatenate([a, a], axis=2).reshape(D_MODEL, HKV * LANES)
    w_kvd = jnp.concatenate([dup(kcols), dup(vcols)], axis=1)
    return dict(
        w_qk=w_in[:, :2 * QK_DIM].astype(BF16),
        w_v=w_in[:, 2 * QK_DIM:CONV_CH].astype(BF16),
        w_z=w_in[:, CONV_CH:CONV_CH + V_DIM].astype(BF16),
        w_ba=w_in[:, CONV_CH + V_DIM:].astype(BF16),
        w_out=p["gdn_w_out"][0].astype(BF16),
        ffn_w1=p["ffn_w1"][0].astype(BF16),
        ffn_w3=p["ffn_w3"][0].astype(BF16),
        ffn_w2=p["ffn_w2"][0].astype(BF16),
        w_kvd=w_kvd.astype(BF16),
        w_kvs=w_kv.astype(BF16),
        w_q=p["swa_w_q"][0].astype(BF16),
        w_o=p["swa_w_o"][0].astype(BF16),
        moe_w1=p["moe_w1"][0].astype(BF16),
        moe_w3=p["moe_w3"][0].astype(BF16),
        moe_w2=p["moe_w2"][0].astype(BF16),
    )


def _dup_heads(a):
    B, T = a.shape[:2]
    return jnp.concatenate([a, a], axis=3).reshape(B * T, HKV * LANES)


def _trunk(x, pos, conv0, s0, win_k, win_v, p, w):
    B, T, _ = x.shape
    R = B * T
    xf = x.reshape(R, D_MODEL)
    row = lambda a: a.reshape(1, -1)

    g_attn = row(p["norm_attn"][0])
    w_conv = p["gdn_w_conv"][0]
    nqk = 2 * QK_DIM
    qk, conv_qk = _qkv_conv(xf, g_attn, w["w_qk"], w_conv[:, :nqk], conv0[:, :, :nqk], B, T, True)
    v, conv_v = _qkv_conv(xf, g_attn, w["w_v"], w_conv[:, nqk:], conv0[:, :, nqk:], B, T, False)
    conv_new = jnp.concatenate([conv_qk, conv_v], axis=-1).reshape(B, -1, CONV_W - 1, CONV_CH)[:, -1]
    z, beta, gc, gct = _zbg(xf, g_attn, w["w_z"], w["w_ba"], p["gdn_a_log"][0], p["gdn_dt_bias"][0], B, T)
    u, wv, att_l = _delta_prep(qk, v, beta, gc, gct, B, T)
    og, s_new = _delta_scan(qk, u, wv, att_l, z, gc, s0, row(p["gdn_o_norm"][0]), B, T)
    x2 = _mixer_out_ffn(xf, og, w["w_out"], row(p["norm_ffn"][0]), w["ffn_w1"], w["ffn_w3"], w["ffn_w2"])

    keep = min(WINDOW, T) if win_k is None else T
    q, kvd, kvs = _qkv_rope(x2, row(p["norm_attn"][1]), row(p["kv_norm"]), w["w_q"], w["w_kvd"], w["w_kvs"],
                            _rope_tables(pos), B, T, keep)
    k_new = kvs[:, :HKV * HD].reshape(B, keep, HKV, HD)
    v_new = kvs[:, HKV * HD:].reshape(B, keep, HKV, HD)
    if win_k is None:
        keys, Tk = kvd, T
    else:
        hist = jnp.concatenate([_dup_heads(win_k), _dup_heads(win_v)], axis=1).astype(BF16)
        Tw = win_k.shape[1]
        keys = jnp.concatenate([hist.reshape(B, Tw, KV_DUP), kvd.reshape(B, T, KV_DUP)], axis=1)
        Tk = Tw + T
        keys = keys.reshape(B * Tk, KV_DUP)
    att = _attention(q, keys, row(p["swa_sinks"][0]), B, T, Tk)
    x3, h, eidx, gates, counts = _oproj_router(x2, att, w["w_o"], row(p["norm_ffn"][1]), p["moe_router"][0])

    te = min(EXPERT_TILE, max(TOP_K * R // N_EXPERTS, LANES))
    n_tiles = TOP_K * R // te + N_EXPERTS
    cnt = counts[0].astype(jnp.int32)
    tiles = (cnt + te - 1) // te
    tile_end = jnp.cumsum(tiles)
    base = (tile_end - tiles) * te
    n_used = tile_end[-1:]
    tile_expert = jnp.minimum(jnp.searchsorted(tile_end, jnp.arange(n_tiles, dtype=jnp.int32), side="right"),
                              N_EXPERTS - 1).astype(jnp.int32)
    pos_slot = _slots(eidx, base.astype(F32).reshape(1, N_EXPERTS)).reshape(-1)
    n_used = n_used.astype(jnp.int32)
    xs = _dispatch(h, pos_slot, cnt, base.astype(jnp.int32), n_used, n_tiles, te)
    ys = _expert_ffn(xs, tile_expert, n_used, w["moe_w1"], w["moe_w3"], w["moe_w2"], te)
    y = _combine(x3, gates, pos_slot, ys, row(p["final_norm"]))
    return (y.reshape(B, T, D_MODEL), conv_new[None], s_new[None], k_new, v_new)


def kernel(x_prompt, x_sample, cache_conv, state_delta, cache_k, cache_v, norm_attn, norm_ffn, gdn_w_in,
           gdn_w_conv, gdn_a_log, gdn_dt_bias, gdn_o_norm, gdn_w_out, kv_norm, w_kv, swa_w_q, swa_sinks, swa_w_o,
           ffn_w1, ffn_w3, ffn_w2, moe_router, moe_w1, moe_w3, moe_w2, final_norm):
    p = dict(norm_attn=norm_attn, norm_ffn=norm_ffn, gdn_w_in=gdn_w_in, gdn_w_conv=gdn_w_conv,
             gdn_a_log=gdn_a_log, gdn_dt_bias=gdn_dt_bias, gdn_o_norm=gdn_o_norm, gdn_w_out=gdn_w_out,
             kv_norm=kv_norm, w_kv=w_kv, swa_w_q=swa_w_q, swa_sinks=swa_sinks, swa_w_o=swa_w_o,
             ffn_w1=ffn_w1, ffn_w3=ffn_w3, ffn_w2=ffn_w2, moe_router=moe_router,
             moe_w1=moe_w1, moe_w3=moe_w3, moe_w2=moe_w2, final_norm=final_norm)
    w = _prep_weights(p)
    Bp, Tp, _ = x_prompt.shape
    Bs, Ts, _ = x_sample.shape
    conv0 = jnp.zeros((Bp, CONV_W - 1, CONV_CH), F32)
    s0 = jnp.zeros((Bp, HV, DK, DV), F32)
    y_p, conv_p, delta_p, k_p, v_p = _trunk(x_prompt, jnp.arange(Tp, dtype=jnp.int32), conv0, s0,
                                            None, None, p, w)
    pos_s = PAST_LEN + jnp.arange(Ts, dtype=jnp.int32)
    y_s, conv_s, delta_s, k_s, v_s = _trunk(x_sample, pos_s, cache_conv[0], state_delta[0],
                                            cache_k, cache_v, p, w)
    return (y_p, y_s, conv_p, delta_p, k_p, v_p, conv_s, delta_s, k_s, v_s)
```

```python
import functools

import jax
import jax.numpy as jnp
import numpy as np
from jax import lax
from jax.experimental import pallas as pl
from jax.experimental.pallas import tpu as pltpu

F32 = jnp.float32
BF16 = jnp.bfloat16

D_MODEL = 1024
CHUNK = 64
HK = 8
HV = 16
DK = 128
DV = 128
QK_DIM = HK * DK
V_DIM = HV * DV
CONV_CH = 2 * QK_DIM + V_DIM
CONV_W = 4
HQ = 16
HKV = 4
HD = 64
WINDOW = 128
ROT_DIM = HD // 4
ROPE_THETA = 500000.0
D_FF = 2816
N_EXPERTS = 8
TOP_K = 2
EPS = 1e-6
PAST_LEN = 2048

LANES = 128
ROW_TILE = 512
EXPERT_TILE = 512
VMEM_LIMIT = 48 * 1024 * 1024
VMEM_LIMIT_BIG = 56 * 1024 * 1024


def _params(sem, vmem=VMEM_LIMIT):
    return pltpu.CompilerParams(dimension_semantics=sem, vmem_limit_bytes=vmem)


def _rms_scale(x):
    return lax.rsqrt(jnp.mean(x * x, axis=-1, keepdims=True) + EPS)


def _bdot(a, b):
    return jnp.dot(a.astype(BF16), b.astype(BF16), preferred_element_type=F32)


def _bdot_nt(a, b):
    return lax.dot_general(a.astype(BF16), b.astype(BF16), (((1,), (1,)), ((), ())),
                           preferred_element_type=F32)


def _bdot_tn(a, b):
    return lax.dot_general(a.astype(BF16), b.astype(BF16), (((0,), (0,)), ((), ())),
                           preferred_element_type=F32)


def _split3(x):
    x1 = x.astype(BF16)
    r1 = x - x1.astype(F32)
    x2 = r1.astype(BF16)
    x3 = (r1 - x2.astype(F32)).astype(BF16)
    return x1, x2, x3


def _split2(x):
    x1 = x.astype(BF16)
    x2 = (x - x1.astype(F32)).astype(BF16)
    return x1, x2


def _dot_hi(a, b):
    a1, a2 = _split2(a)
    b1, b2 = _split2(b)
    d = functools.partial(jnp.dot, preferred_element_type=F32)
    return d(a1, b1) + (d(a1, b2) + d(a2, b1))


def _spread(x, emat, pieces):
    parts = _split3(x)[:pieces]
    out = jnp.dot(parts[0], emat, preferred_element_type=F32)
    for part in parts[1:]:
        out = out + jnp.dot(part, emat, preferred_element_type=F32)
    return out


def _head_spread_matrices():
    h = np.arange(HV)[:, None]
    full = (np.arange(HV * DV)[None, :] // DV == h)
    pair = (np.arange(HV * CHUNK)[None, :] // CHUNK == h)
    return jnp.asarray(full.astype(np.float32), dtype=BF16), jnp.asarray(pair.astype(np.float32), dtype=BF16)


def _sigmoid(x):
    return 0.5 * jnp.tanh(0.5 * x) + 0.5


def _silu(x):
    half = 0.5 * x
    return half * jnp.tanh(half) + half


CONV_SUB = 256


def _qkv_conv_kernel(x_ref, g_ref, w_ref, wc_ref, cs_ref, out_ref, cst_ref,
                     h_scr, pad_scr, carry_scr, *, tm, tn, l2):
    ti = pl.program_id(1)
    j = pl.program_id(2)

    @pl.when(j == 0)
    def _():
        x = x_ref[...]
        h_scr[...] = (x * _rms_scale(x) * g_ref[...]).astype(BF16)

    @pl.when(ti == 0)
    def _():
        carry_scr[j, 5:8, :] = cs_ref[0]

    h = h_scr[...]
    wc = wc_ref[...]
    scale = jnp.where(j == 0, DK ** -0.5, 1.0).astype(F32)
    cols = lambda k: slice(k * CONV_SUB, (k + 1) * CONV_SUB)

    def project(k):
        return jnp.dot(h, w_ref[:, cols(k)], preferred_element_type=F32)

    def finish(k, proj):
        c = cols(k)
        pad_scr[0:8, c] = carry_scr[j, :, c]
        pad_scr[8:, c] = proj
        acc = proj * wc[3:4, c]
        for s in range(1, CONV_W):
            acc = acc + pad_scr[8 - s:8 - s + tm, c] * wc[3 - s:4 - s, c]
        carry_scr[j, :, c] = pad_scr[tm:tm + 8, c]
        cst_ref[0, :, c] = pad_scr[tm + 5:tm + 8, c]
        y = _silu(acc)
        if not l2:
            out_ref[:, c] = y.astype(out_ref.dtype)
            return
        for hh in range(CONV_SUB // DK):
            seg = y[:, hh * DK:(hh + 1) * DK]
            inv = lax.rsqrt(jnp.sum(seg * seg, axis=-1, keepdims=True) + EPS) * scale
            lanes = slice(k * CONV_SUB + hh * DK, k * CONV_SUB + (hh + 1) * DK)
            out_ref[:, lanes] = (seg * inv).astype(out_ref.dtype)

    pending = project(0)
    for k in range(tn // CONV_SUB):
        upcoming = project(k + 1) if k + 1 < tn // CONV_SUB else None
        finish(k, pending)
        pending = upcoming


def _qkv_conv(x, g, w, w_conv, conv_state, B, T, l2):
    R = B * T
    n = w.shape[1]
    tm = min(2 * ROW_TILE, T)
    tn = QK_DIM
    nt = T // tm
    nj = n // tn
    kern = functools.partial(_qkv_conv_kernel, tm=tm, tn=tn, l2=l2)
    return pl.pallas_call(
        kern,
        grid=(B, nt, nj),
        in_specs=[
            pl.BlockSpec((tm, D_MODEL), lambda b, t, j: (b * nt + t, 0)),
            pl.BlockSpec((1, D_MODEL), lambda b, t, j: (0, 0)),
            pl.BlockSpec((D_MODEL, tn), lambda b, t, j: (0, j)),
            pl.BlockSpec((CONV_W, tn), lambda b, t, j: (0, j)),
            pl.BlockSpec((1, CONV_W - 1, tn), lambda b, t, j: (b, 0, j)),
        ],
        out_specs=[
            pl.BlockSpec((tm, tn), lambda b, t, j: (b * nt + t, j)),
            pl.BlockSpec((1, CONV_W - 1, tn), lambda b, t, j: (b * nt + t, 0, j)),
        ],
        out_shape=[
            jax.ShapeDtypeStruct((R, n), BF16),
            jax.ShapeDtypeStruct((B * nt, CONV_W - 1, n), F32),
        ],
        scratch_shapes=[
            pltpu.VMEM((tm, D_MODEL), BF16),
            pltpu.VMEM((tm + 8, tn), F32),
            pltpu.VMEM((nj, 8, tn), F32),
        ],
        compiler_params=_params(("arbitrary", "arbitrary", "arbitrary")),
        name="gdn_qk_conv" if l2 else "gdn_v_conv",
    )(x, g, w, w_conv, conv_state)


def _softplus(x):
    return jnp.maximum(x, 0.0) + jnp.log(1.0 + jnp.exp(-jnp.abs(x)))


def _zbg_kernel(x_ref, g_ref, w_ref, alog_ref, dtb_ref, tri_ref, z_ref, beta_ref, gc_ref, gct_ref, *, tm):
    x = x_ref[...]
    h = (x * _rms_scale(x) * g_ref[...]).astype(BF16)
    ba = jnp.dot(h, w_ref[:, V_DIM:], preferred_element_type=F32)
    z_ref[...] = jnp.dot(h, w_ref[:, :V_DIM], preferred_element_type=F32).astype(z_ref.dtype)
    beta_ref[...] = _sigmoid(ba[:, :HV])
    gcol = -jnp.exp(alog_ref[...]) * _softplus(ba[:, HV:2 * HV] + dtb_ref[...])
    tri = tri_ref[...]
    g1, g2, g3 = _split3(gcol)
    d = functools.partial(jnp.dot, preferred_element_type=F32)
    gc = d(tri, g1) + (d(tri, g2) + d(tri, g3))
    gc_ref[...] = gc
    r = lax.broadcasted_iota(jnp.int32, (HV, HV), 0)
    c = lax.broadcasted_iota(jnp.int32, (HV, HV), 1)
    sel = (c == jnp.where(r < HV // 2, 2 * r, 2 * r - (HV - 1))).astype(BF16)
    tn = lambda a: lax.dot_general(sel, a, (((1,), (1,)), ((), ())), preferred_element_type=F32)
    c1, c2, c3 = _split3(gc)
    gct = tn(c1) + (tn(c2) + tn(c3))
    for ch in range(tm // CHUNK):
        sl = slice(ch * CHUNK, (ch + 1) * CHUNK)
        gct_ref[0, ch] = jnp.concatenate([gct[:HV // 2, sl], gct[HV // 2:, sl]], axis=1)


def _zbg(x, g, w_z, w_ba, a_log, dt_bias, B, T):
    R = B * T
    tm = min(ROW_TILE, T)
    nt = T // tm
    nc = T // CHUNK
    idx = np.arange(tm)
    tri = jnp.asarray(((idx[:, None] // CHUNK == idx[None, :] // CHUNK) & (idx[:, None] >= idx[None, :]))
                      .astype(np.float32), dtype=BF16)
    w_zba = jnp.concatenate([w_z, w_ba, jnp.zeros((D_MODEL, LANES - 2 * HV), BF16)], axis=1)
    full = lambda shape: pl.BlockSpec(shape, lambda b, t: (0,) * len(shape))
    kern = functools.partial(_zbg_kernel, tm=tm)
    return pl.pallas_call(
        kern,
        grid=(B, nt),
        in_specs=[
            pl.BlockSpec((tm, D_MODEL), lambda b, t: (b * nt + t, 0)),
            full((1, D_MODEL)),
            full((D_MODEL, V_DIM + LANES)),
            full((1, HV)), full((1, HV)),
            full((tm, tm)),
        ],
        out_specs=[
            pl.BlockSpec((tm, V_DIM), lambda b, t: (b * nt + t, 0)),
            pl.BlockSpec((tm, HV), lambda b, t: (b * nt + t, 0)),
            pl.BlockSpec((tm, HV), lambda b, t: (b * nt + t, 0)),
            pl.BlockSpec((1, tm // CHUNK, HV // 2, 2 * CHUNK), lambda b, t: (b, t, 0, 0)),
        ],
        out_shape=[
            jax.ShapeDtypeStruct((R, V_DIM), BF16),
            jax.ShapeDtypeStruct((R, HV), F32),
            jax.ShapeDtypeStruct((R, HV), F32),
            jax.ShapeDtypeStruct((B, nc, HV // 2, 2 * CHUNK), F32),
        ],
        compiler_params=_params(("arbitrary", "arbitrary")),
        name="gdn_z_beta_decay",
    )(x, g, w_zba, a_log.reshape(1, HV), dt_bias.reshape(1, HV), tri)


PAIR = HV // HK


def _pair_rows(x, lo):
    zero = jnp.zeros((), x.dtype)
    return jnp.concatenate([jnp.where(lo, x, zero), jnp.where(lo, zero, x)], axis=0)


def _pair_products(lhs_list, rhs_list, lo):
    def split(a):
        hi = a.astype(BF16)
        return hi, (a - hi.astype(F32)).astype(BF16)

    ops = []
    for lhs, rhs in zip(lhs_list, rhs_list):
        lh, ll = split(lhs)
        rh, rl = split(rhs)
        rh2 = _pair_rows(rh, lo)
        ops.append((jnp.concatenate([lh, lh, ll], axis=1), jnp.concatenate([rh2, _pair_rows(rl, lo), rh2], axis=0)))
    return [jnp.dot(a, b, preferred_element_type=F32) for a, b in ops]


def _pair_unit_lower_inverse(low_list, lo, eye2):
    C = low_list[0].shape[0]
    xs = [-low for low in low_list]
    accs = [eye2 + x for x in xs]
    pws = _pair_products(xs, xs, lo)
    n = 2
    while 2 * n < C:
        prods = _pair_products([jnp.concatenate([a, p], axis=0) for a, p in zip(accs, pws)], pws, lo)
        accs = [a + pr[:C] for a, pr in zip(accs, prods)]
        pws = [pr[C:] for pr in prods]
        n *= 2
    prods = _pair_products(accs, pws, lo)
    return [a + pr for a, pr in zip(accs, prods)]


def _delta_prep_kernel(q_ref, k_ref, v_ref, beta_ref, gc_ref, gct_ref, epair_ref, u_ref, w_ref, a_ref, *, cb):
    C = CHUNK
    ri = lax.broadcasted_iota(jnp.int32, (C, 2 * C), 0)
    li = lax.broadcasted_iota(jnp.int32, (C, 2 * C), 1)
    lm = li & (C - 1)
    lo = li < C
    incl = ri >= lm
    strict = ri > lm
    eye2 = (ri == lm).astype(F32)
    units = [(c, p) for c in range(cb) for p in range(HK)]
    rows = lambda c: slice(c * C, (c + 1) * C)

    grams = []
    for c, p in units:
        kb = k_ref[rows(c), p * DK:(p + 1) * DK]
        qb = q_ref[rows(c), p * DK:(p + 1) * DK]
        grams.append(lax.dot_general(jnp.concatenate([kb, qb], axis=0), jnp.concatenate([kb, kb], axis=0),
                                     (((1,), (1,)), ((), ())), preferred_element_type=F32))

    epair = epair_ref[...]
    g2s, b2s, betas, begs = [], [], [], []
    for c in range(cb):
        beta = beta_ref[rows(c), :]
        gc = gc_ref[rows(c), :]
        g2s.append(_spread(gc, epair, 3))
        b2s.append(_spread(beta, epair, 2))
        betas.append(beta)
        begs.append(beta * jnp.exp(gc))

    lows = []
    for (c, p), gram in zip(units, grams):
        pt = slice(p * 2 * C, (p + 1) * 2 * C)
        grow2 = gct_ref[c, p:p + 1, :]
        decay2 = jnp.exp(jnp.where(incl, g2s[c][:, pt] - grow2, -jnp.inf))
        a_ref[rows(c), pt] = (gram[C:] * decay2).astype(a_ref.dtype)
        lows.append(jnp.where(strict, b2s[c][:, pt] * gram[:C] * decay2, 0.0))

    tinvs = _pair_unit_lower_inverse(lows, lo, eye2)

    ops = []
    for (c, p), tinv in zip(units, tinvs):
        kf = k_ref[rows(c), p * DK:(p + 1) * DK].astype(F32)
        rhs = []
        for h in (PAIR * p, PAIR * p + 1):
            ht = slice(h * DV, (h + 1) * DV)
            vf = v_ref[rows(c), ht].astype(F32)
            rhs.append(jnp.concatenate([vf * betas[c][:, h:h + 1], kf * begs[c][:, h:h + 1]], axis=1))
        ops.append((_pair_rows(tinv.astype(BF16), lo), jnp.concatenate(rhs, axis=0).astype(BF16)))
    for (c, p), (lhs, rhs) in zip(units, ops):
        uw = jnp.dot(lhs, rhs, preferred_element_type=F32)
        for i, h in enumerate((PAIR * p, PAIR * p + 1)):
            u_ref[rows(c), h * DV:(h + 1) * DV] = uw[i * C:(i + 1) * C, :DV].astype(u_ref.dtype)
            w_ref[rows(c), h * DV:(h + 1) * DV] = uw[i * C:(i + 1) * C, DV:].astype(w_ref.dtype)


def _delta_prep(qk, v, beta, gc, gct, B, T):
    R = B * T
    cb = 2 if T % (2 * CHUNK) == 0 else 1
    rb = cb * CHUNK
    gct2 = gct.reshape(R // CHUNK, HK, 2 * CHUNK)
    kern = functools.partial(_delta_prep_kernel, cb=cb)
    return pl.pallas_call(
        kern,
        grid=(R // rb,),
        in_specs=[
            pl.BlockSpec((rb, QK_DIM), lambda i: (i, 0)),
            pl.BlockSpec((rb, QK_DIM), lambda i: (i, 1)),
            pl.BlockSpec((rb, V_DIM), lambda i: (i, 0)),
            pl.BlockSpec((rb, HV), lambda i: (i, 0)),
            pl.BlockSpec((rb, HV), lambda i: (i, 0)),
            pl.BlockSpec((cb, HK, 2 * CHUNK), lambda i: (i, 0, 0)),
            pl.BlockSpec((HV, HV * CHUNK), lambda i: (0, 0)),
        ],
        out_specs=[
            pl.BlockSpec((rb, V_DIM), lambda i: (i, 0)),
            pl.BlockSpec((rb, V_DIM), lambda i: (i, 0)),
            pl.BlockSpec((rb, HV * CHUNK), lambda i: (i, 0)),
        ],
        out_shape=[
            jax.ShapeDtypeStruct((R, V_DIM), BF16),
            jax.ShapeDtypeStruct((R, V_DIM), BF16),
            jax.ShapeDtypeStruct((R, HV * CHUNK), BF16),
        ],
        compiler_params=_params(("arbitrary",)),
        name="gdn_delta_prep",
    )(qk, qk, v, beta, gc, gct2, _head_spread_matrices()[1])


def _delta_scan_kernel(q_ref, k_ref, u_ref, w_ref, a_ref, z_ref, gc_ref, s0_ref, on_ref, efull_ref,
                       og_ref, s_ref, *, bb):
    c = pl.program_id(1)

    @pl.when(c == 0)
    def _():
        s_ref[...] = s0_ref[...]

    C = CHUNK
    lo = lax.broadcasted_iota(jnp.int32, (C, 2 * C), 1) < C
    onorm = on_ref[...]
    units = [(b, h) for b in range(bb) for h in range(HV)]
    pairs = [(b, p) for b in range(bb) for p in range(HK)]

    efull = efull_ref[...]
    egs, kgs, dgs = [], [], []
    for b in range(bb):
        gc = gc_ref[b]
        g_last = gc[C - 1:C, :]
        egs.append(_spread(jnp.exp(gc), efull, 2))
        kgs.append(_spread(jnp.exp(g_last - gc), efull, 2))
        dgs.append(_spread(jnp.exp(gc[C - 8:C, :]), efull, 3)[7:8, :])

    ops = []
    kds = []
    dec = []
    for b, h in units:
        p = h // PAIR
        ht = slice(h * DV, (h + 1) * DV)
        qg = (q_ref[b, :, p * DK:(p + 1) * DK].astype(F32) * egs[b][:, ht]).astype(BF16)
        kds.append((k_ref[b, :, p * DK:(p + 1) * DK].astype(F32) * kgs[b][:, ht]).astype(BF16))
        dec.append(dgs[b][:, ht])
        ops.append((jnp.concatenate([w_ref[b, :, ht], qg], axis=0), s_ref[b, h].astype(BF16)))
    wqs = [jnp.dot(a, s, preferred_element_type=F32) for a, s in ops]

    vns = [(u_ref[b, :, h * DV:(h + 1) * DV].astype(F32) - wq[:C]).astype(BF16) for (b, h), wq in zip(units, wqs)]

    oparts = []
    for i, (b, p) in enumerate(pairs):
        att = _pair_rows(a_ref[b, :, p * 2 * C:(p + 1) * 2 * C], lo)
        vn2 = jnp.concatenate([vns[PAIR * i], vns[PAIR * i + 1]], axis=0)
        oparts.append(jnp.dot(att, vn2, preferred_element_type=F32))
    sds = [lax.dot_general(kd, vn, (((0,), (0,)), ((), ())), preferred_element_type=F32)
           for kd, vn in zip(kds, vns)]

    for i, (b, h) in enumerate(units):
        s_ref[b, h] = s_ref[b, h] * dec[i] + sds[i]
        o = wqs[i][C:] + oparts[i // PAIR][(h % PAIR) * C:(h % PAIR + 1) * C]
        on = o * _rms_scale(o) * onorm
        zf = z_ref[b, :, h * DV:(h + 1) * DV].astype(F32)
        og_ref[b, :, h * DV:(h + 1) * DV] = (on * _silu(zf)).astype(og_ref.dtype)


def _delta_scan(qk, u, w, att, z, gc, s0, o_norm, B, T):
    nc = T // CHUNK
    bb = 4 if B % 4 == 0 else (2 if B % 2 == 0 else 1)
    v3 = lambda a: a.reshape(B, T, a.shape[-1])
    blk = lambda n, j=0: pl.BlockSpec((bb, CHUNK, n), lambda b, c: (b, c, j))
    kern = functools.partial(_delta_scan_kernel, bb=bb)
    og, s_new = pl.pallas_call(
        kern,
        grid=(B // bb, nc),
        in_specs=[
            blk(QK_DIM, 0), blk(QK_DIM, 1), blk(V_DIM), blk(V_DIM), blk(HV * CHUNK), blk(V_DIM), blk(HV),
            pl.BlockSpec((bb, HV, DK, DV), lambda b, c: (b, 0, 0, 0)),
            pl.BlockSpec((1, DV), lambda b, c: (0, 0)),
            pl.BlockSpec((HV, HV * DV), lambda b, c: (0, 0)),
        ],
        out_specs=[
            blk(V_DIM),
            pl.BlockSpec((bb, HV, DK, DV), lambda b, c: (b, 0, 0, 0)),
        ],
        out_shape=[
            jax.ShapeDtypeStruct((B, T, V_DIM), BF16),
            jax.ShapeDtypeStruct((B, HV, DK, DV), F32),
        ],
        compiler_params=_params(("arbitrary", "arbitrary")),
        name="gdn_delta_scan",
    )(v3(qk), v3(qk), v3(u), v3(w), v3(att), v3(z), v3(gc), s0, o_norm, _head_spread_matrices()[0])
    return og.reshape(B * T, V_DIM), s_new


FF_TILE = 1408
RESIDENT = pl.Buffered(1)


def _swiglu(h, w1_ref, w3_ref, w2_ref, lead=()):
    out = None
    for f in range(D_FF // FF_TILE):
        sl = slice(f * FF_TILE, (f + 1) * FF_TILE)
        a = jnp.dot(h, w1_ref[lead + (slice(None), sl)], preferred_element_type=F32)
        b = jnp.dot(h, w3_ref[lead + (slice(None), sl)], preferred_element_type=F32)
        mid = (_silu(a) * b).astype(BF16)
        part = jnp.dot(mid, w2_ref[lead + (sl, slice(None))], preferred_element_type=F32)
        out = part if out is None else out + part
    return out


def _mixer_out_ffn_kernel(x_ref, a_ref, wo_ref, g_ref, w1_ref, w3_ref, w2_ref, o_ref):
    x1 = x_ref[...] + jnp.dot(a_ref[...], wo_ref[...], preferred_element_type=F32)
    h = (x1 * _rms_scale(x1) * g_ref[...]).astype(BF16)
    o_ref[...] = x1 + _swiglu(h, w1_ref, w3_ref, w2_ref)


def _mixer_out_ffn(x, a, w_out, g, w1, w3, w2):
    R = x.shape[0]
    K = a.shape[1]
    tm = min(ROW_TILE, R)
    held = lambda shape: pl.BlockSpec(shape, lambda i: (0,) * len(shape), pipeline_mode=RESIDENT)
    return pl.pallas_call(
        _mixer_out_ffn_kernel,
        grid=(R // tm,),
        in_specs=[
            pl.BlockSpec((tm, D_MODEL), lambda i: (i, 0)),
            pl.BlockSpec((tm, K), lambda i: (i, 0)),
            held((K, D_MODEL)),
            held((1, D_MODEL)),
            held((D_MODEL, D_FF)), held((D_MODEL, D_FF)), held((D_FF, D_MODEL)),
        ],
        out_specs=pl.BlockSpec((tm, D_MODEL), lambda i: (i, 0)),
        out_shape=jax.ShapeDtypeStruct((R, D_MODEL), F32),
        compiler_params=_params(("arbitrary",), VMEM_LIMIT_BIG),
        name="gdn_out_dense_swiglu",
    )(x, a, w_out, g, w1, w3, w2)


KV_DUP = 2 * HKV * LANES
KV_STD = 2 * HKV * HD


def _rope(x, cos, sm, sp):
    up = pltpu.roll(x, LANES - ROT_DIM // 2, 1)
    dn = pltpu.roll(x, ROT_DIM // 2, 1)
    return x * cos + up * sm + dn * sp


def _qkv_rope_kernel(x_ref, gq_ref, gkv_ref, wq_ref, wkv_ref, wkvs_ref, cos_ref, sm_ref, sp_ref,
                     q_ref, kvd_ref, kvs_ref, *, nt, keep):
    x = x_ref[...]
    xr = x * _rms_scale(x)
    hq = (xr * gq_ref[...]).astype(BF16)
    hkv = (xr * gkv_ref[...]).astype(BF16)
    cos = cos_ref[...]
    sm = sm_ref[...]
    sp = sp_ref[...]
    q = jnp.dot(hq, wq_ref[...], preferred_element_type=F32)
    for t in range(HQ * HD // LANES):
        sl = slice(t * LANES, (t + 1) * LANES)
        q_ref[:, sl] = (_rope(q[:, sl], cos, sm, sp) * (HD ** -0.5)).astype(q_ref.dtype)
    kv = jnp.dot(hkv, wkv_ref[...], preferred_element_type=F32)
    half = KV_DUP // 2
    for t in range(HKV):
        sl = slice(t * LANES, (t + 1) * LANES)
        kvd_ref[:, sl] = _rope(kv[:, sl], cos, sm, sp).astype(kvd_ref.dtype)
    kvd_ref[:, half:] = kv[:, half:].astype(kvd_ref.dtype)

    @pl.when(pl.program_id(0) % nt == nt - 1)
    def _():
        tm = x.shape[0]
        rows = slice(tm - keep, tm)
        kvs = jnp.dot(hkv[rows], wkvs_ref[...], preferred_element_type=F32)
        for t in range(HKV * HD // LANES):
            sl = slice(t * LANES, (t + 1) * LANES)
            kvs_ref[:, sl] = _rope(kvs[:, sl], cos[rows], sm[rows], sp[rows])
        kvs_ref[:, HKV * HD:] = kvs[:, HKV * HD:]


def _rope_tables(pos):
    half = ROT_DIM // 2
    inv = jnp.power(ROPE_THETA, -jnp.arange(half, dtype=F32) * 2.0 / ROT_DIM)
    ang = pos.astype(F32)[:, None] * inv[None, :]
    cos = jnp.cos(ang)
    sin = jnp.sin(ang)
    T = pos.shape[0]
    one = jnp.ones((T, HD - ROT_DIM), F32)
    zero = jnp.zeros((T, HD - ROT_DIM), F32)
    zh = jnp.zeros((T, half), F32)
    c64 = jnp.concatenate([cos, cos, one], axis=1)
    sm64 = jnp.concatenate([-sin, zh, zero], axis=1)
    sp64 = jnp.concatenate([zh, sin, zero], axis=1)
    dup = lambda a: jnp.concatenate([a, a], axis=1)
    return dup(c64), dup(sm64), dup(sp64)


def _qkv_rope(x, gq, gkv, w_q, w_kvd, w_kvs, tables, B, T, keep):
    R = B * T
    tm = min(ROW_TILE, T)
    nt = T // tm
    assert keep <= tm
    tab = pl.BlockSpec((tm, LANES), lambda i: (i % nt, 0))
    kern = functools.partial(_qkv_rope_kernel, nt=nt, keep=keep)
    return pl.pallas_call(
        kern,
        grid=(R // tm,),
        in_specs=[
            pl.BlockSpec((tm, D_MODEL), lambda i: (i, 0)),
            pl.BlockSpec((1, D_MODEL), lambda i: (0, 0)),
            pl.BlockSpec((1, D_MODEL), lambda i: (0, 0)),
            pl.BlockSpec((D_MODEL, HQ * HD), lambda i: (0, 0)),
            pl.BlockSpec((D_MODEL, KV_DUP), lambda i: (0, 0)),
            pl.BlockSpec((D_MODEL, KV_STD), lambda i: (0, 0)),
            tab, tab, tab,
        ],
        out_specs=[
            pl.BlockSpec((tm, HQ * HD), lambda i: (i, 0)),
            pl.BlockSpec((tm, KV_DUP), lambda i: (i, 0)),
            pl.BlockSpec((keep, KV_STD), lambda i: (i // nt, 0)),
        ],
        out_shape=[
            jax.ShapeDtypeStruct((R, HQ * HD), BF16),
            jax.ShapeDtypeStruct((R, KV_DUP), BF16),
            jax.ShapeDtypeStruct((B * keep, KV_STD), F32),
        ],
        compiler_params=_params(("arbitrary",)),
        name="swa_qkv_rope",
    )(x, gq, gkv, w_q, w_kvd, w_kvs, *tables)


KEY_CHUNKS = WINDOW // CHUNK + 1
KEY_PAD = 256


def _attn_kernel(*refs, off, cq):
    q_ref = refs[0]
    krefs = refs[1:KEY_CHUNKS + cq]
    sink_ref, ones_ref, o_ref = refs[KEY_CHUNKS + cq:]
    c = pl.program_id(1)
    lane = lax.broadcasted_iota(jnp.int32, (KEY_CHUNKS * CHUNK, LANES), 1)
    lo = lane < HD
    col = lax.broadcasted_iota(jnp.int32, (1, KEY_PAD), 1)
    sink_col = col == KEY_CHUNKS * CHUNK
    zpad = jnp.zeros((KEY_PAD - KEY_CHUNKS * CHUNK, LANES), BF16)
    zero = jnp.zeros((), BF16)
    tiles = HQ // HKV // 2
    ones_blk = ones_ref[...]
    units = [(u, g) for u in range(cq) for g in range(HKV)]

    vblks = []
    scores = []
    for u, g in units:
        kr = krefs[u:u + KEY_CHUNKS]
        kg = jnp.concatenate([r[:, g * LANES:(g + 1) * LANES] for r in kr], axis=0)
        vg = jnp.concatenate([r[:, KV_DUP // 2 + g * LANES:KV_DUP // 2 + (g + 1) * LANES] for r in kr], axis=0)
        kblk = jnp.concatenate([jnp.where(lo, kg, zero), zpad, jnp.where(lo, zero, kg), zpad], axis=0)
        vblk = jnp.concatenate([jnp.where(lo, vg, zero), zpad, jnp.where(lo, zero, vg), zpad], axis=0)
        vblks.append(jnp.concatenate([vblk, ones_blk], axis=1))
        qg = jnp.concatenate([q_ref[u * CHUNK:(u + 1) * CHUNK, (g * tiles + p) * LANES:(g * tiles + p + 1) * LANES]
                              for p in range(tiles)], axis=0)
        scores.append(lax.dot_general(qg, kblk, (((1,), (1,)), ((), ())), preferred_element_type=F32))

    probs = []
    for (u, g), sc in zip(units, scores):
        first = c * cq + u + off - (KEY_CHUNKS - 1)
        valid = (col < KEY_CHUNKS * CHUNK) & (col // CHUNK + first >= 0)
        base_bias = jnp.where(valid, 0.0, -jnp.inf).astype(F32)
        rows = []
        for p in range(tiles):
            t = g * tiles + p
            ps = []
            for hh in range(2):
                bias = jnp.where(sink_col, sink_ref[:, 2 * t + hh:2 * t + hh + 1], base_bias)
                sh = sc[p * CHUNK:(p + 1) * CHUNK, hh * KEY_PAD:(hh + 1) * KEY_PAD] + bias
                m = jnp.max(sh, axis=-1, keepdims=True)
                ps.append(jnp.exp(sh - m).astype(BF16))
            rows.append(jnp.concatenate(ps, axis=1))
        probs.append(jnp.concatenate(rows, axis=0))

    outs = [jnp.dot(pr, vb, preferred_element_type=F32) for pr, vb in zip(probs, vblks)]
    for (u, g), out in zip(units, outs):
        for p in range(tiles):
            t = g * tiles + p
            rows = slice(p * CHUNK, (p + 1) * CHUNK)
            o_ref[u * CHUNK:(u + 1) * CHUNK, t * LANES:(t + 1) * LANES] = (
                out[rows, :LANES] / out[rows, LANES:]).astype(o_ref.dtype)


def _attention(q, kvd, sinks, B, Tq, Tk):
    ncq = Tq // CHUNK
    nck = Tk // CHUNK
    off = nck - ncq
    cq = 4 if ncq % 4 == 0 else (2 if ncq % 2 == 0 else 1)
    kern = functools.partial(_attn_kernel, off=off, cq=cq)
    blk_row = np.arange(2 * KEY_PAD)[:, None] // KEY_PAD
    blk_lane = np.arange(LANES)[None, :] // HD
    ones_blk = jnp.asarray((blk_row == blk_lane).astype(np.float32), dtype=BF16)

    def kspec(i):
        return pl.BlockSpec((CHUNK, KV_DUP),
                            lambda b, c: (b * nck + jnp.maximum(c * cq + off - (KEY_CHUNKS - 1) + i, 0), 0))

    qspec = pl.BlockSpec((cq * CHUNK, HQ * HD), lambda b, c: (b * (ncq // cq) + c, 0))
    return pl.pallas_call(
        kern,
        grid=(B, ncq // cq),
        in_specs=[qspec] + [kspec(i) for i in range(KEY_CHUNKS + cq - 1)] + [
            pl.BlockSpec((1, HQ), lambda b, c: (0, 0)),
            pl.BlockSpec((2 * KEY_PAD, LANES), lambda b, c: (0, 0)),
        ],
        out_specs=qspec,
        out_shape=jax.ShapeDtypeStruct((B * Tq, HQ * HD), BF16),
        compiler_params=_params(("arbitrary", "arbitrary")),
        name="swa_attention",
    )(q, *([kvd] * (KEY_CHUNKS + cq - 1)), sinks, ones_blk)


def _oproj_router_kernel(x_ref, a_ref, w_ref, g_ref, wr_ref, x3_ref, h_ref, e_ref, gate_ref, cnt_ref):
    i = pl.program_id(0)
    x3 = x_ref[...] + jnp.dot(a_ref[...], w_ref[...], preferred_element_type=F32)
    x3_ref[...] = x3
    h = x3 * _rms_scale(x3) * g_ref[...]
    h_ref[...] = h
    h1, h2 = _split2(h)
    r1, r2 = _split2(wr_ref[...])
    part = jnp.dot(h1, jnp.concatenate([r1, r2], axis=1), preferred_element_type=F32)
    logits = part[:, :N_EXPERTS] + (part[:, N_EXPERTS:] + jnp.dot(h2, r1, preferred_element_type=F32))
    eidx = lax.broadcasted_iota(jnp.int32, logits.shape, 1)
    m1 = jnp.max(logits, axis=-1, keepdims=True)
    i1 = jnp.min(jnp.where(logits == m1, eidx, N_EXPERTS), axis=-1, keepdims=True)
    rest = jnp.where(eidx == i1, -jnp.inf, logits)
    m2 = jnp.max(rest, axis=-1, keepdims=True)
    i2 = jnp.min(jnp.where(rest == m2, eidx, N_EXPERTS), axis=-1, keepdims=True)
    e2 = jnp.exp(m2 - m1)
    den = 1.0 + e2
    e_ref[...] = jnp.concatenate([i1, i2], axis=1)
    gate_ref[...] = jnp.concatenate([1.0 / den, e2 / den], axis=1)
    hot = ((eidx == i1) | (eidx == i2)).astype(F32)

    @pl.when(i == 0)
    def _():
        cnt_ref[...] = jnp.zeros_like(cnt_ref)

    cnt_ref[...] += jnp.sum(hot, axis=0, keepdims=True)


def _oproj_router(x, a, w_o, g, w_r):
    R = x.shape[0]
    tm = min(ROW_TILE, R)
    row = lambda n: pl.BlockSpec((tm, n), lambda i: (i, 0))
    return pl.pallas_call(
        _oproj_router_kernel,
        grid=(R // tm,),
        in_specs=[
            row(D_MODEL), row(HQ * HD),
            pl.BlockSpec((HQ * HD, D_MODEL), lambda i: (0, 0)),
            pl.BlockSpec((1, D_MODEL), lambda i: (0, 0)),
            pl.BlockSpec((D_MODEL, N_EXPERTS), lambda i: (0, 0)),
        ],
        out_specs=[row(D_MODEL), row(D_MODEL), row(TOP_K), row(TOP_K),
                   pl.BlockSpec((1, N_EXPERTS), lambda i: (0, 0))],
        out_shape=[
            jax.ShapeDtypeStruct((R, D_MODEL), F32),
            jax.ShapeDtypeStruct((R, D_MODEL), F32),
            jax.ShapeDtypeStruct((R, TOP_K), jnp.int32),
            jax.ShapeDtypeStruct((R, TOP_K), F32),
            jax.ShapeDtypeStruct((1, N_EXPERTS), F32),
        ],
        compiler_params=_params(("arbitrary",)),
        name="swa_oproj_router",
    )(x, a, w_o, g, w_r)


def _slot_kernel(e_ref, base_ref, tri_ref, pos_ref, run_scr):
    i = pl.program_id(0)

    @pl.when(i == 0)
    def _():
        run_scr[...] = jnp.zeros_like(run_scr)

    e = e_ref[...]
    tm = e.shape[0]
    eidx = lax.broadcasted_iota(jnp.int32, (tm, N_EXPERTS), 1)
    hot0 = eidx == e[:, 0:1]
    hot1 = eidx == e[:, 1:2]
    hot = (hot0 | hot1).astype(BF16)
    before = jnp.dot(tri_ref[...], hot, preferred_element_type=F32)
    dest = before + run_scr[...] + base_ref[...]
    p0 = jnp.sum(jnp.where(hot0, dest, 0.0), axis=-1, keepdims=True)
    p1 = jnp.sum(jnp.where(hot1, dest, 0.0), axis=-1, keepdims=True)
    pos_ref[...] = jnp.concatenate([p0, p1], axis=1).astype(jnp.int32)
    run_scr[...] += jnp.sum(hot.astype(F32), axis=0, keepdims=True)


def _slots(eidx, base):
    R = eidx.shape[0]
    tm = min(ROW_TILE, R)
    idx = np.arange(tm)
    tri = jnp.asarray((idx[:, None] > idx[None, :]).astype(np.float32), dtype=BF16)
    return pl.pallas_call(
        _slot_kernel,
        grid=(R // tm,),
        in_specs=[
            pl.BlockSpec((tm, TOP_K), lambda i: (i, 0)),
            pl.BlockSpec((1, N_EXPERTS), lambda i: (0, 0)),
            pl.BlockSpec((tm, tm), lambda i: (0, 0)),
        ],
        out_specs=pl.BlockSpec((tm, TOP_K), lambda i: (i, 0)),
        out_shape=jax.ShapeDtypeStruct((R, TOP_K), jnp.int32),
        scratch_shapes=[pltpu.VMEM((1, N_EXPERTS), F32)],
        compiler_params=_params(("arbitrary",)),
        name="moe_slots",
    )(eidx, base, tri)


DMA_UNROLL = 8


def _row_copy(src, i, dst, j, sem):
    return pltpu.make_async_copy(src.at[pl.ds(i, 1)], dst.at[pl.ds(j, 1)], sem)


def _dispatch_kernel(pos_ref, cnt_ref, base_ref, nused_ref, h_ref, xs_ref, zero_scr, sem, zsem, *, tm, te, n_tiles):
    i = pl.program_id(0)

    @pl.when(i == 0)
    def _():
        zero_scr[...] = jnp.zeros_like(zero_scr)

        def tile_copy(t):
            dst = xs_ref.at[pl.ds(pl.multiple_of(t * te, te), te)]
            return pltpu.make_async_copy(zero_scr, dst, zsem)

        def issue_tile(t, carry):
            tile_copy(t).start()
            return carry

        def drain_tile(t, carry):
            tile_copy(t).wait()
            return carry

        lax.fori_loop(nused_ref[0], n_tiles, issue_tile, 0)
        lax.fori_loop(nused_ref[0], n_tiles, drain_tile, 0)
        for e in range(N_EXPERTS):
            n = cnt_ref[e]
            start = base_ref[e] + n
            npad = (te - n % te) % te

            def issue(r, carry):
                _row_copy(zero_scr, 0, xs_ref, start + r, zsem).start()
                return carry

            def drain(r, carry):
                _row_copy(zero_scr, 0, xs_ref, start + r, zsem).wait()
                return carry

            lax.fori_loop(0, npad, issue, 0)
            lax.fori_loop(0, npad, drain, 0)

    def issue(r, carry):
        _row_copy(h_ref, r, xs_ref, pos_ref[2 * r], sem).start()
        _row_copy(h_ref, r, xs_ref, pos_ref[2 * r + 1], sem).start()
        return carry

    lax.fori_loop(0, tm, issue, 0, unroll=DMA_UNROLL)
    for _ in range(TOP_K):
        pltpu.make_async_copy(h_ref, xs_ref.at[pl.ds(0, tm)], sem).wait()


def _dispatch(h, pos_flat, counts, base, n_used, n_tiles, te):
    R = h.shape[0]
    tm = min(ROW_TILE, R)
    kern = functools.partial(_dispatch_kernel, tm=tm, te=te, n_tiles=n_tiles)
    smem = pl.BlockSpec(memory_space=pltpu.SMEM)
    return pl.pallas_call(
        kern,
        grid=(R // tm,),
        in_specs=[
            pl.BlockSpec((TOP_K * tm,), lambda i: (i,), memory_space=pltpu.SMEM),
            smem, smem, smem,
            pl.BlockSpec((tm, D_MODEL), lambda i: (i, 0)),
        ],
        out_specs=pl.BlockSpec(memory_space=pl.ANY),
        out_shape=jax.ShapeDtypeStruct((n_tiles * te, D_MODEL), F32),
        scratch_shapes=[pltpu.VMEM((te, D_MODEL), F32), pltpu.SemaphoreType.DMA(()), pltpu.SemaphoreType.DMA(())],
        compiler_params=pltpu.CompilerParams(dimension_semantics=("arbitrary",), vmem_limit_bytes=VMEM_LIMIT,
                                             has_side_effects=True),
        name="moe_dispatch",
    )(pos_flat, counts, base, n_used, h)


def _expert_ffn_kernel(te_ref, nused_ref, x_ref, w1_ref, w3_ref, w2_ref, y_ref):
    i = pl.program_id(0)

    @pl.when(i < nused_ref[0])
    def _():
        y_ref[...] = _swiglu(x_ref[...].astype(BF16), w1_ref, w3_ref, w2_ref, lead=(0,))

    @pl.when(i >= nused_ref[0])
    def _():
        y_ref[...] = jnp.zeros_like(y_ref)


def _expert_ffn(xs, tile_expert, n_used, w1, w3, w2, te):
    n_rows = xs.shape[0]
    nt = n_rows // te
    wspec = lambda shape: pl.BlockSpec((1,) + shape, lambda i, te_r, nu_r: (te_r[i], 0, 0), pipeline_mode=RESIDENT)
    grid_spec = pltpu.PrefetchScalarGridSpec(
        num_scalar_prefetch=2,
        grid=(nt,),
        in_specs=[
            pl.BlockSpec((te, D_MODEL), lambda i, te_r, nu_r: (i, 0)),
            wspec((D_MODEL, D_FF)), wspec((D_MODEL, D_FF)), wspec((D_FF, D_MODEL)),
        ],
        out_specs=pl.BlockSpec((te, D_MODEL), lambda i, te_r, nu_r: (i, 0)),
    )
    return pl.pallas_call(
        _expert_ffn_kernel,
        grid_spec=grid_spec,
        out_shape=jax.ShapeDtypeStruct((n_rows, D_MODEL), F32),
        compiler_params=_params(("arbitrary",), VMEM_LIMIT_BIG),
        name="moe_expert_swiglu",
    )(tile_expert, n_used, xs, w1, w3, w2)


def _combine_kernel(pos_ref, x_ref, gate_ref, g_ref, y_ref, o_ref, buf0, buf1, sem, *, tm):
    def issue(r, carry):
        _row_copy(y_ref, pos_ref[2 * r], buf0, r, sem).start()
        _row_copy(y_ref, pos_ref[2 * r + 1], buf1, r, sem).start()
        return carry

    lax.fori_loop(0, tm, issue, 0, unroll=DMA_UNROLL)
    for buf in (buf0, buf1):
        pltpu.make_async_copy(y_ref.at[pl.ds(0, tm)], buf, sem).wait()
    gate = gate_ref[...]
    x = x_ref[...] + (buf0[...] * gate[:, 0:1] + buf1[...] * gate[:, 1:2])
    o_ref[...] = x * _rms_scale(x) * g_ref[...]


def _combine(x3, gates, pos_flat, y, g):
    R = x3.shape[0]
    tm = min(ROW_TILE, R)
    kern = functools.partial(_combine_kernel, tm=tm)
    return pl.pallas_call(
        kern,
        grid=(R // tm,),
        in_specs=[
            pl.BlockSpec((TOP_K * tm,), lambda i: (i,), memory_space=pltpu.SMEM),
            pl.BlockSpec((tm, D_MODEL), lambda i: (i, 0)),
            pl.BlockSpec((tm, TOP_K), lambda i: (i, 0)),
            pl.BlockSpec((1, D_MODEL), lambda i: (0, 0)),
            pl.BlockSpec(memory_space=pl.ANY),
        ],
        out_specs=pl.BlockSpec((tm, D_MODEL), lambda i: (i, 0)),
        out_shape=jax.ShapeDtypeStruct((R, D_MODEL), F32),
        scratch_shapes=[pltpu.VMEM((tm, D_MODEL), F32), pltpu.VMEM((tm, D_MODEL), F32),
                        pltpu.SemaphoreType.DMA(())],
        compiler_params=_params(("arbitrary",)),
        name="moe_combine_norm",
    )(pos_flat, x3, gates, g, y)


def _prep_weights(p):
    w_in = p["gdn_w_in"][0]
    w_kv = p["w_kv"]
    kcols = w_kv[:, :HKV * HD].reshape(D_MODEL, HKV, HD)
    vcols = w_kv[:, HKV * HD:].reshape(D_MODEL, HKV, HD)
    dup = lambda a: jnp.concatenate([a, a], axis=2).reshape(D_MODEL, HKV * LANES)
    w_kvd = jnp.concatenate([dup(kcols), dup(vcols)], axis=1)
    return dict(
        w_qk=w_in[:, :2 * QK_DIM].astype(BF16),
        w_v=w_in[:, 2 * QK_DIM:CONV_CH].astype(BF16),
        w_z=w_in[:, CONV_CH:CONV_CH + V_DIM].astype(BF16),
        w_ba=w_in[:, CONV_CH + V_DIM:].astype(BF16),
        w_out=p["gdn_w_out"][0].astype(BF16),
        ffn_w1=p["ffn_w1"][0].astype(BF16),
        ffn_w3=p["ffn_w3"][0].astype(BF16),
        ffn_w2=p["ffn_w2"][0].astype(BF16),
        w_kvd=w_kvd.astype(BF16),
        w_kvs=w_kv.astype(BF16),
        w_q=p["swa_w_q"][0].astype(BF16),
        w_o=p["swa_w_o"][0].astype(BF16),
        moe_w1=p["moe_w1"][0].astype(BF16),
        moe_w3=p["moe_w3"][0].astype(BF16),
        moe_w2=p["moe_w2"][0].astype(BF16),
    )


def _dup_heads(a):
    B, T = a.shape[:2]
    return jnp.concatenate([a, a], axis=3).reshape(B * T, HKV * LANES)


def _trunk(x, pos, conv0, s0, win_k, win_v, p, w):
    B, T, _ = x.shape
    R = B * T
    xf = x.reshape(R, D_MODEL)
    row = lambda a: a.reshape(1, -1)

    g_attn = row(p["norm_attn"][0])
    w_conv = p["gdn_w_conv"][0]
    nqk = 2 * QK_DIM
    qk, conv_qk = _qkv_conv(xf, g_attn, w["w_qk"], w_conv[:, :nqk], conv0[:, :, :nqk], B, T, True)
    v, conv_v = _qkv_conv(xf, g_attn, w["w_v"], w_conv[:, nqk:], conv0[:, :, nqk:], B, T, False)
    conv_new = jnp.concatenate([conv_qk, conv_v], axis=-1).reshape(B, -1, CONV_W - 1, CONV_CH)[:, -1]
    z, beta, gc, gct = _zbg(xf, g_attn, w["w_z"], w["w_ba"], p["gdn_a_log"][0], p["gdn_dt_bias"][0], B, T)
    u, wv, att_l = _delta_prep(qk, v, beta, gc, gct, B, T)
    og, s_new = _delta_scan(qk, u, wv, att_l, z, gc, s0, row(p["gdn_o_norm"][0]), B, T)
    x2 = _mixer_out_ffn(xf, og, w["w_out"], row(p["norm_ffn"][0]), w["ffn_w1"], w["ffn_w3"], w["ffn_w2"])

    keep = min(WINDOW, T) if win_k is None else T
    q, kvd, kvs = _qkv_rope(x2, row(p["norm_attn"][1]), row(p["kv_norm"]), w["w_q"], w["w_kvd"], w["w_kvs"],
                            _rope_tables(pos), B, T, keep)
    k_new = kvs[:, :HKV * HD].reshape(B, keep, HKV, HD)
    v_new = kvs[:, HKV * HD:].reshape(B, keep, HKV, HD)
    if win_k is None:
        keys, Tk = kvd, T
    else:
        hist = jnp.concatenate([_dup_heads(win_k), _dup_heads(win_v)], axis=1).astype(BF16)
        Tw = win_k.shape[1]
        keys = jnp.concatenate([hist.reshape(B, Tw, KV_DUP), kvd.reshape(B, T, KV_DUP)], axis=1)
        Tk = Tw + T
        keys = keys.reshape(B * Tk, KV_DUP)
    att = _attention(q, keys, row(p["swa_sinks"][0]), B, T, Tk)
    x3, h, eidx, gates, counts = _oproj_router(x2, att, w["w_o"], row(p["norm_ffn"][1]), p["moe_router"][0])

    te = min(EXPERT_TILE, max(TOP_K * R // N_EXPERTS, LANES))
    n_tiles = TOP_K * R // te + N_EXPERTS
    cnt = counts[0].astype(jnp.int32)
    tiles = (cnt + te - 1) // te
    tile_end = jnp.cumsum(tiles)
    base = (tile_end - tiles) * te
    n_used = tile_end[-1:]
    tile_expert = jnp.minimum(jnp.searchsorted(tile_end, jnp.arange(n_tiles, dtype=jnp.int32), side="right"),
                              N_EXPERTS - 1).astype(jnp.int32)
    pos_slot = _slots(eidx, base.astype(F32).reshape(1, N_EXPERTS)).reshape(-1)
    n_used = n_used.astype(jnp.int32)
    xs = _dispatch(h, pos_slot, cnt, base.astype(jnp.int32), n_used, n_tiles, te)
    ys = _expert_ffn(xs, tile_expert, n_used, w["moe_w1"], w["moe_w3"], w["moe_w2"], te)
    y = _combine(x3, gates, pos_slot, ys, row(p["final_norm"]))
    return (y.reshape(B, T, D_MODEL), conv_new[None], s_new[None], k_new, v_new)


def kernel(x_prompt, x_sample, cache_conv, state_delta, cache_k, cache_v, norm_attn, norm_ffn, gdn_w_in,
           gdn_w_conv, gdn_a_log, gdn_dt_bias, gdn_o_norm, gdn_w_out, kv_norm, w_kv, swa_w_q, swa_sinks, swa_w_o,
           ffn_w1, ffn_w3, ffn_w2, moe_router, moe_w1, moe_w3, moe_w2, final_norm):
    p = dict(norm_attn=norm_attn, norm_ffn=norm_ffn, gdn_w_in=gdn_w_in, gdn_w_conv=gdn_w_conv,
             gdn_a_log=gdn_a_log, gdn_dt_bias=gdn_dt_bias, gdn_o_norm=gdn_o_norm, gdn_w_out=gdn_w_out,
             kv_norm=kv_norm, w_kv=w_kv, swa_w_q=swa_w_q, swa_sinks=swa_sinks, swa_w_o=swa_w_o,
             ffn_w1=ffn_w1, ffn_w3=ffn_w3, ffn_w2=ffn_w2, moe_router=moe_router,
             moe_w1=moe_w1, moe_w3=moe_w3, moe_w2=moe_w2, final_norm=final_norm)
    w = _prep_weights(p)
    Bp, Tp, _ = x_prompt.shape
    Bs, Ts, _ = x_sample.shape
    conv0 = jnp.zeros((Bp, CONV_W - 1, CONV_CH), F32)
    s0 = jnp.zeros((Bp, HV, DK, DV), F32)
    y_p, conv_p, delta_p, k_p, v_p = _trunk(x_prompt, jnp.arange(Tp, dtype=jnp.int32), conv0, s0,
                                            None, None, p, w)
    pos_s = PAST_LEN + jnp.arange(Ts, dtype=jnp.int32)
    y_s, conv_s, delta_s, k_s, v_s = _trunk(x_sample, pos_s, cache_conv[0], state_delta[0],
                                            cache_k, cache_v, p, w)
    return (y_p, y_s, conv_p, delta_p, k_p, v_p, conv_s, delta_s, k_s, v_s)
```

```python
import functools

import jax
import jax.numpy as jnp
import numpy as np
from jax import lax
from jax.experimental import pallas as pl
from jax.experimental.pallas import tpu as pltpu

F32 = jnp.float32
BF16 = jnp.bfloat16

D_MODEL = 1024
CHUNK = 64
HK = 8
HV = 16
DK = 128
DV = 128
QK_DIM = HK * DK
V_DIM = HV * DV
CONV_CH = 2 * QK_DIM + V_DIM
CONV_W = 4
HQ = 16
HKV = 4
HD = 64
WINDOW = 128
ROT_DIM = HD // 4
ROPE_THETA = 500000.0
D_FF = 2816
N_EXPERTS = 8
TOP_K = 2
EPS = 1e-6
PAST_LEN = 2048

LANES = 128
ROW_TILE = 512
EXPERT_TILE = 512
VMEM_LIMIT = 48 * 1024 * 1024
VMEM_LIMIT_BIG = 56 * 1024 * 1024


def _params(sem, vmem=VMEM_LIMIT):
    return pltpu.CompilerParams(dimension_semantics=sem, vmem_limit_bytes=vmem)


def _rms_scale(x):
    return lax.rsqrt(jnp.mean(x * x, axis=-1, keepdims=True) + EPS)


def _bdot(a, b):
    return jnp.dot(a.astype(BF16), b.astype(BF16), preferred_element_type=F32)


def _bdot_nt(a, b):
    return lax.dot_general(a.astype(BF16), b.astype(BF16), (((1,), (1,)), ((), ())),
                           preferred_element_type=F32)


def _bdot_tn(a, b):
    return lax.dot_general(a.astype(BF16), b.astype(BF16), (((0,), (0,)), ((), ())),
                           preferred_element_type=F32)


def _split3(x):
    x1 = x.astype(BF16)
    r1 = x - x1.astype(F32)
    x2 = r1.astype(BF16)
    x3 = (r1 - x2.astype(F32)).astype(BF16)
    return x1, x2, x3


def _split2(x):
    x1 = x.astype(BF16)
    x2 = (x - x1.astype(F32)).astype(BF16)
    return x1, x2


def _dot_hi(a, b):
    a1, a2 = _split2(a)
    b1, b2 = _split2(b)
    d = functools.partial(jnp.dot, preferred_element_type=F32)
    return d(a1, b1) + (d(a1, b2) + d(a2, b1))


def _spread(x, emat, pieces):
    parts = _split3(x)[:pieces]
    out = jnp.dot(parts[0], emat, preferred_element_type=F32)
    for part in parts[1:]:
        out = out + jnp.dot(part, emat, preferred_element_type=F32)
    return out


def _head_spread_matrices():
    h = np.arange(HV)[:, None]
    full = (np.arange(HV * DV)[None, :] // DV == h)
    pair = (np.arange(HV * CHUNK)[None, :] // CHUNK == h)
    return jnp.asarray(full.astype(np.float32), dtype=BF16), jnp.asarray(pair.astype(np.float32), dtype=BF16)


def _sigmoid(x):
    return 0.5 * jnp.tanh(0.5 * x) + 0.5


def _silu(x):
    half = 0.5 * x
    return half * jnp.tanh(half) + half


CONV_SUB = 256


def _qkv_conv_kernel(x_ref, g_ref, w_ref, wc_ref, cs_ref, out_ref, cst_ref,
                     h_scr, pad_scr, carry_scr, *, tm, tn, l2):
    ti = pl.program_id(1)
    j = pl.program_id(2)

    @pl.when(j == 0)
    def _():
        x = x_ref[...]
        h_scr[...] = (x * _rms_scale(x) * g_ref[...]).astype(BF16)

    @pl.when(ti == 0)
    def _():
        carry_scr[j, 5:8, :] = cs_ref[0]

    h = h_scr[...]
    wc = wc_ref[...]
    scale = jnp.where(j == 0, DK ** -0.5, 1.0).astype(F32)
    cols = lambda k: slice(k * CONV_SUB, (k + 1) * CONV_SUB)

    def project(k):
        return jnp.dot(h, w_ref[:, cols(k)], preferred_element_type=F32)

    def finish(k, proj):
        c = cols(k)
        pad_scr[0:8, c] = carry_scr[j, :, c]
        pad_scr[8:, c] = proj
        acc = proj * wc[3:4, c]
        for s in range(1, CONV_W):
            acc = acc + pad_scr[8 - s:8 - s + tm, c] * wc[3 - s:4 - s, c]
        carry_scr[j, :, c] = pad_scr[tm:tm + 8, c]
        cst_ref[0, :, c] = pad_scr[tm + 5:tm + 8, c]
        y = _silu(acc)
        if not l2:
            out_ref[:, c] = y.astype(out_ref.dtype)
            return
        for hh in range(CONV_SUB // DK):
            seg = y[:, hh * DK:(hh + 1) * DK]
            inv = lax.rsqrt(jnp.sum(seg * seg, axis=-1, keepdims=True) + EPS) * scale
            lanes = slice(k * CONV_SUB + hh * DK, k * CONV_SUB + (hh + 1) * DK)
            out_ref[:, lanes] = (seg * inv).astype(out_ref.dtype)

    pending = project(0)
    for k in range(tn // CONV_SUB):
        upcoming = project(k + 1) if k + 1 < tn // CONV_SUB else None
        finish(k, pending)
        pending = upcoming


def _qkv_conv(x, g, w, w_conv, conv_state, B, T, l2):
    R = B * T
    n = w.shape[1]
    tm = min(2 * ROW_TILE, T)
    tn = QK_DIM
    nt = T // tm
    nj = n // tn
    kern = functools.partial(_qkv_conv_kernel, tm=tm, tn=tn, l2=l2)
    return pl.pallas_call(
        kern,
        grid=(B, nt, nj),
        in_specs=[
            pl.BlockSpec((tm, D_MODEL), lambda b, t, j: (b * nt + t, 0)),
            pl.BlockSpec((1, D_MODEL), lambda b, t, j: (0, 0)),
            pl.BlockSpec((D_MODEL, tn), lambda b, t, j: (0, j)),
            pl.BlockSpec((CONV_W, tn), lambda b, t, j: (0, j)),
            pl.BlockSpec((1, CONV_W - 1, tn), lambda b, t, j: (b, 0, j)),
        ],
        out_specs=[
            pl.BlockSpec((tm, tn), lambda b, t, j: (b * nt + t, j)),
            pl.BlockSpec((1, CONV_W - 1, tn), lambda b, t, j: (b * nt + t, 0, j)),
        ],
        out_shape=[
            jax.ShapeDtypeStruct((R, n), BF16),
            jax.ShapeDtypeStruct((B * nt, CONV_W - 1, n), F32),
        ],
        scratch_shapes=[
            pltpu.VMEM((tm, D_MODEL), BF16),
            pltpu.VMEM((tm + 8, tn), F32),
            pltpu.VMEM((nj, 8, tn), F32),
        ],
        compiler_params=_params(("arbitrary", "arbitrary", "arbitrary")),
        name="gdn_qk_conv" if l2 else "gdn_v_conv",
    )(x, g, w, w_conv, conv_state)


def _softplus(x):
    return jnp.maximum(x, 0.0) + jnp.log(1.0 + jnp.exp(-jnp.abs(x)))


def _zbg_kernel(x_ref, g_ref, w_ref, alog_ref, dtb_ref, tri_ref, z_ref, beta_ref, gc_ref, gct_ref, *, tm):
    x = x_ref[...]
    h = (x * _rms_scale(x) * g_ref[...]).astype(BF16)
    ba = jnp.dot(h, w_ref[:, V_DIM:], preferred_element_type=F32)
    z_ref[...] = jnp.dot(h, w_ref[:, :V_DIM], preferred_element_type=F32).astype(z_ref.dtype)
    beta_ref[...] = _sigmoid(ba[:, :HV])
    gcol = -jnp.exp(alog_ref[...]) * _softplus(ba[:, HV:2 * HV] + dtb_ref[...])
    tri = tri_ref[...]
    g1, g2, g3 = _split3(gcol)
    d = functools.partial(jnp.dot, preferred_element_type=F32)
    gc = d(tri, g1) + (d(tri, g2) + d(tri, g3))
    gc_ref[...] = gc
    r = lax.broadcasted_iota(jnp.int32, (HV, HV), 0)
    c = lax.broadcasted_iota(jnp.int32, (HV, HV), 1)
    sel = (c == jnp.where(r < HV // 2, 2 * r, 2 * r - (HV - 1))).astype(BF16)
    tn = lambda a: lax.dot_general(sel, a, (((1,), (1,)), ((), ())), preferred_element_type=F32)
    c1, c2, c3 = _split3(gc)
    gct = tn(c1) + (tn(c2) + tn(c3))
    for ch in range(tm // CHUNK):
        sl = slice(ch * CHUNK, (ch + 1) * CHUNK)
        gct_ref[0, ch] = jnp.concatenate([gct[:HV // 2, sl], gct[HV // 2:, sl]], axis=1)


def _zbg(x, g, w_z, w_ba, a_log, dt_bias, B, T):
    R = B * T
    tm = min(ROW_TILE, T)
    nt = T // tm
    nc = T // CHUNK
    idx = np.arange(tm)
    tri = jnp.asarray(((idx[:, None] // CHUNK == idx[None, :] // CHUNK) & (idx[:, None] >= idx[None, :]))
                      .astype(np.float32), dtype=BF16)
    w_zba = jnp.concatenate([w_z, w_ba, jnp.zeros((D_MODEL, LANES - 2 * HV), BF16)], axis=1)
    full = lambda shape: pl.BlockSpec(shape, lambda b, t: (0,) * len(shape))
    kern = functools.partial(_zbg_kernel, tm=tm)
    return pl.pallas_call(
        kern,
        grid=(B, nt),
        in_specs=[
            pl.BlockSpec((tm, D_MODEL), lambda b, t: (b * nt + t, 0)),
            full((1, D_MODEL)),
            full((D_MODEL, V_DIM + LANES)),
            full((1, HV)), full((1, HV)),
            full((tm, tm)),
        ],
        out_specs=[
            pl.BlockSpec((tm, V_DIM), lambda b, t: (b * nt + t, 0)),
            pl.BlockSpec((tm, HV), lambda b, t: (b * nt + t, 0)),
            pl.BlockSpec((tm, HV), lambda b, t: (b * nt + t, 0)),
            pl.BlockSpec((1, tm // CHUNK, HV // 2, 2 * CHUNK), lambda b, t: (b, t, 0, 0)),
        ],
        out_shape=[
            jax.ShapeDtypeStruct((R, V_DIM), BF16),
            jax.ShapeDtypeStruct((R, HV), F32),
            jax.ShapeDtypeStruct((R, HV), F32),
            jax.ShapeDtypeStruct((B, nc, HV // 2, 2 * CHUNK), F32),
        ],
        compiler_params=_params(("arbitrary", "arbitrary")),
        name="gdn_z_beta_decay",
    )(x, g, w_zba, a_log.reshape(1, HV), dt_bias.reshape(1, HV), tri)


PAIR = HV // HK


def _pair_rows(x, lo):
    zero = jnp.zeros((), x.dtype)
    return jnp.concatenate([jnp.where(lo, x, zero), jnp.where(lo, zero, x)], axis=0)


def _pair_products(lhs_list, rhs_list, lo):
    def split(a):
        hi = a.astype(BF16)
        return hi, (a - hi.astype(F32)).astype(BF16)

    ops = []
    for lhs, rhs in zip(lhs_list, rhs_list):
        lh, ll = split(lhs)
        rh, rl = split(rhs)
        rh2 = _pair_rows(rh, lo)
        ops.append((jnp.concatenate([lh, lh, ll], axis=1), jnp.concatenate([rh2, _pair_rows(rl, lo), rh2], axis=0)))
    return [jnp.dot(a, b, preferred_element_type=F32) for a, b in ops]


INV_BASE = 16


def _pair_unit_lower_inverse(low_list, lo, eye2, same_block):
    C = low_list[0].shape[0]
    base = same_block(INV_BASE)
    xs = [jnp.where(base, -low, 0.0) for low in low_list]
    accs = [eye2 + x for x in xs]
    pws = _pair_products(xs, xs, lo)
    n = 2
    while 2 * n < INV_BASE:
        prods = _pair_products([jnp.concatenate([a, p], axis=0) for a, p in zip(accs, pws)], pws, lo)
        accs = [a + pr[:C] for a, pr in zip(accs, prods)]
        pws = [pr[C:] for pr in prods]
        n *= 2
    prods = _pair_products(accs, pws, lo)
    tinvs = [a + pr for a, pr in zip(accs, prods)]
    size = 2 * INV_BASE
    while size <= C:
        below = same_block(size) & jnp.logical_not(same_block(size // 2))
        offs = [jnp.where(below, low, 0.0) for low in low_list]
        inner = _pair_products(offs, tinvs, lo)
        outer = _pair_products(tinvs, inner, lo)
        tinvs = [t - o for t, o in zip(tinvs, outer)]
        size *= 2
    return tinvs


def _delta_prep_kernel(q_ref, k_ref, v_ref, beta_ref, gc_ref, gct_ref, epair_ref, u_ref, w_ref, a_ref, *, cb):
    C = CHUNK
    ri = lax.broadcasted_iota(jnp.int32, (C, 2 * C), 0)
    li = lax.broadcasted_iota(jnp.int32, (C, 2 * C), 1)
    lm = li & (C - 1)
    lo = li < C
    incl = ri >= lm
    strict = ri > lm
    eye2 = (ri == lm).astype(F32)
    units = [(c, p) for c in range(cb) for p in range(HK)]
    rows = lambda c: slice(c * C, (c + 1) * C)

    grams = []
    for c, p in units:
        kb = k_ref[rows(c), p * DK:(p + 1) * DK]
        qb = q_ref[rows(c), p * DK:(p + 1) * DK]
        grams.append(lax.dot_general(jnp.concatenate([kb, qb], axis=0), jnp.concatenate([kb, kb], axis=0),
                                     (((1,), (1,)), ((), ())), preferred_element_type=F32))

    epair = epair_ref[...]
    g2s, b2s, betas, begs = [], [], [], []
    for c in range(cb):
        beta = beta_ref[rows(c), :]
        gc = gc_ref[rows(c), :]
        g2s.append(_spread(gc, epair, 3))
        b2s.append(_spread(beta, epair, 2))
        betas.append(beta)
        begs.append(beta * jnp.exp(gc))

    lows = []
    for (c, p), gram in zip(units, grams):
        pt = slice(p * 2 * C, (p + 1) * 2 * C)
        grow2 = gct_ref[c, p:p + 1, :]
        decay2 = jnp.exp(jnp.where(incl, g2s[c][:, pt] - grow2, -jnp.inf))
        a_ref[rows(c), pt] = (gram[C:] * decay2).astype(a_ref.dtype)
        lows.append(jnp.where(strict, b2s[c][:, pt] * gram[:C] * decay2, 0.0))

    same_block = lambda s: (ri // s) == (lm // s)
    tinvs = _pair_unit_lower_inverse(lows, lo, eye2, same_block)

    ops = []
    for (c, p), tinv in zip(units, tinvs):
        kf = k_ref[rows(c), p * DK:(p + 1) * DK].astype(F32)
        rhs = []
        for h in (PAIR * p, PAIR * p + 1):
            ht = slice(h * DV, (h + 1) * DV)
            vf = v_ref[rows(c), ht].astype(F32)
            rhs.append(jnp.concatenate([vf * betas[c][:, h:h + 1], kf * begs[c][:, h:h + 1]], axis=1))
        ops.append((_pair_rows(tinv.astype(BF16), lo), jnp.concatenate(rhs, axis=0).astype(BF16)))
    for (c, p), (lhs, rhs) in zip(units, ops):
        uw = jnp.dot(lhs, rhs, preferred_element_type=F32)
        for i, h in enumerate((PAIR * p, PAIR * p + 1)):
            u_ref[rows(c), h * DV:(h + 1) * DV] = uw[i * C:(i + 1) * C, :DV].astype(u_ref.dtype)
            w_ref[rows(c), h * DV:(h + 1) * DV] = uw[i * C:(i + 1) * C, DV:].astype(w_ref.dtype)


def _delta_prep(qk, v, beta, gc, gct, B, T):
    R = B * T
    cb = 2 if T % (2 * CHUNK) == 0 else 1
    rb = cb * CHUNK
    gct2 = gct.reshape(R // CHUNK, HK, 2 * CHUNK)
    kern = functools.partial(_delta_prep_kernel, cb=cb)
    return pl.pallas_call(
        kern,
        grid=(R // rb,),
        in_specs=[
            pl.BlockSpec((rb, QK_DIM), lambda i: (i, 0)),
            pl.BlockSpec((rb, QK_DIM), lambda i: (i, 1)),
            pl.BlockSpec((rb, V_DIM), lambda i: (i, 0)),
            pl.BlockSpec((rb, HV), lambda i: (i, 0)),
            pl.BlockSpec((rb, HV), lambda i: (i, 0)),
            pl.BlockSpec((cb, HK, 2 * CHUNK), lambda i: (i, 0, 0)),
            pl.BlockSpec((HV, HV * CHUNK), lambda i: (0, 0)),
        ],
        out_specs=[
            pl.BlockSpec((rb, V_DIM), lambda i: (i, 0)),
            pl.BlockSpec((rb, V_DIM), lambda i: (i, 0)),
            pl.BlockSpec((rb, HV * CHUNK), lambda i: (i, 0)),
        ],
        out_shape=[
            jax.ShapeDtypeStruct((R, V_DIM), BF16),
            jax.ShapeDtypeStruct((R, V_DIM), BF16),
            jax.ShapeDtypeStruct((R, HV * CHUNK), BF16),
        ],
        compiler_params=_params(("arbitrary",)),
        name="gdn_delta_prep",
    )(qk, qk, v, beta, gc, gct2, _head_spread_matrices()[1])


def _delta_scan_kernel(q_ref, k_ref, u_ref, w_ref, a_ref, z_ref, gc_ref, s0_ref, on_ref, efull_ref,
                       og_ref, s_ref, *, bb):
    c = pl.program_id(1)

    @pl.when(c == 0)
    def _():
        s_ref[...] = s0_ref[...]

    C = CHUNK
    lo = lax.broadcasted_iota(jnp.int32, (C, 2 * C), 1) < C
    onorm = on_ref[...]
    units = [(b, h) for b in range(bb) for h in range(HV)]
    pairs = [(b, p) for b in range(bb) for p in range(HK)]

    efull = efull_ref[...]
    egs, kgs, dgs = [], [], []
    for b in range(bb):
        gc = gc_ref[b]
        g_last = gc[C - 1:C, :]
        egs.append(_spread(jnp.exp(gc), efull, 2))
        kgs.append(_spread(jnp.exp(g_last - gc), efull, 2))
        dgs.append(_spread(jnp.exp(gc[C - 8:C, :]), efull, 3)[7:8, :])

    ops = []
    kds = []
    dec = []
    for b, h in units:
        p = h // PAIR
        ht = slice(h * DV, (h + 1) * DV)
        qg = (q_ref[b, :, p * DK:(p + 1) * DK].astype(F32) * egs[b][:, ht]).astype(BF16)
        kds.append((k_ref[b, :, p * DK:(p + 1) * DK].astype(F32) * kgs[b][:, ht]).astype(BF16))
        dec.append(dgs[b][:, ht])
        ops.append((jnp.concatenate([w_ref[b, :, ht], qg], axis=0), s_ref[b, h].astype(BF16)))
    wqs = [jnp.dot(a, s, preferred_element_type=F32) for a, s in ops]

    vns = [(u_ref[b, :, h * DV:(h + 1) * DV].astype(F32) - wq[:C]).astype(BF16) for (b, h), wq in zip(units, wqs)]

    oparts = []
    for i, (b, p) in enumerate(pairs):
        att = _pair_rows(a_ref[b, :, p * 2 * C:(p + 1) * 2 * C], lo)
        vn2 = jnp.concatenate([vns[PAIR * i], vns[PAIR * i + 1]], axis=0)
        oparts.append(jnp.dot(att, vn2, preferred_element_type=F32))
    sds = [lax.dot_general(kd, vn, (((0,), (0,)), ((), ())), preferred_element_type=F32)
           for kd, vn in zip(kds, vns)]

    for i, (b, h) in enumerate(units):
        s_ref[b, h] = s_ref[b, h] * dec[i] + sds[i]
        o = wqs[i][C:] + oparts[i // PAIR][(h % PAIR) * C:(h % PAIR + 1) * C]
        on = o * _rms_scale(o) * onorm
        zf = z_ref[b, :, h * DV:(h + 1) * DV].astype(F32)
        og_ref[b, :, h * DV:(h + 1) * DV] = (on * _silu(zf)).astype(og_ref.dtype)


def _delta_scan(qk, u, w, att, z, gc, s0, o_norm, B, T):
    nc = T // CHUNK
    bb = 4 if B % 4 == 0 else (2 if B % 2 == 0 else 1)
    v3 = lambda a: a.reshape(B, T, a.shape[-1])
    blk = lambda n, j=0: pl.BlockSpec((bb, CHUNK, n), lambda b, c: (b, c, j))
    kern = functools.partial(_delta_scan_kernel, bb=bb)
    og, s_new = pl.pallas_call(
        kern,
        grid=(B // bb, nc),
        in_specs=[
            blk(QK_DIM, 0), blk(QK_DIM, 1), blk(V_DIM), blk(V_DIM), blk(HV * CHUNK), blk(V_DIM), blk(HV),
            pl.BlockSpec((bb, HV, DK, DV), lambda b, c: (b, 0, 0, 0)),
            pl.BlockSpec((1, DV), lambda b, c: (0, 0)),
            pl.BlockSpec((HV, HV * DV), lambda b, c: (0, 0)),
        ],
        out_specs=[
            blk(V_DIM),
            pl.BlockSpec((bb, HV, DK, DV), lambda b, c: (b, 0, 0, 0)),
        ],
        out_shape=[
            jax.ShapeDtypeStruct((B, T, V_DIM), BF16),
            jax.ShapeDtypeStruct((B, HV, DK, DV), F32),
        ],
        compiler_params=_params(("arbitrary", "arbitrary")),
        name="gdn_delta_scan",
    )(v3(qk), v3(qk), v3(u), v3(w), v3(att), v3(z), v3(gc), s0, o_norm, _head_spread_matrices()[0])
    return og.reshape(B * T, V_DIM), s_new


FF_TILE = 1408
RESIDENT = pl.Buffered(1)


def _swiglu(h, w1_ref, w3_ref, w2_ref, lead=()):
    out = None
    for f in range(D_FF // FF_TILE):
        sl = slice(f * FF_TILE, (f + 1) * FF_TILE)
        a = jnp.dot(h, w1_ref[lead + (slice(None), sl)], preferred_element_type=F32)
        b = jnp.dot(h, w3_ref[lead + (slice(None), sl)], preferred_element_type=F32)
        mid = (_silu(a) * b).astype(BF16)
        part = jnp.dot(mid, w2_ref[lead + (sl, slice(None))], preferred_element_type=F32)
        out = part if out is None else out + part
    return out


def _mixer_out_ffn_kernel(x_ref, a_ref, wo_ref, g_ref, w1_ref, w3_ref, w2_ref, o_ref):
    x1 = x_ref[...] + jnp.dot(a_ref[...], wo_ref[...], preferred_element_type=F32)
    h = (x1 * _rms_scale(x1) * g_ref[...]).astype(BF16)
    o_ref[...] = x1 + _swiglu(h, w1_ref, w3_ref, w2_ref)


def _mixer_out_ffn(x, a, w_out, g, w1, w3, w2):
    R = x.shape[0]
    K = a.shape[1]
    tm = min(ROW_TILE, R)
    held = lambda shape: pl.BlockSpec(shape, lambda i: (0,) * len(shape), pipeline_mode=RESIDENT)
    return pl.pallas_call(
        _mixer_out_ffn_kernel,
        grid=(R // tm,),
        in_specs=[
            pl.BlockSpec((tm, D_MODEL), lambda i: (i, 0)),
            pl.BlockSpec((tm, K), lambda i: (i, 0)),
            held((K, D_MODEL)),
            held((1, D_MODEL)),
            held((D_MODEL, D_FF)), held((D_MODEL, D_FF)), held((D_FF, D_MODEL)),
        ],
        out_specs=pl.BlockSpec((tm, D_MODEL), lambda i: (i, 0)),
        out_shape=jax.ShapeDtypeStruct((R, D_MODEL), F32),
        compiler_params=_params(("arbitrary",), VMEM_LIMIT_BIG),
        name="gdn_out_dense_swiglu",
    )(x, a, w_out, g, w1, w3, w2)


KV_DUP = 2 * HKV * LANES
KV_STD = 2 * HKV * HD


def _rope(x, cos, sm, sp):
    up = pltpu.roll(x, LANES - ROT_DIM // 2, 1)
    dn = pltpu.roll(x, ROT_DIM // 2, 1)
    return x * cos + up * sm + dn * sp


def _qkv_rope_kernel(x_ref, gq_ref, gkv_ref, wq_ref, wkv_ref, wkvs_ref, cos_ref, sm_ref, sp_ref,
                     q_ref, kvd_ref, kvs_ref, *, nt, keep):
    x = x_ref[...]
    xr = x * _rms_scale(x)
    hq = (xr * gq_ref[...]).astype(BF16)
    hkv = (xr * gkv_ref[...]).astype(BF16)
    cos = cos_ref[...]
    sm = sm_ref[...]
    sp = sp_ref[...]
    q = jnp.dot(hq, wq_ref[...], preferred_element_type=F32)
    for t in range(HQ * HD // LANES):
        sl = slice(t * LANES, (t + 1) * LANES)
        q_ref[:, sl] = (_rope(q[:, sl], cos, sm, sp) * (HD ** -0.5)).astype(q_ref.dtype)
    kv = jnp.dot(hkv, wkv_ref[...], preferred_element_type=F32)
    half = KV_DUP // 2
    for t in range(HKV):
        sl = slice(t * LANES, (t + 1) * LANES)
        kvd_ref[:, sl] = _rope(kv[:, sl], cos, sm, sp).astype(kvd_ref.dtype)
    kvd_ref[:, half:] = kv[:, half:].astype(kvd_ref.dtype)

    @pl.when(pl.program_id(0) % nt == nt - 1)
    def _():
        tm = x.shape[0]
        rows = slice(tm - keep, tm)
        kvs = jnp.dot(hkv[rows], wkvs_ref[...], preferred_element_type=F32)
        for t in range(HKV * HD // LANES):
            sl = slice(t * LANES, (t + 1) * LANES)
            kvs_ref[:, sl] = _rope(kvs[:, sl], cos[rows], sm[rows], sp[rows])
        kvs_ref[:, HKV * HD:] = kvs[:, HKV * HD:]


def _rope_tables(pos):
    half = ROT_DIM // 2
    inv = jnp.power(ROPE_THETA, -jnp.arange(half, dtype=F32) * 2.0 / ROT_DIM)
    ang = pos.astype(F32)[:, None] * inv[None, :]
    cos = jnp.cos(ang)
    sin = jnp.sin(ang)
    T = pos.shape[0]
    one = jnp.ones((T, HD - ROT_DIM), F32)
    zero = jnp.zeros((T, HD - ROT_DIM), F32)
    zh = jnp.zeros((T, half), F32)
    c64 = jnp.concatenate([cos, cos, one], axis=1)
    sm64 = jnp.concatenate([-sin, zh, zero], axis=1)
    sp64 = jnp.concatenate([zh, sin, zero], axis=1)
    dup = lambda a: jnp.concatenate([a, a], axis=1)
    return dup(c64), dup(sm64), dup(sp64)


def _qkv_rope(x, gq, gkv, w_q, w_kvd, w_kvs, tables, B, T, keep):
    R = B * T
    tm = min(ROW_TILE, T)
    nt = T // tm
    assert keep <= tm
    tab = pl.BlockSpec((tm, LANES), lambda i: (i % nt, 0))
    kern = functools.partial(_qkv_rope_kernel, nt=nt, keep=keep)
    return pl.pallas_call(
        kern,
        grid=(R // tm,),
        in_specs=[
            pl.BlockSpec((tm, D_MODEL), lambda i: (i, 0)),
            pl.BlockSpec((1, D_MODEL), lambda i: (0, 0)),
            pl.BlockSpec((1, D_MODEL), lambda i: (0, 0)),
            pl.BlockSpec((D_MODEL, HQ * HD), lambda i: (0, 0)),
            pl.BlockSpec((D_MODEL, KV_DUP), lambda i: (0, 0)),
            pl.BlockSpec((D_MODEL, KV_STD), lambda i: (0, 0)),
            tab, tab, tab,
        ],
        out_specs=[
            pl.BlockSpec((tm, HQ * HD), lambda i: (i, 0)),
            pl.BlockSpec((tm, KV_DUP), lambda i: (i, 0)),
            pl.BlockSpec((keep, KV_STD), lambda i: (i // nt, 0)),
        ],
        out_shape=[
            jax.ShapeDtypeStruct((R, HQ * HD), BF16),
            jax.ShapeDtypeStruct((R, KV_DUP), BF16),
            jax.ShapeDtypeStruct((B * keep, KV_STD), F32),
        ],
        compiler_params=_params(("arbitrary",)),
        name="swa_qkv_rope",
    )(x, gq, gkv, w_q, w_kvd, w_kvs, *tables)


KEY_CHUNKS = WINDOW // CHUNK + 1
KEY_PAD = 256


def _attn_kernel(*refs, off, cq):
    q_ref = refs[0]
    krefs = refs[1:KEY_CHUNKS + cq]
    sink_ref, ones_ref, o_ref = refs[KEY_CHUNKS + cq:]
    c = pl.program_id(1)
    lane = lax.broadcasted_iota(jnp.int32, (KEY_CHUNKS * CHUNK, LANES), 1)
    lo = lane < HD
    col = lax.broadcasted_iota(jnp.int32, (1, KEY_PAD), 1)
    sink_col = col == KEY_CHUNKS * CHUNK
    zpad = jnp.zeros((KEY_PAD - KEY_CHUNKS * CHUNK, LANES), BF16)
    zero = jnp.zeros((), BF16)
    tiles = HQ // HKV // 2
    ones_blk = ones_ref[...]
    units = [(u, g) for u in range(cq) for g in range(HKV)]

    vblks = []
    scores = []
    for u, g in units:
        kr = krefs[u:u + KEY_CHUNKS]
        kg = jnp.concatenate([r[:, g * LANES:(g + 1) * LANES] for r in kr], axis=0)
        vg = jnp.concatenate([r[:, KV_DUP // 2 + g * LANES:KV_DUP // 2 + (g + 1) * LANES] for r in kr], axis=0)
        kblk = jnp.concatenate([jnp.where(lo, kg, zero), zpad, jnp.where(lo, zero, kg), zpad], axis=0)
        vblk = jnp.concatenate([jnp.where(lo, vg, zero), zpad, jnp.where(lo, zero, vg), zpad], axis=0)
        vblks.append(jnp.concatenate([vblk, ones_blk], axis=1))
        qg = jnp.concatenate([q_ref[u * CHUNK:(u + 1) * CHUNK, (g * tiles + p) * LANES:(g * tiles + p + 1) * LANES]
                              for p in range(tiles)], axis=0)
        scores.append(lax.dot_general(qg, kblk, (((1,), (1,)), ((), ())), preferred_element_type=F32))

    probs = []
    for (u, g), sc in zip(units, scores):
        first = c * cq + u + off - (KEY_CHUNKS - 1)
        valid = (col < KEY_CHUNKS * CHUNK) & (col // CHUNK + first >= 0)
        base_bias = jnp.where(valid, 0.0, -jnp.inf).astype(F32)
        rows = []
        for p in range(tiles):
            t = g * tiles + p
            ps = []
            for hh in range(2):
                bias = jnp.where(sink_col, sink_ref[:, 2 * t + hh:2 * t + hh + 1], base_bias)
                sh = sc[p * CHUNK:(p + 1) * CHUNK, hh * KEY_PAD:(hh + 1) * KEY_PAD] + bias
                m = jnp.max(sh, axis=-1, keepdims=True)
                ps.append(jnp.exp(sh - m).astype(BF16))
            rows.append(jnp.concatenate(ps, axis=1))
        probs.append(jnp.concatenate(rows, axis=0))

    outs = [jnp.dot(pr, vb, preferred_element_type=F32) for pr, vb in zip(probs, vblks)]
    for (u, g), out in zip(units, outs):
        for p in range(tiles):
            t = g * tiles + p
            rows = slice(p * CHUNK, (p + 1) * CHUNK)
            o_ref[u * CHUNK:(u + 1) * CHUNK, t * LANES:(t + 1) * LANES] = (
                out[rows, :LANES] / out[rows, LANES:]).astype(o_ref.dtype)


def _attention(q, kvd, sinks, B, Tq, Tk):
    ncq = Tq // CHUNK
    nck = Tk // CHUNK
    off = nck - ncq
    cq = 4 if ncq % 4 == 0 else (2 if ncq % 2 == 0 else 1)
    kern = functools.partial(_attn_kernel, off=off, cq=cq)
    blk_row = np.arange(2 * KEY_PAD)[:, None] // KEY_PAD
    blk_lane = np.arange(LANES)[None, :] // HD
    ones_blk = jnp.asarray((blk_row == blk_lane).astype(np.float32), dtype=BF16)

    def kspec(i):
        return pl.BlockSpec((CHUNK, KV_DUP),
                            lambda b, c: (b * nck + jnp.maximum(c * cq + off - (KEY_CHUNKS - 1) + i, 0), 0))

    qspec = pl.BlockSpec((cq * CHUNK, HQ * HD), lambda b, c: (b * (ncq // cq) + c, 0))
    return pl.pallas_call(
        kern,
        grid=(B, ncq // cq),
        in_specs=[qspec] + [kspec(i) for i in range(KEY_CHUNKS + cq - 1)] + [
            pl.BlockSpec((1, HQ), lambda b, c: (0, 0)),
            pl.BlockSpec((2 * KEY_PAD, LANES), lambda b, c: (0, 0)),
        ],
        out_specs=qspec,
        out_shape=jax.ShapeDtypeStruct((B * Tq, HQ * HD), BF16),
        compiler_params=_params(("arbitrary", "arbitrary")),
        name="swa_attention",
    )(q, *([kvd] * (KEY_CHUNKS + cq - 1)), sinks, ones_blk)


def _oproj_router_kernel(x_ref, a_ref, w_ref, g_ref, wr_ref, x3_ref, h_ref, e_ref, gate_ref, cnt_ref):
    i = pl.program_id(0)
    x3 = x_ref[...] + jnp.dot(a_ref[...], w_ref[...], preferred_element_type=F32)
    x3_ref[...] = x3
    h = x3 * _rms_scale(x3) * g_ref[...]
    h_ref[...] = h
    h1, h2 = _split2(h)
    r1, r2 = _split2(wr_ref[...])
    part = jnp.dot(h1, jnp.concatenate([r1, r2], axis=1), preferred_element_type=F32)
    logits = part[:, :N_EXPERTS] + (part[:, N_EXPERTS:] + jnp.dot(h2, r1, preferred_element_type=F32))
    eidx = lax.broadcasted_iota(jnp.int32, logits.shape, 1)
    m1 = jnp.max(logits, axis=-1, keepdims=True)
    i1 = jnp.min(jnp.where(logits == m1, eidx, N_EXPERTS), axis=-1, keepdims=True)
    rest = jnp.where(eidx == i1, -jnp.inf, logits)
    m2 = jnp.max(rest, axis=-1, keepdims=True)
    i2 = jnp.min(jnp.where(rest == m2, eidx, N_EXPERTS), axis=-1, keepdims=True)
    e2 = jnp.exp(m2 - m1)
    den = 1.0 + e2
    e_ref[...] = jnp.concatenate([i1, i2], axis=1)
    gate_ref[...] = jnp.concatenate([1.0 / den, e2 / den], axis=1)
    hot = ((eidx == i1) | (eidx == i2)).astype(F32)

    @pl.when(i == 0)
    def _():
        cnt_ref[...] = jnp.zeros_like(cnt_ref)

    cnt_ref[...] += jnp.sum(hot, axis=0, keepdims=True)


def _oproj_router(x, a, w_o, g, w_r):
    R = x.shape[0]
    tm = min(ROW_TILE, R)
    row = lambda n: pl.BlockSpec((tm, n), lambda i: (i, 0))
    return pl.pallas_call(
        _oproj_router_kernel,
        grid=(R // tm,),
        in_specs=[
            row(D_MODEL), row(HQ * HD),
            pl.BlockSpec((HQ * HD, D_MODEL), lambda i: (0, 0)),
            pl.BlockSpec((1, D_MODEL), lambda i: (0, 0)),
            pl.BlockSpec((D_MODEL, N_EXPERTS), lambda i: (0, 0)),
        ],
        out_specs=[row(D_MODEL), row(D_MODEL), row(TOP_K), row(TOP_K),
                   pl.BlockSpec((1, N_EXPERTS), lambda i: (0, 0))],
        out_shape=[
            jax.ShapeDtypeStruct((R, D_MODEL), F32),
            jax.ShapeDtypeStruct((R, D_MODEL), F32),
            jax.ShapeDtypeStruct((R, TOP_K), jnp.int32),
            jax.ShapeDtypeStruct((R, TOP_K), F32),
            jax.ShapeDtypeStruct((1, N_EXPERTS), F32),
        ],
        compiler_params=_params(("arbitrary",)),
        name="swa_oproj_router",
    )(x, a, w_o, g, w_r)


def _slot_kernel(e_ref, base_ref, tri_ref, pos_ref, run_scr):
    i = pl.program_id(0)

    @pl.when(i == 0)
    def _():
        run_scr[...] = jnp.zeros_like(run_scr)

    e = e_ref[...]
    tm = e.shape[0]
    eidx = lax.broadcasted_iota(jnp.int32, (tm, N_EXPERTS), 1)
    hot0 = eidx == e[:, 0:1]
    hot1 = eidx == e[:, 1:2]
    hot = (hot0 | hot1).astype(BF16)
    before = jnp.dot(tri_ref[...], hot, preferred_element_type=F32)
    dest = before + run_scr[...] + base_ref[...]
    p0 = jnp.sum(jnp.where(hot0, dest, 0.0), axis=-1, keepdims=True)
    p1 = jnp.sum(jnp.where(hot1, dest, 0.0), axis=-1, keepdims=True)
    pos_ref[...] = jnp.concatenate([p0, p1], axis=1).astype(jnp.int32)
    run_scr[...] += jnp.sum(hot.astype(F32), axis=0, keepdims=True)


def _slots(eidx, base):
    R = eidx.shape[0]
    tm = min(ROW_TILE, R)
    idx = np.arange(tm)
    tri = jnp.asarray((idx[:, None] > idx[None, :]).astype(np.float32), dtype=BF16)
    return pl.pallas_call(
        _slot_kernel,
        grid=(R // tm,),
        in_specs=[
            pl.BlockSpec((tm, TOP_K), lambda i: (i, 0)),
            pl.BlockSpec((1, N_EXPERTS), lambda i: (0, 0)),
            pl.BlockSpec((tm, tm), lambda i: (0, 0)),
        ],
        out_specs=pl.BlockSpec((tm, TOP_K), lambda i: (i, 0)),
        out_shape=jax.ShapeDtypeStruct((R, TOP_K), jnp.int32),
        scratch_shapes=[pltpu.VMEM((1, N_EXPERTS), F32)],
        compiler_params=_params(("arbitrary",)),
        name="moe_slots",
    )(eidx, base, tri)


DMA_UNROLL = 8


def _row_copy(src, i, dst, j, sem):
    return pltpu.make_async_copy(src.at[pl.ds(i, 1)], dst.at[pl.ds(j, 1)], sem)


def _dispatch_kernel(pos_ref, cnt_ref, base_ref, nused_ref, h_ref, xs_ref, zero_scr, sem, zsem, *, tm, te, n_tiles):
    i = pl.program_id(0)

    @pl.when(i == 0)
    def _():
        zero_scr[...] = jnp.zeros_like(zero_scr)

        def tile_copy(t):
            dst = xs_ref.at[pl.ds(pl.multiple_of(t * te, te), te)]
            return pltpu.make_async_copy(zero_scr, dst, zsem)

        def issue_tile(t, carry):
            tile_copy(t).start()
            return carry

        def drain_tile(t, carry):
            tile_copy(t).wait()
            return carry

        lax.fori_loop(nused_ref[0], n_tiles, issue_tile, 0)
        lax.fori_loop(nused_ref[0], n_tiles, drain_tile, 0)
        for e in range(N_EXPERTS):
            n = cnt_ref[e]
            start = base_ref[e] + n
            npad = (te - n % te) % te

            def issue(r, carry):
                _row_copy(zero_scr, 0, xs_ref, start + r, zsem).start()
                return carry

            def drain(r, carry):
                _row_copy(zero_scr, 0, xs_ref, start + r, zsem).wait()
                return carry

            lax.fori_loop(0, npad, issue, 0)
            lax.fori_loop(0, npad, drain, 0)

    def issue(r, carry):
        _row_copy(h_ref, r, xs_ref, pos_ref[2 * r], sem).start()
        _row_copy(h_ref, r, xs_ref, pos_ref[2 * r + 1], sem).start()
        return carry

    lax.fori_loop(0, tm, issue, 0, unroll=DMA_UNROLL)
    for _ in range(TOP_K):
        pltpu.make_async_copy(h_ref, xs_ref.at[pl.ds(0, tm)], sem).wait()


def _dispatch(h, pos_flat, counts, base, n_used, n_tiles, te):
    R = h.shape[0]
    tm = min(ROW_TILE, R)
    kern = functools.partial(_dispatch_kernel, tm=tm, te=te, n_tiles=n_tiles)
    smem = pl.BlockSpec(memory_space=pltpu.SMEM)
    return pl.pallas_call(
        kern,
        grid=(R // tm,),
        in_specs=[
            pl.BlockSpec((TOP_K * tm,), lambda i: (i,), memory_space=pltpu.SMEM),
            smem, smem, smem,
            pl.BlockSpec((tm, D_MODEL), lambda i: (i, 0)),
        ],
        out_specs=pl.BlockSpec(memory_space=pl.ANY),
        out_shape=jax.ShapeDtypeStruct((n_tiles * te, D_MODEL), F32),
        scratch_shapes=[pltpu.VMEM((te, D_MODEL), F32), pltpu.SemaphoreType.DMA(()), pltpu.SemaphoreType.DMA(())],
        compiler_params=pltpu.CompilerParams(dimension_semantics=("arbitrary",), vmem_limit_bytes=VMEM_LIMIT,
                                             has_side_effects=True),
        name="moe_dispatch",
    )(pos_flat, counts, base, n_used, h)


def _expert_ffn_kernel(te_ref, nused_ref, x_ref, w1_ref, w3_ref, w2_ref, y_ref):
    i = pl.program_id(0)

    @pl.when(i < nused_ref[0])
    def _():
        y_ref[...] = _swiglu(x_ref[...].astype(BF16), w1_ref, w3_ref, w2_ref, lead=(0,))

    @pl.when(i >= nused_ref[0])
    def _():
        y_ref[...] = jnp.zeros_like(y_ref)


def _expert_ffn(xs, tile_expert, n_used, w1, w3, w2, te):
    n_rows = xs.shape[0]
    nt = n_rows // te
    wspec = lambda shape: pl.BlockSpec((1,) + shape, lambda i, te_r, nu_r: (te_r[i], 0, 0), pipeline_mode=RESIDENT)
    grid_spec = pltpu.PrefetchScalarGridSpec(
        num_scalar_prefetch=2,
        grid=(nt,),
        in_specs=[
            pl.BlockSpec((te, D_MODEL), lambda i, te_r, nu_r: (i, 0)),
            wspec((D_MODEL, D_FF)), wspec((D_MODEL, D_FF)), wspec((D_FF, D_MODEL)),
        ],
        out_specs=pl.BlockSpec((te, D_MODEL), lambda i, te_r, nu_r: (i, 0)),
    )
    return pl.pallas_call(
        _expert_ffn_kernel,
        grid_spec=grid_spec,
        out_shape=jax.ShapeDtypeStruct((n_rows, D_MODEL), F32),
        compiler_params=_params(("arbitrary",), VMEM_LIMIT_BIG),
        name="moe_expert_swiglu",
    )(tile_expert, n_used, xs, w1, w3, w2)


def _combine_kernel(pos_ref, x_ref, gate_ref, g_ref, y_ref, o_ref, buf0, buf1, sem, *, tm):
    def issue(r, carry):
        _row_copy(y_ref, pos_ref[2 * r], buf0, r, sem).start()
        _row_copy(y_ref, pos_ref[2 * r + 1], buf1, r, sem).start()
        return carry

    lax.fori_loop(0, tm, issue, 0, unroll=DMA_UNROLL)
    for buf in (buf0, buf1):
        pltpu.make_async_copy(y_ref.at[pl.ds(0, tm)], buf, sem).wait()
    gate = gate_ref[...]
    x = x_ref[...] + (buf0[...] * gate[:, 0:1] + buf1[...] * gate[:, 1:2])
    o_ref[...] = x * _rms_scale(x) * g_ref[...]


def _combine(x3, gates, pos_flat, y, g):
    R = x3.shape[0]
    tm = min(ROW_TILE, R)
    kern = functools.partial(_combine_kernel, tm=tm)
    return pl.pallas_call(
        kern,
        grid=(R // tm,),
        in_specs=[
            pl.BlockSpec((TOP_K * tm,), lambda i: (i,), memory_space=pltpu.SMEM),
            pl.BlockSpec((tm, D_MODEL), lambda i: (i, 0)),
            pl.BlockSpec((tm, TOP_K), lambda i: (i, 0)),
            pl.BlockSpec((1, D_MODEL), lambda i: (0, 0)),
            pl.BlockSpec(memory_space=pl.ANY),
        ],
        out_specs=pl.BlockSpec((tm, D_MODEL), lambda i: (i, 0)),
        out_shape=jax.ShapeDtypeStruct((R, D_MODEL), F32),
        scratch_shapes=[pltpu.VMEM((tm, D_MODEL), F32), pltpu.VMEM((tm, D_MODEL), F32),
                        pltpu.SemaphoreType.DMA(())],
        compiler_params=_params(("arbitrary",)),
        name="moe_combine_norm",
    )(pos_flat, x3, gates, g, y)


def _prep_weights(p):
    w_in = p["gdn_w_in"][0]
    w_kv = p["w_kv"]
    kcols = w_kv[:, :HKV * HD].reshape(D_MODEL, HKV, HD)
    vcols = w_kv[:, HKV * HD:].reshape(D_MODEL, HKV, HD)
    dup = lambda a: jnp.concatenate([a, a], axis=2).reshape(D_MODEL, HKV * LANES)
    w_kvd = jnp.concatenate([dup(kcols), dup(vcols)], axis=1)
    return dict(
        w_qk=w_in[:, :2 * QK_DIM].astype(BF16),
        w_v=w_in[:, 2 * QK_DIM:CONV_CH].astype(BF16),
        w_z=w_in[:, CONV_CH:CONV_CH + V_DIM].astype(BF16),
        w_ba=w_in[:, CONV_CH + V_DIM:].astype(BF16),
        w_out=p["gdn_w_out"][0].astype(BF16),
        ffn_w1=p["ffn_w1"][0].astype(BF16),
        ffn_w3=p["ffn_w3"][0].astype(BF16),
        ffn_w2=p["ffn_w2"][0].astype(BF16),
        w_kvd=w_kvd.astype(BF16),
        w_kvs=w_kv.astype(BF16),
        w_q=p["swa_w_q"][0].astype(BF16),
        w_o=p["swa_w_o"][0].astype(BF16),
        moe_w1=p["moe_w1"][0].astype(BF16),
        moe_w3=p["moe_w3"][0].astype(BF16),
        moe_w2=p["moe_w2"][0].astype(BF16),
    )


def _dup_heads(a):
    B, T = a.shape[:2]
    return jnp.concatenate([a, a], axis=3).reshape(B * T, HKV * LANES)


def _trunk(x, pos, conv0, s0, win_k, win_v, p, w):
    B, T, _ = x.shape
    R = B * T
    xf = x.reshape(R, D_MODEL)
    row = lambda a: a.reshape(1, -1)

    g_attn = row(p["norm_attn"][0])
    w_conv = p["gdn_w_conv"][0]
    nqk = 2 * QK_DIM
    qk, conv_qk = _qkv_conv(xf, g_attn, w["w_qk"], w_conv[:, :nqk], conv0[:, :, :nqk], B, T, True)
    v, conv_v = _qkv_conv(xf, g_attn, w["w_v"], w_conv[:, nqk:], conv0[:, :, nqk:], B, T, False)
    conv_new = jnp.concatenate([conv_qk, conv_v], axis=-1).reshape(B, -1, CONV_W - 1, CONV_CH)[:, -1]
    z, beta, gc, gct = _zbg(xf, g_attn, w["w_z"], w["w_ba"], p["gdn_a_log"][0], p["gdn_dt_bias"][0], B, T)
    u, wv, att_l = _delta_prep(qk, v, beta, gc, gct, B, T)
    og, s_new = _delta_scan(qk, u, wv, att_l, z, gc, s0, row(p["gdn_o_norm"][0]), B, T)
    x2 = _mixer_out_ffn(xf, og, w["w_out"], row(p["norm_ffn"][0]), w["ffn_w1"], w["ffn_w3"], w["ffn_w2"])

    keep = min(WINDOW, T) if win_k is None else T
    q, kvd, kvs = _qkv_rope(x2, row(p["norm_attn"][1]), row(p["kv_norm"]), w["w_q"], w["w_kvd"], w["w_kvs"],
                            _rope_tables(pos), B, T, keep)
    k_new = kvs[:, :HKV * HD].reshape(B, keep, HKV, HD)
    v_new = kvs[:, HKV * HD:].reshape(B, keep, HKV, HD)
    if win_k is None:
        keys, Tk = kvd, T
    else:
        hist = jnp.concatenate([_dup_heads(win_k), _dup_heads(win_v)], axis=1).astype(BF16)
        Tw = win_k.shape[1]
        keys = jnp.concatenate([hist.reshape(B, Tw, KV_DUP), kvd.reshape(B, T, KV_DUP)], axis=1)
        Tk = Tw + T
        keys = keys.reshape(B * Tk, KV_DUP)
    att = _attention(q, keys, row(p["swa_sinks"][0]), B, T, Tk)
    x3, h, eidx, gates, counts = _oproj_router(x2, att, w["w_o"], row(p["norm_ffn"][1]), p["moe_router"][0])

    te = min(EXPERT_TILE, max(TOP_K * R // N_EXPERTS, LANES))
    n_tiles = TOP_K * R // te + N_EXPERTS
    cnt = counts[0].astype(jnp.int32)
    tiles = (cnt + te - 1) // te
    tile_end = jnp.cumsum(tiles)
    base = (tile_end - tiles) * te
    n_used = tile_end[-1:]
    tile_expert = jnp.minimum(jnp.searchsorted(tile_end, jnp.arange(n_tiles, dtype=jnp.int32), side="right"),
                              N_EXPERTS - 1).astype(jnp.int32)
    pos_slot = _slots(eidx, base.astype(F32).reshape(1, N_EXPERTS)).reshape(-1)
    n_used = n_used.astype(jnp.int32)
    xs = _dispatch(h, pos_slot, cnt, base.astype(jnp.int32), n_used, n_tiles, te)
    ys = _expert_ffn(xs, tile_expert, n_used, w["moe_w1"], w["moe_w3"], w["moe_w2"], te)
    y = _combine(x3, gates, pos_slot, ys, row(p["final_norm"]))
    return (y.reshape(B, T, D_MODEL), conv_new[None], s_new[None], k_new, v_new)


def kernel(x_prompt, x_sample, cache_conv, state_delta, cache_k, cache_v, norm_attn, norm_ffn, gdn_w_in,
           gdn_w_conv, gdn_a_log, gdn_dt_bias, gdn_o_norm, gdn_w_out, kv_norm, w_kv, swa_w_q, swa_sinks, swa_w_o,
           ffn_w1, ffn_w3, ffn_w2, moe_router, moe_w1, moe_w3, moe_w2, final_norm):
    p = dict(norm_attn=norm_attn, norm_ffn=norm_ffn, gdn_w_in=gdn_w_in, gdn_w_conv=gdn_w_conv,
             gdn_a_log=gdn_a_log, gdn_dt_bias=gdn_dt_bias, gdn_o_norm=gdn_o_norm, gdn_w_out=gdn_w_out,
             kv_norm=kv_norm, w_kv=w_kv, swa_w_q=swa_w_q, swa_sinks=swa_sinks, swa_w_o=swa_w_o,
             ffn_w1=ffn_w1, ffn_w3=ffn_w3, ffn_w2=ffn_w2, moe_router=moe_router,
             moe_w1=moe_w1, moe_w3=moe_w3, moe_w2=moe_w2, final_norm=final_norm)
    w = _prep_weights(p)
    Bp, Tp, _ = x_prompt.shape
    Bs, Ts, _ = x_sample.shape
    conv0 = jnp.zeros((Bp, CONV_W - 1, CONV_CH), F32)
    s0 = jnp.zeros((Bp, HV, DK, DV), F32)
    y_p, conv_p, delta_p, k_p, v_p = _trunk(x_prompt, jnp.arange(Tp, dtype=jnp.int32), conv0, s0,
                                            None, None, p, w)
    pos_s = PAST_LEN + jnp.arange(Ts, dtype=jnp.int32)
    y_s, conv_s, delta_s, k_s, v_s = _trunk(x_sample, pos_s, cache_conv[0], state_delta[0],
                                            cache_k, cache_v, p, w)
    return (y_p, y_s, conv_p, delta_p, k_p, v_p, conv_s, delta_s, k_s, v_s)
```

```python
import functools

import jax
import jax.numpy as jnp
import numpy as np
from jax import lax
from jax.experimental import pallas as pl
from jax.experimental.pallas import tpu as pltpu

F32 = jnp.float32
BF16 = jnp.bfloat16

D_MODEL = 1024
CHUNK = 64
HK = 8
HV = 16
DK = 128
DV = 128
QK_DIM = HK * DK
V_DIM = HV * DV
CONV_CH = 2 * QK_DIM + V_DIM
CONV_W = 4
HQ = 16
HKV = 4
HD = 64
WINDOW = 128
ROT_DIM = HD // 4
ROPE_THETA = 500000.0
D_FF = 2816
N_EXPERTS = 8
TOP_K = 2
EPS = 1e-6
PAST_LEN = 2048

LANES = 128
ROW_TILE = 512
EXPERT_TILE = 512
VMEM_LIMIT = 48 * 1024 * 1024
VMEM_LIMIT_BIG = 56 * 1024 * 1024


def _params(sem, vmem=VMEM_LIMIT):
    return pltpu.CompilerParams(dimension_semantics=sem, vmem_limit_bytes=vmem)


def _rms_scale(x):
    return lax.rsqrt(jnp.mean(x * x, axis=-1, keepdims=True) + EPS)


def _bdot(a, b):
    return jnp.dot(a.astype(BF16), b.astype(BF16), preferred_element_type=F32)


def _bdot_nt(a, b):
    return lax.dot_general(a.astype(BF16), b.astype(BF16), (((1,), (1,)), ((), ())),
                           preferred_element_type=F32)


def _bdot_tn(a, b):
    return lax.dot_general(a.astype(BF16), b.astype(BF16), (((0,), (0,)), ((), ())),
                           preferred_element_type=F32)


def _split3(x):
    x1 = x.astype(BF16)
    r1 = x - x1.astype(F32)
    x2 = r1.astype(BF16)
    x3 = (r1 - x2.astype(F32)).astype(BF16)
    return x1, x2, x3


def _split2(x):
    x1 = x.astype(BF16)
    x2 = (x - x1.astype(F32)).astype(BF16)
    return x1, x2


def _dot_hi(a, b):
    a1, a2 = _split2(a)
    b1, b2 = _split2(b)
    d = functools.partial(jnp.dot, preferred_element_type=F32)
    return d(a1, b1) + (d(a1, b2) + d(a2, b1))


def _spread(x, emat, pieces):
    parts = _split3(x)[:pieces]
    out = jnp.dot(parts[0], emat, preferred_element_type=F32)
    for part in parts[1:]:
        out = out + jnp.dot(part, emat, preferred_element_type=F32)
    return out


def _head_spread_matrices():
    h = np.arange(HV)[:, None]
    full = (np.arange(HV * DV)[None, :] // DV == h)
    pair = (np.arange(HV * CHUNK)[None, :] // CHUNK == h)
    return jnp.asarray(full.astype(np.float32), dtype=BF16), jnp.asarray(pair.astype(np.float32), dtype=BF16)


def _sigmoid(x):
    return 0.5 * jnp.tanh(0.5 * x) + 0.5


def _silu(x):
    half = 0.5 * x
    return half * jnp.tanh(half) + half


CONV_SUB = 256


def _qkv_conv_kernel(x_ref, g_ref, w_ref, wc_ref, cs_ref, out_ref, cst_ref,
                     h_scr, pad_scr, carry_scr, *, tm, tn, l2):
    ti = pl.program_id(1)
    j = pl.program_id(2)

    @pl.when(j == 0)
    def _():
        x = x_ref[...]
        h_scr[...] = (x * _rms_scale(x) * g_ref[...]).astype(BF16)

    @pl.when(ti == 0)
    def _():
        carry_scr[j, 5:8, :] = cs_ref[0]

    h = h_scr[...]
    wc = wc_ref[...]
    scale = jnp.where(j == 0, DK ** -0.5, 1.0).astype(F32)
    cols = lambda k: slice(k * CONV_SUB, (k + 1) * CONV_SUB)

    def project(k):
        return jnp.dot(h, w_ref[:, cols(k)], preferred_element_type=F32)

    def finish(k, proj):
        c = cols(k)
        pad_scr[0:8, c] = carry_scr[j, :, c]
        pad_scr[8:, c] = proj
        acc = proj * wc[3:4, c]
        for s in range(1, CONV_W):
            acc = acc + pad_scr[8 - s:8 - s + tm, c] * wc[3 - s:4 - s, c]
        carry_scr[j, :, c] = pad_scr[tm:tm + 8, c]
        cst_ref[0, :, c] = pad_scr[tm + 5:tm + 8, c]
        y = _silu(acc)
        if not l2:
            out_ref[:, c] = y.astype(out_ref.dtype)
            return
        for hh in range(CONV_SUB // DK):
            seg = y[:, hh * DK:(hh + 1) * DK]
            inv = lax.rsqrt(jnp.sum(seg * seg, axis=-1, keepdims=True) + EPS) * scale
            lanes = slice(k * CONV_SUB + hh * DK, k * CONV_SUB + (hh + 1) * DK)
            out_ref[:, lanes] = (seg * inv).astype(out_ref.dtype)

    pending = project(0)
    for k in range(tn // CONV_SUB):
        upcoming = project(k + 1) if k + 1 < tn // CONV_SUB else None
        finish(k, pending)
        pending = upcoming


def _qkv_conv(x, g, w, w_conv, conv_state, B, T, l2):
    R = B * T
    n = w.shape[1]
    tm = min(2 * ROW_TILE, T)
    tn = QK_DIM
    nt = T // tm
    nj = n // tn
    kern = functools.partial(_qkv_conv_kernel, tm=tm, tn=tn, l2=l2)
    return pl.pallas_call(
        kern,
        grid=(B, nt, nj),
        in_specs=[
            pl.BlockSpec((tm, D_MODEL), lambda b, t, j: (b * nt + t, 0)),
            pl.BlockSpec((1, D_MODEL), lambda b, t, j: (0, 0)),
            pl.BlockSpec((D_MODEL, tn), lambda b, t, j: (0, j)),
            pl.BlockSpec((CONV_W, tn), lambda b, t, j: (0, j)),
            pl.BlockSpec((1, CONV_W - 1, tn), lambda b, t, j: (b, 0, j)),
        ],
        out_specs=[
            pl.BlockSpec((tm, tn), lambda b, t, j: (b * nt + t, j)),
            pl.BlockSpec((1, CONV_W - 1, tn), lambda b, t, j: (b * nt + t, 0, j)),
        ],
        out_shape=[
            jax.ShapeDtypeStruct((R, n), BF16),
            jax.ShapeDtypeStruct((B * nt, CONV_W - 1, n), F32),
        ],
        scratch_shapes=[
            pltpu.VMEM((tm, D_MODEL), BF16),
            pltpu.VMEM((tm + 8, tn), F32),
            pltpu.VMEM((nj, 8, tn), F32),
        ],
        compiler_params=_params(("arbitrary", "arbitrary", "arbitrary")),
        name="gdn_qk_conv" if l2 else "gdn_v_conv",
    )(x, g, w, w_conv, conv_state)


def _softplus(x):
    return jnp.maximum(x, 0.0) + jnp.log(1.0 + jnp.exp(-jnp.abs(x)))


def _zbg_kernel(x_ref, g_ref, w_ref, alog_ref, dtb_ref, tri_ref, z_ref, beta_ref, gc_ref, gct_ref, *, tm):
    x = x_ref[...]
    h = (x * _rms_scale(x) * g_ref[...]).astype(BF16)
    ba = jnp.dot(h, w_ref[:, V_DIM:], preferred_element_type=F32)
    z_ref[...] = jnp.dot(h, w_ref[:, :V_DIM], preferred_element_type=F32).astype(z_ref.dtype)
    beta_ref[...] = _sigmoid(ba[:, :HV])
    gcol = -jnp.exp(alog_ref[...]) * _softplus(ba[:, HV:2 * HV] + dtb_ref[...])
    tri = tri_ref[...]
    g1, g2, g3 = _split3(gcol)
    d = functools.partial(jnp.dot, preferred_element_type=F32)
    gc = d(tri, g1) + (d(tri, g2) + d(tri, g3))
    gc_ref[...] = gc
    r = lax.broadcasted_iota(jnp.int32, (HV, HV), 0)
    c = lax.broadcasted_iota(jnp.int32, (HV, HV), 1)
    sel = (c == jnp.where(r < HV // 2, 2 * r, 2 * r - (HV - 1))).astype(BF16)
    tn = lambda a: lax.dot_general(sel, a, (((1,), (1,)), ((), ())), preferred_element_type=F32)
    c1, c2, c3 = _split3(gc)
    gct = tn(c1) + (tn(c2) + tn(c3))
    for ch in range(tm // CHUNK):
        sl = slice(ch * CHUNK, (ch + 1) * CHUNK)
        gct_ref[0, ch] = jnp.concatenate([gct[:HV // 2, sl], gct[HV // 2:, sl]], axis=1)


def _zbg(x, g, w_z, w_ba, a_log, dt_bias, B, T):
    R = B * T
    tm = min(ROW_TILE, T)
    nt = T // tm
    nc = T // CHUNK
    idx = np.arange(tm)
    tri = jnp.asarray(((idx[:, None] // CHUNK == idx[None, :] // CHUNK) & (idx[:, None] >= idx[None, :]))
                      .astype(np.float32), dtype=BF16)
    w_zba = jnp.concatenate([w_z, w_ba, jnp.zeros((D_MODEL, LANES - 2 * HV), BF16)], axis=1)
    full = lambda shape: pl.BlockSpec(shape, lambda b, t: (0,) * len(shape))
    kern = functools.partial(_zbg_kernel, tm=tm)
    return pl.pallas_call(
        kern,
        grid=(B, nt),
        in_specs=[
            pl.BlockSpec((tm, D_MODEL), lambda b, t: (b * nt + t, 0)),
            full((1, D_MODEL)),
            full((D_MODEL, V_DIM + LANES)),
            full((1, HV)), full((1, HV)),
            full((tm, tm)),
        ],
        out_specs=[
            pl.BlockSpec((tm, V_DIM), lambda b, t: (b * nt + t, 0)),
            pl.BlockSpec((tm, HV), lambda b, t: (b * nt + t, 0)),
            pl.BlockSpec((tm, HV), lambda b, t: (b * nt + t, 0)),
            pl.BlockSpec((1, tm // CHUNK, HV // 2, 2 * CHUNK), lambda b, t: (b, t, 0, 0)),
        ],
        out_shape=[
            jax.ShapeDtypeStruct((R, V_DIM), BF16),
            jax.ShapeDtypeStruct((R, HV), F32),
            jax.ShapeDtypeStruct((R, HV), F32),
            jax.ShapeDtypeStruct((B, nc, HV // 2, 2 * CHUNK), F32),
        ],
        compiler_params=_params(("arbitrary", "arbitrary")),
        name="gdn_z_beta_decay",
    )(x, g, w_zba, a_log.reshape(1, HV), dt_bias.reshape(1, HV), tri)


PAIR = HV // HK


def _pair_rows(x, lo):
    zero = jnp.zeros((), x.dtype)
    return jnp.concatenate([jnp.where(lo, x, zero), jnp.where(lo, zero, x)], axis=0)


def _pair_products(lhs_list, rhs_list, lo):
    def split(a):
        hi = a.astype(BF16)
        return hi, (a - hi.astype(F32)).astype(BF16)

    ops = []
    for lhs, rhs in zip(lhs_list, rhs_list):
        lh, ll = split(lhs)
        rh, rl = split(rhs)
        rh2 = _pair_rows(rh, lo)
        ops.append((jnp.concatenate([lh, lh, ll], axis=1), jnp.concatenate([rh2, _pair_rows(rl, lo), rh2], axis=0)))
    return [jnp.dot(a, b, preferred_element_type=F32) for a, b in ops]


def _pair_products_bf16(lhs_list, rhs_list, lo):
    ops = [(lhs.astype(BF16), _pair_rows(rhs.astype(BF16), lo)) for lhs, rhs in zip(lhs_list, rhs_list)]
    return [jnp.dot(a, b, preferred_element_type=F32) for a, b in ops]


INV_BASE = 16


def _pair_unit_lower_inverse(low_list, lo, eye2, same_block):
    C = low_list[0].shape[0]
    base = same_block(INV_BASE)
    xs = [jnp.where(base, -low, 0.0) for low in low_list]
    accs = [eye2 + x for x in xs]
    pws = _pair_products(xs, xs, lo)
    n = 2
    while 2 * n < INV_BASE:
        prods = _pair_products([jnp.concatenate([a, p], axis=0) for a, p in zip(accs, pws)], pws, lo)
        accs = [a + pr[:C] for a, pr in zip(accs, prods)]
        pws = [pr[C:] for pr in prods]
        n *= 2
    prods = _pair_products(accs, pws, lo)
    tinvs = [a + pr for a, pr in zip(accs, prods)]
    size = 2 * INV_BASE
    while size <= C:
        below = same_block(size) & jnp.logical_not(same_block(size // 2))
        offs = [jnp.where(below, low, 0.0) for low in low_list]
        inner = _pair_products_bf16(offs, tinvs, lo)
        outer = _pair_products_bf16(tinvs, inner, lo)
        tinvs = [t - o for t, o in zip(tinvs, outer)]
        size *= 2
    return tinvs


def _delta_prep_kernel(q_ref, k_ref, v_ref, beta_ref, gc_ref, gct_ref, epair_ref, u_ref, w_ref, a_ref, *, cb):
    C = CHUNK
    ri = lax.broadcasted_iota(jnp.int32, (C, 2 * C), 0)
    li = lax.broadcasted_iota(jnp.int32, (C, 2 * C), 1)
    lm = li & (C - 1)
    lo = li < C
    incl = ri >= lm
    strict = ri > lm
    eye2 = (ri == lm).astype(F32)
    units = [(c, p) for c in range(cb) for p in range(HK)]
    rows = lambda c: slice(c * C, (c + 1) * C)

    grams = []
    for c, p in units:
        kb = k_ref[rows(c), p * DK:(p + 1) * DK]
        qb = q_ref[rows(c), p * DK:(p + 1) * DK]
        grams.append(lax.dot_general(jnp.concatenate([kb, qb], axis=0), jnp.concatenate([kb, kb], axis=0),
                                     (((1,), (1,)), ((), ())), preferred_element_type=F32))

    epair = epair_ref[...]
    g2s, b2s, betas, begs = [], [], [], []
    for c in range(cb):
        beta = beta_ref[rows(c), :]
        gc = gc_ref[rows(c), :]
        g2s.append(_spread(gc, epair, 3))
        b2s.append(_spread(beta, epair, 2))
        betas.append(beta)
        begs.append(beta * jnp.exp(gc))

    lows = []
    for (c, p), gram in zip(units, grams):
        pt = slice(p * 2 * C, (p + 1) * 2 * C)
        grow2 = gct_ref[c, p:p + 1, :]
        decay2 = jnp.exp(jnp.where(incl, g2s[c][:, pt] - grow2, -jnp.inf))
        a_ref[rows(c), pt] = (gram[C:] * decay2).astype(a_ref.dtype)
        lows.append(jnp.where(strict, b2s[c][:, pt] * gram[:C] * decay2, 0.0))

    same_block = lambda s: (ri // s) == (lm // s)
    tinvs = _pair_unit_lower_inverse(lows, lo, eye2, same_block)

    ops = []
    for (c, p), tinv in zip(units, tinvs):
        kf = k_ref[rows(c), p * DK:(p + 1) * DK].astype(F32)
        rhs = []
        for h in (PAIR * p, PAIR * p + 1):
            ht = slice(h * DV, (h + 1) * DV)
            vf = v_ref[rows(c), ht].astype(F32)
            rhs.append(jnp.concatenate([vf * betas[c][:, h:h + 1], kf * begs[c][:, h:h + 1]], axis=1))
        ops.append((_pair_rows(tinv.astype(BF16), lo), jnp.concatenate(rhs, axis=0).astype(BF16)))
    for (c, p), (lhs, rhs) in zip(units, ops):
        uw = jnp.dot(lhs, rhs, preferred_element_type=F32)
        for i, h in enumerate((PAIR * p, PAIR * p + 1)):
            u_ref[rows(c), h * DV:(h + 1) * DV] = uw[i * C:(i + 1) * C, :DV].astype(u_ref.dtype)
            w_ref[rows(c), h * DV:(h + 1) * DV] = uw[i * C:(i + 1) * C, DV:].astype(w_ref.dtype)


def _delta_prep(qk, v, beta, gc, gct, B, T):
    R = B * T
    cb = 2 if T % (2 * CHUNK) == 0 else 1
    rb = cb * CHUNK
    gct2 = gct.reshape(R // CHUNK, HK, 2 * CHUNK)
    kern = functools.partial(_delta_prep_kernel, cb=cb)
    return pl.pallas_call(
        kern,
        grid=(R // rb,),
        in_specs=[
            pl.BlockSpec((rb, QK_DIM), lambda i: (i, 0)),
            pl.BlockSpec((rb, QK_DIM), lambda i: (i, 1)),
            pl.BlockSpec((rb, V_DIM), lambda i: (i, 0)),
            pl.BlockSpec((rb, HV), lambda i: (i, 0)),
            pl.BlockSpec((rb, HV), lambda i: (i, 0)),
            pl.BlockSpec((cb, HK, 2 * CHUNK), lambda i: (i, 0, 0)),
            pl.BlockSpec((HV, HV * CHUNK), lambda i: (0, 0)),
        ],
        out_specs=[
            pl.BlockSpec((rb, V_DIM), lambda i: (i, 0)),
            pl.BlockSpec((rb, V_DIM), lambda i: (i, 0)),
            pl.BlockSpec((rb, HV * CHUNK), lambda i: (i, 0)),
        ],
        out_shape=[
            jax.ShapeDtypeStruct((R, V_DIM), BF16),
            jax.ShapeDtypeStruct((R, V_DIM), BF16),
            jax.ShapeDtypeStruct((R, HV * CHUNK), BF16),
        ],
        compiler_params=_params(("arbitrary",)),
        name="gdn_delta_prep",
    )(qk, qk, v, beta, gc, gct2, _head_spread_matrices()[1])


def _delta_scan_kernel(q_ref, k_ref, u_ref, w_ref, a_ref, z_ref, gc_ref, s0_ref, on_ref, efull_ref,
                       og_ref, s_ref, *, bb):
    c = pl.program_id(1)

    @pl.when(c == 0)
    def _():
        s_ref[...] = s0_ref[...]

    C = CHUNK
    lo = lax.broadcasted_iota(jnp.int32, (C, 2 * C), 1) < C
    onorm = on_ref[...]
    units = [(b, h) for b in range(bb) for h in range(HV)]
    pairs = [(b, p) for b in range(bb) for p in range(HK)]

    efull = efull_ref[...]
    egs, kgs, dgs = [], [], []
    for b in range(bb):
        gc = gc_ref[b]
        g_last = gc[C - 1:C, :]
        egs.append(_spread(jnp.exp(gc), efull, 2))
        kgs.append(_spread(jnp.exp(g_last - gc), efull, 2))
        dgs.append(_spread(jnp.exp(gc[C - 8:C, :]), efull, 3)[7:8, :])

    ops = []
    kds = []
    dec = []
    for b, h in units:
        p = h // PAIR
        ht = slice(h * DV, (h + 1) * DV)
        qg = (q_ref[b, :, p * DK:(p + 1) * DK].astype(F32) * egs[b][:, ht]).astype(BF16)
        kds.append((k_ref[b, :, p * DK:(p + 1) * DK].astype(F32) * kgs[b][:, ht]).astype(BF16))
        dec.append(dgs[b][:, ht])
        ops.append((jnp.concatenate([w_ref[b, :, ht], qg], axis=0), s_ref[b, h].astype(BF16)))
    wqs = [jnp.dot(a, s, preferred_element_type=F32) for a, s in ops]

    vns = [(u_ref[b, :, h * DV:(h + 1) * DV].astype(F32) - wq[:C]).astype(BF16) for (b, h), wq in zip(units, wqs)]

    oparts = []
    for i, (b, p) in enumerate(pairs):
        att = _pair_rows(a_ref[b, :, p * 2 * C:(p + 1) * 2 * C], lo)
        vn2 = jnp.concatenate([vns[PAIR * i], vns[PAIR * i + 1]], axis=0)
        oparts.append(jnp.dot(att, vn2, preferred_element_type=F32))
    sds = [lax.dot_general(kd, vn, (((0,), (0,)), ((), ())), preferred_element_type=F32)
           for kd, vn in zip(kds, vns)]

    for i, (b, h) in enumerate(units):
        s_ref[b, h] = s_ref[b, h] * dec[i] + sds[i]
        o = wqs[i][C:] + oparts[i // PAIR][(h % PAIR) * C:(h % PAIR + 1) * C]
        on = o * _rms_scale(o) * onorm
        zf = z_ref[b, :, h * DV:(h + 1) * DV].astype(F32)
        og_ref[b, :, h * DV:(h + 1) * DV] = (on * _silu(zf)).astype(og_ref.dtype)


def _delta_scan(qk, u, w, att, z, gc, s0, o_norm, B, T):
    nc = T // CHUNK
    bb = 4 if B % 4 == 0 else (2 if B % 2 == 0 else 1)
    v3 = lambda a: a.reshape(B, T, a.shape[-1])
    blk = lambda n, j=0: pl.BlockSpec((bb, CHUNK, n), lambda b, c: (b, c, j))
    kern = functools.partial(_delta_scan_kernel, bb=bb)
    og, s_new = pl.pallas_call(
        kern,
        grid=(B // bb, nc),
        in_specs=[
            blk(QK_DIM, 0), blk(QK_DIM, 1), blk(V_DIM), blk(V_DIM), blk(HV * CHUNK), blk(V_DIM), blk(HV),
            pl.BlockSpec((bb, HV, DK, DV), lambda b, c: (b, 0, 0, 0)),
            pl.BlockSpec((1, DV), lambda b, c: (0, 0)),
            pl.BlockSpec((HV, HV * DV), lambda b, c: (0, 0)),
        ],
        out_specs=[
            blk(V_DIM),
            pl.BlockSpec((bb, HV, DK, DV), lambda b, c: (b, 0, 0, 0)),
        ],
        out_shape=[
            jax.ShapeDtypeStruct((B, T, V_DIM), BF16),
            jax.ShapeDtypeStruct((B, HV, DK, DV), F32),
        ],
        compiler_params=_params(("arbitrary", "arbitrary")),
        name="gdn_delta_scan",
    )(v3(qk), v3(qk), v3(u), v3(w), v3(att), v3(z), v3(gc), s0, o_norm, _head_spread_matrices()[0])
    return og.reshape(B * T, V_DIM), s_new


FF_TILE = 1408
RESIDENT = pl.Buffered(1)


def _swiglu(h, w1_ref, w3_ref, w2_ref, lead=()):
    out = None
    for f in range(D_FF // FF_TILE):
        sl = slice(f * FF_TILE, (f + 1) * FF_TILE)
        a = jnp.dot(h, w1_ref[lead + (slice(None), sl)], preferred_element_type=F32)
        b = jnp.dot(h, w3_ref[lead + (slice(None), sl)], preferred_element_type=F32)
        mid = (_silu(a) * b).astype(BF16)
        part = jnp.dot(mid, w2_ref[lead + (sl, slice(None))], preferred_element_type=F32)
        out = part if out is None else out + part
    return out


def _mixer_out_ffn_kernel(x_ref, a_ref, wo_ref, g_ref, w1_ref, w3_ref, w2_ref, o_ref):
    x1 = x_ref[...] + jnp.dot(a_ref[...], wo_ref[...], preferred_element_type=F32)
    h = (x1 * _rms_scale(x1) * g_ref[...]).astype(BF16)
    o_ref[...] = x1 + _swiglu(h, w1_ref, w3_ref, w2_ref)


def _mixer_out_ffn(x, a, w_out, g, w1, w3, w2):
    R = x.shape[0]
    K = a.shape[1]
    tm = min(ROW_TILE, R)
    held = lambda shape: pl.BlockSpec(shape, lambda i: (0,) * len(shape), pipeline_mode=RESIDENT)
    return pl.pallas_call(
        _mixer_out_ffn_kernel,
        grid=(R // tm,),
        in_specs=[
            pl.BlockSpec((tm, D_MODEL), lambda i: (i, 0)),
            pl.BlockSpec((tm, K), lambda i: (i, 0)),
            held((K, D_MODEL)),
            held((1, D_MODEL)),
            held((D_MODEL, D_FF)), held((D_MODEL, D_FF)), held((D_FF, D_MODEL)),
        ],
        out_specs=pl.BlockSpec((tm, D_MODEL), lambda i: (i, 0)),
        out_shape=jax.ShapeDtypeStruct((R, D_MODEL), F32),
        compiler_params=_params(("arbitrary",), VMEM_LIMIT_BIG),
        name="gdn_out_dense_swiglu",
    )(x, a, w_out, g, w1, w3, w2)


KV_DUP = 2 * HKV * LANES
KV_STD = 2 * HKV * HD


def _rope(x, cos, sm, sp):
    up = pltpu.roll(x, LANES - ROT_DIM // 2, 1)
    dn = pltpu.roll(x, ROT_DIM // 2, 1)
    return x * cos + up * sm + dn * sp


def _qkv_rope_kernel(x_ref, gq_ref, gkv_ref, wq_ref, wkv_ref, wkvs_ref, cos_ref, sm_ref, sp_ref,
                     q_ref, kvd_ref, kvs_ref, *, nt, keep):
    x = x_ref[...]
    xr = x * _rms_scale(x)
    hq = (xr * gq_ref[...]).astype(BF16)
    hkv = (xr * gkv_ref[...]).astype(BF16)
    cos = cos_ref[...]
    sm = sm_ref[...]
    sp = sp_ref[...]
    q = jnp.dot(hq, wq_ref[...], preferred_element_type=F32)
    for t in range(HQ * HD // LANES):
        sl = slice(t * LANES, (t + 1) * LANES)
        q_ref[:, sl] = (_rope(q[:, sl], cos, sm, sp) * (HD ** -0.5)).astype(q_ref.dtype)
    kv = jnp.dot(hkv, wkv_ref[...], preferred_element_type=F32)
    half = KV_DUP // 2
    for t in range(HKV):
        sl = slice(t * LANES, (t + 1) * LANES)
        kvd_ref[:, sl] = _rope(kv[:, sl], cos, sm, sp).astype(kvd_ref.dtype)
    kvd_ref[:, half:] = kv[:, half:].astype(kvd_ref.dtype)

    @pl.when(pl.program_id(0) % nt == nt - 1)
    def _():
        tm = x.shape[0]
        rows = slice(tm - keep, tm)
        kvs = jnp.dot(hkv[rows], wkvs_ref[...], preferred_element_type=F32)
        for t in range(HKV * HD // LANES):
            sl = slice(t * LANES, (t + 1) * LANES)
            kvs_ref[:, sl] = _rope(kvs[:, sl], cos[rows], sm[rows], sp[rows])
        kvs_ref[:, HKV * HD:] = kvs[:, HKV * HD:]


def _rope_tables(pos):
    half = ROT_DIM // 2
    inv = jnp.power(ROPE_THETA, -jnp.arange(half, dtype=F32) * 2.0 / ROT_DIM)
    ang = pos.astype(F32)[:, None] * inv[None, :]
    cos = jnp.cos(ang)
    sin = jnp.sin(ang)
    T = pos.shape[0]
    one = jnp.ones((T, HD - ROT_DIM), F32)
    zero = jnp.zeros((T, HD - ROT_DIM), F32)
    zh = jnp.zeros((T, half), F32)
    c64 = jnp.concatenate([cos, cos, one], axis=1)
    sm64 = jnp.concatenate([-sin, zh, zero], axis=1)
    sp64 = jnp.concatenate([zh, sin, zero], axis=1)
    dup = lambda a: jnp.concatenate([a, a], axis=1)
    return dup(c64), dup(sm64), dup(sp64)


def _qkv_rope(x, gq, gkv, w_q, w_kvd, w_kvs, tables, B, T, keep):
    R = B * T
    tm = min(ROW_TILE, T)
    nt = T // tm
    assert keep <= tm
    tab = pl.BlockSpec((tm, LANES), lambda i: (i % nt, 0))
    kern = functools.partial(_qkv_rope_kernel, nt=nt, keep=keep)
    return pl.pallas_call(
        kern,
        grid=(R // tm,),
        in_specs=[
            pl.BlockSpec((tm, D_MODEL), lambda i: (i, 0)),
            pl.BlockSpec((1, D_MODEL), lambda i: (0, 0)),
            pl.BlockSpec((1, D_MODEL), lambda i: (0, 0)),
            pl.BlockSpec((D_MODEL, HQ * HD), lambda i: (0, 0)),
            pl.BlockSpec((D_MODEL, KV_DUP), lambda i: (0, 0)),
            pl.BlockSpec((D_MODEL, KV_STD), lambda i: (0, 0)),
            tab, tab, tab,
        ],
        out_specs=[
            pl.BlockSpec((tm, HQ * HD), lambda i: (i, 0)),
            pl.BlockSpec((tm, KV_DUP), lambda i: (i, 0)),
            pl.BlockSpec((keep, KV_STD), lambda i: (i // nt, 0)),
        ],
        out_shape=[
            jax.ShapeDtypeStruct((R, HQ * HD), BF16),
            jax.ShapeDtypeStruct((R, KV_DUP), BF16),
            jax.ShapeDtypeStruct((B * keep, KV_STD), F32),
        ],
        compiler_params=_params(("arbitrary",)),
        name="swa_qkv_rope",
    )(x, gq, gkv, w_q, w_kvd, w_kvs, *tables)


KEY_CHUNKS = WINDOW // CHUNK + 1
KEY_PAD = 256


def _attn_kernel(*refs, off, cq):
    q_ref = refs[0]
    krefs = refs[1:KEY_CHUNKS + cq]
    sink_ref, ones_ref, o_ref = refs[KEY_CHUNKS + cq:]
    c = pl.program_id(1)
    lane = lax.broadcasted_iota(jnp.int32, (KEY_CHUNKS * CHUNK, LANES), 1)
    lo = lane < HD
    col = lax.broadcasted_iota(jnp.int32, (1, KEY_PAD), 1)
    sink_col = col == KEY_CHUNKS * CHUNK
    zpad = jnp.zeros((KEY_PAD - KEY_CHUNKS * CHUNK, LANES), BF16)
    zero = jnp.zeros((), BF16)
    tiles = HQ // HKV // 2
    ones_blk = ones_ref[...]
    units = [(u, g) for u in range(cq) for g in range(HKV)]

    vblks = []
    scores = []
    for u, g in units:
        kr = krefs[u:u + KEY_CHUNKS]
        kg = jnp.concatenate([r[:, g * LANES:(g + 1) * LANES] for r in kr], axis=0)
        vg = jnp.concatenate([r[:, KV_DUP // 2 + g * LANES:KV_DUP // 2 + (g + 1) * LANES] for r in kr], axis=0)
        kblk = jnp.concatenate([jnp.where(lo, kg, zero), zpad, jnp.where(lo, zero, kg), zpad], axis=0)
        vblk = jnp.concatenate([jnp.where(lo, vg, zero), zpad, jnp.where(lo, zero, vg), zpad], axis=0)
        vblks.append(jnp.concatenate([vblk, ones_blk], axis=1))
        qg = jnp.concatenate([q_ref[u * CHUNK:(u + 1) * CHUNK, (g * tiles + p) * LANES:(g * tiles + p + 1) * LANES]
                              for p in range(tiles)], axis=0)
        scores.append(lax.dot_general(qg, kblk, (((1,), (1,)), ((), ())), preferred_element_type=F32))

    probs = []
    for (u, g), sc in zip(units, scores):
        first = c * cq + u + off - (KEY_CHUNKS - 1)
        valid = (col < KEY_CHUNKS * CHUNK) & (col // CHUNK + first >= 0)
        base_bias = jnp.where(valid, 0.0, -jnp.inf).astype(F32)
        rows = []
        for p in range(tiles):
            t = g * tiles + p
            ps = []
            for hh in range(2):
                bias = jnp.where(sink_col, sink_ref[:, 2 * t + hh:2 * t + hh + 1], base_bias)
                sh = sc[p * CHUNK:(p + 1) * CHUNK, hh * KEY_PAD:(hh + 1) * KEY_PAD] + bias
                m = jnp.max(sh, axis=-1, keepdims=True)
                ps.append(jnp.exp(sh - m).astype(BF16))
            rows.append(jnp.concatenate(ps, axis=1))
        probs.append(jnp.concatenate(rows, axis=0))

    outs = [jnp.dot(pr, vb, preferred_element_type=F32) for pr, vb in zip(probs, vblks)]
    for (u, g), out in zip(units, outs):
        for p in range(tiles):
            t = g * tiles + p
            rows = slice(p * CHUNK, (p + 1) * CHUNK)
            o_ref[u * CHUNK:(u + 1) * CHUNK, t * LANES:(t + 1) * LANES] = (
                out[rows, :LANES] / out[rows, LANES:]).astype(o_ref.dtype)


def _attention(q, kvd, sinks, B, Tq, Tk):
    ncq = Tq // CHUNK
    nck = Tk // CHUNK
    off = nck - ncq
    cq = 4 if ncq % 4 == 0 else (2 if ncq % 2 == 0 else 1)
    kern = functools.partial(_attn_kernel, off=off, cq=cq)
    blk_row = np.arange(2 * KEY_PAD)[:, None] // KEY_PAD
    blk_lane = np.arange(LANES)[None, :] // HD
    ones_blk = jnp.asarray((blk_row == blk_lane).astype(np.float32), dtype=BF16)

    def kspec(i):
        return pl.BlockSpec((CHUNK, KV_DUP),
                            lambda b, c: (b * nck + jnp.maximum(c * cq + off - (KEY_CHUNKS - 1) + i, 0), 0))

    qspec = pl.BlockSpec((cq * CHUNK, HQ * HD), lambda b, c: (b * (ncq // cq) + c, 0))
    return pl.pallas_call(
        kern,
        grid=(B, ncq // cq),
        in_specs=[qspec] + [kspec(i) for i in range(KEY_CHUNKS + cq - 1)] + [
            pl.BlockSpec((1, HQ), lambda b, c: (0, 0)),
            pl.BlockSpec((2 * KEY_PAD, LANES), lambda b, c: (0, 0)),
        ],
        out_specs=qspec,
        out_shape=jax.ShapeDtypeStruct((B * Tq, HQ * HD), BF16),
        compiler_params=_params(("arbitrary", "arbitrary")),
        name="swa_attention",
    )(q, *([kvd] * (KEY_CHUNKS + cq - 1)), sinks, ones_blk)


def _oproj_router_kernel(x_ref, a_ref, w_ref, g_ref, wr_ref, x3_ref, h_ref, e_ref, gate_ref, cnt_ref):
    i = pl.program_id(0)
    x3 = x_ref[...] + jnp.dot(a_ref[...], w_ref[...], preferred_element_type=F32)
    x3_ref[...] = x3
    h = x3 * _rms_scale(x3) * g_ref[...]
    h_ref[...] = h
    h1, h2 = _split2(h)
    r1, r2 = _split2(wr_ref[...])
    part = jnp.dot(h1, jnp.concatenate([r1, r2], axis=1), preferred_element_type=F32)
    logits = part[:, :N_EXPERTS] + (part[:, N_EXPERTS:] + jnp.dot(h2, r1, preferred_element_type=F32))
    eidx = lax.broadcasted_iota(jnp.int32, logits.shape, 1)
    m1 = jnp.max(logits, axis=-1, keepdims=True)
    i1 = jnp.min(jnp.where(logits == m1, eidx, N_EXPERTS), axis=-1, keepdims=True)
    rest = jnp.where(eidx == i1, -jnp.inf, logits)
    m2 = jnp.max(rest, axis=-1, keepdims=True)
    i2 = jnp.min(jnp.where(rest == m2, eidx, N_EXPERTS), axis=-1, keepdims=True)
    e2 = jnp.exp(m2 - m1)
    den = 1.0 + e2
    e_ref[...] = jnp.concatenate([i1, i2], axis=1)
    gate_ref[...] = jnp.concatenate([1.0 / den, e2 / den], axis=1)
    hot = ((eidx == i1) | (eidx == i2)).astype(F32)

    @pl.when(i == 0)
    def _():
        cnt_ref[...] = jnp.zeros_like(cnt_ref)

    cnt_ref[...] += jnp.sum(hot, axis=0, keepdims=True)


def _oproj_router(x, a, w_o, g, w_r):
    R = x.shape[0]
    tm = min(ROW_TILE, R)
    row = lambda n: pl.BlockSpec((tm, n), lambda i: (i, 0))
    return pl.pallas_call(
        _oproj_router_kernel,
        grid=(R // tm,),
        in_specs=[
            row(D_MODEL), row(HQ * HD),
            pl.BlockSpec((HQ * HD, D_MODEL), lambda i: (0, 0)),
            pl.BlockSpec((1, D_MODEL), lambda i: (0, 0)),
            pl.BlockSpec((D_MODEL, N_EXPERTS), lambda i: (0, 0)),
        ],
        out_specs=[row(D_MODEL), row(D_MODEL), row(TOP_K), row(TOP_K),
                   pl.BlockSpec((1, N_EXPERTS), lambda i: (0, 0))],
        out_shape=[
            jax.ShapeDtypeStruct((R, D_MODEL), F32),
            jax.ShapeDtypeStruct((R, D_MODEL), F32),
            jax.ShapeDtypeStruct((R, TOP_K), jnp.int32),
            jax.ShapeDtypeStruct((R, TOP_K), F32),
            jax.ShapeDtypeStruct((1, N_EXPERTS), F32),
        ],
        compiler_params=_params(("arbitrary",)),
        name="swa_oproj_router",
    )(x, a, w_o, g, w_r)


def _slot_kernel(e_ref, base_ref, tri_ref, pos_ref, run_scr):
    i = pl.program_id(0)

    @pl.when(i == 0)
    def _():
        run_scr[...] = jnp.zeros_like(run_scr)

    e = e_ref[...]
    tm = e.shape[0]
    eidx = lax.broadcasted_iota(jnp.int32, (tm, N_EXPERTS), 1)
    hot0 = eidx == e[:, 0:1]
    hot1 = eidx == e[:, 1:2]
    hot = (hot0 | hot1).astype(BF16)
    before = jnp.dot(tri_ref[...], hot, preferred_element_type=F32)
    dest = before + run_scr[...] + base_ref[...]
    p0 = jnp.sum(jnp.where(hot0, dest, 0.0), axis=-1, keepdims=True)
    p1 = jnp.sum(jnp.where(hot1, dest, 0.0), axis=-1, keepdims=True)
    pos_ref[...] = jnp.concatenate([p0, p1], axis=1).astype(jnp.int32)
    run_scr[...] += jnp.sum(hot.astype(F32), axis=0, keepdims=True)


def _slots(eidx, base):
    R = eidx.shape[0]
    tm = min(ROW_TILE, R)
    idx = np.arange(tm)
    tri = jnp.asarray((idx[:, None] > idx[None, :]).astype(np.float32), dtype=BF16)
    return pl.pallas_call(
        _slot_kernel,
        grid=(R // tm,),
        in_specs=[
            pl.BlockSpec((tm, TOP_K), lambda i: (i, 0)),
            pl.BlockSpec((1, N_EXPERTS), lambda i: (0, 0)),
            pl.BlockSpec((tm, tm), lambda i: (0, 0)),
        ],
        out_specs=pl.BlockSpec((tm, TOP_K), lambda i: (i, 0)),
        out_shape=jax.ShapeDtypeStruct((R, TOP_K), jnp.int32),
        scratch_shapes=[pltpu.VMEM((1, N_EXPERTS), F32)],
        compiler_params=_params(("arbitrary",)),
        name="moe_slots",
    )(eidx, base, tri)


DMA_UNROLL = 8


def _row_copy(src, i, dst, j, sem):
    return pltpu.make_async_copy(src.at[pl.ds(i, 1)], dst.at[pl.ds(j, 1)], sem)


def _dispatch_kernel(pos_ref, cnt_ref, base_ref, nused_ref, h_ref, xs_ref, zero_scr, sem, zsem, *, tm, te, n_tiles):
    i = pl.program_id(0)

    @pl.when(i == 0)
    def _():
        zero_scr[...] = jnp.zeros_like(zero_scr)

        def tile_copy(t):
            dst = xs_ref.at[pl.ds(pl.multiple_of(t * te, te), te)]
            return pltpu.make_async_copy(zero_scr, dst, zsem)

        def issue_tile(t, carry):
            tile_copy(t).start()
            return carry

        def drain_tile(t, carry):
            tile_copy(t).wait()
            return carry

        lax.fori_loop(nused_ref[0], n_tiles, issue_tile, 0)
        lax.fori_loop(nused_ref[0], n_tiles, drain_tile, 0)
        for e in range(N_EXPERTS):
            n = cnt_ref[e]
            start = base_ref[e] + n
            npad = (te - n % te) % te

            def issue(r, carry):
                _row_copy(zero_scr, 0, xs_ref, start + r, zsem).start()
                return carry

            def drain(r, carry):
                _row_copy(zero_scr, 0, xs_ref, start + r, zsem).wait()
                return carry

            lax.fori_loop(0, npad, issue, 0)
            lax.fori_loop(0, npad, drain, 0)

    def issue(r, carry):
        _row_copy(h_ref, r, xs_ref, pos_ref[2 * r], sem).start()
        _row_copy(h_ref, r, xs_ref, pos_ref[2 * r + 1], sem).start()
        return carry

    lax.fori_loop(0, tm, issue, 0, unroll=DMA_UNROLL)
    for _ in range(TOP_K):
        pltpu.make_async_copy(h_ref, xs_ref.at[pl.ds(0, tm)], sem).wait()


def _dispatch(h, pos_flat, counts, base, n_used, n_tiles, te):
    R = h.shape[0]
    tm = min(ROW_TILE, R)
    kern = functools.partial(_dispatch_kernel, tm=tm, te=te, n_tiles=n_tiles)
    smem = pl.BlockSpec(memory_space=pltpu.SMEM)
    return pl.pallas_call(
        kern,
        grid=(R // tm,),
        in_specs=[
            pl.BlockSpec((TOP_K * tm,), lambda i: (i,), memory_space=pltpu.SMEM),
            smem, smem, smem,
            pl.BlockSpec((tm, D_MODEL), lambda i: (i, 0)),
        ],
        out_specs=pl.BlockSpec(memory_space=pl.ANY),
        out_shape=jax.ShapeDtypeStruct((n_tiles * te, D_MODEL), F32),
        scratch_shapes=[pltpu.VMEM((te, D_MODEL), F32), pltpu.SemaphoreType.DMA(()), pltpu.SemaphoreType.DMA(())],
        compiler_params=pltpu.CompilerParams(dimension_semantics=("arbitrary",), vmem_limit_bytes=VMEM_LIMIT,
                                             has_side_effects=True),
        name="moe_dispatch",
    )(pos_flat, counts, base, n_used, h)


def _expert_ffn_kernel(te_ref, nused_ref, x_ref, w1_ref, w3_ref, w2_ref, y_ref):
    i = pl.program_id(0)

    @pl.when(i < nused_ref[0])
    def _():
        y_ref[...] = _swiglu(x_ref[...].astype(BF16), w1_ref, w3_ref, w2_ref, lead=(0,))

    @pl.when(i >= nused_ref[0])
    def _():
        y_ref[...] = jnp.zeros_like(y_ref)


def _expert_ffn(xs, tile_expert, n_used, w1, w3, w2, te):
    n_rows = xs.shape[0]
    nt = n_rows // te
    wspec = lambda shape: pl.BlockSpec((1,) + shape, lambda i, te_r, nu_r: (te_r[i], 0, 0), pipeline_mode=RESIDENT)
    grid_spec = pltpu.PrefetchScalarGridSpec(
        num_scalar_prefetch=2,
        grid=(nt,),
        in_specs=[
            pl.BlockSpec((te, D_MODEL), lambda i, te_r, nu_r: (i, 0)),
            wspec((D_MODEL, D_FF)), wspec((D_MODEL, D_FF)), wspec((D_FF, D_MODEL)),
        ],
        out_specs=pl.BlockSpec((te, D_MODEL), lambda i, te_r, nu_r: (i, 0)),
    )
    return pl.pallas_call(
        _expert_ffn_kernel,
        grid_spec=grid_spec,
        out_shape=jax.ShapeDtypeStruct((n_rows, D_MODEL), F32),
        compiler_params=_params(("arbitrary",), VMEM_LIMIT_BIG),
        name="moe_expert_swiglu",
    )(tile_expert, n_used, xs, w1, w3, w2)


def _combine_kernel(pos_ref, x_ref, gate_ref, g_ref, y_ref, o_ref, buf0, buf1, sem, *, tm):
    def issue(r, carry):
        _row_copy(y_ref, pos_ref[2 * r], buf0, r, sem).start()
        _row_copy(y_ref, pos_ref[2 * r + 1], buf1, r, sem).start()
        return carry

    lax.fori_loop(0, tm, issue, 0, unroll=DMA_UNROLL)
    for buf in (buf0, buf1):
        pltpu.make_async_copy(y_ref.at[pl.ds(0, tm)], buf, sem).wait()
    gate = gate_ref[...]
    x = x_ref[...] + (buf0[...] * gate[:, 0:1] + buf1[...] * gate[:, 1:2])
    o_ref[...] = x * _rms_scale(x) * g_ref[...]


def _combine(x3, gates, pos_flat, y, g):
    R = x3.shape[0]
    tm = min(ROW_TILE, R)
    kern = functools.partial(_combine_kernel, tm=tm)
    return pl.pallas_call(
        kern,
        grid=(R // tm,),
        in_specs=[
            pl.BlockSpec((TOP_K * tm,), lambda i: (i,), memory_space=pltpu.SMEM),
            pl.BlockSpec((tm, D_MODEL), lambda i: (i, 0)),
            pl.BlockSpec((tm, TOP_K), lambda i: (i, 0)),
            pl.BlockSpec((1, D_MODEL), lambda i: (0, 0)),
            pl.BlockSpec(memory_space=pl.ANY),
        ],
        out_specs=pl.BlockSpec((tm, D_MODEL), lambda i: (i, 0)),
        out_shape=jax.ShapeDtypeStruct((R, D_MODEL), F32),
        scratch_shapes=[pltpu.VMEM((tm, D_MODEL), F32), pltpu.VMEM((tm, D_MODEL), F32),
                        pltpu.SemaphoreType.DMA(())],
        compiler_params=_params(("arbitrary",)),
        name="moe_combine_norm",
    )(pos_flat, x3, gates, g, y)


def _prep_weights(p):
    w_in = p["gdn_w_in"][0]
    w_kv = p["w_kv"]
    kcols = w_kv[:, :HKV * HD].reshape(D_MODEL, HKV, HD)
    vcols = w_kv[:, HKV * HD:].reshape(D_MODEL, HKV, HD)
    dup = lambda a: jnp.concatenate([a, a], axis=2).reshape(D_MODEL, HKV * LANES)
    w_kvd = jnp.concatenate([dup(kcols), dup(vcols)], axis=1)
    return dict(
        w_qk=w_in[:, :2 * QK_DIM].astype(BF16),
        w_v=w_in[:, 2 * QK_DIM:CONV_CH].astype(BF16),
        w_z=w_in[:, CONV_CH:CONV_CH + V_DIM].astype(BF16),
        w_ba=w_in[:, CONV_CH + V_DIM:].astype(BF16),
        w_out=p["gdn_w_out"][0].astype(BF16),
        ffn_w1=p["ffn_w1"][0].astype(BF16),
        ffn_w3=p["ffn_w3"][0].astype(BF16),
        ffn_w2=p["ffn_w2"][0].astype(BF16),
        w_kvd=w_kvd.astype(BF16),
        w_kvs=w_kv.astype(BF16),
        w_q=p["swa_w_q"][0].astype(BF16),
        w_o=p["swa_w_o"][0].astype(BF16),
        moe_w1=p["moe_w1"][0].astype(BF16),
        moe_w3=p["moe_w3"][0].astype(BF16),
        moe_w2=p["moe_w2"][0].astype(BF16),
    )


def _dup_heads(a):
    B, T = a.shape[:2]
    return jnp.concatenate([a, a], axis=3).reshape(B * T, HKV * LANES)


def _trunk(x, pos, conv0, s0, win_k, win_v, p, w):
    B, T, _ = x.shape
    R = B * T
    xf = x.reshape(R, D_MODEL)
    row = lambda a: a.reshape(1, -1)

    g_attn = row(p["norm_attn"][0])
    w_conv = p["gdn_w_conv"][0]
    nqk = 2 * QK_DIM
    qk, conv_qk = _qkv_conv(xf, g_attn, w["w_qk"], w_conv[:, :nqk], conv0[:, :, :nqk], B, T, True)
    v, conv_v = _qkv_conv(xf, g_attn, w["w_v"], w_conv[:, nqk:], conv0[:, :, nqk:], B, T, False)
    conv_new = jnp.concatenate([conv_qk, conv_v], axis=-1).reshape(B, -1, CONV_W - 1, CONV_CH)[:, -1]
    z, beta, gc, gct = _zbg(xf, g_attn, w["w_z"], w["w_ba"], p["gdn_a_log"][0], p["gdn_dt_bias"][0], B, T)
    u, wv, att_l = _delta_prep(qk, v, beta, gc, gct, B, T)
    og, s_new = _delta_scan(qk, u, wv, att_l, z, gc, s0, row(p["gdn_o_norm"][0]), B, T)
    x2 = _mixer_out_ffn(xf, og, w["w_out"], row(p["norm_ffn"][0]), w["ffn_w1"], w["ffn_w3"], w["ffn_w2"])

    keep = min(WINDOW, T) if win_k is None else T
    q, kvd, kvs = _qkv_rope(x2, row(p["norm_attn"][1]), row(p["kv_norm"]), w["w_q"], w["w_kvd"], w["w_kvs"],
                            _rope_tables(pos), B, T, keep)
    k_new = kvs[:, :HKV * HD].reshape(B, keep, HKV, HD)
    v_new = kvs[:, HKV * HD:].reshape(B, keep, HKV, HD)
    if win_k is None:
        keys, Tk = kvd, T
    else:
        hist = jnp.concatenate([_dup_heads(win_k), _dup_heads(win_v)], axis=1).astype(BF16)
        Tw = win_k.shape[1]
        keys = jnp.concatenate([hist.reshape(B, Tw, KV_DUP), kvd.reshape(B, T, KV_DUP)], axis=1)
        Tk = Tw + T
        keys = keys.reshape(B * Tk, KV_DUP)
    att = _attention(q, keys, row(p["swa_sinks"][0]), B, T, Tk)
    x3, h, eidx, gates, counts = _oproj_router(x2, att, w["w_o"], row(p["norm_ffn"][1]), p["moe_router"][0])

    te = min(EXPERT_TILE, max(TOP_K * R // N_EXPERTS, LANES))
    n_tiles = TOP_K * R // te + N_EXPERTS
    cnt = counts[0].astype(jnp.int32)
    tiles = (cnt + te - 1) // te
    tile_end = jnp.cumsum(tiles)
    base = (tile_end - tiles) * te
    n_used = tile_end[-1:]
    tile_expert = jnp.minimum(jnp.searchsorted(tile_end, jnp.arange(n_tiles, dtype=jnp.int32), side="right"),
                              N_EXPERTS - 1).astype(jnp.int32)
    pos_slot = _slots(eidx, base.astype(F32).reshape(1, N_EXPERTS)).reshape(-1)
    n_used = n_used.astype(jnp.int32)
    xs = _dispatch(h, pos_slot, cnt, base.astype(jnp.int32), n_used, n_tiles, te)
    ys = _expert_ffn(xs, tile_expert, n_used, w["moe_w1"], w["moe_w3"], w["moe_w2"], te)
    y = _combine(x3, gates, pos_slot, ys, row(p["final_norm"]))
    return (y.reshape(B, T, D_MODEL), conv_new[None], s_new[None], k_new, v_new)


def kernel(x_prompt, x_sample, cache_conv, state_delta, cache_k, cache_v, norm_attn, norm_ffn, gdn_w_in,
           gdn_w_conv, gdn_a_log, gdn_dt_bias, gdn_o_norm, gdn_w_out, kv_norm, w_kv, swa_w_q, swa_sinks, swa_w_o,
           ffn_w1, ffn_w3, ffn_w2, moe_router, moe_w1, moe_w3, moe_w2, final_norm):
    p = dict(norm_attn=norm_attn, norm_ffn=norm_ffn, gdn_w_in=gdn_w_in, gdn_w_conv=gdn_w_conv,
             gdn_a_log=gdn_a_log, gdn_dt_bias=gdn_dt_bias, gdn_o_norm=gdn_o_norm, gdn_w_out=gdn_w_out,
             kv_norm=kv_norm, w_kv=w_kv, swa_w_q=swa_w_q, swa_sinks=swa_sinks, swa_w_o=swa_w_o,
             ffn_w1=ffn_w1, ffn_w3=ffn_w3, ffn_w2=ffn_w2, moe_router=moe_router,
             moe_w1=moe_w1, moe_w3=moe_w3, moe_w2=moe_w2, final_norm=final_norm)
    w = _prep_weights(p)
    Bp, Tp, _ = x_prompt.shape
    Bs, Ts, _ = x_sample.shape
    conv0 = jnp.zeros((Bp, CONV_W - 1, CONV_CH), F32)
    s0 = jnp.zeros((Bp, HV, DK, DV), F32)
    y_p, conv_p, delta_p, k_p, v_p = _trunk(x_prompt, jnp.arange(Tp, dtype=jnp.int32), conv0, s0,
                                            None, None, p, w)
    pos_s = PAST_LEN + jnp.arange(Ts, dtype=jnp.int32)
    y_s, conv_s, delta_s, k_s, v_s = _trunk(x_sample, pos_s, cache_conv[0], state_delta[0],
                                            cache_k, cache_v, p, w)
    return (y_p, y_s, conv_p, delta_p, k_p, v_p, conv_s, delta_s, k_s, v_s)
```

```python
import functools

import jax
import jax.numpy as jnp
import numpy as np
from jax import lax
from jax.experimental import pallas as pl
from jax.experimental.pallas import tpu as pltpu

F32 = jnp.float32
BF16 = jnp.bfloat16

D_MODEL = 1024
CHUNK = 64
HK = 8
HV = 16
DK = 128
DV = 128
QK_DIM = HK * DK
V_DIM = HV * DV
CONV_CH = 2 * QK_DIM + V_DIM
CONV_W = 4
HQ = 16
HKV = 4
HD = 64
WINDOW = 128
ROT_DIM = HD // 4
ROPE_THETA = 500000.0
D_FF = 2816
N_EXPERTS = 8
TOP_K = 2
EPS = 1e-6
PAST_LEN = 2048

LANES = 128
ROW_TILE = 512
EXPERT_TILE = 512
VMEM_LIMIT = 48 * 1024 * 1024
VMEM_LIMIT_BIG = 56 * 1024 * 1024


def _params(sem, vmem=VMEM_LIMIT):
    return pltpu.CompilerParams(dimension_semantics=sem, vmem_limit_bytes=vmem)


def _rms_scale(x):
    return lax.rsqrt(jnp.mean(x * x, axis=-1, keepdims=True) + EPS)


def _bdot(a, b):
    return jnp.dot(a.astype(BF16), b.astype(BF16), preferred_element_type=F32)


def _bdot_nt(a, b):
    return lax.dot_general(a.astype(BF16), b.astype(BF16), (((1,), (1,)), ((), ())),
                           preferred_element_type=F32)


def _bdot_tn(a, b):
    return lax.dot_general(a.astype(BF16), b.astype(BF16), (((0,), (0,)), ((), ())),
                           preferred_element_type=F32)


def _split3(x):
    x1 = x.astype(BF16)
    r1 = x - x1.astype(F32)
    x2 = r1.astype(BF16)
    x3 = (r1 - x2.astype(F32)).astype(BF16)
    return x1, x2, x3


def _split2(x):
    x1 = x.astype(BF16)
    x2 = (x - x1.astype(F32)).astype(BF16)
    return x1, x2


def _dot_hi(a, b):
    a1, a2 = _split2(a)
    b1, b2 = _split2(b)
    d = functools.partial(jnp.dot, preferred_element_type=F32)
    return d(a1, b1) + (d(a1, b2) + d(a2, b1))


def _spread(x, emat, pieces):
    parts = _split3(x)[:pieces]
    out = jnp.dot(parts[0], emat, preferred_element_type=F32)
    for part in parts[1:]:
        out = out + jnp.dot(part, emat, preferred_element_type=F32)
    return out


def _head_spread_matrices():
    h = np.arange(HV)[:, None]
    full = (np.arange(HV * DV)[None, :] // DV == h)
    pair = (np.arange(HV * CHUNK)[None, :] // CHUNK == h)
    return jnp.asarray(full.astype(np.float32), dtype=BF16), jnp.asarray(pair.astype(np.float32), dtype=BF16)


def _sigmoid(x):
    return 0.5 * jnp.tanh(0.5 * x) + 0.5


def _silu(x):
    half = 0.5 * x
    return half * jnp.tanh(half) + half


CONV_SUB = 256


def _qkv_conv_kernel(x_ref, g_ref, w_ref, wc_ref, cs_ref, out_ref, cst_ref,
                     h_scr, pad_scr, carry_scr, *, tm, tn, l2):
    ti = pl.program_id(1)
    j = pl.program_id(2)

    @pl.when(j == 0)
    def _():
        x = x_ref[...]
        h_scr[...] = (x * _rms_scale(x) * g_ref[...]).astype(BF16)

    @pl.when(ti == 0)
    def _():
        carry_scr[j, 5:8, :] = cs_ref[0]

    h = h_scr[...]
    wc = wc_ref[...]
    scale = jnp.where(j == 0, DK ** -0.5, 1.0).astype(F32)
    cols = lambda k: slice(k * CONV_SUB, (k + 1) * CONV_SUB)

    def project(k):
        return jnp.dot(h, w_ref[:, cols(k)], preferred_element_type=F32)

    def finish(k, proj):
        c = cols(k)
        pad_scr[0:8, c] = carry_scr[j, :, c]
        pad_scr[8:, c] = proj
        acc = proj * wc[3:4, c]
        for s in range(1, CONV_W):
            acc = acc + pad_scr[8 - s:8 - s + tm, c] * wc[3 - s:4 - s, c]
        carry_scr[j, :, c] = pad_scr[tm:tm + 8, c]
        cst_ref[0, :, c] = pad_scr[tm + 5:tm + 8, c]
        y = _silu(acc)
        if not l2:
            out_ref[:, c] = y.astype(out_ref.dtype)
            return
        for hh in range(CONV_SUB // DK):
            seg = y[:, hh * DK:(hh + 1) * DK]
            inv = lax.rsqrt(jnp.sum(seg * seg, axis=-1, keepdims=True) + EPS) * scale
            lanes = slice(k * CONV_SUB + hh * DK, k * CONV_SUB + (hh + 1) * DK)
            out_ref[:, lanes] = (seg * inv).astype(out_ref.dtype)

    pending = project(0)
    for k in range(tn // CONV_SUB):
        upcoming = project(k + 1) if k + 1 < tn // CONV_SUB else None
        finish(k, pending)
        pending = upcoming


def _qkv_conv(x, g, w, w_conv, conv_state, B, T, l2):
    R = B * T
    n = w.shape[1]
    tm = min(2 * ROW_TILE, T)
    tn = QK_DIM
    nt = T // tm
    nj = n // tn
    kern = functools.partial(_qkv_conv_kernel, tm=tm, tn=tn, l2=l2)
    return pl.pallas_call(
        kern,
        grid=(B, nt, nj),
        in_specs=[
            pl.BlockSpec((tm, D_MODEL), lambda b, t, j: (b * nt + t, 0)),
            pl.BlockSpec((1, D_MODEL), lambda b, t, j: (0, 0)),
            pl.BlockSpec((D_MODEL, tn), lambda b, t, j: (0, j)),
            pl.BlockSpec((CONV_W, tn), lambda b, t, j: (0, j)),
            pl.BlockSpec((1, CONV_W - 1, tn), lambda b, t, j: (b, 0, j)),
        ],
        out_specs=[
            pl.BlockSpec((tm, tn), lambda b, t, j: (b * nt + t, j)),
            pl.BlockSpec((1, CONV_W - 1, tn), lambda b, t, j: (b * nt + t, 0, j)),
        ],
        out_shape=[
            jax.ShapeDtypeStruct((R, n), BF16),
            jax.ShapeDtypeStruct((B * nt, CONV_W - 1, n), F32),
        ],
        scratch_shapes=[
            pltpu.VMEM((tm, D_MODEL), BF16),
            pltpu.VMEM((tm + 8, tn), F32),
            pltpu.VMEM((nj, 8, tn), F32),
        ],
        compiler_params=_params(("arbitrary", "arbitrary", "arbitrary")),
        name="gdn_qk_conv" if l2 else "gdn_v_conv",
    )(x, g, w, w_conv, conv_state)


def _softplus(x):
    return jnp.maximum(x, 0.0) + jnp.log(1.0 + jnp.exp(-jnp.abs(x)))


def _zbg_kernel(x_ref, g_ref, w_ref, alog_ref, dtb_ref, tri_ref, z_ref, beta_ref, gc_ref, gct_ref, *, tm):
    x = x_ref[...]
    h = (x * _rms_scale(x) * g_ref[...]).astype(BF16)
    ba = jnp.dot(h, w_ref[:, V_DIM:], preferred_element_type=F32)
    z_ref[...] = jnp.dot(h, w_ref[:, :V_DIM], preferred_element_type=F32).astype(z_ref.dtype)
    beta_ref[...] = _sigmoid(ba[:, :HV])
    gcol = -jnp.exp(alog_ref[...]) * _softplus(ba[:, HV:2 * HV] + dtb_ref[...])
    tri = tri_ref[...]
    g1, g2, g3 = _split3(gcol)
    d = functools.partial(jnp.dot, preferred_element_type=F32)
    gc = d(tri, g1) + (d(tri, g2) + d(tri, g3))
    gc_ref[...] = gc
    r = lax.broadcasted_iota(jnp.int32, (HV, HV), 0)
    c = lax.broadcasted_iota(jnp.int32, (HV, HV), 1)
    sel = (c == jnp.where(r < HV // 2, 2 * r, 2 * r - (HV - 1))).astype(BF16)
    tn = lambda a: lax.dot_general(sel, a, (((1,), (1,)), ((), ())), preferred_element_type=F32)
    c1, c2, c3 = _split3(gc)
    gct = tn(c1) + (tn(c2) + tn(c3))
    for ch in range(tm // CHUNK):
        sl = slice(ch * CHUNK, (ch + 1) * CHUNK)
        gct_ref[0, ch] = jnp.concatenate([gct[:HV // 2, sl], gct[HV // 2:, sl]], axis=1)


def _zbg(x, g, w_z, w_ba, a_log, dt_bias, B, T):
    R = B * T
    tm = min(ROW_TILE, T)
    nt = T // tm
    nc = T // CHUNK
    idx = np.arange(tm)
    tri = jnp.asarray(((idx[:, None] // CHUNK == idx[None, :] // CHUNK) & (idx[:, None] >= idx[None, :]))
                      .astype(np.float32), dtype=BF16)
    w_zba = jnp.concatenate([w_z, w_ba, jnp.zeros((D_MODEL, LANES - 2 * HV), BF16)], axis=1)
    full = lambda shape: pl.BlockSpec(shape, lambda b, t: (0,) * len(shape))
    kern = functools.partial(_zbg_kernel, tm=tm)
    return pl.pallas_call(
        kern,
        grid=(B, nt),
        in_specs=[
            pl.BlockSpec((tm, D_MODEL), lambda b, t: (b * nt + t, 0)),
            full((1, D_MODEL)),
            full((D_MODEL, V_DIM + LANES)),
            full((1, HV)), full((1, HV)),
            full((tm, tm)),
        ],
        out_specs=[
            pl.BlockSpec((tm, V_DIM), lambda b, t: (b * nt + t, 0)),
            pl.BlockSpec((tm, HV), lambda b, t: (b * nt + t, 0)),
            pl.BlockSpec((tm, HV), lambda b, t: (b * nt + t, 0)),
            pl.BlockSpec((1, tm // CHUNK, HV // 2, 2 * CHUNK), lambda b, t: (b, t, 0, 0)),
        ],
        out_shape=[
            jax.ShapeDtypeStruct((R, V_DIM), BF16),
            jax.ShapeDtypeStruct((R, HV), F32),
            jax.ShapeDtypeStruct((R, HV), F32),
            jax.ShapeDtypeStruct((B, nc, HV // 2, 2 * CHUNK), F32),
        ],
        compiler_params=_params(("arbitrary", "arbitrary")),
        name="gdn_z_beta_decay",
    )(x, g, w_zba, a_log.reshape(1, HV), dt_bias.reshape(1, HV), tri)


PAIR = HV // HK


def _pair_rows(x, lo):
    zero = jnp.zeros((), x.dtype)
    return jnp.concatenate([jnp.where(lo, x, zero), jnp.where(lo, zero, x)], axis=0)


def _pair_products(lhs_list, rhs_list, lo):
    def split(a):
        hi = a.astype(BF16)
        return hi, (a - hi.astype(F32)).astype(BF16)

    ops = []
    for lhs, rhs in zip(lhs_list, rhs_list):
        lh, ll = split(lhs)
        rh, rl = split(rhs)
        rh2 = _pair_rows(rh, lo)
        ops.append((jnp.concatenate([lh, lh, ll], axis=1), jnp.concatenate([rh2, _pair_rows(rl, lo), rh2], axis=0)))
    return [jnp.dot(a, b, preferred_element_type=F32) for a, b in ops]


def _pair_products_bf16(lhs_list, rhs_list, lo):
    ops = [(lhs.astype(BF16), _pair_rows(rhs.astype(BF16), lo)) for lhs, rhs in zip(lhs_list, rhs_list)]
    return [jnp.dot(a, b, preferred_element_type=F32) for a, b in ops]


INV_BASE = 16


def _pair_unit_lower_inverse(low_list, lo, eye2, same_block):
    C = low_list[0].shape[0]
    base = same_block(INV_BASE)
    xs = [jnp.where(base, -low, 0.0) for low in low_list]
    accs = [eye2 + x for x in xs]
    pws = _pair_products(xs, xs, lo)
    n = 2
    while 2 * n < INV_BASE:
        prods = _pair_products([jnp.concatenate([a, p], axis=0) for a, p in zip(accs, pws)], pws, lo)
        accs = [a + pr[:C] for a, pr in zip(accs, prods)]
        pws = [pr[C:] for pr in prods]
        n *= 2
    prods = _pair_products(accs, pws, lo)
    tinvs = [a + pr for a, pr in zip(accs, prods)]
    size = 2 * INV_BASE
    while size <= C:
        below = same_block(size) & jnp.logical_not(same_block(size // 2))
        offs = [jnp.where(below, low, 0.0) for low in low_list]
        inner = _pair_products_bf16(offs, tinvs, lo)
        outer = _pair_products_bf16(tinvs, inner, lo)
        tinvs = [t - o for t, o in zip(tinvs, outer)]
        size *= 2
    return tinvs


def _delta_prep_kernel(q_ref, k_ref, v_ref, beta_ref, gc_ref, gct_ref, epair_ref, u_ref, w_ref, a_ref, *, cb):
    C = CHUNK
    ri = lax.broadcasted_iota(jnp.int32, (C, 2 * C), 0)
    li = lax.broadcasted_iota(jnp.int32, (C, 2 * C), 1)
    lm = li & (C - 1)
    lo = li < C
    incl = ri >= lm
    strict = ri > lm
    eye2 = (ri == lm).astype(F32)
    units = [(c, p) for c in range(cb) for p in range(HK)]
    rows = lambda c: slice(c * C, (c + 1) * C)

    grams = []
    for c, p in units:
        kb = k_ref[rows(c), p * DK:(p + 1) * DK]
        qb = q_ref[rows(c), p * DK:(p + 1) * DK]
        grams.append(lax.dot_general(jnp.concatenate([kb, qb], axis=0), jnp.concatenate([kb, kb], axis=0),
                                     (((1,), (1,)), ((), ())), preferred_element_type=F32))

    epair = epair_ref[...]
    g2s, b2s, betas, begs = [], [], [], []
    for c in range(cb):
        beta = beta_ref[rows(c), :]
        gc = gc_ref[rows(c), :]
        g2s.append(_spread(gc, epair, 3))
        b2s.append(_spread(beta, epair, 2))
        betas.append(beta)
        begs.append(beta * jnp.exp(gc))

    lows = []
    for (c, p), gram in zip(units, grams):
        pt = slice(p * 2 * C, (p + 1) * 2 * C)
        grow2 = gct_ref[c, p:p + 1, :]
        decay2 = jnp.exp(jnp.where(incl, g2s[c][:, pt] - grow2, -jnp.inf))
        a_ref[rows(c), pt] = (gram[C:] * decay2).astype(a_ref.dtype)
        lows.append(jnp.where(strict, b2s[c][:, pt] * gram[:C] * decay2, 0.0))

    same_block = lambda s: (ri // s) == (lm // s)
    tinvs = _pair_unit_lower_inverse(lows, lo, eye2, same_block)

    ops = []
    for (c, p), tinv in zip(units, tinvs):
        kf = k_ref[rows(c), p * DK:(p + 1) * DK].astype(F32)
        rhs = []
        for h in (PAIR * p, PAIR * p + 1):
            ht = slice(h * DV, (h + 1) * DV)
            vf = v_ref[rows(c), ht].astype(F32)
            rhs.append(jnp.concatenate([vf * betas[c][:, h:h + 1], kf * begs[c][:, h:h + 1]], axis=1))
        ops.append((_pair_rows(tinv.astype(BF16), lo), jnp.concatenate(rhs, axis=0).astype(BF16)))
    for (c, p), (lhs, rhs) in zip(units, ops):
        uw = jnp.dot(lhs, rhs, preferred_element_type=F32)
        for i, h in enumerate((PAIR * p, PAIR * p + 1)):
            u_ref[rows(c), h * DV:(h + 1) * DV] = uw[i * C:(i + 1) * C, :DV].astype(u_ref.dtype)
            w_ref[rows(c), h * DV:(h + 1) * DV] = uw[i * C:(i + 1) * C, DV:].astype(w_ref.dtype)


def _delta_prep(qk, v, beta, gc, gct, B, T):
    R = B * T
    cb = 2 if T % (2 * CHUNK) == 0 else 1
    rb = cb * CHUNK
    gct2 = gct.reshape(R // CHUNK, HK, 2 * CHUNK)
    kern = functools.partial(_delta_prep_kernel, cb=cb)
    return pl.pallas_call(
        kern,
        grid=(R // rb,),
        in_specs=[
            pl.BlockSpec((rb, QK_DIM), lambda i: (i, 0)),
            pl.BlockSpec((rb, QK_DIM), lambda i: (i, 1)),
            pl.BlockSpec((rb, V_DIM), lambda i: (i, 0)),
            pl.BlockSpec((rb, HV), lambda i: (i, 0)),
            pl.BlockSpec((rb, HV), lambda i: (i, 0)),
            pl.BlockSpec((cb, HK, 2 * CHUNK), lambda i: (i, 0, 0)),
            pl.BlockSpec((HV, HV * CHUNK), lambda i: (0, 0)),
        ],
        out_specs=[
            pl.BlockSpec((rb, V_DIM), lambda i: (i, 0)),
            pl.BlockSpec((rb, V_DIM), lambda i: (i, 0)),
            pl.BlockSpec((rb, HV * CHUNK), lambda i: (i, 0)),
        ],
        out_shape=[
            jax.ShapeDtypeStruct((R, V_DIM), BF16),
            jax.ShapeDtypeStruct((R, V_DIM), BF16),
            jax.ShapeDtypeStruct((R, HV * CHUNK), BF16),
        ],
        compiler_params=_params(("arbitrary",)),
        name="gdn_delta_prep",
    )(qk, qk, v, beta, gc, gct2, _head_spread_matrices()[1])


def _delta_scan_kernel(q_ref, k_ref, u_ref, w_ref, a_ref, z_ref, gc_ref, s0_ref, on_ref, efull_ref,
                       og_ref, s_ref, *, bb):
    c = pl.program_id(1)

    @pl.when(c == 0)
    def _():
        s_ref[...] = s0_ref[...]

    C = CHUNK
    lo = lax.broadcasted_iota(jnp.int32, (C, 2 * C), 1) < C
    onorm = on_ref[...]
    units = [(b, h) for b in range(bb) for h in range(HV)]
    pairs = [(b, p) for b in range(bb) for p in range(HK)]

    efull = efull_ref[...]
    egs, kgs, dgs = [], [], []
    for b in range(bb):
        gc = gc_ref[b]
        g_last = gc[C - 1:C, :]
        egs.append(_spread(jnp.exp(gc), efull, 2))
        kgs.append(_spread(jnp.exp(g_last - gc), efull, 2))
        dgs.append(_spread(jnp.exp(gc[C - 8:C, :]), efull, 3)[7:8, :])

    ops = []
    kds = []
    dec = []
    for b, h in units:
        p = h // PAIR
        ht = slice(h * DV, (h + 1) * DV)
        qg = (q_ref[b, :, p * DK:(p + 1) * DK].astype(F32) * egs[b][:, ht]).astype(BF16)
        kds.append((k_ref[b, :, p * DK:(p + 1) * DK].astype(F32) * kgs[b][:, ht]).astype(BF16))
        dec.append(dgs[b][:, ht])
        ops.append((jnp.concatenate([w_ref[b, :, ht], qg], axis=0), s_ref[b, h].astype(BF16)))
    wqs = [jnp.dot(a, s, preferred_element_type=F32) for a, s in ops]

    vns = [(u_ref[b, :, h * DV:(h + 1) * DV].astype(F32) - wq[:C]).astype(BF16) for (b, h), wq in zip(units, wqs)]

    oparts = []
    for i, (b, p) in enumerate(pairs):
        att = _pair_rows(a_ref[b, :, p * 2 * C:(p + 1) * 2 * C], lo)
        vn2 = jnp.concatenate([vns[PAIR * i], vns[PAIR * i + 1]], axis=0)
        oparts.append(jnp.dot(att, vn2, preferred_element_type=F32))
    sds = [lax.dot_general(kd, vn, (((0,), (0,)), ((), ())), preferred_element_type=F32)
           for kd, vn in zip(kds, vns)]

    for i, (b, h) in enumerate(units):
        s_ref[b, h] = s_ref[b, h] * dec[i] + sds[i]
        o = wqs[i][C:] + oparts[i // PAIR][(h % PAIR) * C:(h % PAIR + 1) * C]
        on = o * _rms_scale(o) * onorm
        zf = z_ref[b, :, h * DV:(h + 1) * DV].astype(F32)
        og_ref[b, :, h * DV:(h + 1) * DV] = (on * _silu(zf)).astype(og_ref.dtype)


def _delta_scan(qk, u, w, att, z, gc, s0, o_norm, B, T):
    nc = T // CHUNK
    bb = 4 if B % 4 == 0 else (2 if B % 2 == 0 else 1)
    v3 = lambda a: a.reshape(B, T, a.shape[-1])
    blk = lambda n, j=0: pl.BlockSpec((bb, CHUNK, n), lambda b, c: (b, c, j))
    kern = functools.partial(_delta_scan_kernel, bb=bb)
    og, s_new = pl.pallas_call(
        kern,
        grid=(B // bb, nc),
        in_specs=[
            blk(QK_DIM, 0), blk(QK_DIM, 1), blk(V_DIM), blk(V_DIM), blk(HV * CHUNK), blk(V_DIM), blk(HV),
            pl.BlockSpec((bb, HV, DK, DV), lambda b, c: (b, 0, 0, 0)),
            pl.BlockSpec((1, DV), lambda b, c: (0, 0)),
            pl.BlockSpec((HV, HV * DV), lambda b, c: (0, 0)),
        ],
        out_specs=[
            blk(V_DIM),
            pl.BlockSpec((bb, HV, DK, DV), lambda b, c: (b, 0, 0, 0)),
        ],
        out_shape=[
            jax.ShapeDtypeStruct((B, T, V_DIM), BF16),
            jax.ShapeDtypeStruct((B, HV, DK, DV), F32),
        ],
        compiler_params=_params(("arbitrary", "arbitrary")),
        name="gdn_delta_scan",
    )(v3(qk), v3(qk), v3(u), v3(w), v3(att), v3(z), v3(gc), s0, o_norm, _head_spread_matrices()[0])
    return og.reshape(B * T, V_DIM), s_new


MXU_COLS = 256
FF_SPLITS = (6 * MXU_COLS, 5 * MXU_COLS)
assert sum(FF_SPLITS) == D_FF
RESIDENT = pl.Buffered(1)


def _swiglu(h, w1_ref, w3_ref, w2_ref, lead=()):
    out = None
    start = 0
    for width in FF_SPLITS:
        sl = slice(start, start + width)
        start += width
        a = jnp.dot(h, w1_ref[lead + (slice(None), sl)], preferred_element_type=F32)
        b = jnp.dot(h, w3_ref[lead + (slice(None), sl)], preferred_element_type=F32)
        mid = (_silu(a) * b).astype(BF16)
        part = jnp.dot(mid, w2_ref[lead + (sl, slice(None))], preferred_element_type=F32)
        out = part if out is None else out + part
    return out


def _mixer_out_ffn_kernel(x_ref, a_ref, wo_ref, g_ref, w1_ref, w3_ref, w2_ref, o_ref):
    x1 = x_ref[...] + jnp.dot(a_ref[...], wo_ref[...], preferred_element_type=F32)
    h = (x1 * _rms_scale(x1) * g_ref[...]).astype(BF16)
    o_ref[...] = x1 + _swiglu(h, w1_ref, w3_ref, w2_ref)


def _mixer_out_ffn(x, a, w_out, g, w1, w3, w2):
    R = x.shape[0]
    K = a.shape[1]
    tm = min(ROW_TILE, R)
    held = lambda shape: pl.BlockSpec(shape, lambda i: (0,) * len(shape), pipeline_mode=RESIDENT)
    return pl.pallas_call(
        _mixer_out_ffn_kernel,
        grid=(R // tm,),
        in_specs=[
            pl.BlockSpec((tm, D_MODEL), lambda i: (i, 0)),
            pl.BlockSpec((tm, K), lambda i: (i, 0)),
            held((K, D_MODEL)),
            held((1, D_MODEL)),
            held((D_MODEL, D_FF)), held((D_MODEL, D_FF)), held((D_FF, D_MODEL)),
        ],
        out_specs=pl.BlockSpec((tm, D_MODEL), lambda i: (i, 0)),
        out_shape=jax.ShapeDtypeStruct((R, D_MODEL), F32),
        compiler_params=_params(("arbitrary",), VMEM_LIMIT_BIG),
        name="gdn_out_dense_swiglu",
    )(x, a, w_out, g, w1, w3, w2)


KV_DUP = 2 * HKV * LANES
KV_STD = 2 * HKV * HD


def _rope(x, cos, sm, sp):
    up = pltpu.roll(x, LANES - ROT_DIM // 2, 1)
    dn = pltpu.roll(x, ROT_DIM // 2, 1)
    return x * cos + up * sm + dn * sp


def _qkv_rope_kernel(x_ref, gq_ref, gkv_ref, wq_ref, wkv_ref, wkvs_ref, cos_ref, sm_ref, sp_ref,
                     q_ref, kvd_ref, kvs_ref, *, nt, keep):
    x = x_ref[...]
    xr = x * _rms_scale(x)
    hq = (xr * gq_ref[...]).astype(BF16)
    hkv = (xr * gkv_ref[...]).astype(BF16)
    cos = cos_ref[...]
    sm = sm_ref[...]
    sp = sp_ref[...]
    q = jnp.dot(hq, wq_ref[...], preferred_element_type=F32)
    for t in range(HQ * HD // LANES):
        sl = slice(t * LANES, (t + 1) * LANES)
        q_ref[:, sl] = (_rope(q[:, sl], cos, sm, sp) * (HD ** -0.5)).astype(q_ref.dtype)
    kv = jnp.dot(hkv, wkv_ref[...], preferred_element_type=F32)
    half = KV_DUP // 2
    for t in range(HKV):
        sl = slice(t * LANES, (t + 1) * LANES)
        kvd_ref[:, sl] = _rope(kv[:, sl], cos, sm, sp).astype(kvd_ref.dtype)
    kvd_ref[:, half:] = kv[:, half:].astype(kvd_ref.dtype)

    @pl.when(pl.program_id(0) % nt == nt - 1)
    def _():
        tm = x.shape[0]
        rows = slice(tm - keep, tm)
        kvs = jnp.dot(hkv[rows], wkvs_ref[...], preferred_element_type=F32)
        for t in range(HKV * HD // LANES):
            sl = slice(t * LANES, (t + 1) * LANES)
            kvs_ref[:, sl] = _rope(kvs[:, sl], cos[rows], sm[rows], sp[rows])
        kvs_ref[:, HKV * HD:] = kvs[:, HKV * HD:]


def _rope_tables(pos):
    half = ROT_DIM // 2
    inv = jnp.power(ROPE_THETA, -jnp.arange(half, dtype=F32) * 2.0 / ROT_DIM)
    ang = pos.astype(F32)[:, None] * inv[None, :]
    cos = jnp.cos(ang)
    sin = jnp.sin(ang)
    T = pos.shape[0]
    one = jnp.ones((T, HD - ROT_DIM), F32)
    zero = jnp.zeros((T, HD - ROT_DIM), F32)
    zh = jnp.zeros((T, half), F32)
    c64 = jnp.concatenate([cos, cos, one], axis=1)
    sm64 = jnp.concatenate([-sin, zh, zero], axis=1)
    sp64 = jnp.concatenate([zh, sin, zero], axis=1)
    dup = lambda a: jnp.concatenate([a, a], axis=1)
    return dup(c64), dup(sm64), dup(sp64)


def _qkv_rope(x, gq, gkv, w_q, w_kvd, w_kvs, tables, B, T, keep):
    R = B * T
    tm = min(ROW_TILE, T)
    nt = T // tm
    assert keep <= tm
    tab = pl.BlockSpec((tm, LANES), lambda i: (i % nt, 0))
    kern = functools.partial(_qkv_rope_kernel, nt=nt, keep=keep)
    return pl.pallas_call(
        kern,
        grid=(R // tm,),
        in_specs=[
            pl.BlockSpec((tm, D_MODEL), lambda i: (i, 0)),
            pl.BlockSpec((1, D_MODEL), lambda i: (0, 0)),
            pl.BlockSpec((1, D_MODEL), lambda i: (0, 0)),
            pl.BlockSpec((D_MODEL, HQ * HD), lambda i: (0, 0)),
            pl.BlockSpec((D_MODEL, KV_DUP), lambda i: (0, 0)),
            pl.BlockSpec((D_MODEL, KV_STD), lambda i: (0, 0)),
            tab, tab, tab,
        ],
        out_specs=[
            pl.BlockSpec((tm, HQ * HD), lambda i: (i, 0)),
            pl.BlockSpec((tm, KV_DUP), lambda i: (i, 0)),
            pl.BlockSpec((keep, KV_STD), lambda i: (i // nt, 0)),
        ],
        out_shape=[
            jax.ShapeDtypeStruct((R, HQ * HD), BF16),
            jax.ShapeDtypeStruct((R, KV_DUP), BF16),
            jax.ShapeDtypeStruct((B * keep, KV_STD), F32),
        ],
        compiler_params=_params(("arbitrary",)),
        name="swa_qkv_rope",
    )(x, gq, gkv, w_q, w_kvd, w_kvs, *tables)


KEY_CHUNKS = WINDOW // CHUNK + 1
KEY_PAD = 256


def _attn_kernel(*refs, off, cq):
    q_ref = refs[0]
    krefs = refs[1:KEY_CHUNKS + cq]
    sink_ref, ones_ref, o_ref = refs[KEY_CHUNKS + cq:]
    c = pl.program_id(1)
    lane = lax.broadcasted_iota(jnp.int32, (KEY_CHUNKS * CHUNK, LANES), 1)
    lo = lane < HD
    col = lax.broadcasted_iota(jnp.int32, (1, KEY_PAD), 1)
    sink_col = col == KEY_CHUNKS * CHUNK
    zpad = jnp.zeros((KEY_PAD - KEY_CHUNKS * CHUNK, LANES), BF16)
    zero = jnp.zeros((), BF16)
    tiles = HQ // HKV // 2
    ones_blk = ones_ref[...]
    units = [(u, g) for u in range(cq) for g in range(HKV)]

    vblks = []
    scores = []
    for u, g in units:
        kr = krefs[u:u + KEY_CHUNKS]
        kg = jnp.concatenate([r[:, g * LANES:(g + 1) * LANES] for r in kr], axis=0)
        vg = jnp.concatenate([r[:, KV_DUP // 2 + g * LANES:KV_DUP // 2 + (g + 1) * LANES] for r in kr], axis=0)
        kblk = jnp.concatenate([jnp.where(lo, kg, zero), zpad, jnp.where(lo, zero, kg), zpad], axis=0)
        vblk = jnp.concatenate([jnp.where(lo, vg, zero), zpad, jnp.where(lo, zero, vg), zpad], axis=0)
        vblks.append(jnp.concatenate([vblk, ones_blk], axis=1))
        qg = jnp.concatenate([q_ref[u * CHUNK:(u + 1) * CHUNK, (g * tiles + p) * LANES:(g * tiles + p + 1) * LANES]
                              for p in range(tiles)], axis=0)
        scores.append(lax.dot_general(qg, kblk, (((1,), (1,)), ((), ())), preferred_element_type=F32))

    probs = []
    for (u, g), sc in zip(units, scores):
        first = c * cq + u + off - (KEY_CHUNKS - 1)
        valid = (col < KEY_CHUNKS * CHUNK) & (col // CHUNK + first >= 0)
        base_bias = jnp.where(valid, 0.0, -jnp.inf).astype(F32)
        rows = []
        for p in range(tiles):
            t = g * tiles + p
            ps = []
            for hh in range(2):
                bias = jnp.where(sink_col, sink_ref[:, 2 * t + hh:2 * t + hh + 1], base_bias)
                sh = sc[p * CHUNK:(p + 1) * CHUNK, hh * KEY_PAD:(hh + 1) * KEY_PAD] + bias
                m = jnp.max(sh, axis=-1, keepdims=True)
                ps.append(jnp.exp(sh - m).astype(BF16))
            rows.append(jnp.concatenate(ps, axis=1))
        probs.append(jnp.concatenate(rows, axis=0))

    outs = [jnp.dot(pr, vb, preferred_element_type=F32) for pr, vb in zip(probs, vblks)]
    for (u, g), out in zip(units, outs):
        for p in range(tiles):
            t = g * tiles + p
            rows = slice(p * CHUNK, (p + 1) * CHUNK)
            o_ref[u * CHUNK:(u + 1) * CHUNK, t * LANES:(t + 1) * LANES] = (
                out[rows, :LANES] / out[rows, LANES:]).astype(o_ref.dtype)


def _attention(q, kvd, sinks, B, Tq, Tk):
    ncq = Tq // CHUNK
    nck = Tk // CHUNK
    off = nck - ncq
    cq = 4 if ncq % 4 == 0 else (2 if ncq % 2 == 0 else 1)
    kern = functools.partial(_attn_kernel, off=off, cq=cq)
    blk_row = np.arange(2 * KEY_PAD)[:, None] // KEY_PAD
    blk_lane = np.arange(LANES)[None, :] // HD
    ones_blk = jnp.asarray((blk_row == blk_lane).astype(np.float32), dtype=BF16)

    def kspec(i):
        return pl.BlockSpec((CHUNK, KV_DUP),
                            lambda b, c: (b * nck + jnp.maximum(c * cq + off - (KEY_CHUNKS - 1) + i, 0), 0))

    qspec = pl.BlockSpec((cq * CHUNK, HQ * HD), lambda b, c: (b * (ncq // cq) + c, 0))
    return pl.pallas_call(
        kern,
        grid=(B, ncq // cq),
        in_specs=[qspec] + [kspec(i) for i in range(KEY_CHUNKS + cq - 1)] + [
            pl.BlockSpec((1, HQ), lambda b, c: (0, 0)),
            pl.BlockSpec((2 * KEY_PAD, LANES), lambda b, c: (0, 0)),
        ],
        out_specs=qspec,
        out_shape=jax.ShapeDtypeStruct((B * Tq, HQ * HD), BF16),
        compiler_params=_params(("arbitrary", "arbitrary")),
        name="swa_attention",
    )(q, *([kvd] * (KEY_CHUNKS + cq - 1)), sinks, ones_blk)


def _oproj_router_kernel(x_ref, a_ref, w_ref, g_ref, wr_ref, x3_ref, h_ref, e_ref, gate_ref, cnt_ref):
    i = pl.program_id(0)
    x3 = x_ref[...] + jnp.dot(a_ref[...], w_ref[...], preferred_element_type=F32)
    x3_ref[...] = x3
    h = x3 * _rms_scale(x3) * g_ref[...]
    h_ref[...] = h
    h1, h2 = _split2(h)
    r1, r2 = _split2(wr_ref[...])
    part = jnp.dot(h1, jnp.concatenate([r1, r2], axis=1), preferred_element_type=F32)
    logits = part[:, :N_EXPERTS] + (part[:, N_EXPERTS:] + jnp.dot(h2, r1, preferred_element_type=F32))
    eidx = lax.broadcasted_iota(jnp.int32, logits.shape, 1)
    m1 = jnp.max(logits, axis=-1, keepdims=True)
    i1 = jnp.min(jnp.where(logits == m1, eidx, N_EXPERTS), axis=-1, keepdims=True)
    rest = jnp.where(eidx == i1, -jnp.inf, logits)
    m2 = jnp.max(rest, axis=-1, keepdims=True)
    i2 = jnp.min(jnp.where(rest == m2, eidx, N_EXPERTS), axis=-1, keepdims=True)
    e2 = jnp.exp(m2 - m1)
    den = 1.0 + e2
    e_ref[...] = jnp.concatenate([i1, i2], axis=1)
    gate_ref[...] = jnp.concatenate([1.0 / den, e2 / den], axis=1)
    hot = ((eidx == i1) | (eidx == i2)).astype(F32)

    @pl.when(i == 0)
    def _():
        cnt_ref[...] = jnp.zeros_like(cnt_ref)

    cnt_ref[...] += jnp.sum(hot, axis=0, keepdims=True)


def _oproj_router(x, a, w_o, g, w_r):
    R = x.shape[0]
    tm = min(ROW_TILE, R)
    row = lambda n: pl.BlockSpec((tm, n), lambda i: (i, 0))
    return pl.pallas_call(
        _oproj_router_kernel,
        grid=(R // tm,),
        in_specs=[
            row(D_MODEL), row(HQ * HD),
            pl.BlockSpec((HQ * HD, D_MODEL), lambda i: (0, 0)),
            pl.BlockSpec((1, D_MODEL), lambda i: (0, 0)),
            pl.BlockSpec((D_MODEL, N_EXPERTS), lambda i: (0, 0)),
        ],
        out_specs=[row(D_MODEL), row(D_MODEL), row(TOP_K), row(TOP_K),
                   pl.BlockSpec((1, N_EXPERTS), lambda i: (0, 0))],
        out_shape=[
            jax.ShapeDtypeStruct((R, D_MODEL), F32),
            jax.ShapeDtypeStruct((R, D_MODEL), F32),
            jax.ShapeDtypeStruct((R, TOP_K), jnp.int32),
            jax.ShapeDtypeStruct((R, TOP_K), F32),
            jax.ShapeDtypeStruct((1, N_EXPERTS), F32),
        ],
        compiler_params=_params(("arbitrary",)),
        name="swa_oproj_router",
    )(x, a, w_o, g, w_r)


def _slot_kernel(e_ref, base_ref, tri_ref, pos_ref, run_scr):
    i = pl.program_id(0)

    @pl.when(i == 0)
    def _():
        run_scr[...] = jnp.zeros_like(run_scr)

    e = e_ref[...]
    tm = e.shape[0]
    eidx = lax.broadcasted_iota(jnp.int32, (tm, N_EXPERTS), 1)
    hot0 = eidx == e[:, 0:1]
    hot1 = eidx == e[:, 1:2]
    hot = (hot0 | hot1).astype(BF16)
    before = jnp.dot(tri_ref[...], hot, preferred_element_type=F32)
    dest = before + run_scr[...] + base_ref[...]
    p0 = jnp.sum(jnp.where(hot0, dest, 0.0), axis=-1, keepdims=True)
    p1 = jnp.sum(jnp.where(hot1, dest, 0.0), axis=-1, keepdims=True)
    pos_ref[...] = jnp.concatenate([p0, p1], axis=1).astype(jnp.int32)
    run_scr[...] += jnp.sum(hot.astype(F32), axis=0, keepdims=True)


def _slots(eidx, base):
    R = eidx.shape[0]
    tm = min(ROW_TILE, R)
    idx = np.arange(tm)
    tri = jnp.asarray((idx[:, None] > idx[None, :]).astype(np.float32), dtype=BF16)
    return pl.pallas_call(
        _slot_kernel,
        grid=(R // tm,),
        in_specs=[
            pl.BlockSpec((tm, TOP_K), lambda i: (i, 0)),
            pl.BlockSpec((1, N_EXPERTS), lambda i: (0, 0)),
            pl.BlockSpec((tm, tm), lambda i: (0, 0)),
        ],
        out_specs=pl.BlockSpec((tm, TOP_K), lambda i: (i, 0)),
        out_shape=jax.ShapeDtypeStruct((R, TOP_K), jnp.int32),
        scratch_shapes=[pltpu.VMEM((1, N_EXPERTS), F32)],
        compiler_params=_params(("arbitrary",)),
        name="moe_slots",
    )(eidx, base, tri)


DMA_UNROLL = 8


def _row_copy(src, i, dst, j, sem):
    return pltpu.make_async_copy(src.at[pl.ds(i, 1)], dst.at[pl.ds(j, 1)], sem)


def _dispatch_kernel(pos_ref, cnt_ref, base_ref, nused_ref, h_ref, xs_ref, zero_scr, sem, zsem, *, tm, te, n_tiles):
    i = pl.program_id(0)

    @pl.when(i == 0)
    def _():
        zero_scr[...] = jnp.zeros_like(zero_scr)

        def tile_copy(t):
            dst = xs_ref.at[pl.ds(pl.multiple_of(t * te, te), te)]
            return pltpu.make_async_copy(zero_scr, dst, zsem)

        def issue_tile(t, carry):
            tile_copy(t).start()
            return carry

        def drain_tile(t, carry):
            tile_copy(t).wait()
            return carry

        lax.fori_loop(nused_ref[0], n_tiles, issue_tile, 0)
        lax.fori_loop(nused_ref[0], n_tiles, drain_tile, 0)
        for e in range(N_EXPERTS):
            n = cnt_ref[e]
            start = base_ref[e] + n
            npad = (te - n % te) % te

            def issue(r, carry):
                _row_copy(zero_scr, 0, xs_ref, start + r, zsem).start()
                return carry

            def drain(r, carry):
                _row_copy(zero_scr, 0, xs_ref, start + r, zsem).wait()
                return carry

            lax.fori_loop(0, npad, issue, 0)
            lax.fori_loop(0, npad, drain, 0)

    def issue(r, carry):
        _row_copy(h_ref, r, xs_ref, pos_ref[2 * r], sem).start()
        _row_copy(h_ref, r, xs_ref, pos_ref[2 * r + 1], sem).start()
        return carry

    lax.fori_loop(0, tm, issue, 0, unroll=DMA_UNROLL)
    for _ in range(TOP_K):
        pltpu.make_async_copy(h_ref, xs_ref.at[pl.ds(0, tm)], sem).wait()


def _dispatch(h, pos_flat, counts, base, n_used, n_tiles, te):
    R = h.shape[0]
    tm = min(ROW_TILE, R)
    kern = functools.partial(_dispatch_kernel, tm=tm, te=te, n_tiles=n_tiles)
    smem = pl.BlockSpec(memory_space=pltpu.SMEM)
    return pl.pallas_call(
        kern,
        grid=(R // tm,),
        in_specs=[
            pl.BlockSpec((TOP_K * tm,), lambda i: (i,), memory_space=pltpu.SMEM),
            smem, smem, smem,
            pl.BlockSpec((tm, D_MODEL), lambda i: (i, 0)),
        ],
        out_specs=pl.BlockSpec(memory_space=pl.ANY),
        out_shape=jax.ShapeDtypeStruct((n_tiles * te, D_MODEL), F32),
        scratch_shapes=[pltpu.VMEM((te, D_MODEL), F32), pltpu.SemaphoreType.DMA(()), pltpu.SemaphoreType.DMA(())],
        compiler_params=pltpu.CompilerParams(dimension_semantics=("arbitrary",), vmem_limit_bytes=VMEM_LIMIT,
                                             has_side_effects=True),
        name="moe_dispatch",
    )(pos_flat, counts, base, n_used, h)


def _expert_ffn_kernel(te_ref, nused_ref, x_ref, w1_ref, w3_ref, w2_ref, y_ref):
    i = pl.program_id(0)

    @pl.when(i < nused_ref[0])
    def _():
        y_ref[...] = _swiglu(x_ref[...].astype(BF16), w1_ref, w3_ref, w2_ref, lead=(0,))

    @pl.when(i >= nused_ref[0])
    def _():
        y_ref[...] = jnp.zeros_like(y_ref)


def _expert_ffn(xs, tile_expert, n_used, w1, w3, w2, te):
    n_rows = xs.shape[0]
    nt = n_rows // te
    wspec = lambda shape: pl.BlockSpec((1,) + shape, lambda i, te_r, nu_r: (te_r[i], 0, 0), pipeline_mode=RESIDENT)
    grid_spec = pltpu.PrefetchScalarGridSpec(
        num_scalar_prefetch=2,
        grid=(nt,),
        in_specs=[
            pl.BlockSpec((te, D_MODEL), lambda i, te_r, nu_r: (i, 0)),
            wspec((D_MODEL, D_FF)), wspec((D_MODEL, D_FF)), wspec((D_FF, D_MODEL)),
        ],
        out_specs=pl.BlockSpec((te, D_MODEL), lambda i, te_r, nu_r: (i, 0)),
    )
    return pl.pallas_call(
        _expert_ffn_kernel,
        grid_spec=grid_spec,
        out_shape=jax.ShapeDtypeStruct((n_rows, D_MODEL), F32),
        compiler_params=_params(("arbitrary",), VMEM_LIMIT_BIG),
        name="moe_expert_swiglu",
    )(tile_expert, n_used, xs, w1, w3, w2)


def _combine_kernel(pos_ref, x_ref, gate_ref, g_ref, y_ref, o_ref, buf0, buf1, sem, *, tm):
    def issue(r, carry):
        _row_copy(y_ref, pos_ref[2 * r], buf0, r, sem).start()
        _row_copy(y_ref, pos_ref[2 * r + 1], buf1, r, sem).start()
        return carry

    lax.fori_loop(0, tm, issue, 0, unroll=DMA_UNROLL)
    for buf in (buf0, buf1):
        pltpu.make_async_copy(y_ref.at[pl.ds(0, tm)], buf, sem).wait()
    gate = gate_ref[...]
    x = x_ref[...] + (buf0[...] * gate[:, 0:1] + buf1[...] * gate[:, 1:2])
    o_ref[...] = x * _rms_scale(x) * g_ref[...]


def _combine(x3, gates, pos_flat, y, g):
    R = x3.shape[0]
    tm = min(ROW_TILE, R)
    kern = functools.partial(_combine_kernel, tm=tm)
    return pl.pallas_call(
        kern,
        grid=(R // tm,),
        in_specs=[
            pl.BlockSpec((TOP_K * tm,), lambda i: (i,), memory_space=pltpu.SMEM),
            pl.BlockSpec((tm, D_MODEL), lambda i: (i, 0)),
            pl.BlockSpec((tm, TOP_K), lambda i: (i, 0)),
            pl.BlockSpec((1, D_MODEL), lambda i: (0, 0)),
            pl.BlockSpec(memory_space=pl.ANY),
        ],
        out_specs=pl.BlockSpec((tm, D_MODEL), lambda i: (i, 0)),
        out_shape=jax.ShapeDtypeStruct((R, D_MODEL), F32),
        scratch_shapes=[pltpu.VMEM((tm, D_MODEL), F32), pltpu.VMEM((tm, D_MODEL), F32),
                        pltpu.SemaphoreType.DMA(())],
        compiler_params=_params(("arbitrary",)),
        name="moe_combine_norm",
    )(pos_flat, x3, gates, g, y)


def _prep_weights(p):
    w_in = p["gdn_w_in"][0]
    w_kv = p["w_kv"]
    kcols = w_kv[:, :HKV * HD].reshape(D_MODEL, HKV, HD)
    vcols = w_kv[:, HKV * HD:].reshape(D_MODEL, HKV, HD)
    dup = lambda a: jnp.concatenate([a, a], axis=2).reshape(D_MODEL, HKV * LANES)
    w_kvd = jnp.concatenate([dup(kcols), dup(vcols)], axis=1)
    return dict(
        w_qk=w_in[:, :2 * QK_DIM].astype(BF16),
        w_v=w_in[:, 2 * QK_DIM:CONV_CH].astype(BF16),
        w_z=w_in[:, CONV_CH:CONV_CH + V_DIM].astype(BF16),
        w_ba=w_in[:, CONV_CH + V_DIM:].astype(BF16),
        w_out=p["gdn_w_out"][0].astype(BF16),
        ffn_w1=p["ffn_w1"][0].astype(BF16),
        ffn_w3=p["ffn_w3"][0].astype(BF16),
        ffn_w2=p["ffn_w2"][0].astype(BF16),
        w_kvd=w_kvd.astype(BF16),
        w_kvs=w_kv.astype(BF16),
        w_q=p["swa_w_q"][0].astype(BF16),
        w_o=p["swa_w_o"][0].astype(BF16),
        moe_w1=p["moe_w1"][0].astype(BF16),
        moe_w3=p["moe_w3"][0].astype(BF16),
        moe_w2=p["moe_w2"][0].astype(BF16),
    )


def _dup_heads(a):
    B, T = a.shape[:2]
    return jnp.concatenate([a, a], axis=3).reshape(B * T, HKV * LANES)


def _trunk(x, pos, conv0, s0, win_k, win_v, p, w):
    B, T, _ = x.shape
    R = B * T
    xf = x.reshape(R, D_MODEL)
    row = lambda a: a.reshape(1, -1)

    g_attn = row(p["norm_attn"][0])
    w_conv = p["gdn_w_conv"][0]
    nqk = 2 * QK_DIM
    qk, conv_qk = _qkv_conv(xf, g_attn, w["w_qk"], w_conv[:, :nqk], conv0[:, :, :nqk], B, T, True)
    v, conv_v = _qkv_conv(xf, g_attn, w["w_v"], w_conv[:, nqk:], conv0[:, :, nqk:], B, T, False)
    conv_new = jnp.concatenate([conv_qk, conv_v], axis=-1).reshape(B, -1, CONV_W - 1, CONV_CH)[:, -1]
    z, beta, gc, gct = _zbg(xf, g_attn, w["w_z"], w["w_ba"], p["gdn_a_log"][0], p["gdn_dt_bias"][0], B, T)
    u, wv, att_l = _delta_prep(qk, v, beta, gc, gct, B, T)
    og, s_new = _delta_scan(qk, u, wv, att_l, z, gc, s0, row(p["gdn_o_norm"][0]), B, T)
    x2 = _mixer_out_ffn(xf, og, w["w_out"], row(p["norm_ffn"][0]), w["ffn_w1"], w["ffn_w3"], w["ffn_w2"])

    keep = min(WINDOW, T) if win_k is None else T
    q, kvd, kvs = _qkv_rope(x2, row(p["norm_attn"][1]), row(p["kv_norm"]), w["w_q"], w["w_kvd"], w["w_kvs"],
                            _rope_tables(pos), B, T, keep)
    k_new = kvs[:, :HKV * HD].reshape(B, keep, HKV, HD)
    v_new = kvs[:, HKV * HD:].reshape(B, keep, HKV, HD)
    if win_k is None:
        keys, Tk = kvd, T
    else:
        hist = jnp.concatenate([_dup_heads(win_k), _dup_heads(win_v)], axis=1).astype(BF16)
        Tw = win_k.shape[1]
        keys = jnp.concatenate([hist.reshape(B, Tw, KV_DUP), kvd.reshape(B, T, KV_DUP)], axis=1)
        Tk = Tw + T
        keys = keys.reshape(B * Tk, KV_DUP)
    att = _attention(q, keys, row(p["swa_sinks"][0]), B, T, Tk)
    x3, h, eidx, gates, counts = _oproj_router(x2, att, w["w_o"], row(p["norm_ffn"][1]), p["moe_router"][0])

    te = min(EXPERT_TILE, max(TOP_K * R // N_EXPERTS, LANES))
    n_tiles = TOP_K * R // te + N_EXPERTS
    cnt = counts[0].astype(jnp.int32)
    tiles = (cnt + te - 1) // te
    tile_end = jnp.cumsum(tiles)
    base = (tile_end - tiles) * te
    n_used = tile_end[-1:]
    tile_expert = jnp.minimum(jnp.searchsorted(tile_end, jnp.arange(n_tiles, dtype=jnp.int32), side="right"),
                              N_EXPERTS - 1).astype(jnp.int32)
    pos_slot = _slots(eidx, base.astype(F32).reshape(1, N_EXPERTS)).reshape(-1)
    n_used = n_used.astype(jnp.int32)
    xs = _dispatch(h, pos_slot, cnt, base.astype(jnp.int32), n_used, n_tiles, te)
    ys = _expert_ffn(xs, tile_expert, n_used, w["moe_w1"], w["moe_w3"], w["moe_w2"], te)
    y = _combine(x3, gates, pos_slot, ys, row(p["final_norm"]))
    return (y.reshape(B, T, D_MODEL), conv_new[None], s_new[None], k_new, v_new)


def kernel(x_prompt, x_sample, cache_conv, state_delta, cache_k, cache_v, norm_attn, norm_ffn, gdn_w_in,
           gdn_w_conv, gdn_a_log, gdn_dt_bias, gdn_o_norm, gdn_w_out, kv_norm, w_kv, swa_w_q, swa_sinks, swa_w_o,
           ffn_w1, ffn_w3, ffn_w2, moe_router, moe_w1, moe_w3, moe_w2, final_norm):
    p = dict(norm_attn=norm_attn, norm_ffn=norm_ffn, gdn_w_in=gdn_w_in, gdn_w_conv=gdn_w_conv,
             gdn_a_log=gdn_a_log, gdn_dt_bias=gdn_dt_bias, gdn_o_norm=gdn_o_norm, gdn_w_out=gdn_w_out,
             kv_norm=kv_norm, w_kv=w_kv, swa_w_q=swa_w_q, swa_sinks=swa_sinks, swa_w_o=swa_w_o,
             ffn_w1=ffn_w1, ffn_w3=ffn_w3, ffn_w2=ffn_w2, moe_router=moe_router,
             moe_w1=moe_w1, moe_w3=moe_w3, moe_w2=moe_w2, final_norm=final_norm)
    w = _prep_weights(p)
    Bp, Tp, _ = x_prompt.shape
    Bs, Ts, _ = x_sample.shape
    conv0 = jnp.zeros((Bp, CONV_W - 1, CONV_CH), F32)
    s0 = jnp.zeros((Bp, HV, DK, DV), F32)
    y_p, conv_p, delta_p, k_p, v_p = _trunk(x_prompt, jnp.arange(Tp, dtype=jnp.int32), conv0, s0,
                                            None, None, p, w)
    pos_s = PAST_LEN + jnp.arange(Ts, dtype=jnp.int32)
    y_s, conv_s, delta_s, k_s, v_s = _trunk(x_sample, pos_s, cache_conv[0], state_delta[0],
                                            cache_k, cache_v, p, w)
    return (y_p, y_s, conv_p, delta_p, k_p, v_p, conv_s, delta_s, k_s, v_s)
```

```python
import functools

import jax
import jax.numpy as jnp
import numpy as np
from jax import lax
from jax.experimental import pallas as pl
from jax.experimental.pallas import tpu as pltpu

F32 = jnp.float32
BF16 = jnp.bfloat16

D_MODEL = 1024
CHUNK = 64
HK = 8
HV = 16
DK = 128
DV = 128
QK_DIM = HK * DK
V_DIM = HV * DV
CONV_CH = 2 * QK_DIM + V_DIM
CONV_W = 4
HQ = 16
HKV = 4
HD = 64
WINDOW = 128
ROT_DIM = HD // 4
ROPE_THETA = 500000.0
D_FF = 2816
N_EXPERTS = 8
TOP_K = 2
EPS = 1e-6
PAST_LEN = 2048

LANES = 128
ROW_TILE = 512
EXPERT_TILE = 512
VMEM_LIMIT = 48 * 1024 * 1024
VMEM_LIMIT_BIG = 56 * 1024 * 1024


def _params(sem, vmem=VMEM_LIMIT):
    return pltpu.CompilerParams(dimension_semantics=sem, vmem_limit_bytes=vmem)


def _rms_scale(x):
    return lax.rsqrt(jnp.mean(x * x, axis=-1, keepdims=True) + EPS)


def _bdot(a, b):
    return jnp.dot(a.astype(BF16), b.astype(BF16), preferred_element_type=F32)


def _bdot_nt(a, b):
    return lax.dot_general(a.astype(BF16), b.astype(BF16), (((1,), (1,)), ((), ())),
                           preferred_element_type=F32)


def _bdot_tn(a, b):
    return lax.dot_general(a.astype(BF16), b.astype(BF16), (((0,), (0,)), ((), ())),
                           preferred_element_type=F32)


def _split3(x):
    x1 = x.astype(BF16)
    r1 = x - x1.astype(F32)
    x2 = r1.astype(BF16)
    x3 = (r1 - x2.astype(F32)).astype(BF16)
    return x1, x2, x3


def _split2(x):
    x1 = x.astype(BF16)
    x2 = (x - x1.astype(F32)).astype(BF16)
    return x1, x2


def _dot_hi(a, b):
    a1, a2 = _split2(a)
    b1, b2 = _split2(b)
    d = functools.partial(jnp.dot, preferred_element_type=F32)
    return d(a1, b1) + (d(a1, b2) + d(a2, b1))


def _spread(x, emat, pieces):
    parts = _split3(x)[:pieces]
    out = jnp.dot(parts[0], emat, preferred_element_type=F32)
    for part in parts[1:]:
        out = out + jnp.dot(part, emat, preferred_element_type=F32)
    return out


def _head_spread_matrices():
    h = np.arange(HV)[:, None]
    full = (np.arange(HV * DV)[None, :] // DV == h)
    pair = (np.arange(HV * CHUNK)[None, :] // CHUNK == h)
    return jnp.asarray(full.astype(np.float32), dtype=BF16), jnp.asarray(pair.astype(np.float32), dtype=BF16)


def _sigmoid(x):
    return 0.5 * jnp.tanh(0.5 * x) + 0.5


def _silu(x):
    half = 0.5 * x
    return half * jnp.tanh(half) + half


CONV_SUB = 256


def _qkv_conv_kernel(x_ref, g_ref, w_ref, wc_ref, cs_ref, out_ref, cst_ref,
                     h_scr, pad_scr, carry_scr, *, tm, tn, l2):
    ti = pl.program_id(1)
    j = pl.program_id(2)

    @pl.when(j == 0)
    def _():
        x = x_ref[...]
        h_scr[...] = (x * _rms_scale(x) * g_ref[...]).astype(BF16)

    @pl.when(ti == 0)
    def _():
        carry_scr[j, 5:8, :] = cs_ref[0]

    h = h_scr[...]
    wc = wc_ref[...]
    scale = jnp.where(j == 0, DK ** -0.5, 1.0).astype(F32)
    cols = lambda k: slice(k * CONV_SUB, (k + 1) * CONV_SUB)

    def project(k):
        return jnp.dot(h, w_ref[:, cols(k)], preferred_element_type=F32)

    def finish(k, proj):
        c = cols(k)
        pad_scr[0:8, c] = carry_scr[j, :, c]
        pad_scr[8:, c] = proj
        acc = proj * wc[3:4, c]
        for s in range(1, CONV_W):
            acc = acc + pad_scr[8 - s:8 - s + tm, c] * wc[3 - s:4 - s, c]
        carry_scr[j, :, c] = pad_scr[tm:tm + 8, c]
        cst_ref[0, :, c] = pad_scr[tm + 5:tm + 8, c]
        y = _silu(acc)
        if not l2:
            out_ref[:, c] = y.astype(out_ref.dtype)
            return
        for hh in range(CONV_SUB // DK):
            seg = y[:, hh * DK:(hh + 1) * DK]
            inv = lax.rsqrt(jnp.sum(seg * seg, axis=-1, keepdims=True) + EPS) * scale
            lanes = slice(k * CONV_SUB + hh * DK, k * CONV_SUB + (hh + 1) * DK)
            out_ref[:, lanes] = (seg * inv).astype(out_ref.dtype)

    pending = project(0)
    for k in range(tn // CONV_SUB):
        upcoming = project(k + 1) if k + 1 < tn // CONV_SUB else None
        finish(k, pending)
        pending = upcoming


def _qkv_conv(x, g, w, w_conv, conv_state, B, T, l2):
    R = B * T
    n = w.shape[1]
    tm = min(2 * ROW_TILE, T)
    tn = QK_DIM
    nt = T // tm
    nj = n // tn
    kern = functools.partial(_qkv_conv_kernel, tm=tm, tn=tn, l2=l2)
    return pl.pallas_call(
        kern,
        grid=(B, nt, nj),
        in_specs=[
            pl.BlockSpec((tm, D_MODEL), lambda b, t, j: (b * nt + t, 0)),
            pl.BlockSpec((1, D_MODEL), lambda b, t, j: (0, 0)),
            pl.BlockSpec((D_MODEL, tn), lambda b, t, j: (0, j)),
            pl.BlockSpec((CONV_W, tn), lambda b, t, j: (0, j)),
            pl.BlockSpec((1, CONV_W - 1, tn), lambda b, t, j: (b, 0, j)),
        ],
        out_specs=[
            pl.BlockSpec((tm, tn), lambda b, t, j: (b * nt + t, j)),
            pl.BlockSpec((1, CONV_W - 1, tn), lambda b, t, j: (b * nt + t, 0, j)),
        ],
        out_shape=[
            jax.ShapeDtypeStruct((R, n), BF16),
            jax.ShapeDtypeStruct((B * nt, CONV_W - 1, n), F32),
        ],
        scratch_shapes=[
            pltpu.VMEM((tm, D_MODEL), BF16),
            pltpu.VMEM((tm + 8, tn), F32),
            pltpu.VMEM((nj, 8, tn), F32),
        ],
        compiler_params=_params(("arbitrary", "arbitrary", "arbitrary")),
        name="gdn_qk_conv" if l2 else "gdn_v_conv",
    )(x, g, w, w_conv, conv_state)


def _softplus(x):
    return jnp.maximum(x, 0.0) + jnp.log(1.0 + jnp.exp(-jnp.abs(x)))


def _zbg_kernel(x_ref, g_ref, w_ref, alog_ref, dtb_ref, tri_ref, z_ref, beta_ref, gc_ref, gct_ref, *, tm):
    x = x_ref[...]
    h = (x * _rms_scale(x) * g_ref[...]).astype(BF16)
    ba = jnp.dot(h, w_ref[:, V_DIM:], preferred_element_type=F32)
    z_ref[...] = jnp.dot(h, w_ref[:, :V_DIM], preferred_element_type=F32).astype(z_ref.dtype)
    beta_ref[...] = _sigmoid(ba[:, :HV])
    gcol = -jnp.exp(alog_ref[...]) * _softplus(ba[:, HV:2 * HV] + dtb_ref[...])
    tri = tri_ref[...]
    g1, g2, g3 = _split3(gcol)
    d = functools.partial(jnp.dot, preferred_element_type=F32)
    gc = d(tri, g1) + (d(tri, g2) + d(tri, g3))
    gc_ref[...] = gc
    r = lax.broadcasted_iota(jnp.int32, (HV, HV), 0)
    c = lax.broadcasted_iota(jnp.int32, (HV, HV), 1)
    sel = (c == jnp.where(r < HV // 2, 2 * r, 2 * r - (HV - 1))).astype(BF16)
    tn = lambda a: lax.dot_general(sel, a, (((1,), (1,)), ((), ())), preferred_element_type=F32)
    c1, c2, c3 = _split3(gc)
    gct = tn(c1) + (tn(c2) + tn(c3))
    for ch in range(tm // CHUNK):
        sl = slice(ch * CHUNK, (ch + 1) * CHUNK)
        gct_ref[0, ch] = jnp.concatenate([gct[:HV // 2, sl], gct[HV // 2:, sl]], axis=1)


def _zbg(x, g, w_z, w_ba, a_log, dt_bias, B, T):
    R = B * T
    tm = min(ROW_TILE, T)
    nt = T // tm
    nc = T // CHUNK
    idx = np.arange(tm)
    tri = jnp.asarray(((idx[:, None] // CHUNK == idx[None, :] // CHUNK) & (idx[:, None] >= idx[None, :]))
                      .astype(np.float32), dtype=BF16)
    w_zba = jnp.concatenate([w_z, w_ba, jnp.zeros((D_MODEL, LANES - 2 * HV), BF16)], axis=1)
    full = lambda shape: pl.BlockSpec(shape, lambda b, t: (0,) * len(shape))
    kern = functools.partial(_zbg_kernel, tm=tm)
    return pl.pallas_call(
        kern,
        grid=(B, nt),
        in_specs=[
            pl.BlockSpec((tm, D_MODEL), lambda b, t: (b * nt + t, 0)),
            full((1, D_MODEL)),
            full((D_MODEL, V_DIM + LANES)),
            full((1, HV)), full((1, HV)),
            full((tm, tm)),
        ],
        out_specs=[
            pl.BlockSpec((tm, V_DIM), lambda b, t: (b * nt + t, 0)),
            pl.BlockSpec((tm, HV), lambda b, t: (b * nt + t, 0)),
            pl.BlockSpec((tm, HV), lambda b, t: (b * nt + t, 0)),
            pl.BlockSpec((1, tm // CHUNK, HV // 2, 2 * CHUNK), lambda b, t: (b, t, 0, 0)),
        ],
        out_shape=[
            jax.ShapeDtypeStruct((R, V_DIM), BF16),
            jax.ShapeDtypeStruct((R, HV), F32),
            jax.ShapeDtypeStruct((R, HV), F32),
            jax.ShapeDtypeStruct((B, nc, HV // 2, 2 * CHUNK), F32),
        ],
        compiler_params=_params(("arbitrary", "arbitrary")),
        name="gdn_z_beta_decay",
    )(x, g, w_zba, a_log.reshape(1, HV), dt_bias.reshape(1, HV), tri)


PAIR = HV // HK


def _pair_rows(x, lo):
    zero = jnp.zeros((), x.dtype)
    return jnp.concatenate([jnp.where(lo, x, zero), jnp.where(lo, zero, x)], axis=0)


def _pair_products(lhs_list, rhs_list, lo):
    def split(a):
        hi = a.astype(BF16)
        return hi, (a - hi.astype(F32)).astype(BF16)

    ops = []
    for lhs, rhs in zip(lhs_list, rhs_list):
        lh, ll = split(lhs)
        rh, rl = split(rhs)
        rh2 = _pair_rows(rh, lo)
        ops.append((jnp.concatenate([lh, lh, ll], axis=1), jnp.concatenate([rh2, _pair_rows(rl, lo), rh2], axis=0)))
    return [jnp.dot(a, b, preferred_element_type=F32) for a, b in ops]


def _pair_products_bf16(lhs_list, rhs_list, lo):
    ops = [(lhs.astype(BF16), _pair_rows(rhs.astype(BF16), lo)) for lhs, rhs in zip(lhs_list, rhs_list)]
    return [jnp.dot(a, b, preferred_element_type=F32) for a, b in ops]


INV_BASE = 16


def _pair_unit_lower_inverse(low_list, lo, eye2, same_block):
    C = low_list[0].shape[0]
    base = same_block(INV_BASE)
    xs = [jnp.where(base, -low, 0.0) for low in low_list]
    accs = [eye2 + x for x in xs]
    pws = _pair_products(xs, xs, lo)
    n = 2
    while 2 * n < INV_BASE:
        prods = _pair_products([jnp.concatenate([a, p], axis=0) for a, p in zip(accs, pws)], pws, lo)
        accs = [a + pr[:C] for a, pr in zip(accs, prods)]
        pws = [pr[C:] for pr in prods]
        n *= 2
    prods = _pair_products(accs, pws, lo)
    tinvs = [a + pr for a, pr in zip(accs, prods)]
    size = 2 * INV_BASE
    while size <= C:
        below = same_block(size) & jnp.logical_not(same_block(size // 2))
        offs = [jnp.where(below, low, 0.0) for low in low_list]
        inner = _pair_products_bf16(offs, tinvs, lo)
        outer = _pair_products_bf16(tinvs, inner, lo)
        tinvs = [t - o for t, o in zip(tinvs, outer)]
        size *= 2
    return tinvs


def _delta_prep_kernel(q_ref, k_ref, v_ref, beta_ref, gc_ref, gct_ref, epair_ref, u_ref, w_ref, a_ref, *, cb):
    C = CHUNK
    ri = lax.broadcasted_iota(jnp.int32, (C, 2 * C), 0)
    li = lax.broadcasted_iota(jnp.int32, (C, 2 * C), 1)
    lm = li & (C - 1)
    lo = li < C
    incl = ri >= lm
    strict = ri > lm
    eye2 = (ri == lm).astype(F32)
    units = [(c, p) for c in range(cb) for p in range(HK)]
    rows = lambda c: slice(c * C, (c + 1) * C)

    grams = []
    for c, p in units:
        kb = k_ref[rows(c), p * DK:(p + 1) * DK]
        qb = q_ref[rows(c), p * DK:(p + 1) * DK]
        grams.append(lax.dot_general(jnp.concatenate([kb, qb], axis=0), jnp.concatenate([kb, kb], axis=0),
                                     (((1,), (1,)), ((), ())), preferred_element_type=F32))

    epair = epair_ref[...]
    g2s, b2s, betas, begs = [], [], [], []
    for c in range(cb):
        beta = beta_ref[rows(c), :]
        gc = gc_ref[rows(c), :]
        g2s.append(_spread(gc, epair, 3))
        b2s.append(_spread(beta, epair, 2))
        betas.append(beta)
        begs.append(beta * jnp.exp(gc))

    lows = []
    for (c, p), gram in zip(units, grams):
        pt = slice(p * 2 * C, (p + 1) * 2 * C)
        grow2 = gct_ref[c, p:p + 1, :]
        decay2 = jnp.exp(jnp.where(incl, g2s[c][:, pt] - grow2, -jnp.inf))
        a_ref[rows(c), pt] = (gram[C:] * decay2).astype(a_ref.dtype)
        lows.append(jnp.where(strict, b2s[c][:, pt] * gram[:C] * decay2, 0.0))

    same_block = lambda s: (ri // s) == (lm // s)
    tinvs = _pair_unit_lower_inverse(lows, lo, eye2, same_block)

    ops = []
    for (c, p), tinv in zip(units, tinvs):
        kf = k_ref[rows(c), p * DK:(p + 1) * DK].astype(F32)
        rhs = []
        for h in (PAIR * p, PAIR * p + 1):
            ht = slice(h * DV, (h + 1) * DV)
            vf = v_ref[rows(c), ht].astype(F32)
            rhs.append(jnp.concatenate([vf * betas[c][:, h:h + 1], kf * begs[c][:, h:h + 1]], axis=1))
        ops.append((_pair_rows(tinv.astype(BF16), lo), jnp.concatenate(rhs, axis=0).astype(BF16)))
    for (c, p), (lhs, rhs) in zip(units, ops):
        uw = jnp.dot(lhs, rhs, preferred_element_type=F32)
        for i, h in enumerate((PAIR * p, PAIR * p + 1)):
            u_ref[rows(c), h * DV:(h + 1) * DV] = uw[i * C:(i + 1) * C, :DV].astype(u_ref.dtype)
            w_ref[rows(c), h * DV:(h + 1) * DV] = uw[i * C:(i + 1) * C, DV:].astype(w_ref.dtype)


def _delta_prep(qk, v, beta, gc, gct, B, T):
    R = B * T
    cb = 4 if T % (4 * CHUNK) == 0 else (2 if T % (2 * CHUNK) == 0 else 1)
    rb = cb * CHUNK
    gct2 = gct.reshape(R // CHUNK, HK, 2 * CHUNK)
    kern = functools.partial(_delta_prep_kernel, cb=cb)
    return pl.pallas_call(
        kern,
        grid=(R // rb,),
        in_specs=[
            pl.BlockSpec((rb, QK_DIM), lambda i: (i, 0)),
            pl.BlockSpec((rb, QK_DIM), lambda i: (i, 1)),
            pl.BlockSpec((rb, V_DIM), lambda i: (i, 0)),
            pl.BlockSpec((rb, HV), lambda i: (i, 0)),
            pl.BlockSpec((rb, HV), lambda i: (i, 0)),
            pl.BlockSpec((cb, HK, 2 * CHUNK), lambda i: (i, 0, 0)),
            pl.BlockSpec((HV, HV * CHUNK), lambda i: (0, 0)),
        ],
        out_specs=[
            pl.BlockSpec((rb, V_DIM), lambda i: (i, 0)),
            pl.BlockSpec((rb, V_DIM), lambda i: (i, 0)),
            pl.BlockSpec((rb, HV * CHUNK), lambda i: (i, 0)),
        ],
        out_shape=[
            jax.ShapeDtypeStruct((R, V_DIM), BF16),
            jax.ShapeDtypeStruct((R, V_DIM), BF16),
            jax.ShapeDtypeStruct((R, HV * CHUNK), BF16),
        ],
        compiler_params=_params(("arbitrary",)),
        name="gdn_delta_prep",
    )(qk, qk, v, beta, gc, gct2, _head_spread_matrices()[1])


def _delta_scan_kernel(q_ref, k_ref, u_ref, w_ref, a_ref, z_ref, gc_ref, s0_ref, on_ref, efull_ref,
                       og_ref, s_ref, *, bb):
    c = pl.program_id(1)

    @pl.when(c == 0)
    def _():
        s_ref[...] = s0_ref[...]

    C = CHUNK
    lo = lax.broadcasted_iota(jnp.int32, (C, 2 * C), 1) < C
    onorm = on_ref[...]
    units = [(b, h) for b in range(bb) for h in range(HV)]
    pairs = [(b, p) for b in range(bb) for p in range(HK)]

    efull = efull_ref[...]
    egs, kgs, dgs = [], [], []
    for b in range(bb):
        gc = gc_ref[b]
        g_last = gc[C - 1:C, :]
        egs.append(_spread(jnp.exp(gc), efull, 2))
        kgs.append(_spread(jnp.exp(g_last - gc), efull, 2))
        dgs.append(_spread(jnp.exp(gc[C - 8:C, :]), efull, 3)[7:8, :])

    ops = []
    kds = []
    dec = []
    for b, h in units:
        p = h // PAIR
        ht = slice(h * DV, (h + 1) * DV)
        qg = (q_ref[b, :, p * DK:(p + 1) * DK].astype(F32) * egs[b][:, ht]).astype(BF16)
        kds.append((k_ref[b, :, p * DK:(p + 1) * DK].astype(F32) * kgs[b][:, ht]).astype(BF16))
        dec.append(dgs[b][:, ht])
        ops.append((jnp.concatenate([w_ref[b, :, ht], qg], axis=0), s_ref[b, h].astype(BF16)))
    wqs = [jnp.dot(a, s, preferred_element_type=F32) for a, s in ops]

    vns = [(u_ref[b, :, h * DV:(h + 1) * DV].astype(F32) - wq[:C]).astype(BF16) for (b, h), wq in zip(units, wqs)]

    oparts = []
    for i, (b, p) in enumerate(pairs):
        att = _pair_rows(a_ref[b, :, p * 2 * C:(p + 1) * 2 * C], lo)
        vn2 = jnp.concatenate([vns[PAIR * i], vns[PAIR * i + 1]], axis=0)
        oparts.append(jnp.dot(att, vn2, preferred_element_type=F32))
    sds = [lax.dot_general(kd, vn, (((0,), (0,)), ((), ())), preferred_element_type=F32)
           for kd, vn in zip(kds, vns)]

    for i, (b, h) in enumerate(units):
        s_ref[b, h] = s_ref[b, h] * dec[i] + sds[i]
        o = wqs[i][C:] + oparts[i // PAIR][(h % PAIR) * C:(h % PAIR + 1) * C]
        on = o * _rms_scale(o) * onorm
        zf = z_ref[b, :, h * DV:(h + 1) * DV].astype(F32)
        og_ref[b, :, h * DV:(h + 1) * DV] = (on * _silu(zf)).astype(og_ref.dtype)


def _delta_scan(qk, u, w, att, z, gc, s0, o_norm, B, T):
    nc = T // CHUNK
    bb = 4 if B % 4 == 0 else (2 if B % 2 == 0 else 1)
    v3 = lambda a: a.reshape(B, T, a.shape[-1])
    blk = lambda n, j=0: pl.BlockSpec((bb, CHUNK, n), lambda b, c: (b, c, j))
    kern = functools.partial(_delta_scan_kernel, bb=bb)
    og, s_new = pl.pallas_call(
        kern,
        grid=(B // bb, nc),
        in_specs=[
            blk(QK_DIM, 0), blk(QK_DIM, 1), blk(V_DIM), blk(V_DIM), blk(HV * CHUNK), blk(V_DIM), blk(HV),
            pl.BlockSpec((bb, HV, DK, DV), lambda b, c: (b, 0, 0, 0)),
            pl.BlockSpec((1, DV), lambda b, c: (0, 0)),
            pl.BlockSpec((HV, HV * DV), lambda b, c: (0, 0)),
        ],
        out_specs=[
            blk(V_DIM),
            pl.BlockSpec((bb, HV, DK, DV), lambda b, c: (b, 0, 0, 0)),
        ],
        out_shape=[
            jax.ShapeDtypeStruct((B, T, V_DIM), BF16),
            jax.ShapeDtypeStruct((B, HV, DK, DV), F32),
        ],
        compiler_params=_params(("arbitrary", "arbitrary")),
        name="gdn_delta_scan",
    )(v3(qk), v3(qk), v3(u), v3(w), v3(att), v3(z), v3(gc), s0, o_norm, _head_spread_matrices()[0])
    return og.reshape(B * T, V_DIM), s_new


MXU_COLS = 256
FF_SPLITS = (6 * MXU_COLS, 5 * MXU_COLS)
assert sum(FF_SPLITS) == D_FF
RESIDENT = pl.Buffered(1)


def _swiglu(h, w1_ref, w3_ref, w2_ref, lead=()):
    out = None
    start = 0
    for width in FF_SPLITS:
        sl = slice(start, start + width)
        start += width
        a = jnp.dot(h, w1_ref[lead + (slice(None), sl)], preferred_element_type=F32)
        b = jnp.dot(h, w3_ref[lead + (slice(None), sl)], preferred_element_type=F32)
        mid = (_silu(a) * b).astype(BF16)
        part = jnp.dot(mid, w2_ref[lead + (sl, slice(None))], preferred_element_type=F32)
        out = part if out is None else out + part
    return out


def _mixer_out_ffn_kernel(x_ref, a_ref, wo_ref, g_ref, w1_ref, w3_ref, w2_ref, o_ref):
    x1 = x_ref[...] + jnp.dot(a_ref[...], wo_ref[...], preferred_element_type=F32)
    h = (x1 * _rms_scale(x1) * g_ref[...]).astype(BF16)
    o_ref[...] = x1 + _swiglu(h, w1_ref, w3_ref, w2_ref)


def _mixer_out_ffn(x, a, w_out, g, w1, w3, w2):
    R = x.shape[0]
    K = a.shape[1]
    tm = min(ROW_TILE, R)
    held = lambda shape: pl.BlockSpec(shape, lambda i: (0,) * len(shape), pipeline_mode=RESIDENT)
    return pl.pallas_call(
        _mixer_out_ffn_kernel,
        grid=(R // tm,),
        in_specs=[
            pl.BlockSpec((tm, D_MODEL), lambda i: (i, 0)),
            pl.BlockSpec((tm, K), lambda i: (i, 0)),
            held((K, D_MODEL)),
            held((1, D_MODEL)),
            held((D_MODEL, D_FF)), held((D_MODEL, D_FF)), held((D_FF, D_MODEL)),
        ],
        out_specs=pl.BlockSpec((tm, D_MODEL), lambda i: (i, 0)),
        out_shape=jax.ShapeDtypeStruct((R, D_MODEL), F32),
        compiler_params=_params(("arbitrary",), VMEM_LIMIT_BIG),
        name="gdn_out_dense_swiglu",
    )(x, a, w_out, g, w1, w3, w2)


KV_DUP = 2 * HKV * LANES
KV_STD = 2 * HKV * HD


def _rope(x, cos, sm, sp):
    up = pltpu.roll(x, LANES - ROT_DIM // 2, 1)
    dn = pltpu.roll(x, ROT_DIM // 2, 1)
    return x * cos + up * sm + dn * sp


def _qkv_rope_kernel(x_ref, gq_ref, gkv_ref, wq_ref, wkv_ref, wkvs_ref, cos_ref, sm_ref, sp_ref,
                     q_ref, kvd_ref, kvs_ref, *, nt, keep):
    x = x_ref[...]
    xr = x * _rms_scale(x)
    hq = (xr * gq_ref[...]).astype(BF16)
    hkv = (xr * gkv_ref[...]).astype(BF16)
    cos = cos_ref[...]
    sm = sm_ref[...]
    sp = sp_ref[...]
    q = jnp.dot(hq, wq_ref[...], preferred_element_type=F32)
    for t in range(HQ * HD // LANES):
        sl = slice(t * LANES, (t + 1) * LANES)
        q_ref[:, sl] = (_rope(q[:, sl], cos, sm, sp) * (HD ** -0.5)).astype(q_ref.dtype)
    kv = jnp.dot(hkv, wkv_ref[...], preferred_element_type=F32)
    half = KV_DUP // 2
    for t in range(HKV):
        sl = slice(t * LANES, (t + 1) * LANES)
        kvd_ref[:, sl] = _rope(kv[:, sl], cos, sm, sp).astype(kvd_ref.dtype)
    kvd_ref[:, half:] = kv[:, half:].astype(kvd_ref.dtype)

    @pl.when(pl.program_id(0) % nt == nt - 1)
    def _():
        tm = x.shape[0]
        rows = slice(tm - keep, tm)
        kvs = jnp.dot(hkv[rows], wkvs_ref[...], preferred_element_type=F32)
        for t in range(HKV * HD // LANES):
            sl = slice(t * LANES, (t + 1) * LANES)
            kvs_ref[:, sl] = _rope(kvs[:, sl], cos[rows], sm[rows], sp[rows])
        kvs_ref[:, HKV * HD:] = kvs[:, HKV * HD:]


def _rope_tables(pos):
    half = ROT_DIM // 2
    inv = jnp.power(ROPE_THETA, -jnp.arange(half, dtype=F32) * 2.0 / ROT_DIM)
    ang = pos.astype(F32)[:, None] * inv[None, :]
    cos = jnp.cos(ang)
    sin = jnp.sin(ang)
    T = pos.shape[0]
    one = jnp.ones((T, HD - ROT_DIM), F32)
    zero = jnp.zeros((T, HD - ROT_DIM), F32)
    zh = jnp.zeros((T, half), F32)
    c64 = jnp.concatenate([cos, cos, one], axis=1)
    sm64 = jnp.concatenate([-sin, zh, zero], axis=1)
    sp64 = jnp.concatenate([zh, sin, zero], axis=1)
    dup = lambda a: jnp.concatenate([a, a], axis=1)
    return dup(c64), dup(sm64), dup(sp64)


def _qkv_rope(x, gq, gkv, w_q, w_kvd, w_kvs, tables, B, T, keep):
    R = B * T
    tm = min(2 * ROW_TILE, T)
    nt = T // tm
    assert keep <= tm
    tab = pl.BlockSpec((tm, LANES), lambda i: (i % nt, 0))
    kern = functools.partial(_qkv_rope_kernel, nt=nt, keep=keep)
    return pl.pallas_call(
        kern,
        grid=(R // tm,),
        in_specs=[
            pl.BlockSpec((tm, D_MODEL), lambda i: (i, 0)),
            pl.BlockSpec((1, D_MODEL), lambda i: (0, 0)),
            pl.BlockSpec((1, D_MODEL), lambda i: (0, 0)),
            pl.BlockSpec((D_MODEL, HQ * HD), lambda i: (0, 0)),
            pl.BlockSpec((D_MODEL, KV_DUP), lambda i: (0, 0)),
            pl.BlockSpec((D_MODEL, KV_STD), lambda i: (0, 0)),
            tab, tab, tab,
        ],
        out_specs=[
            pl.BlockSpec((tm, HQ * HD), lambda i: (i, 0)),
            pl.BlockSpec((tm, KV_DUP), lambda i: (i, 0)),
            pl.BlockSpec((keep, KV_STD), lambda i: (i // nt, 0)),
        ],
        out_shape=[
            jax.ShapeDtypeStruct((R, HQ * HD), BF16),
            jax.ShapeDtypeStruct((R, KV_DUP), BF16),
            jax.ShapeDtypeStruct((B * keep, KV_STD), F32),
        ],
        compiler_params=_params(("arbitrary",)),
        name="swa_qkv_rope",
    )(x, gq, gkv, w_q, w_kvd, w_kvs, *tables)


KEY_CHUNKS = WINDOW // CHUNK + 1
KEY_PAD = 256


def _attn_kernel(*refs, off, cq):
    q_ref = refs[0]
    krefs = refs[1:KEY_CHUNKS + cq]
    sink_ref, ones_ref, o_ref = refs[KEY_CHUNKS + cq:]
    c = pl.program_id(1)
    lane = lax.broadcasted_iota(jnp.int32, (KEY_CHUNKS * CHUNK, LANES), 1)
    lo = lane < HD
    col = lax.broadcasted_iota(jnp.int32, (1, KEY_PAD), 1)
    sink_col = col == KEY_CHUNKS * CHUNK
    zpad = jnp.zeros((KEY_PAD - KEY_CHUNKS * CHUNK, LANES), BF16)
    zero = jnp.zeros((), BF16)
    tiles = HQ // HKV // 2
    ones_blk = ones_ref[...]
    units = [(u, g) for u in range(cq) for g in range(HKV)]

    vblks = []
    scores = []
    for u, g in units:
        kr = krefs[u:u + KEY_CHUNKS]
        kg = jnp.concatenate([r[:, g * LANES:(g + 1) * LANES] for r in kr], axis=0)
        vg = jnp.concatenate([r[:, KV_DUP // 2 + g * LANES:KV_DUP // 2 + (g + 1) * LANES] for r in kr], axis=0)
        kblk = jnp.concatenate([jnp.where(lo, kg, zero), zpad, jnp.where(lo, zero, kg), zpad], axis=0)
        vblk = jnp.concatenate([jnp.where(lo, vg, zero), zpad, jnp.where(lo, zero, vg), zpad], axis=0)
        vblks.append(jnp.concatenate([vblk, ones_blk], axis=1))
        qg = jnp.concatenate([q_ref[u * CHUNK:(u + 1) * CHUNK, (g * tiles + p) * LANES:(g * tiles + p + 1) * LANES]
                              for p in range(tiles)], axis=0)
        scores.append(lax.dot_general(qg, kblk, (((1,), (1,)), ((), ())), preferred_element_type=F32))

    probs = []
    for (u, g), sc in zip(units, scores):
        first = c * cq + u + off - (KEY_CHUNKS - 1)
        valid = (col < KEY_CHUNKS * CHUNK) & (col // CHUNK + first >= 0)
        base_bias = jnp.where(valid, 0.0, -jnp.inf).astype(F32)
        rows = []
        for p in range(tiles):
            t = g * tiles + p
            ps = []
            for hh in range(2):
                bias = jnp.where(sink_col, sink_ref[:, 2 * t + hh:2 * t + hh + 1], base_bias)
                sh = sc[p * CHUNK:(p + 1) * CHUNK, hh * KEY_PAD:(hh + 1) * KEY_PAD] + bias
                m = jnp.max(sh, axis=-1, keepdims=True)
                ps.append(jnp.exp(sh - m).astype(BF16))
            rows.append(jnp.concatenate(ps, axis=1))
        probs.append(jnp.concatenate(rows, axis=0))

    outs = [jnp.dot(pr, vb, preferred_element_type=F32) for pr, vb in zip(probs, vblks)]
    for (u, g), out in zip(units, outs):
        for p in range(tiles):
            t = g * tiles + p
            rows = slice(p * CHUNK, (p + 1) * CHUNK)
            o_ref[u * CHUNK:(u + 1) * CHUNK, t * LANES:(t + 1) * LANES] = (
                out[rows, :LANES] / out[rows, LANES:]).astype(o_ref.dtype)


def _attention(q, kvd, sinks, B, Tq, Tk):
    ncq = Tq // CHUNK
    nck = Tk // CHUNK
    off = nck - ncq
    cq = 4 if ncq % 4 == 0 else (2 if ncq % 2 == 0 else 1)
    kern = functools.partial(_attn_kernel, off=off, cq=cq)
    blk_row = np.arange(2 * KEY_PAD)[:, None] // KEY_PAD
    blk_lane = np.arange(LANES)[None, :] // HD
    ones_blk = jnp.asarray((blk_row == blk_lane).astype(np.float32), dtype=BF16)

    def kspec(i):
        return pl.BlockSpec((CHUNK, KV_DUP),
                            lambda b, c: (b * nck + jnp.maximum(c * cq + off - (KEY_CHUNKS - 1) + i, 0), 0))

    qspec = pl.BlockSpec((cq * CHUNK, HQ * HD), lambda b, c: (b * (ncq // cq) + c, 0))
    return pl.pallas_call(
        kern,
        grid=(B, ncq // cq),
        in_specs=[qspec] + [kspec(i) for i in range(KEY_CHUNKS + cq - 1)] + [
            pl.BlockSpec((1, HQ), lambda b, c: (0, 0)),
            pl.BlockSpec((2 * KEY_PAD, LANES), lambda b, c: (0, 0)),
        ],
        out_specs=qspec,
        out_shape=jax.ShapeDtypeStruct((B * Tq, HQ * HD), BF16),
        compiler_params=_params(("arbitrary", "arbitrary")),
        name="swa_attention",
    )(q, *([kvd] * (KEY_CHUNKS + cq - 1)), sinks, ones_blk)


def _oproj_router_kernel(x_ref, a_ref, w_ref, g_ref, wr_ref, x3_ref, h_ref, e_ref, gate_ref, cnt_ref):
    i = pl.program_id(0)
    x3 = x_ref[...] + jnp.dot(a_ref[...], w_ref[...], preferred_element_type=F32)
    x3_ref[...] = x3
    h = x3 * _rms_scale(x3) * g_ref[...]
    h_ref[...] = h
    h1, h2 = _split2(h)
    r1, r2 = _split2(wr_ref[...])
    part = jnp.dot(h1, jnp.concatenate([r1, r2], axis=1), preferred_element_type=F32)
    logits = part[:, :N_EXPERTS] + (part[:, N_EXPERTS:] + jnp.dot(h2, r1, preferred_element_type=F32))
    eidx = lax.broadcasted_iota(jnp.int32, logits.shape, 1)
    m1 = jnp.max(logits, axis=-1, keepdims=True)
    i1 = jnp.min(jnp.where(logits == m1, eidx, N_EXPERTS), axis=-1, keepdims=True)
    rest = jnp.where(eidx == i1, -jnp.inf, logits)
    m2 = jnp.max(rest, axis=-1, keepdims=True)
    i2 = jnp.min(jnp.where(rest == m2, eidx, N_EXPERTS), axis=-1, keepdims=True)
    e2 = jnp.exp(m2 - m1)
    den = 1.0 + e2
    e_ref[...] = jnp.concatenate([i1, i2], axis=1)
    gate_ref[...] = jnp.concatenate([1.0 / den, e2 / den], axis=1)
    hot = ((eidx == i1) | (eidx == i2)).astype(F32)

    @pl.when(i == 0)
    def _():
        cnt_ref[...] = jnp.zeros_like(cnt_ref)

    cnt_ref[...] += jnp.sum(hot, axis=0, keepdims=True)


def _oproj_router(x, a, w_o, g, w_r):
    R = x.shape[0]
    tm = min(ROW_TILE, R)
    row = lambda n: pl.BlockSpec((tm, n), lambda i: (i, 0))
    return pl.pallas_call(
        _oproj_router_kernel,
        grid=(R // tm,),
        in_specs=[
            row(D_MODEL), row(HQ * HD),
            pl.BlockSpec((HQ * HD, D_MODEL), lambda i: (0, 0)),
            pl.BlockSpec((1, D_MODEL), lambda i: (0, 0)),
            pl.BlockSpec((D_MODEL, N_EXPERTS), lambda i: (0, 0)),
        ],
        out_specs=[row(D_MODEL), row(D_MODEL), row(TOP_K), row(TOP_K),
                   pl.BlockSpec((1, N_EXPERTS), lambda i: (0, 0))],
        out_shape=[
            jax.ShapeDtypeStruct((R, D_MODEL), F32),
            jax.ShapeDtypeStruct((R, D_MODEL), F32),
            jax.ShapeDtypeStruct((R, TOP_K), jnp.int32),
            jax.ShapeDtypeStruct((R, TOP_K), F32),
            jax.ShapeDtypeStruct((1, N_EXPERTS), F32),
        ],
        compiler_params=_params(("arbitrary",)),
        name="swa_oproj_router",
    )(x, a, w_o, g, w_r)


def _slot_kernel(e_ref, base_ref, tri_ref, pos_ref, run_scr):
    i = pl.program_id(0)

    @pl.when(i == 0)
    def _():
        run_scr[...] = jnp.zeros_like(run_scr)

    e = e_ref[...]
    tm = e.shape[0]
    eidx = lax.broadcasted_iota(jnp.int32, (tm, N_EXPERTS), 1)
    hot0 = eidx == e[:, 0:1]
    hot1 = eidx == e[:, 1:2]
    hot = (hot0 | hot1).astype(BF16)
    before = jnp.dot(tri_ref[...], hot, preferred_element_type=F32)
    dest = before + run_scr[...] + base_ref[...]
    p0 = jnp.sum(jnp.where(hot0, dest, 0.0), axis=-1, keepdims=True)
    p1 = jnp.sum(jnp.where(hot1, dest, 0.0), axis=-1, keepdims=True)
    pos_ref[...] = jnp.concatenate([p0, p1], axis=1).astype(jnp.int32)
    run_scr[...] += jnp.sum(hot.astype(F32), axis=0, keepdims=True)


def _slots(eidx, base):
    R = eidx.shape[0]
    tm = min(ROW_TILE, R)
    idx = np.arange(tm)
    tri = jnp.asarray((idx[:, None] > idx[None, :]).astype(np.float32), dtype=BF16)
    return pl.pallas_call(
        _slot_kernel,
        grid=(R // tm,),
        in_specs=[
            pl.BlockSpec((tm, TOP_K), lambda i: (i, 0)),
            pl.BlockSpec((1, N_EXPERTS), lambda i: (0, 0)),
            pl.BlockSpec((tm, tm), lambda i: (0, 0)),
        ],
        out_specs=pl.BlockSpec((tm, TOP_K), lambda i: (i, 0)),
        out_shape=jax.ShapeDtypeStruct((R, TOP_K), jnp.int32),
        scratch_shapes=[pltpu.VMEM((1, N_EXPERTS), F32)],
        compiler_params=_params(("arbitrary",)),
        name="moe_slots",
    )(eidx, base, tri)


DMA_UNROLL = 8


def _row_copy(src, i, dst, j, sem):
    return pltpu.make_async_copy(src.at[pl.ds(i, 1)], dst.at[pl.ds(j, 1)], sem)


def _dispatch_kernel(pos_ref, cnt_ref, base_ref, nused_ref, h_ref, xs_ref, zero_scr, sem, zsem, *, tm, te, n_tiles):
    i = pl.program_id(0)

    @pl.when(i == 0)
    def _():
        zero_scr[...] = jnp.zeros_like(zero_scr)

        def tile_copy(t):
            dst = xs_ref.at[pl.ds(pl.multiple_of(t * te, te), te)]
            return pltpu.make_async_copy(zero_scr, dst, zsem)

        def issue_tile(t, carry):
            tile_copy(t).start()
            return carry

        def drain_tile(t, carry):
            tile_copy(t).wait()
            return carry

        lax.fori_loop(nused_ref[0], n_tiles, issue_tile, 0)
        lax.fori_loop(nused_ref[0], n_tiles, drain_tile, 0)
        for e in range(N_EXPERTS):
            n = cnt_ref[e]
            start = base_ref[e] + n
            npad = (te - n % te) % te

            def issue(r, carry):
                _row_copy(zero_scr, 0, xs_ref, start + r, zsem).start()
                return carry

            def drain(r, carry):
                _row_copy(zero_scr, 0, xs_ref, start + r, zsem).wait()
                return carry

            lax.fori_loop(0, npad, issue, 0)
            lax.fori_loop(0, npad, drain, 0)

    def issue(r, carry):
        _row_copy(h_ref, r, xs_ref, pos_ref[2 * r], sem).start()
        _row_copy(h_ref, r, xs_ref, pos_ref[2 * r + 1], sem).start()
        return carry

    lax.fori_loop(0, tm, issue, 0, unroll=DMA_UNROLL)
    for _ in range(TOP_K):
        pltpu.make_async_copy(h_ref, xs_ref.at[pl.ds(0, tm)], sem).wait()


def _dispatch(h, pos_flat, counts, base, n_used, n_tiles, te):
    R = h.shape[0]
    tm = min(ROW_TILE, R)
    kern = functools.partial(_dispatch_kernel, tm=tm, te=te, n_tiles=n_tiles)
    smem = pl.BlockSpec(memory_space=pltpu.SMEM)
    return pl.pallas_call(
        kern,
        grid=(R // tm,),
        in_specs=[
            pl.BlockSpec((TOP_K * tm,), lambda i: (i,), memory_space=pltpu.SMEM),
            smem, smem, smem,
            pl.BlockSpec((tm, D_MODEL), lambda i: (i, 0)),
        ],
        out_specs=pl.BlockSpec(memory_space=pl.ANY),
        out_shape=jax.ShapeDtypeStruct((n_tiles * te, D_MODEL), F32),
        scratch_shapes=[pltpu.VMEM((te, D_MODEL), F32), pltpu.SemaphoreType.DMA(()), pltpu.SemaphoreType.DMA(())],
        compiler_params=pltpu.CompilerParams(dimension_semantics=("arbitrary",), vmem_limit_bytes=VMEM_LIMIT,
                                             has_side_effects=True),
        name="moe_dispatch",
    )(pos_flat, counts, base, n_used, h)


def _expert_ffn_kernel(te_ref, nused_ref, x_ref, w1_ref, w3_ref, w2_ref, y_ref):
    i = pl.program_id(0)

    @pl.when(i < nused_ref[0])
    def _():
        y_ref[...] = _swiglu(x_ref[...].astype(BF16), w1_ref, w3_ref, w2_ref, lead=(0,))

    @pl.when(i >= nused_ref[0])
    def _():
        y_ref[...] = jnp.zeros_like(y_ref)


def _expert_ffn(xs, tile_expert, n_used, w1, w3, w2, te):
    n_rows = xs.shape[0]
    nt = n_rows // te
    wspec = lambda shape: pl.BlockSpec((1,) + shape, lambda i, te_r, nu_r: (te_r[i], 0, 0), pipeline_mode=RESIDENT)
    grid_spec = pltpu.PrefetchScalarGridSpec(
        num_scalar_prefetch=2,
        grid=(nt,),
        in_specs=[
            pl.BlockSpec((te, D_MODEL), lambda i, te_r, nu_r: (i, 0)),
            wspec((D_MODEL, D_FF)), wspec((D_MODEL, D_FF)), wspec((D_FF, D_MODEL)),
        ],
        out_specs=pl.BlockSpec((te, D_MODEL), lambda i, te_r, nu_r: (i, 0)),
    )
    return pl.pallas_call(
        _expert_ffn_kernel,
        grid_spec=grid_spec,
        out_shape=jax.ShapeDtypeStruct((n_rows, D_MODEL), F32),
        compiler_params=_params(("arbitrary",), VMEM_LIMIT_BIG),
        name="moe_expert_swiglu",
    )(tile_expert, n_used, xs, w1, w3, w2)


def _combine_kernel(pos_ref, x_ref, gate_ref, g_ref, y_ref, o_ref, buf0, buf1, sem, *, tm):
    def issue(r, carry):
        _row_copy(y_ref, pos_ref[2 * r], buf0, r, sem).start()
        _row_copy(y_ref, pos_ref[2 * r + 1], buf1, r, sem).start()
        return carry

    lax.fori_loop(0, tm, issue, 0, unroll=DMA_UNROLL)
    for buf in (buf0, buf1):
        pltpu.make_async_copy(y_ref.at[pl.ds(0, tm)], buf, sem).wait()
    gate = gate_ref[...]
    x = x_ref[...] + (buf0[...] * gate[:, 0:1] + buf1[...] * gate[:, 1:2])
    o_ref[...] = x * _rms_scale(x) * g_ref[...]


def _combine(x3, gates, pos_flat, y, g):
    R = x3.shape[0]
    tm = min(ROW_TILE, R)
    kern = functools.partial(_combine_kernel, tm=tm)
    return pl.pallas_call(
        kern,
        grid=(R // tm,),
        in_specs=[
            pl.BlockSpec((TOP_K * tm,), lambda i: (i,), memory_space=pltpu.SMEM),
            pl.BlockSpec((tm, D_MODEL), lambda i: (i, 0)),
            pl.BlockSpec((tm, TOP_K), lambda i: (i, 0)),
            pl.BlockSpec((1, D_MODEL), lambda i: (0, 0)),
            pl.BlockSpec(memory_space=pl.ANY),
        ],
        out_specs=pl.BlockSpec((tm, D_MODEL), lambda i: (i, 0)),
        out_shape=jax.ShapeDtypeStruct((R, D_MODEL), F32),
        scratch_shapes=[pltpu.VMEM((tm, D_MODEL), F32), pltpu.VMEM((tm, D_MODEL), F32),
                        pltpu.SemaphoreType.DMA(())],
        compiler_params=_params(("arbitrary",)),
        name="moe_combine_norm",
    )(pos_flat, x3, gates, g, y)


def _prep_weights(p):
    w_in = p["gdn_w_in"][0]
    w_kv = p["w_kv"]
    kcols = w_kv[:, :HKV * HD].reshape(D_MODEL, HKV, HD)
    vcols = w_kv[:, HKV * HD:].reshape(D_MODEL, HKV, HD)
    dup = lambda a: jnp.concatenate([a, a], axis=2).reshape(D_MODEL, HKV * LANES)
    w_kvd = jnp.concatenate([dup(kcols), dup(vcols)], axis=1)
    return dict(
        w_qk=w_in[:, :2 * QK_DIM].astype(BF16),
        w_v=w_in[:, 2 * QK_DIM:CONV_CH].astype(BF16),
        w_z=w_in[:, CONV_CH:CONV_CH + V_DIM].astype(BF16),
        w_ba=w_in[:, CONV_CH + V_DIM:].astype(BF16),
        w_out=p["gdn_w_out"][0].astype(BF16),
        ffn_w1=p["ffn_w1"][0].astype(BF16),
        ffn_w3=p["ffn_w3"][0].astype(BF16),
        ffn_w2=p["ffn_w2"][0].astype(BF16),
        w_kvd=w_kvd.astype(BF16),
        w_kvs=w_kv.astype(BF16),
        w_q=p["swa_w_q"][0].astype(BF16),
        w_o=p["swa_w_o"][0].astype(BF16),
        moe_w1=p["moe_w1"][0].astype(BF16),
        moe_w3=p["moe_w3"][0].astype(BF16),
        moe_w2=p["moe_w2"][0].astype(BF16),
    )


def _dup_heads(a):
    B, T = a.shape[:2]
    return jnp.concatenate([a, a], axis=3).reshape(B * T, HKV * LANES)


def _trunk(x, pos, conv0, s0, win_k, win_v, p, w):
    B, T, _ = x.shape
    R = B * T
    xf = x.reshape(R, D_MODEL)
    row = lambda a: a.reshape(1, -1)

    g_attn = row(p["norm_attn"][0])
    w_conv = p["gdn_w_conv"][0]
    nqk = 2 * QK_DIM
    qk, conv_qk = _qkv_conv(xf, g_attn, w["w_qk"], w_conv[:, :nqk], conv0[:, :, :nqk], B, T, True)
    v, conv_v = _qkv_conv(xf, g_attn, w["w_v"], w_conv[:, nqk:], conv0[:, :, nqk:], B, T, False)
    conv_new = jnp.concatenate([conv_qk, conv_v], axis=-1).reshape(B, -1, CONV_W - 1, CONV_CH)[:, -1]
    z, beta, gc, gct = _zbg(xf, g_attn, w["w_z"], w["w_ba"], p["gdn_a_log"][0], p["gdn_dt_bias"][0], B, T)
    u, wv, att_l = _delta_prep(qk, v, beta, gc, gct, B, T)
    og, s_new = _delta_scan(qk, u, wv, att_l, z, gc, s0, row(p["gdn_o_norm"][0]), B, T)
    x2 = _mixer_out_ffn(xf, og, w["w_out"], row(p["norm_ffn"][0]), w["ffn_w1"], w["ffn_w3"], w["ffn_w2"])

    keep = min(WINDOW, T) if win_k is None else T
    q, kvd, kvs = _qkv_rope(x2, row(p["norm_attn"][1]), row(p["kv_norm"]), w["w_q"], w["w_kvd"], w["w_kvs"],
                            _rope_tables(pos), B, T, keep)
    k_new = kvs[:, :HKV * HD].reshape(B, keep, HKV, HD)
    v_new = kvs[:, HKV * HD:].reshape(B, keep, HKV, HD)
    if win_k is None:
        keys, Tk = kvd, T
    else:
        hist = jnp.concatenate([_dup_heads(win_k), _dup_heads(win_v)], axis=1).astype(BF16)
        Tw = win_k.shape[1]
        keys = jnp.concatenate([hist.reshape(B, Tw, KV_DUP), kvd.reshape(B, T, KV_DUP)], axis=1)
        Tk = Tw + T
        keys = keys.reshape(B * Tk, KV_DUP)
    att = _attention(q, keys, row(p["swa_sinks"][0]), B, T, Tk)
    x3, h, eidx, gates, counts = _oproj_router(x2, att, w["w_o"], row(p["norm_ffn"][1]), p["moe_router"][0])

    te = min(EXPERT_TILE, max(TOP_K * R // N_EXPERTS, LANES))
    n_tiles = TOP_K * R // te + N_EXPERTS
    cnt = counts[0].astype(jnp.int32)
    tiles = (cnt + te - 1) // te
    tile_end = jnp.cumsum(tiles)
    base = (tile_end - tiles) * te
    n_used = tile_end[-1:]
    tile_ids = jnp.arange(n_tiles, dtype=jnp.int32)
    tile_expert = jnp.minimum(jnp.sum((tile_end[None, :] <= tile_ids[:, None]).astype(jnp.int32), axis=1),
                              N_EXPERTS - 1)
    pos_slot = _slots(eidx, base.astype(F32).reshape(1, N_EXPERTS)).reshape(-1)
    n_used = n_used.astype(jnp.int32)
    xs = _dispatch(h, pos_slot, cnt, base.astype(jnp.int32), n_used, n_tiles, te)
    ys = _expert_ffn(xs, tile_expert, n_used, w["moe_w1"], w["moe_w3"], w["moe_w2"], te)
    y = _combine(x3, gates, pos_slot, ys, row(p["final_norm"]))
    return (y.reshape(B, T, D_MODEL), conv_new[None], s_new[None], k_new, v_new)


def kernel(x_prompt, x_sample, cache_conv, state_delta, cache_k, cache_v, norm_attn, norm_ffn, gdn_w_in,
           gdn_w_conv, gdn_a_log, gdn_dt_bias, gdn_o_norm, gdn_w_out, kv_norm, w_kv, swa_w_q, swa_sinks, swa_w_o,
           ffn_w1, ffn_w3, ffn_w2, moe_router, moe_w1, moe_w3, moe_w2, final_norm):
    p = dict(norm_attn=norm_attn, norm_ffn=norm_ffn, gdn_w_in=gdn_w_in, gdn_w_conv=gdn_w_conv,
             gdn_a_log=gdn_a_log, gdn_dt_bias=gdn_dt_bias, gdn_o_norm=gdn_o_norm, gdn_w_out=gdn_w_out,
             kv_norm=kv_norm, w_kv=w_kv, swa_w_q=swa_w_q, swa_sinks=swa_sinks, swa_w_o=swa_w_o,
             ffn_w1=ffn_w1, ffn_w3=ffn_w3, ffn_w2=ffn_w2, moe_router=moe_router,
             moe_w1=moe_w1, moe_w3=moe_w3, moe_w2=moe_w2, final_norm=final_norm)
    w = _prep_weights(p)
    Bp, Tp, _ = x_prompt.shape
    Bs, Ts, _ = x_sample.shape
    conv0 = jnp.zeros((Bp, CONV_W - 1, CONV_CH), F32)
    s0 = jnp.zeros((Bp, HV, DK, DV), F32)
    y_p, conv_p, delta_p, k_p, v_p = _trunk(x_prompt, jnp.arange(Tp, dtype=jnp.int32), conv0, s0,
                                            None, None, p, w)
    pos_s = PAST_LEN + jnp.arange(Ts, dtype=jnp.int32)
    y_s, conv_s, delta_s, k_s, v_s = _trunk(x_sample, pos_s, cache_conv[0], state_delta[0],
                                            cache_k, cache_v, p, w)
    return (y_p, y_s, conv_p, delta_p, k_p, v_p, conv_s, delta_s, k_s, v_s)
```

```python
import functools

import jax
import jax.numpy as jnp
import numpy as np
from jax import lax
from jax.experimental import pallas as pl
from jax.experimental.pallas import tpu as pltpu

F32 = jnp.float32
BF16 = jnp.bfloat16

D_MODEL = 1024
CHUNK = 64
HK = 8
HV = 16
DK = 128
DV = 128
QK_DIM = HK * DK
V_DIM = HV * DV
CONV_CH = 2 * QK_DIM + V_DIM
CONV_W = 4
HQ = 16
HKV = 4
HD = 64
WINDOW = 128
ROT_DIM = HD // 4
ROPE_THETA = 500000.0
D_FF = 2816
N_EXPERTS = 8
TOP_K = 2
EPS = 1e-6
PAST_LEN = 2048

LANES = 128
ROW_TILE = 512
EXPERT_TILE = 512
VMEM_LIMIT = 48 * 1024 * 1024
VMEM_LIMIT_BIG = 56 * 1024 * 1024


def _params(sem, vmem=VMEM_LIMIT):
    return pltpu.CompilerParams(dimension_semantics=sem, vmem_limit_bytes=vmem)


def _rms_scale(x):
    return lax.rsqrt(jnp.mean(x * x, axis=-1, keepdims=True) + EPS)


def _bdot(a, b):
    return jnp.dot(a.astype(BF16), b.astype(BF16), preferred_element_type=F32)


def _bdot_nt(a, b):
    return lax.dot_general(a.astype(BF16), b.astype(BF16), (((1,), (1,)), ((), ())),
                           preferred_element_type=F32)


def _bdot_tn(a, b):
    return lax.dot_general(a.astype(BF16), b.astype(BF16), (((0,), (0,)), ((), ())),
                           preferred_element_type=F32)


def _split3(x):
    x1 = x.astype(BF16)
    r1 = x - x1.astype(F32)
    x2 = r1.astype(BF16)
    x3 = (r1 - x2.astype(F32)).astype(BF16)
    return x1, x2, x3


def _split2(x):
    x1 = x.astype(BF16)
    x2 = (x - x1.astype(F32)).astype(BF16)
    return x1, x2


def _dot_hi(a, b):
    a1, a2 = _split2(a)
    b1, b2 = _split2(b)
    d = functools.partial(jnp.dot, preferred_element_type=F32)
    return d(a1, b1) + (d(a1, b2) + d(a2, b1))


def _spread(x, emat, pieces):
    parts = _split3(x)[:pieces]
    out = jnp.dot(parts[0], emat, preferred_element_type=F32)
    for part in parts[1:]:
        out = out + jnp.dot(part, emat, preferred_element_type=F32)
    return out


def _head_spread_matrices():
    h = np.arange(HV)[:, None]
    full = (np.arange(HV * DV)[None, :] // DV == h)
    pair = (np.arange(HV * CHUNK)[None, :] // CHUNK == h)
    return jnp.asarray(full.astype(np.float32), dtype=BF16), jnp.asarray(pair.astype(np.float32), dtype=BF16)


def _sigmoid(x):
    return 0.5 * jnp.tanh(0.5 * x) + 0.5


def _silu(x):
    half = 0.5 * x
    return half * jnp.tanh(half) + half


CONV_SUB = 256


def _qkv_conv_kernel(x_ref, g_ref, w_ref, wc_ref, cs_ref, out_ref, cst_ref,
                     h_scr, pad_scr, carry_scr, *, tm, tn, l2):
    ti = pl.program_id(1)
    j = pl.program_id(2)

    @pl.when(j == 0)
    def _():
        x = x_ref[...]
        h_scr[...] = (x * _rms_scale(x) * g_ref[...]).astype(BF16)

    @pl.when(ti == 0)
    def _():
        carry_scr[j, 5:8, :] = cs_ref[0]

    h = h_scr[...]
    wc = wc_ref[...]
    scale = jnp.where(j == 0, DK ** -0.5, 1.0).astype(F32)
    cols = lambda k: slice(k * CONV_SUB, (k + 1) * CONV_SUB)

    def project(k):
        return jnp.dot(h, w_ref[:, cols(k)], preferred_element_type=F32)

    def finish(k, proj):
        c = cols(k)
        pad_scr[0:8, c] = carry_scr[j, :, c]
        pad_scr[8:, c] = proj
        acc = proj * wc[3:4, c]
        for s in range(1, CONV_W):
            acc = acc + pad_scr[8 - s:8 - s + tm, c] * wc[3 - s:4 - s, c]
        carry_scr[j, :, c] = pad_scr[tm:tm + 8, c]
        cst_ref[0, :, c] = pad_scr[tm + 5:tm + 8, c]
        y = _silu(acc)
        if not l2:
            out_ref[:, c] = y.astype(out_ref.dtype)
            return
        for hh in range(CONV_SUB // DK):
            seg = y[:, hh * DK:(hh + 1) * DK]
            inv = lax.rsqrt(jnp.sum(seg * seg, axis=-1, keepdims=True) + EPS) * scale
            lanes = slice(k * CONV_SUB + hh * DK, k * CONV_SUB + (hh + 1) * DK)
            out_ref[:, lanes] = (seg * inv).astype(out_ref.dtype)

    pending = project(0)
    for k in range(tn // CONV_SUB):
        upcoming = project(k + 1) if k + 1 < tn // CONV_SUB else None
        finish(k, pending)
        pending = upcoming


def _qkv_conv(x, g, w, w_conv, conv_state, B, T, l2):
    R = B * T
    n = w.shape[1]
    tm = min(2 * ROW_TILE, T)
    tn = QK_DIM
    nt = T // tm
    nj = n // tn
    kern = functools.partial(_qkv_conv_kernel, tm=tm, tn=tn, l2=l2)
    return pl.pallas_call(
        kern,
        grid=(B, nt, nj),
        in_specs=[
            pl.BlockSpec((tm, D_MODEL), lambda b, t, j: (b * nt + t, 0)),
            pl.BlockSpec((1, D_MODEL), lambda b, t, j: (0, 0)),
            pl.BlockSpec((D_MODEL, tn), lambda b, t, j: (0, j)),
            pl.BlockSpec((CONV_W, tn), lambda b, t, j: (0, j)),
            pl.BlockSpec((1, CONV_W - 1, tn), lambda b, t, j: (b, 0, j)),
        ],
        out_specs=[
            pl.BlockSpec((tm, tn), lambda b, t, j: (b * nt + t, j)),
            pl.BlockSpec((1, CONV_W - 1, tn), lambda b, t, j: (b * nt + t, 0, j)),
        ],
        out_shape=[
            jax.ShapeDtypeStruct((R, n), BF16),
            jax.ShapeDtypeStruct((B * nt, CONV_W - 1, n), F32),
        ],
        scratch_shapes=[
            pltpu.VMEM((tm, D_MODEL), BF16),
            pltpu.VMEM((tm + 8, tn), F32),
            pltpu.VMEM((nj, 8, tn), F32),
        ],
        compiler_params=_params(("arbitrary", "arbitrary", "arbitrary")),
        name="gdn_qk_conv" if l2 else "gdn_v_conv",
    )(x, g, w, w_conv, conv_state)


def _softplus(x):
    return jnp.maximum(x, 0.0) + jnp.log(1.0 + jnp.exp(-jnp.abs(x)))


def _zbg_kernel(x_ref, g_ref, w_ref, alog_ref, dtb_ref, tri_ref, z_ref, beta_ref, gc_ref, gct_ref, *, tm):
    x = x_ref[...]
    h = (x * _rms_scale(x) * g_ref[...]).astype(BF16)
    ba = jnp.dot(h, w_ref[:, V_DIM:], preferred_element_type=F32)
    z_ref[...] = jnp.dot(h, w_ref[:, :V_DIM], preferred_element_type=F32).astype(z_ref.dtype)
    beta_ref[...] = _sigmoid(ba[:, :HV])
    gcol = -jnp.exp(alog_ref[...]) * _softplus(ba[:, HV:2 * HV] + dtb_ref[...])
    tri = tri_ref[...]
    g1, g2, g3 = _split3(gcol)
    d = functools.partial(jnp.dot, preferred_element_type=F32)
    gc = d(tri, g1) + (d(tri, g2) + d(tri, g3))
    gc_ref[...] = gc
    r = lax.broadcasted_iota(jnp.int32, (HV, HV), 0)
    c = lax.broadcasted_iota(jnp.int32, (HV, HV), 1)
    sel = (c == jnp.where(r < HV // 2, 2 * r, 2 * r - (HV - 1))).astype(BF16)
    tn = lambda a: lax.dot_general(sel, a, (((1,), (1,)), ((), ())), preferred_element_type=F32)
    c1, c2, c3 = _split3(gc)
    gct = tn(c1) + (tn(c2) + tn(c3))
    for ch in range(tm // CHUNK):
        sl = slice(ch * CHUNK, (ch + 1) * CHUNK)
        gct_ref[0, ch] = jnp.concatenate([gct[:HV // 2, sl], gct[HV // 2:, sl]], axis=1)


def _zbg(x, g, w_z, w_ba, a_log, dt_bias, B, T):
    R = B * T
    tm = min(ROW_TILE, T)
    nt = T // tm
    nc = T // CHUNK
    idx = np.arange(tm)
    tri = jnp.asarray(((idx[:, None] // CHUNK == idx[None, :] // CHUNK) & (idx[:, None] >= idx[None, :]))
                      .astype(np.float32), dtype=BF16)
    w_zba = jnp.concatenate([w_z, w_ba, jnp.zeros((D_MODEL, LANES - 2 * HV), BF16)], axis=1)
    full = lambda shape: pl.BlockSpec(shape, lambda b, t: (0,) * len(shape))
    kern = functools.partial(_zbg_kernel, tm=tm)
    return pl.pallas_call(
        kern,
        grid=(B, nt),
        in_specs=[
            pl.BlockSpec((tm, D_MODEL), lambda b, t: (b * nt + t, 0)),
            full((1, D_MODEL)),
            full((D_MODEL, V_DIM + LANES)),
            full((1, HV)), full((1, HV)),
            full((tm, tm)),
        ],
        out_specs=[
            pl.BlockSpec((tm, V_DIM), lambda b, t: (b * nt + t, 0)),
            pl.BlockSpec((tm, HV), lambda b, t: (b * nt + t, 0)),
            pl.BlockSpec((tm, HV), lambda b, t: (b * nt + t, 0)),
            pl.BlockSpec((1, tm // CHUNK, HV // 2, 2 * CHUNK), lambda b, t: (b, t, 0, 0)),
        ],
        out_shape=[
            jax.ShapeDtypeStruct((R, V_DIM), BF16),
            jax.ShapeDtypeStruct((R, HV), F32),
            jax.ShapeDtypeStruct((R, HV), F32),
            jax.ShapeDtypeStruct((B, nc, HV // 2, 2 * CHUNK), F32),
        ],
        compiler_params=_params(("arbitrary", "arbitrary")),
        name="gdn_z_beta_decay",
    )(x, g, w_zba, a_log.reshape(1, HV), dt_bias.reshape(1, HV), tri)


PAIR = HV // HK


def _pair_rows(x, lo):
    zero = jnp.zeros((), x.dtype)
    return jnp.concatenate([jnp.where(lo, x, zero), jnp.where(lo, zero, x)], axis=0)


def _pair_products(lhs_list, rhs_list, lo):
    def split(a):
        hi = a.astype(BF16)
        return hi, (a - hi.astype(F32)).astype(BF16)

    ops = []
    for lhs, rhs in zip(lhs_list, rhs_list):
        lh, ll = split(lhs)
        rh, rl = split(rhs)
        rh2 = _pair_rows(rh, lo)
        ops.append((jnp.concatenate([lh, lh, ll], axis=1), jnp.concatenate([rh2, _pair_rows(rl, lo), rh2], axis=0)))
    return [jnp.dot(a, b, preferred_element_type=F32) for a, b in ops]


def _pair_products_bf16(lhs_list, rhs_list, lo):
    ops = [(lhs.astype(BF16), _pair_rows(rhs.astype(BF16), lo)) for lhs, rhs in zip(lhs_list, rhs_list)]
    return [jnp.dot(a, b, preferred_element_type=F32) for a, b in ops]


INV_BASE = 16


def _pair_unit_lower_inverse(low_list, lo, eye2, same_block):
    C = low_list[0].shape[0]
    base = same_block(INV_BASE)
    xs = [jnp.where(base, -low, 0.0) for low in low_list]
    accs = [eye2 + x for x in xs]
    pws = _pair_products(xs, xs, lo)
    n = 2
    while 2 * n < INV_BASE:
        prods = _pair_products([jnp.concatenate([a, p], axis=0) for a, p in zip(accs, pws)], pws, lo)
        accs = [a + pr[:C] for a, pr in zip(accs, prods)]
        pws = [pr[C:] for pr in prods]
        n *= 2
    prods = _pair_products(accs, pws, lo)
    tinvs = [a + pr for a, pr in zip(accs, prods)]
    size = 2 * INV_BASE
    while size <= C:
        below = same_block(size) & jnp.logical_not(same_block(size // 2))
        offs = [jnp.where(below, low, 0.0) for low in low_list]
        inner = _pair_products_bf16(offs, tinvs, lo)
        outer = _pair_products_bf16(tinvs, inner, lo)
        tinvs = [t - o for t, o in zip(tinvs, outer)]
        size *= 2
    return tinvs


def _delta_prep_kernel(q_ref, k_ref, v_ref, beta_ref, gc_ref, gct_ref, epair_ref,
                       u_ref, w_ref, a_ref, qg_ref, kd_ref, *, cb):
    C = CHUNK
    ri = lax.broadcasted_iota(jnp.int32, (C, 2 * C), 0)
    li = lax.broadcasted_iota(jnp.int32, (C, 2 * C), 1)
    lm = li & (C - 1)
    lo = li < C
    incl = ri >= lm
    strict = ri > lm
    eye2 = (ri == lm).astype(F32)
    units = [(c, p) for c in range(cb) for p in range(HK)]
    rows = lambda c: slice(c * C, (c + 1) * C)

    grams = []
    for c, p in units:
        kb = k_ref[rows(c), p * DK:(p + 1) * DK]
        qb = q_ref[rows(c), p * DK:(p + 1) * DK]
        grams.append(lax.dot_general(jnp.concatenate([kb, qb], axis=0), jnp.concatenate([kb, kb], axis=0),
                                     (((1,), (1,)), ((), ())), preferred_element_type=F32))

    epair = epair_ref[...]
    g2s, b2s, betas, begs, egs, kgs = [], [], [], [], [], []
    for c in range(cb):
        beta = beta_ref[rows(c), :]
        gc = gc_ref[rows(c), :]
        g2s.append(_spread(gc, epair, 3))
        b2s.append(_spread(beta, epair, 2))
        betas.append(beta)
        egs.append(jnp.exp(gc))
        begs.append(beta * egs[c])
        kgs.append(jnp.exp(gc[C - 1:C, :] - gc))

    lows = []
    for (c, p), gram in zip(units, grams):
        pt = slice(p * 2 * C, (p + 1) * 2 * C)
        grow2 = gct_ref[c, p:p + 1, :]
        decay2 = jnp.exp(jnp.where(incl, g2s[c][:, pt] - grow2, -jnp.inf))
        a_ref[rows(c), pt] = (gram[C:] * decay2).astype(a_ref.dtype)
        lows.append(jnp.where(strict, b2s[c][:, pt] * gram[:C] * decay2, 0.0))

    same_block = lambda s: (ri // s) == (lm // s)
    tinvs = _pair_unit_lower_inverse(lows, lo, eye2, same_block)

    ops = []
    for (c, p), tinv in zip(units, tinvs):
        kf = k_ref[rows(c), p * DK:(p + 1) * DK].astype(F32)
        qf = q_ref[rows(c), p * DK:(p + 1) * DK].astype(F32)
        rhs = []
        for h in (PAIR * p, PAIR * p + 1):
            ht = slice(h * DV, (h + 1) * DV)
            vf = v_ref[rows(c), ht].astype(F32)
            rhs.append(jnp.concatenate([vf * betas[c][:, h:h + 1], kf * begs[c][:, h:h + 1]], axis=1))
            qg_ref[rows(c), ht] = (qf * egs[c][:, h:h + 1]).astype(qg_ref.dtype)
            kd_ref[rows(c), ht] = (kf * kgs[c][:, h:h + 1]).astype(kd_ref.dtype)
        ops.append((_pair_rows(tinv.astype(BF16), lo), jnp.concatenate(rhs, axis=0).astype(BF16)))
    for (c, p), (lhs, rhs) in zip(units, ops):
        uw = jnp.dot(lhs, rhs, preferred_element_type=F32)
        for i, h in enumerate((PAIR * p, PAIR * p + 1)):
            u_ref[rows(c), h * DV:(h + 1) * DV] = uw[i * C:(i + 1) * C, :DV].astype(u_ref.dtype)
            w_ref[rows(c), h * DV:(h + 1) * DV] = uw[i * C:(i + 1) * C, DV:].astype(w_ref.dtype)


def _delta_prep(qk, v, beta, gc, gct, B, T):
    R = B * T
    cb = 4 if T % (4 * CHUNK) == 0 else (2 if T % (2 * CHUNK) == 0 else 1)
    rb = cb * CHUNK
    gct2 = gct.reshape(R // CHUNK, HK, 2 * CHUNK)
    kern = functools.partial(_delta_prep_kernel, cb=cb)
    return pl.pallas_call(
        kern,
        grid=(R // rb,),
        in_specs=[
            pl.BlockSpec((rb, QK_DIM), lambda i: (i, 0)),
            pl.BlockSpec((rb, QK_DIM), lambda i: (i, 1)),
            pl.BlockSpec((rb, V_DIM), lambda i: (i, 0)),
            pl.BlockSpec((rb, HV), lambda i: (i, 0)),
            pl.BlockSpec((rb, HV), lambda i: (i, 0)),
            pl.BlockSpec((cb, HK, 2 * CHUNK), lambda i: (i, 0, 0)),
            pl.BlockSpec((HV, HV * CHUNK), lambda i: (0, 0)),
        ],
        out_specs=[
            pl.BlockSpec((rb, V_DIM), lambda i: (i, 0)),
            pl.BlockSpec((rb, V_DIM), lambda i: (i, 0)),
            pl.BlockSpec((rb, HV * CHUNK), lambda i: (i, 0)),
            pl.BlockSpec((rb, V_DIM), lambda i: (i, 0)),
            pl.BlockSpec((rb, V_DIM), lambda i: (i, 0)),
        ],
        out_shape=[
            jax.ShapeDtypeStruct((R, V_DIM), BF16),
            jax.ShapeDtypeStruct((R, V_DIM), BF16),
            jax.ShapeDtypeStruct((R, HV * CHUNK), BF16),
            jax.ShapeDtypeStruct((R, V_DIM), BF16),
            jax.ShapeDtypeStruct((R, V_DIM), BF16),
        ],
        compiler_params=_params(("arbitrary",)),
        name="gdn_delta_prep",
    )(qk, qk, v, beta, gc, gct2, _head_spread_matrices()[1])


def _delta_scan_kernel(qg_ref, kd_ref, u_ref, w_ref, a_ref, z_ref, gc_ref, s0_ref, on_ref, efull_ref,
                       og_ref, s_ref, *, bb):
    c = pl.program_id(1)

    @pl.when(c == 0)
    def _():
        s_ref[...] = s0_ref[...]

    C = CHUNK
    lo = lax.broadcasted_iota(jnp.int32, (C, 2 * C), 1) < C
    onorm = on_ref[...]
    units = [(b, h) for b in range(bb) for h in range(HV)]
    pairs = [(b, p) for b in range(bb) for p in range(HK)]

    efull = efull_ref[...]
    dgs = [_spread(jnp.exp(gc_ref[b, C - 8:C, :]), efull, 3)[7:8, :] for b in range(bb)]

    ops = []
    kds = []
    dec = []
    for b, h in units:
        ht = slice(h * DV, (h + 1) * DV)
        kds.append(kd_ref[b, :, ht])
        dec.append(dgs[b][:, ht])
        ops.append((jnp.concatenate([w_ref[b, :, ht], qg_ref[b, :, ht]], axis=0), s_ref[b, h].astype(BF16)))
    wqs = [jnp.dot(a, s, preferred_element_type=F32) for a, s in ops]

    vns = [(u_ref[b, :, h * DV:(h + 1) * DV].astype(F32) - wq[:C]).astype(BF16) for (b, h), wq in zip(units, wqs)]

    oparts = []
    for i, (b, p) in enumerate(pairs):
        att = _pair_rows(a_ref[b, :, p * 2 * C:(p + 1) * 2 * C], lo)
        vn2 = jnp.concatenate([vns[PAIR * i], vns[PAIR * i + 1]], axis=0)
        oparts.append(jnp.dot(att, vn2, preferred_element_type=F32))
    sds = [lax.dot_general(kd, vn, (((0,), (0,)), ((), ())), preferred_element_type=F32)
           for kd, vn in zip(kds, vns)]

    for i, (b, h) in enumerate(units):
        s_ref[b, h] = s_ref[b, h] * dec[i] + sds[i]
        o = wqs[i][C:] + oparts[i // PAIR][(h % PAIR) * C:(h % PAIR + 1) * C]
        on = o * _rms_scale(o) * onorm
        zf = z_ref[b, :, h * DV:(h + 1) * DV].astype(F32)
        og_ref[b, :, h * DV:(h + 1) * DV] = (on * _silu(zf)).astype(og_ref.dtype)


def _delta_scan(qg, kd, u, w, att, z, gc, s0, o_norm, B, T):
    nc = T // CHUNK
    bb = 4 if B % 4 == 0 else (2 if B % 2 == 0 else 1)
    v3 = lambda a: a.reshape(B, T, a.shape[-1])
    blk = lambda n, j=0: pl.BlockSpec((bb, CHUNK, n), lambda b, c: (b, c, j))
    kern = functools.partial(_delta_scan_kernel, bb=bb)
    og, s_new = pl.pallas_call(
        kern,
        grid=(B // bb, nc),
        in_specs=[
            blk(V_DIM), blk(V_DIM), blk(V_DIM), blk(V_DIM), blk(HV * CHUNK), blk(V_DIM), blk(HV),
            pl.BlockSpec((bb, HV, DK, DV), lambda b, c: (b, 0, 0, 0)),
            pl.BlockSpec((1, DV), lambda b, c: (0, 0)),
            pl.BlockSpec((HV, HV * DV), lambda b, c: (0, 0)),
        ],
        out_specs=[
            blk(V_DIM),
            pl.BlockSpec((bb, HV, DK, DV), lambda b, c: (b, 0, 0, 0)),
        ],
        out_shape=[
            jax.ShapeDtypeStruct((B, T, V_DIM), BF16),
            jax.ShapeDtypeStruct((B, HV, DK, DV), F32),
        ],
        compiler_params=_params(("arbitrary", "arbitrary")),
        name="gdn_delta_scan",
    )(v3(qg), v3(kd), v3(u), v3(w), v3(att), v3(z), v3(gc), s0, o_norm, _head_spread_matrices()[0])
    return og.reshape(B * T, V_DIM), s_new


MXU_COLS = 256
FF_SPLITS = (6 * MXU_COLS, 5 * MXU_COLS)
assert sum(FF_SPLITS) == D_FF
RESIDENT = pl.Buffered(1)


def _swiglu(h, w1_ref, w3_ref, w2_ref, lead=()):
    out = None
    start = 0
    for width in FF_SPLITS:
        sl = slice(start, start + width)
        start += width
        a = jnp.dot(h, w1_ref[lead + (slice(None), sl)], preferred_element_type=F32)
        b = jnp.dot(h, w3_ref[lead + (slice(None), sl)], preferred_element_type=F32)
        mid = (_silu(a) * b).astype(BF16)
        part = jnp.dot(mid, w2_ref[lead + (sl, slice(None))], preferred_element_type=F32)
        out = part if out is None else out + part
    return out


def _mixer_out_ffn_kernel(x_ref, a_ref, wo_ref, g_ref, w1_ref, w3_ref, w2_ref, o_ref):
    x1 = x_ref[...] + jnp.dot(a_ref[...], wo_ref[...], preferred_element_type=F32)
    h = (x1 * _rms_scale(x1) * g_ref[...]).astype(BF16)
    o_ref[...] = x1 + _swiglu(h, w1_ref, w3_ref, w2_ref)


def _mixer_out_ffn(x, a, w_out, g, w1, w3, w2):
    R = x.shape[0]
    K = a.shape[1]
    tm = min(ROW_TILE, R)
    held = lambda shape: pl.BlockSpec(shape, lambda i: (0,) * len(shape), pipeline_mode=RESIDENT)
    return pl.pallas_call(
        _mixer_out_ffn_kernel,
        grid=(R // tm,),
        in_specs=[
            pl.BlockSpec((tm, D_MODEL), lambda i: (i, 0)),
            pl.BlockSpec((tm, K), lambda i: (i, 0)),
            held((K, D_MODEL)),
            held((1, D_MODEL)),
            held((D_MODEL, D_FF)), held((D_MODEL, D_FF)), held((D_FF, D_MODEL)),
        ],
        out_specs=pl.BlockSpec((tm, D_MODEL), lambda i: (i, 0)),
        out_shape=jax.ShapeDtypeStruct((R, D_MODEL), F32),
        compiler_params=_params(("arbitrary",), VMEM_LIMIT_BIG),
        name="gdn_out_dense_swiglu",
    )(x, a, w_out, g, w1, w3, w2)


KV_DUP = 2 * HKV * LANES
KV_STD = 2 * HKV * HD


def _rope(x, cos, sm, sp):
    up = pltpu.roll(x, LANES - ROT_DIM // 2, 1)
    dn = pltpu.roll(x, ROT_DIM // 2, 1)
    return x * cos + up * sm + dn * sp


def _qkv_rope_kernel(x_ref, gq_ref, gkv_ref, wq_ref, wkv_ref, wkvs_ref, cos_ref, sm_ref, sp_ref,
                     q_ref, kvd_ref, kvs_ref, *, nt, keep):
    x = x_ref[...]
    xr = x * _rms_scale(x)
    hq = (xr * gq_ref[...]).astype(BF16)
    hkv = (xr * gkv_ref[...]).astype(BF16)
    cos = cos_ref[...]
    sm = sm_ref[...]
    sp = sp_ref[...]
    q = jnp.dot(hq, wq_ref[...], preferred_element_type=F32)
    for t in range(HQ * HD // LANES):
        sl = slice(t * LANES, (t + 1) * LANES)
        q_ref[:, sl] = (_rope(q[:, sl], cos, sm, sp) * (HD ** -0.5)).astype(q_ref.dtype)
    kv = jnp.dot(hkv, wkv_ref[...], preferred_element_type=F32)
    half = KV_DUP // 2
    for t in range(HKV):
        sl = slice(t * LANES, (t + 1) * LANES)
        kvd_ref[:, sl] = _rope(kv[:, sl], cos, sm, sp).astype(kvd_ref.dtype)
    kvd_ref[:, half:] = kv[:, half:].astype(kvd_ref.dtype)

    @pl.when(pl.program_id(0) % nt == nt - 1)
    def _():
        tm = x.shape[0]
        rows = slice(tm - keep, tm)
        kvs = jnp.dot(hkv[rows], wkvs_ref[...], preferred_element_type=F32)
        for t in range(HKV * HD // LANES):
            sl = slice(t * LANES, (t + 1) * LANES)
            kvs_ref[:, sl] = _rope(kvs[:, sl], cos[rows], sm[rows], sp[rows])
        kvs_ref[:, HKV * HD:] = kvs[:, HKV * HD:]


def _rope_tables(pos):
    half = ROT_DIM // 2
    inv = jnp.power(ROPE_THETA, -jnp.arange(half, dtype=F32) * 2.0 / ROT_DIM)
    ang = pos.astype(F32)[:, None] * inv[None, :]
    cos = jnp.cos(ang)
    sin = jnp.sin(ang)
    T = pos.shape[0]
    one = jnp.ones((T, HD - ROT_DIM), F32)
    zero = jnp.zeros((T, HD - ROT_DIM), F32)
    zh = jnp.zeros((T, half), F32)
    c64 = jnp.concatenate([cos, cos, one], axis=1)
    sm64 = jnp.concatenate([-sin, zh, zero], axis=1)
    sp64 = jnp.concatenate([zh, sin, zero], axis=1)
    dup = lambda a: jnp.concatenate([a, a], axis=1)
    return dup(c64), dup(sm64), dup(sp64)


def _qkv_rope(x, gq, gkv, w_q, w_kvd, w_kvs, tables, B, T, keep):
    R = B * T
    tm = min(2 * ROW_TILE, T)
    nt = T // tm
    assert keep <= tm
    tab = pl.BlockSpec((tm, LANES), lambda i: (i % nt, 0))
    kern = functools.partial(_qkv_rope_kernel, nt=nt, keep=keep)
    return pl.pallas_call(
        kern,
        grid=(R // tm,),
        in_specs=[
            pl.BlockSpec((tm, D_MODEL), lambda i: (i, 0)),
            pl.BlockSpec((1, D_MODEL), lambda i: (0, 0)),
            pl.BlockSpec((1, D_MODEL), lambda i: (0, 0)),
            pl.BlockSpec((D_MODEL, HQ * HD), lambda i: (0, 0)),
            pl.BlockSpec((D_MODEL, KV_DUP), lambda i: (0, 0)),
            pl.BlockSpec((D_MODEL, KV_STD), lambda i: (0, 0)),
            tab, tab, tab,
        ],
        out_specs=[
            pl.BlockSpec((tm, HQ * HD), lambda i: (i, 0)),
            pl.BlockSpec((tm, KV_DUP), lambda i: (i, 0)),
            pl.BlockSpec((keep, KV_STD), lambda i: (i // nt, 0)),
        ],
        out_shape=[
            jax.ShapeDtypeStruct((R, HQ * HD), BF16),
            jax.ShapeDtypeStruct((R, KV_DUP), BF16),
            jax.ShapeDtypeStruct((B * keep, KV_STD), F32),
        ],
        compiler_params=_params(("arbitrary",)),
        name="swa_qkv_rope",
    )(x, gq, gkv, w_q, w_kvd, w_kvs, *tables)


KEY_CHUNKS = WINDOW // CHUNK + 1
KEY_PAD = 256


def _attn_kernel(*refs, off, cq):
    q_ref = refs[0]
    krefs = refs[1:KEY_CHUNKS + cq]
    sink_ref, ones_ref, o_ref = refs[KEY_CHUNKS + cq:]
    c = pl.program_id(1)
    lane = lax.broadcasted_iota(jnp.int32, (KEY_CHUNKS * CHUNK, LANES), 1)
    lo = lane < HD
    col = lax.broadcasted_iota(jnp.int32, (1, KEY_PAD), 1)
    sink_col = col == KEY_CHUNKS * CHUNK
    zpad = jnp.zeros((KEY_PAD - KEY_CHUNKS * CHUNK, LANES), BF16)
    zero = jnp.zeros((), BF16)
    tiles = HQ // HKV // 2
    ones_blk = ones_ref[...]
    units = [(u, g) for u in range(cq) for g in range(HKV)]

    vblks = []
    scores = []
    for u, g in units:
        kr = krefs[u:u + KEY_CHUNKS]
        kg = jnp.concatenate([r[:, g * LANES:(g + 1) * LANES] for r in kr], axis=0)
        vg = jnp.concatenate([r[:, KV_DUP // 2 + g * LANES:KV_DUP // 2 + (g + 1) * LANES] for r in kr], axis=0)
        kblk = jnp.concatenate([jnp.where(lo, kg, zero), zpad, jnp.where(lo, zero, kg), zpad], axis=0)
        vblk = jnp.concatenate([jnp.where(lo, vg, zero), zpad, jnp.where(lo, zero, vg), zpad], axis=0)
        vblks.append(jnp.concatenate([vblk, ones_blk], axis=1))
        qg = jnp.concatenate([q_ref[u * CHUNK:(u + 1) * CHUNK, (g * tiles + p) * LANES:(g * tiles + p + 1) * LANES]
                              for p in range(tiles)], axis=0)
        scores.append(lax.dot_general(qg, kblk, (((1,), (1,)), ((), ())), preferred_element_type=F32))

    probs = []
    for (u, g), sc in zip(units, scores):
        first = c * cq + u + off - (KEY_CHUNKS - 1)
        valid = (col < KEY_CHUNKS * CHUNK) & (col // CHUNK + first >= 0)
        base_bias = jnp.where(valid, 0.0, -jnp.inf).astype(F32)
        rows = []
        for p in range(tiles):
            t = g * tiles + p
            ps = []
            for hh in range(2):
                bias = jnp.where(sink_col, sink_ref[:, 2 * t + hh:2 * t + hh + 1], base_bias)
                sh = sc[p * CHUNK:(p + 1) * CHUNK, hh * KEY_PAD:(hh + 1) * KEY_PAD] + bias
                m = jnp.max(sh, axis=-1, keepdims=True)
                ps.append(jnp.exp(sh - m).astype(BF16))
            rows.append(jnp.concatenate(ps, axis=1))
        probs.append(jnp.concatenate(rows, axis=0))

    outs = [jnp.dot(pr, vb, preferred_element_type=F32) for pr, vb in zip(probs, vblks)]
    for (u, g), out in zip(units, outs):
        for p in range(tiles):
            t = g * tiles + p
            rows = slice(p * CHUNK, (p + 1) * CHUNK)
            o_ref[u * CHUNK:(u + 1) * CHUNK, t * LANES:(t + 1) * LANES] = (
                out[rows, :LANES] / out[rows, LANES:]).astype(o_ref.dtype)


def _attention(q, kvd, sinks, B, Tq, Tk):
    ncq = Tq // CHUNK
    nck = Tk // CHUNK
    off = nck - ncq
    cq = 4 if ncq % 4 == 0 else (2 if ncq % 2 == 0 else 1)
    kern = functools.partial(_attn_kernel, off=off, cq=cq)
    blk_row = np.arange(2 * KEY_PAD)[:, None] // KEY_PAD
    blk_lane = np.arange(LANES)[None, :] // HD
    ones_blk = jnp.asarray((blk_row == blk_lane).astype(np.float32), dtype=BF16)

    def kspec(i):
        return pl.BlockSpec((CHUNK, KV_DUP),
                            lambda b, c: (b * nck + jnp.maximum(c * cq + off - (KEY_CHUNKS - 1) + i, 0), 0))

    qspec = pl.BlockSpec((cq * CHUNK, HQ * HD), lambda b, c: (b * (ncq // cq) + c, 0))
    return pl.pallas_call(
        kern,
        grid=(B, ncq // cq),
        in_specs=[qspec] + [kspec(i) for i in range(KEY_CHUNKS + cq - 1)] + [
            pl.BlockSpec((1, HQ), lambda b, c: (0, 0)),
            pl.BlockSpec((2 * KEY_PAD, LANES), lambda b, c: (0, 0)),
        ],
        out_specs=qspec,
        out_shape=jax.ShapeDtypeStruct((B * Tq, HQ * HD), BF16),
        compiler_params=_params(("arbitrary", "arbitrary")),
        name="swa_attention",
    )(q, *([kvd] * (KEY_CHUNKS + cq - 1)), sinks, ones_blk)


def _oproj_router_kernel(x_ref, a_ref, w_ref, g_ref, wr_ref, x3_ref, h_ref, e_ref, gate_ref, cnt_ref):
    i = pl.program_id(0)
    x3 = x_ref[...] + jnp.dot(a_ref[...], w_ref[...], preferred_element_type=F32)
    x3_ref[...] = x3
    h = x3 * _rms_scale(x3) * g_ref[...]
    h_ref[...] = h
    h1, h2 = _split2(h)
    r1, r2 = _split2(wr_ref[...])
    part = jnp.dot(h1, jnp.concatenate([r1, r2], axis=1), preferred_element_type=F32)
    logits = part[:, :N_EXPERTS] + (part[:, N_EXPERTS:] + jnp.dot(h2, r1, preferred_element_type=F32))
    eidx = lax.broadcasted_iota(jnp.int32, logits.shape, 1)
    m1 = jnp.max(logits, axis=-1, keepdims=True)
    i1 = jnp.min(jnp.where(logits == m1, eidx, N_EXPERTS), axis=-1, keepdims=True)
    rest = jnp.where(eidx == i1, -jnp.inf, logits)
    m2 = jnp.max(rest, axis=-1, keepdims=True)
    i2 = jnp.min(jnp.where(rest == m2, eidx, N_EXPERTS), axis=-1, keepdims=True)
    e2 = jnp.exp(m2 - m1)
    den = 1.0 + e2
    e_ref[...] = jnp.concatenate([i1, i2], axis=1)
    gate_ref[...] = jnp.concatenate([1.0 / den, e2 / den], axis=1)
    hot = ((eidx == i1) | (eidx == i2)).astype(F32)

    @pl.when(i == 0)
    def _():
        cnt_ref[...] = jnp.zeros_like(cnt_ref)

    cnt_ref[...] += jnp.sum(hot, axis=0, keepdims=True)


def _oproj_router(x, a, w_o, g, w_r):
    R = x.shape[0]
    tm = min(ROW_TILE, R)
    row = lambda n: pl.BlockSpec((tm, n), lambda i: (i, 0))
    return pl.pallas_call(
        _oproj_router_kernel,
        grid=(R // tm,),
        in_specs=[
            row(D_MODEL), row(HQ * HD),
            pl.BlockSpec((HQ * HD, D_MODEL), lambda i: (0, 0)),
            pl.BlockSpec((1, D_MODEL), lambda i: (0, 0)),
            pl.BlockSpec((D_MODEL, N_EXPERTS), lambda i: (0, 0)),
        ],
        out_specs=[row(D_MODEL), row(D_MODEL), row(TOP_K), row(TOP_K),
                   pl.BlockSpec((1, N_EXPERTS), lambda i: (0, 0))],
        out_shape=[
            jax.ShapeDtypeStruct((R, D_MODEL), F32),
            jax.ShapeDtypeStruct((R, D_MODEL), F32),
            jax.ShapeDtypeStruct((R, TOP_K), jnp.int32),
            jax.ShapeDtypeStruct((R, TOP_K), F32),
            jax.ShapeDtypeStruct((1, N_EXPERTS), F32),
        ],
        compiler_params=_params(("arbitrary",)),
        name="swa_oproj_router",
    )(x, a, w_o, g, w_r)


def _slot_kernel(e_ref, base_ref, tri_ref, pos_ref, run_scr):
    i = pl.program_id(0)

    @pl.when(i == 0)
    def _():
        run_scr[...] = jnp.zeros_like(run_scr)

    e = e_ref[...]
    tm = e.shape[0]
    eidx = lax.broadcasted_iota(jnp.int32, (tm, N_EXPERTS), 1)
    hot0 = eidx == e[:, 0:1]
    hot1 = eidx == e[:, 1:2]
    hot = (hot0 | hot1).astype(BF16)
    before = jnp.dot(tri_ref[...], hot, preferred_element_type=F32)
    dest = before + run_scr[...] + base_ref[...]
    p0 = jnp.sum(jnp.where(hot0, dest, 0.0), axis=-1, keepdims=True)
    p1 = jnp.sum(jnp.where(hot1, dest, 0.0), axis=-1, keepdims=True)
    pos_ref[...] = jnp.concatenate([p0, p1], axis=1).astype(jnp.int32)
    run_scr[...] += jnp.sum(hot.astype(F32), axis=0, keepdims=True)


def _slots(eidx, base):
    R = eidx.shape[0]
    tm = min(ROW_TILE, R)
    idx = np.arange(tm)
    tri = jnp.asarray((idx[:, None] > idx[None, :]).astype(np.float32), dtype=BF16)
    return pl.pallas_call(
        _slot_kernel,
        grid=(R // tm,),
        in_specs=[
            pl.BlockSpec((tm, TOP_K), lambda i: (i, 0)),
            pl.BlockSpec((1, N_EXPERTS), lambda i: (0, 0)),
            pl.BlockSpec((tm, tm), lambda i: (0, 0)),
        ],
        out_specs=pl.BlockSpec((tm, TOP_K), lambda i: (i, 0)),
        out_shape=jax.ShapeDtypeStruct((R, TOP_K), jnp.int32),
        scratch_shapes=[pltpu.VMEM((1, N_EXPERTS), F32)],
        compiler_params=_params(("arbitrary",)),
        name="moe_slots",
    )(eidx, base, tri)


DMA_UNROLL = 8


def _row_copy(src, i, dst, j, sem):
    return pltpu.make_async_copy(src.at[pl.ds(i, 1)], dst.at[pl.ds(j, 1)], sem)


def _dispatch_kernel(pos_ref, cnt_ref, base_ref, nused_ref, h_ref, xs_ref, zero_scr, sem, zsem, *, tm, te, n_tiles):
    i = pl.program_id(0)

    @pl.when(i == 0)
    def _():
        zero_scr[...] = jnp.zeros_like(zero_scr)

        def tile_copy(t):
            dst = xs_ref.at[pl.ds(pl.multiple_of(t * te, te), te)]
            return pltpu.make_async_copy(zero_scr, dst, zsem)

        def issue_tile(t, carry):
            tile_copy(t).start()
            return carry

        def drain_tile(t, carry):
            tile_copy(t).wait()
            return carry

        lax.fori_loop(nused_ref[0], n_tiles, issue_tile, 0)
        lax.fori_loop(nused_ref[0], n_tiles, drain_tile, 0)
        for e in range(N_EXPERTS):
            n = cnt_ref[e]
            start = base_ref[e] + n
            npad = (te - n % te) % te

            def issue(r, carry):
                _row_copy(zero_scr, 0, xs_ref, start + r, zsem).start()
                return carry

            def drain(r, carry):
                _row_copy(zero_scr, 0, xs_ref, start + r, zsem).wait()
                return carry

            lax.fori_loop(0, npad, issue, 0)
            lax.fori_loop(0, npad, drain, 0)

    def issue(r, carry):
        _row_copy(h_ref, r, xs_ref, pos_ref[2 * r], sem).start()
        _row_copy(h_ref, r, xs_ref, pos_ref[2 * r + 1], sem).start()
        return carry

    lax.fori_loop(0, tm, issue, 0, unroll=DMA_UNROLL)
    for _ in range(TOP_K):
        pltpu.make_async_copy(h_ref, xs_ref.at[pl.ds(0, tm)], sem).wait()


def _dispatch(h, pos_flat, counts, base, n_used, n_tiles, te):
    R = h.shape[0]
    tm = min(ROW_TILE, R)
    kern = functools.partial(_dispatch_kernel, tm=tm, te=te, n_tiles=n_tiles)
    smem = pl.BlockSpec(memory_space=pltpu.SMEM)
    return pl.pallas_call(
        kern,
        grid=(R // tm,),
        in_specs=[
            pl.BlockSpec((TOP_K * tm,), lambda i: (i,), memory_space=pltpu.SMEM),
            smem, smem, smem,
            pl.BlockSpec((tm, D_MODEL), lambda i: (i, 0)),
        ],
        out_specs=pl.BlockSpec(memory_space=pl.ANY),
        out_shape=jax.ShapeDtypeStruct((n_tiles * te, D_MODEL), F32),
        scratch_shapes=[pltpu.VMEM((te, D_MODEL), F32), pltpu.SemaphoreType.DMA(()), pltpu.SemaphoreType.DMA(())],
        compiler_params=pltpu.CompilerParams(dimension_semantics=("arbitrary",), vmem_limit_bytes=VMEM_LIMIT,
                                             has_side_effects=True),
        name="moe_dispatch",
    )(pos_flat, counts, base, n_used, h)


def _expert_ffn_kernel(te_ref, nused_ref, x_ref, w1_ref, w3_ref, w2_ref, y_ref):
    i = pl.program_id(0)

    @pl.when(i < nused_ref[0])
    def _():
        y_ref[...] = _swiglu(x_ref[...].astype(BF16), w1_ref, w3_ref, w2_ref, lead=(0,))

    @pl.when(i >= nused_ref[0])
    def _():
        y_ref[...] = jnp.zeros_like(y_ref)


def _expert_ffn(xs, tile_expert, n_used, w1, w3, w2, te):
    n_rows = xs.shape[0]
    nt = n_rows // te
    wspec = lambda shape: pl.BlockSpec((1,) + shape, lambda i, te_r, nu_r: (te_r[i], 0, 0), pipeline_mode=RESIDENT)
    grid_spec = pltpu.PrefetchScalarGridSpec(
        num_scalar_prefetch=2,
        grid=(nt,),
        in_specs=[
            pl.BlockSpec((te, D_MODEL), lambda i, te_r, nu_r: (i, 0)),
            wspec((D_MODEL, D_FF)), wspec((D_MODEL, D_FF)), wspec((D_FF, D_MODEL)),
        ],
        out_specs=pl.BlockSpec((te, D_MODEL), lambda i, te_r, nu_r: (i, 0)),
    )
    return pl.pallas_call(
        _expert_ffn_kernel,
        grid_spec=grid_spec,
        out_shape=jax.ShapeDtypeStruct((n_rows, D_MODEL), F32),
        compiler_params=_params(("arbitrary",), VMEM_LIMIT_BIG),
        name="moe_expert_swiglu",
    )(tile_expert, n_used, xs, w1, w3, w2)


def _combine_kernel(pos_ref, x_ref, gate_ref, g_ref, y_ref, o_ref, buf0, buf1, sem, *, tm):
    def issue(r, carry):
        _row_copy(y_ref, pos_ref[2 * r], buf0, r, sem).start()
        _row_copy(y_ref, pos_ref[2 * r + 1], buf1, r, sem).start()
        return carry

    lax.fori_loop(0, tm, issue, 0, unroll=DMA_UNROLL)
    for buf in (buf0, buf1):
        pltpu.make_async_copy(y_ref.at[pl.ds(0, tm)], buf, sem).wait()
    gate = gate_ref[...]
    x = x_ref[...] + (buf0[...] * gate[:, 0:1] + buf1[...] * gate[:, 1:2])
    o_ref[...] = x * _rms_scale(x) * g_ref[...]


def _combine(x3, gates, pos_flat, y, g):
    R = x3.shape[0]
    tm = min(ROW_TILE, R)
    kern = functools.partial(_combine_kernel, tm=tm)
    return pl.pallas_call(
        kern,
        grid=(R // tm,),
        in_specs=[
            pl.BlockSpec((TOP_K * tm,), lambda i: (i,), memory_space=pltpu.SMEM),
            pl.BlockSpec((tm, D_MODEL), lambda i: (i, 0)),
            pl.BlockSpec((tm, TOP_K), lambda i: (i, 0)),
            pl.BlockSpec((1, D_MODEL), lambda i: (0, 0)),
            pl.BlockSpec(memory_space=pl.ANY),
        ],
        out_specs=pl.BlockSpec((tm, D_MODEL), lambda i: (i, 0)),
        out_shape=jax.ShapeDtypeStruct((R, D_MODEL), F32),
        scratch_shapes=[pltpu.VMEM((tm, D_MODEL), F32), pltpu.VMEM((tm, D_MODEL), F32),
                        pltpu.SemaphoreType.DMA(())],
        compiler_params=_params(("arbitrary",)),
        name="moe_combine_norm",
    )(pos_flat, x3, gates, g, y)


def _prep_weights(p):
    w_in = p["gdn_w_in"][0]
    w_kv = p["w_kv"]
    kcols = w_kv[:, :HKV * HD].reshape(D_MODEL, HKV, HD)
    vcols = w_kv[:, HKV * HD:].reshape(D_MODEL, HKV, HD)
    dup = lambda a: jnp.concatenate([a, a], axis=2).reshape(D_MODEL, HKV * LANES)
    w_kvd = jnp.concatenate([dup(kcols), dup(vcols)], axis=1)
    return dict(
        w_qk=w_in[:, :2 * QK_DIM].astype(BF16),
        w_v=w_in[:, 2 * QK_DIM:CONV_CH].astype(BF16),
        w_z=w_in[:, CONV_CH:CONV_CH + V_DIM].astype(BF16),
        w_ba=w_in[:, CONV_CH + V_DIM:].astype(BF16),
        w_out=p["gdn_w_out"][0].astype(BF16),
        ffn_w1=p["ffn_w1"][0].astype(BF16),
        ffn_w3=p["ffn_w3"][0].astype(BF16),
        ffn_w2=p["ffn_w2"][0].astype(BF16),
        w_kvd=w_kvd.astype(BF16),
        w_kvs=w_kv.astype(BF16),
        w_q=p["swa_w_q"][0].astype(BF16),
        w_o=p["swa_w_o"][0].astype(BF16),
        moe_w1=p["moe_w1"][0].astype(BF16),
        moe_w3=p["moe_w3"][0].astype(BF16),
        moe_w2=p["moe_w2"][0].astype(BF16),
    )


def _dup_heads(a):
    B, T = a.shape[:2]
    return jnp.concatenate([a, a], axis=3).reshape(B * T, HKV * LANES)


def _trunk(x, pos, conv0, s0, win_k, win_v, p, w):
    B, T, _ = x.shape
    R = B * T
    xf = x.reshape(R, D_MODEL)
    row = lambda a: a.reshape(1, -1)

    g_attn = row(p["norm_attn"][0])
    w_conv = p["gdn_w_conv"][0]
    nqk = 2 * QK_DIM
    qk, conv_qk = _qkv_conv(xf, g_attn, w["w_qk"], w_conv[:, :nqk], conv0[:, :, :nqk], B, T, True)
    v, conv_v = _qkv_conv(xf, g_attn, w["w_v"], w_conv[:, nqk:], conv0[:, :, nqk:], B, T, False)
    conv_new = jnp.concatenate([conv_qk, conv_v], axis=-1).reshape(B, -1, CONV_W - 1, CONV_CH)[:, -1]
    z, beta, gc, gct = _zbg(xf, g_attn, w["w_z"], w["w_ba"], p["gdn_a_log"][0], p["gdn_dt_bias"][0], B, T)
    u, wv, att_l, qg, kd = _delta_prep(qk, v, beta, gc, gct, B, T)
    og, s_new = _delta_scan(qg, kd, u, wv, att_l, z, gc, s0, row(p["gdn_o_norm"][0]), B, T)
    x2 = _mixer_out_ffn(xf, og, w["w_out"], row(p["norm_ffn"][0]), w["ffn_w1"], w["ffn_w3"], w["ffn_w2"])

    keep = min(WINDOW, T) if win_k is None else T
    q, kvd, kvs = _qkv_rope(x2, row(p["norm_attn"][1]), row(p["kv_norm"]), w["w_q"], w["w_kvd"], w["w_kvs"],
                            _rope_tables(pos), B, T, keep)
    k_new = kvs[:, :HKV * HD].reshape(B, keep, HKV, HD)
    v_new = kvs[:, HKV * HD:].reshape(B, keep, HKV, HD)
    if win_k is None:
        keys, Tk = kvd, T
    else:
        hist = jnp.concatenate([_dup_heads(win_k), _dup_heads(win_v)], axis=1).astype(BF16)
        Tw = win_k.shape[1]
        keys = jnp.concatenate([hist.reshape(B, Tw, KV_DUP), kvd.reshape(B, T, KV_DUP)], axis=1)
        Tk = Tw + T
        keys = keys.reshape(B * Tk, KV_DUP)
    att = _attention(q, keys, row(p["swa_sinks"][0]), B, T, Tk)
    x3, h, eidx, gates, counts = _oproj_router(x2, att, w["w_o"], row(p["norm_ffn"][1]), p["moe_router"][0])

    te = min(EXPERT_TILE, max(TOP_K * R // N_EXPERTS, LANES))
    n_tiles = TOP_K * R // te + N_EXPERTS
    cnt = counts[0].astype(jnp.int32)
    tiles = (cnt + te - 1) // te
    tile_end = jnp.cumsum(tiles)
    base = (tile_end - tiles) * te
    n_used = tile_end[-1:]
    tile_ids = jnp.arange(n_tiles, dtype=jnp.int32)
    tile_expert = jnp.minimum(jnp.sum((tile_end[None, :] <= tile_ids[:, None]).astype(jnp.int32), axis=1),
                              N_EXPERTS - 1)
    pos_slot = _slots(eidx, base.astype(F32).reshape(1, N_EXPERTS)).reshape(-1)
    n_used = n_used.astype(jnp.int32)
    xs = _dispatch(h, pos_slot, cnt, base.astype(jnp.int32), n_used, n_tiles, te)
    ys = _expert_ffn(xs, tile_expert, n_used, w["moe_w1"], w["moe_w3"], w["moe_w2"], te)
    y = _combine(x3, gates, pos_slot, ys, row(p["final_norm"]))
    return (y.reshape(B, T, D_MODEL), conv_new[None], s_new[None], k_new, v_new)


def kernel(x_prompt, x_sample, cache_conv, state_delta, cache_k, cache_v, norm_attn, norm_ffn, gdn_w_in,
           gdn_w_conv, gdn_a_log, gdn_dt_bias, gdn_o_norm, gdn_w_out, kv_norm, w_kv, swa_w_q, swa_sinks, swa_w_o,
           ffn_w1, ffn_w3, ffn_w2, moe_router, moe_w1, moe_w3, moe_w2, final_norm):
    p = dict(norm_attn=norm_attn, norm_ffn=norm_ffn, gdn_w_in=gdn_w_in, gdn_w_conv=gdn_w_conv,
             gdn_a_log=gdn_a_log, gdn_dt_bias=gdn_dt_bias, gdn_o_norm=gdn_o_norm, gdn_w_out=gdn_w_out,
             kv_norm=kv_norm, w_kv=w_kv, swa_w_q=swa_w_q, swa_sinks=swa_sinks, swa_w_o=swa_w_o,
             ffn_w1=ffn_w1, ffn_w3=ffn_w3, ffn_w2=ffn_w2, moe_router=moe_router,
             moe_w1=moe_w1, moe_w3=moe_w3, moe_w2=moe_w2, final_norm=final_norm)
    w = _prep_weights(p)
    Bp, Tp, _ = x_prompt.shape
    Bs, Ts, _ = x_sample.shape
    conv0 = jnp.zeros((Bp, CONV_W - 1, CONV_CH), F32)
    s0 = jnp.zeros((Bp, HV, DK, DV), F32)
    y_p, conv_p, delta_p, k_p, v_p = _trunk(x_prompt, jnp.arange(Tp, dtype=jnp.int32), conv0, s0,
                                            None, None, p, w)
    pos_s = PAST_LEN + jnp.arange(Ts, dtype=jnp.int32)
    y_s, conv_s, delta_s, k_s, v_s = _trunk(x_sample, pos_s, cache_conv[0], state_delta[0],
                                            cache_k, cache_v, p, w)
    return (y_p, y_s, conv_p, delta_p, k_p, v_p, conv_s, delta_s, k_s, v_s)
```

```python
import functools

import jax
import jax.numpy as jnp
import numpy as np
from jax import lax
from jax.experimental import pallas as pl
from jax.experimental.pallas import tpu as pltpu

F32 = jnp.float32
BF16 = jnp.bfloat16

D_MODEL = 1024
CHUNK = 64
HK = 8
HV = 16
DK = 128
DV = 128
QK_DIM = HK * DK
V_DIM = HV * DV
CONV_CH = 2 * QK_DIM + V_DIM
CONV_W = 4
HQ = 16
HKV = 4
HD = 64
WINDOW = 128
ROT_DIM = HD // 4
ROPE_THETA = 500000.0
D_FF = 2816
N_EXPERTS = 8
TOP_K = 2
EPS = 1e-6
PAST_LEN = 2048

LANES = 128
ROW_TILE = 512
EXPERT_TILE = 512
VMEM_LIMIT = 48 * 1024 * 1024
VMEM_LIMIT_BIG = 56 * 1024 * 1024


def _params(sem, vmem=VMEM_LIMIT):
    return pltpu.CompilerParams(dimension_semantics=sem, vmem_limit_bytes=vmem)


def _rms_scale(x):
    return lax.rsqrt(jnp.mean(x * x, axis=-1, keepdims=True) + EPS)


def _bdot(a, b):
    return jnp.dot(a.astype(BF16), b.astype(BF16), preferred_element_type=F32)


def _bdot_nt(a, b):
    return lax.dot_general(a.astype(BF16), b.astype(BF16), (((1,), (1,)), ((), ())),
                           preferred_element_type=F32)


def _bdot_tn(a, b):
    return lax.dot_general(a.astype(BF16), b.astype(BF16), (((0,), (0,)), ((), ())),
                           preferred_element_type=F32)


def _split3(x):
    x1 = x.astype(BF16)
    r1 = x - x1.astype(F32)
    x2 = r1.astype(BF16)
    x3 = (r1 - x2.astype(F32)).astype(BF16)
    return x1, x2, x3


def _split2(x):
    x1 = x.astype(BF16)
    x2 = (x - x1.astype(F32)).astype(BF16)
    return x1, x2


def _dot_hi(a, b):
    a1, a2 = _split2(a)
    b1, b2 = _split2(b)
    d = functools.partial(jnp.dot, preferred_element_type=F32)
    return d(a1, b1) + (d(a1, b2) + d(a2, b1))


def _spread(x, emat, pieces):
    parts = _split3(x)[:pieces]
    out = jnp.dot(parts[0], emat, preferred_element_type=F32)
    for part in parts[1:]:
        out = out + jnp.dot(part, emat, preferred_element_type=F32)
    return out


def _head_spread_matrices():
    h = np.arange(HV)[:, None]
    full = (np.arange(HV * DV)[None, :] // DV == h)
    pair = (np.arange(HV * CHUNK)[None, :] // CHUNK == h)
    return jnp.asarray(full.astype(np.float32), dtype=BF16), jnp.asarray(pair.astype(np.float32), dtype=BF16)


def _sigmoid(x):
    return 0.5 * jnp.tanh(0.5 * x) + 0.5


def _silu(x):
    half = 0.5 * x
    return half * jnp.tanh(half) + half


CONV_SUB = 256


def _qkv_conv_kernel(h_ref, w_ref, wc_ref, cs_ref, out_ref, cst_ref, pad_scr, carry_scr, *, tm, tn, l2):
    ti = pl.program_id(1)
    j = pl.program_id(2)

    @pl.when(ti == 0)
    def _():
        carry_scr[j, 5:8, :] = cs_ref[0]

    h = h_ref[...]
    wc = wc_ref[...]
    scale = jnp.where(j == 0, DK ** -0.5, 1.0).astype(F32)
    cols = lambda k: slice(k * CONV_SUB, (k + 1) * CONV_SUB)

    def project(k):
        return jnp.dot(h, w_ref[:, cols(k)], preferred_element_type=F32)

    def finish(k, proj):
        c = cols(k)
        pad_scr[0:8, c] = carry_scr[j, :, c]
        pad_scr[8:, c] = proj
        acc = proj * wc[3:4, c]
        for s in range(1, CONV_W):
            acc = acc + pad_scr[8 - s:8 - s + tm, c] * wc[3 - s:4 - s, c]
        carry_scr[j, :, c] = pad_scr[tm:tm + 8, c]
        cst_ref[0, :, c] = pad_scr[tm + 5:tm + 8, c]
        y = _silu(acc)
        if not l2:
            out_ref[:, c] = y.astype(out_ref.dtype)
            return
        for hh in range(CONV_SUB // DK):
            seg = y[:, hh * DK:(hh + 1) * DK]
            inv = lax.rsqrt(jnp.sum(seg * seg, axis=-1, keepdims=True) + EPS) * scale
            lanes = slice(k * CONV_SUB + hh * DK, k * CONV_SUB + (hh + 1) * DK)
            out_ref[:, lanes] = (seg * inv).astype(out_ref.dtype)

    pending = project(0)
    for k in range(tn // CONV_SUB):
        upcoming = project(k + 1) if k + 1 < tn // CONV_SUB else None
        finish(k, pending)
        pending = upcoming


def _qkv_conv(h, w, w_conv, conv_state, B, T, l2):
    R = B * T
    n = w.shape[1]
    tm = min(2 * ROW_TILE, T)
    tn = QK_DIM
    nt = T // tm
    nj = n // tn
    kern = functools.partial(_qkv_conv_kernel, tm=tm, tn=tn, l2=l2)
    return pl.pallas_call(
        kern,
        grid=(B, nt, nj),
        in_specs=[
            pl.BlockSpec((tm, D_MODEL), lambda b, t, j: (b * nt + t, 0)),
            pl.BlockSpec((D_MODEL, tn), lambda b, t, j: (0, j)),
            pl.BlockSpec((CONV_W, tn), lambda b, t, j: (0, j)),
            pl.BlockSpec((1, CONV_W - 1, tn), lambda b, t, j: (b, 0, j)),
        ],
        out_specs=[
            pl.BlockSpec((tm, tn), lambda b, t, j: (b * nt + t, j)),
            pl.BlockSpec((1, CONV_W - 1, tn), lambda b, t, j: (b * nt + t, 0, j)),
        ],
        out_shape=[
            jax.ShapeDtypeStruct((R, n), BF16),
            jax.ShapeDtypeStruct((B * nt, CONV_W - 1, n), F32),
        ],
        scratch_shapes=[
            pltpu.VMEM((tm + 8, tn), F32),
            pltpu.VMEM((nj, 8, tn), F32),
        ],
        compiler_params=_params(("arbitrary", "arbitrary", "arbitrary")),
        name="gdn_qk_conv" if l2 else "gdn_v_conv",
    )(h, w, w_conv, conv_state)


def _softplus(x):
    return jnp.maximum(x, 0.0) + jnp.log(1.0 + jnp.exp(-jnp.abs(x)))


def _zbg_kernel(x_ref, g_ref, w_ref, alog_ref, dtb_ref, tri_ref, z_ref, beta_ref, gc_ref, gct_ref, h_ref, *, tm):
    x = x_ref[...]
    h = (x * _rms_scale(x) * g_ref[...]).astype(BF16)
    h_ref[...] = h
    ba = jnp.dot(h, w_ref[:, V_DIM:], preferred_element_type=F32)
    z_ref[...] = jnp.dot(h, w_ref[:, :V_DIM], preferred_element_type=F32).astype(z_ref.dtype)
    beta_ref[...] = _sigmoid(ba[:, :HV])
    gcol = -jnp.exp(alog_ref[...]) * _softplus(ba[:, HV:2 * HV] + dtb_ref[...])
    tri = tri_ref[...]
    g1, g2, g3 = _split3(gcol)
    d = functools.partial(jnp.dot, preferred_element_type=F32)
    gc = d(tri, g1) + (d(tri, g2) + d(tri, g3))
    gc_ref[...] = gc
    r = lax.broadcasted_iota(jnp.int32, (HV, HV), 0)
    c = lax.broadcasted_iota(jnp.int32, (HV, HV), 1)
    sel = (c == jnp.where(r < HV // 2, 2 * r, 2 * r - (HV - 1))).astype(BF16)
    tn = lambda a: lax.dot_general(sel, a, (((1,), (1,)), ((), ())), preferred_element_type=F32)
    c1, c2, c3 = _split3(gc)
    gct = tn(c1) + (tn(c2) + tn(c3))
    for ch in range(tm // CHUNK):
        sl = slice(ch * CHUNK, (ch + 1) * CHUNK)
        gct_ref[0, ch] = jnp.concatenate([gct[:HV // 2, sl], gct[HV // 2:, sl]], axis=1)


def _zbg(x, g, w_z, w_ba, a_log, dt_bias, B, T):
    R = B * T
    tm = min(ROW_TILE, T)
    nt = T // tm
    nc = T // CHUNK
    idx = np.arange(tm)
    tri = jnp.asarray(((idx[:, None] // CHUNK == idx[None, :] // CHUNK) & (idx[:, None] >= idx[None, :]))
                      .astype(np.float32), dtype=BF16)
    w_zba = jnp.concatenate([w_z, w_ba, jnp.zeros((D_MODEL, LANES - 2 * HV), BF16)], axis=1)
    full = lambda shape: pl.BlockSpec(shape, lambda b, t: (0,) * len(shape))
    kern = functools.partial(_zbg_kernel, tm=tm)
    return pl.pallas_call(
        kern,
        grid=(B, nt),
        in_specs=[
            pl.BlockSpec((tm, D_MODEL), lambda b, t: (b * nt + t, 0)),
            full((1, D_MODEL)),
            full((D_MODEL, V_DIM + LANES)),
            full((1, HV)), full((1, HV)),
            full((tm, tm)),
        ],
        out_specs=[
            pl.BlockSpec((tm, V_DIM), lambda b, t: (b * nt + t, 0)),
            pl.BlockSpec((tm, HV), lambda b, t: (b * nt + t, 0)),
            pl.BlockSpec((tm, HV), lambda b, t: (b * nt + t, 0)),
            pl.BlockSpec((1, tm // CHUNK, HV // 2, 2 * CHUNK), lambda b, t: (b, t, 0, 0)),
            pl.BlockSpec((tm, D_MODEL), lambda b, t: (b * nt + t, 0)),
        ],
        out_shape=[
            jax.ShapeDtypeStruct((R, V_DIM), BF16),
            jax.ShapeDtypeStruct((R, HV), F32),
            jax.ShapeDtypeStruct((R, HV), F32),
            jax.ShapeDtypeStruct((B, nc, HV // 2, 2 * CHUNK), F32),
            jax.ShapeDtypeStruct((R, D_MODEL), BF16),
        ],
        compiler_params=_params(("arbitrary", "arbitrary")),
        name="gdn_z_beta_decay",
    )(x, g, w_zba, a_log.reshape(1, HV), dt_bias.reshape(1, HV), tri)


PAIR = HV // HK


def _pair_rows(x, lo):
    zero = jnp.zeros((), x.dtype)
    return jnp.concatenate([jnp.where(lo, x, zero), jnp.where(lo, zero, x)], axis=0)


def _pair_products(lhs_list, rhs_list, lo):
    def split(a):
        hi = a.astype(BF16)
        return hi, (a - hi.astype(F32)).astype(BF16)

    ops = []
    for lhs, rhs in zip(lhs_list, rhs_list):
        lh, ll = split(lhs)
        rh, rl = split(rhs)
        rh2 = _pair_rows(rh, lo)
        ops.append((jnp.concatenate([lh, lh, ll], axis=1), jnp.concatenate([rh2, _pair_rows(rl, lo), rh2], axis=0)))
    return [jnp.dot(a, b, preferred_element_type=F32) for a, b in ops]


def _pair_products_bf16(lhs_list, rhs_list, lo):
    ops = [(lhs.astype(BF16), _pair_rows(rhs.astype(BF16), lo)) for lhs, rhs in zip(lhs_list, rhs_list)]
    return [jnp.dot(a, b, preferred_element_type=F32) for a, b in ops]


INV_BASE = 16


def _pair_unit_lower_inverse(low_list, lo, eye2, same_block):
    C = low_list[0].shape[0]
    base = same_block(INV_BASE)
    xs = [jnp.where(base, -low, 0.0) for low in low_list]
    accs = [eye2 + x for x in xs]
    pws = _pair_products(xs, xs, lo)
    n = 2
    while 2 * n < INV_BASE:
        prods = _pair_products([jnp.concatenate([a, p], axis=0) for a, p in zip(accs, pws)], pws, lo)
        accs = [a + pr[:C] for a, pr in zip(accs, prods)]
        pws = [pr[C:] for pr in prods]
        n *= 2
    prods = _pair_products(accs, pws, lo)
    tinvs = [a + pr for a, pr in zip(accs, prods)]
    size = 2 * INV_BASE
    while size <= C:
        below = same_block(size) & jnp.logical_not(same_block(size // 2))
        offs = [jnp.where(below, low, 0.0) for low in low_list]
        inner = _pair_products_bf16(offs, tinvs, lo)
        outer = _pair_products_bf16(tinvs, inner, lo)
        tinvs = [t - o for t, o in zip(tinvs, outer)]
        size *= 2
    return tinvs


def _delta_prep_kernel(q_ref, k_ref, v_ref, beta_ref, gc_ref, gct_ref, epair_ref,
                       u_ref, w_ref, a_ref, qg_ref, kd_ref, *, cb):
    C = CHUNK
    ri = lax.broadcasted_iota(jnp.int32, (C, 2 * C), 0)
    li = lax.broadcasted_iota(jnp.int32, (C, 2 * C), 1)
    lm = li & (C - 1)
    lo = li < C
    incl = ri >= lm
    strict = ri > lm
    eye2 = (ri == lm).astype(F32)
    units = [(c, p) for c in range(cb) for p in range(HK)]
    rows = lambda c: slice(c * C, (c + 1) * C)

    grams = []
    for c, p in units:
        kb = k_ref[rows(c), p * DK:(p + 1) * DK]
        qb = q_ref[rows(c), p * DK:(p + 1) * DK]
        grams.append(lax.dot_general(jnp.concatenate([kb, qb], axis=0), jnp.concatenate([kb, kb], axis=0),
                                     (((1,), (1,)), ((), ())), preferred_element_type=F32))

    epair = epair_ref[...]
    g2s, b2s, betas, begs, egs, kgs = [], [], [], [], [], []
    for c in range(cb):
        beta = beta_ref[rows(c), :]
        gc = gc_ref[rows(c), :]
        g2s.append(_spread(gc, epair, 3))
        b2s.append(_spread(beta, epair, 2))
        betas.append(beta)
        egs.append(jnp.exp(gc))
        begs.append(beta * egs[c])
        kgs.append(jnp.exp(gc[C - 1:C, :] - gc))

    lows = []
    for (c, p), gram in zip(units, grams):
        pt = slice(p * 2 * C, (p + 1) * 2 * C)
        grow2 = gct_ref[c, p:p + 1, :]
        decay2 = jnp.exp(jnp.where(incl, g2s[c][:, pt] - grow2, -jnp.inf))
        a_ref[rows(c), pt] = (gram[C:] * decay2).astype(a_ref.dtype)
        lows.append(jnp.where(strict, b2s[c][:, pt] * gram[:C] * decay2, 0.0))

    same_block = lambda s: (ri // s) == (lm // s)
    tinvs = _pair_unit_lower_inverse(lows, lo, eye2, same_block)

    ops = []
    for (c, p), tinv in zip(units, tinvs):
        kf = k_ref[rows(c), p * DK:(p + 1) * DK].astype(F32)
        qf = q_ref[rows(c), p * DK:(p + 1) * DK].astype(F32)
        rhs = []
        for h in (PAIR * p, PAIR * p + 1):
            ht = slice(h * DV, (h + 1) * DV)
            vf = v_ref[rows(c), ht].astype(F32)
            rhs.append(jnp.concatenate([vf * betas[c][:, h:h + 1], kf * begs[c][:, h:h + 1]], axis=1))
            qg_ref[rows(c), ht] = (qf * egs[c][:, h:h + 1]).astype(qg_ref.dtype)
            kd_ref[rows(c), ht] = (kf * kgs[c][:, h:h + 1]).astype(kd_ref.dtype)
        ops.append((_pair_rows(tinv.astype(BF16), lo), jnp.concatenate(rhs, axis=0).astype(BF16)))
    for (c, p), (lhs, rhs) in zip(units, ops):
        uw = jnp.dot(lhs, rhs, preferred_element_type=F32)
        for i, h in enumerate((PAIR * p, PAIR * p + 1)):
            u_ref[rows(c), h * DV:(h + 1) * DV] = uw[i * C:(i + 1) * C, :DV].astype(u_ref.dtype)
            w_ref[rows(c), h * DV:(h + 1) * DV] = uw[i * C:(i + 1) * C, DV:].astype(w_ref.dtype)


def _delta_prep(qk, v, beta, gc, gct, B, T):
    R = B * T
    cb = 4 if T % (4 * CHUNK) == 0 else (2 if T % (2 * CHUNK) == 0 else 1)
    rb = cb * CHUNK
    gct2 = gct.reshape(R // CHUNK, HK, 2 * CHUNK)
    kern = functools.partial(_delta_prep_kernel, cb=cb)
    return pl.pallas_call(
        kern,
        grid=(R // rb,),
        in_specs=[
            pl.BlockSpec((rb, QK_DIM), lambda i: (i, 0)),
            pl.BlockSpec((rb, QK_DIM), lambda i: (i, 1)),
            pl.BlockSpec((rb, V_DIM), lambda i: (i, 0)),
            pl.BlockSpec((rb, HV), lambda i: (i, 0)),
            pl.BlockSpec((rb, HV), lambda i: (i, 0)),
            pl.BlockSpec((cb, HK, 2 * CHUNK), lambda i: (i, 0, 0)),
            pl.BlockSpec((HV, HV * CHUNK), lambda i: (0, 0)),
        ],
        out_specs=[
            pl.BlockSpec((rb, V_DIM), lambda i: (i, 0)),
            pl.BlockSpec((rb, V_DIM), lambda i: (i, 0)),
            pl.BlockSpec((rb, HV * CHUNK), lambda i: (i, 0)),
            pl.BlockSpec((rb, V_DIM), lambda i: (i, 0)),
            pl.BlockSpec((rb, V_DIM), lambda i: (i, 0)),
        ],
        out_shape=[
            jax.ShapeDtypeStruct((R, V_DIM), BF16),
            jax.ShapeDtypeStruct((R, V_DIM), BF16),
            jax.ShapeDtypeStruct((R, HV * CHUNK), BF16),
            jax.ShapeDtypeStruct((R, V_DIM), BF16),
            jax.ShapeDtypeStruct((R, V_DIM), BF16),
        ],
        compiler_params=_params(("arbitrary",)),
        name="gdn_delta_prep",
    )(qk, qk, v, beta, gc, gct2, _head_spread_matrices()[1])


def _delta_scan_kernel(qg_ref, kd_ref, u_ref, w_ref, a_ref, z_ref, gc_ref, s0_ref, on_ref, efull_ref,
                       og_ref, s_ref, *, bb):
    c = pl.program_id(1)

    @pl.when(c == 0)
    def _():
        s_ref[...] = s0_ref[...]

    C = CHUNK
    lo = lax.broadcasted_iota(jnp.int32, (C, 2 * C), 1) < C
    onorm = on_ref[...]
    units = [(b, h) for b in range(bb) for h in range(HV)]
    pairs = [(b, p) for b in range(bb) for p in range(HK)]

    efull = efull_ref[...]
    dgs = [_spread(jnp.exp(gc_ref[b, C - 8:C, :]), efull, 3)[7:8, :] for b in range(bb)]

    ops = []
    kds = []
    dec = []
    for b, h in units:
        ht = slice(h * DV, (h + 1) * DV)
        kds.append(kd_ref[b, :, ht])
        dec.append(dgs[b][:, ht])
        ops.append((jnp.concatenate([w_ref[b, :, ht], qg_ref[b, :, ht]], axis=0), s_ref[b, h].astype(BF16)))
    wqs = [jnp.dot(a, s, preferred_element_type=F32) for a, s in ops]

    vns = [(u_ref[b, :, h * DV:(h + 1) * DV].astype(F32) - wq[:C]).astype(BF16) for (b, h), wq in zip(units, wqs)]

    oparts = []
    for i, (b, p) in enumerate(pairs):
        att = _pair_rows(a_ref[b, :, p * 2 * C:(p + 1) * 2 * C], lo)
        vn2 = jnp.concatenate([vns[PAIR * i], vns[PAIR * i + 1]], axis=0)
        oparts.append(jnp.dot(att, vn2, preferred_element_type=F32))
    sds = [lax.dot_general(kd, vn, (((0,), (0,)), ((), ())), preferred_element_type=F32)
           for kd, vn in zip(kds, vns)]

    for i, (b, h) in enumerate(units):
        s_ref[b, h] = s_ref[b, h] * dec[i] + sds[i]
        o = wqs[i][C:] + oparts[i // PAIR][(h % PAIR) * C:(h % PAIR + 1) * C]
        on = o * _rms_scale(o) * onorm
        zf = z_ref[b, :, h * DV:(h + 1) * DV].astype(F32)
        og_ref[b, :, h * DV:(h + 1) * DV] = (on * _silu(zf)).astype(og_ref.dtype)


def _delta_scan(qg, kd, u, w, att, z, gc, s0, o_norm, B, T):
    nc = T // CHUNK
    bb = 4 if B % 4 == 0 else (2 if B % 2 == 0 else 1)
    v3 = lambda a: a.reshape(B, T, a.shape[-1])
    blk = lambda n, j=0: pl.BlockSpec((bb, CHUNK, n), lambda b, c: (b, c, j))
    kern = functools.partial(_delta_scan_kernel, bb=bb)
    og, s_new = pl.pallas_call(
        kern,
        grid=(B // bb, nc),
        in_specs=[
            blk(V_DIM), blk(V_DIM), blk(V_DIM), blk(V_DIM), blk(HV * CHUNK), blk(V_DIM), blk(HV),
            pl.BlockSpec((bb, HV, DK, DV), lambda b, c: (b, 0, 0, 0)),
            pl.BlockSpec((1, DV), lambda b, c: (0, 0)),
            pl.BlockSpec((HV, HV * DV), lambda b, c: (0, 0)),
        ],
        out_specs=[
            blk(V_DIM),
            pl.BlockSpec((bb, HV, DK, DV), lambda b, c: (b, 0, 0, 0)),
        ],
        out_shape=[
            jax.ShapeDtypeStruct((B, T, V_DIM), BF16),
            jax.ShapeDtypeStruct((B, HV, DK, DV), F32),
        ],
        compiler_params=_params(("arbitrary", "arbitrary")),
        name="gdn_delta_scan",
    )(v3(qg), v3(kd), v3(u), v3(w), v3(att), v3(z), v3(gc), s0, o_norm, _head_spread_matrices()[0])
    return og.reshape(B * T, V_DIM), s_new


MXU_COLS = 256
FF_SPLITS = (6 * MXU_COLS, 5 * MXU_COLS)
assert sum(FF_SPLITS) == D_FF
RESIDENT = pl.Buffered(1)


def _swiglu(h, w1_ref, w3_ref, w2_ref, lead=()):
    out = None
    start = 0
    for width in FF_SPLITS:
        sl = slice(start, start + width)
        start += width
        a = jnp.dot(h, w1_ref[lead + (slice(None), sl)], preferred_element_type=F32)
        b = jnp.dot(h, w3_ref[lead + (slice(None), sl)], preferred_element_type=F32)
        mid = (_silu(a) * b).astype(BF16)
        part = jnp.dot(mid, w2_ref[lead + (sl, slice(None))], preferred_element_type=F32)
        out = part if out is None else out + part
    return out


def _mixer_out_ffn_kernel(x_ref, a_ref, wo_ref, g_ref, w1_ref, w3_ref, w2_ref, o_ref):
    x1 = x_ref[...] + jnp.dot(a_ref[...], wo_ref[...], preferred_element_type=F32)
    h = (x1 * _rms_scale(x1) * g_ref[...]).astype(BF16)
    o_ref[...] = x1 + _swiglu(h, w1_ref, w3_ref, w2_ref)


def _mixer_out_ffn(x, a, w_out, g, w1, w3, w2):
    R = x.shape[0]
    K = a.shape[1]
    tm = min(ROW_TILE, R)
    held = lambda shape: pl.BlockSpec(shape, lambda i: (0,) * len(shape), pipeline_mode=RESIDENT)
    return pl.pallas_call(
        _mixer_out_ffn_kernel,
        grid=(R // tm,),
        in_specs=[
            pl.BlockSpec((tm, D_MODEL), lambda i: (i, 0)),
            pl.BlockSpec((tm, K), lambda i: (i, 0)),
            held((K, D_MODEL)),
            held((1, D_MODEL)),
            held((D_MODEL, D_FF)), held((D_MODEL, D_FF)), held((D_FF, D_MODEL)),
        ],
        out_specs=pl.BlockSpec((tm, D_MODEL), lambda i: (i, 0)),
        out_shape=jax.ShapeDtypeStruct((R, D_MODEL), F32),
        compiler_params=_params(("arbitrary",), VMEM_LIMIT_BIG),
        name="gdn_out_dense_swiglu",
    )(x, a, w_out, g, w1, w3, w2)


KV_DUP = 2 * HKV * LANES
KV_STD = 2 * HKV * HD


def _rope(x, cos, sm, sp):
    up = pltpu.roll(x, LANES - ROT_DIM // 2, 1)
    dn = pltpu.roll(x, ROT_DIM // 2, 1)
    return x * cos + up * sm + dn * sp


def _qkv_rope_kernel(x_ref, gq_ref, gkv_ref, wq_ref, wkv_ref, wkvs_ref, cos_ref, sm_ref, sp_ref,
                     q_ref, kvd_ref, kvs_ref, *, nt, keep):
    x = x_ref[...]
    xr = x * _rms_scale(x)
    hq = (xr * gq_ref[...]).astype(BF16)
    hkv = (xr * gkv_ref[...]).astype(BF16)
    cos = cos_ref[...]
    sm = sm_ref[...]
    sp = sp_ref[...]
    q = jnp.dot(hq, wq_ref[...], preferred_element_type=F32)
    for t in range(HQ * HD // LANES):
        sl = slice(t * LANES, (t + 1) * LANES)
        q_ref[:, sl] = (_rope(q[:, sl], cos, sm, sp) * (HD ** -0.5)).astype(q_ref.dtype)
    kv = jnp.dot(hkv, wkv_ref[...], preferred_element_type=F32)
    half = KV_DUP // 2
    for t in range(HKV):
        sl = slice(t * LANES, (t + 1) * LANES)
        kvd_ref[:, sl] = _rope(kv[:, sl], cos, sm, sp).astype(kvd_ref.dtype)
    kvd_ref[:, half:] = kv[:, half:].astype(kvd_ref.dtype)

    @pl.when(pl.program_id(0) % nt == nt - 1)
    def _():
        tm = x.shape[0]
        rows = slice(tm - keep, tm)
        kvs = jnp.dot(hkv[rows], wkvs_ref[...], preferred_element_type=F32)
        for t in range(HKV * HD // LANES):
            sl = slice(t * LANES, (t + 1) * LANES)
            kvs_ref[:, sl] = _rope(kvs[:, sl], cos[rows], sm[rows], sp[rows])
        kvs_ref[:, HKV * HD:] = kvs[:, HKV * HD:]


def _rope_tables(pos):
    half = ROT_DIM // 2
    inv = jnp.power(ROPE_THETA, -jnp.arange(half, dtype=F32) * 2.0 / ROT_DIM)
    ang = pos.astype(F32)[:, None] * inv[None, :]
    cos = jnp.cos(ang)
    sin = jnp.sin(ang)
    T = pos.shape[0]
    one = jnp.ones((T, HD - ROT_DIM), F32)
    zero = jnp.zeros((T, HD - ROT_DIM), F32)
    zh = jnp.zeros((T, half), F32)
    c64 = jnp.concatenate([cos, cos, one], axis=1)
    sm64 = jnp.concatenate([-sin, zh, zero], axis=1)
    sp64 = jnp.concatenate([zh, sin, zero], axis=1)
    dup = lambda a: jnp.concatenate([a, a], axis=1)
    return dup(c64), dup(sm64), dup(sp64)


def _qkv_rope(x, gq, gkv, w_q, w_kvd, w_kvs, tables, B, T, keep):
    R = B * T
    tm = min(2 * ROW_TILE, T)
    nt = T // tm
    assert keep <= tm
    tab = pl.BlockSpec((tm, LANES), lambda i: (i % nt, 0))
    kern = functools.partial(_qkv_rope_kernel, nt=nt, keep=keep)
    return pl.pallas_call(
        kern,
        grid=(R // tm,),
        in_specs=[
            pl.BlockSpec((tm, D_MODEL), lambda i: (i, 0)),
            pl.BlockSpec((1, D_MODEL), lambda i: (0, 0)),
            pl.BlockSpec((1, D_MODEL), lambda i: (0, 0)),
            pl.BlockSpec((D_MODEL, HQ * HD), lambda i: (0, 0)),
            pl.BlockSpec((D_MODEL, KV_DUP), lambda i: (0, 0)),
            pl.BlockSpec((D_MODEL, KV_STD), lambda i: (0, 0)),
            tab, tab, tab,
        ],
        out_specs=[
            pl.BlockSpec((tm, HQ * HD), lambda i: (i, 0)),
            pl.BlockSpec((tm, KV_DUP), lambda i: (i, 0)),
            pl.BlockSpec((keep, KV_STD), lambda i: (i // nt, 0)),
        ],
        out_shape=[
            jax.ShapeDtypeStruct((R, HQ * HD), BF16),
            jax.ShapeDtypeStruct((R, KV_DUP), BF16),
            jax.ShapeDtypeStruct((B * keep, KV_STD), F32),
        ],
        compiler_params=_params(("arbitrary",)),
        name="swa_qkv_rope",
    )(x, gq, gkv, w_q, w_kvd, w_kvs, *tables)


KEY_CHUNKS = WINDOW // CHUNK + 1
KEY_PAD = 256


def _attn_kernel(*refs, off, cq):
    q_ref = refs[0]
    krefs = refs[1:KEY_CHUNKS + cq]
    sink_ref, ones_ref, o_ref = refs[KEY_CHUNKS + cq:]
    c = pl.program_id(1)
    lane = lax.broadcasted_iota(jnp.int32, (KEY_CHUNKS * CHUNK, LANES), 1)
    lo = lane < HD
    col = lax.broadcasted_iota(jnp.int32, (1, KEY_PAD), 1)
    sink_col = col == KEY_CHUNKS * CHUNK
    zpad = jnp.zeros((KEY_PAD - KEY_CHUNKS * CHUNK, LANES), BF16)
    zero = jnp.zeros((), BF16)
    tiles = HQ // HKV // 2
    ones_blk = ones_ref[...]
    units = [(u, g) for u in range(cq) for g in range(HKV)]

    vblks = []
    scores = []
    for u, g in units:
        kr = krefs[u:u + KEY_CHUNKS]
        kg = jnp.concatenate([r[:, g * LANES:(g + 1) * LANES] for r in kr], axis=0)
        vg = jnp.concatenate([r[:, KV_DUP // 2 + g * LANES:KV_DUP // 2 + (g + 1) * LANES] for r in kr], axis=0)
        kblk = jnp.concatenate([jnp.where(lo, kg, zero), zpad, jnp.where(lo, zero, kg), zpad], axis=0)
        vblk = jnp.concatenate([jnp.where(lo, vg, zero), zpad, jnp.where(lo, zero, vg), zpad], axis=0)
        vblks.append(jnp.concatenate([vblk, ones_blk], axis=1))
        qg = jnp.concatenate([q_ref[u * CHUNK:(u + 1) * CHUNK, (g * tiles + p) * LANES:(g * tiles + p + 1) * LANES]
                              for p in range(tiles)], axis=0)
        scores.append(lax.dot_general(qg, kblk, (((1,), (1,)), ((), ())), preferred_element_type=F32))

    probs = []
    for (u, g), sc in zip(units, scores):
        first = c * cq + u + off - (KEY_CHUNKS - 1)
        valid = (col < KEY_CHUNKS * CHUNK) & (col // CHUNK + first >= 0)
        base_bias = jnp.where(valid, 0.0, -jnp.inf).astype(F32)
        rows = []
        for p in range(tiles):
            t = g * tiles + p
            ps = []
            for hh in range(2):
                bias = jnp.where(sink_col, sink_ref[:, 2 * t + hh:2 * t + hh + 1], base_bias)
                sh = sc[p * CHUNK:(p + 1) * CHUNK, hh * KEY_PAD:(hh + 1) * KEY_PAD] + bias
                m = jnp.max(sh, axis=-1, keepdims=True)
                ps.append(jnp.exp(sh - m).astype(BF16))
            rows.append(jnp.concatenate(ps, axis=1))
        probs.append(jnp.concatenate(rows, axis=0))

    outs = [jnp.dot(pr, vb, preferred_element_type=F32) for pr, vb in zip(probs, vblks)]
    for (u, g), out in zip(units, outs):
        for p in range(tiles):
            t = g * tiles + p
            rows = slice(p * CHUNK, (p + 1) * CHUNK)
            o_ref[u * CHUNK:(u + 1) * CHUNK, t * LANES:(t + 1) * LANES] = (
                out[rows, :LANES] / out[rows, LANES:]).astype(o_ref.dtype)


def _attention(q, kvd, sinks, B, Tq, Tk):
    ncq = Tq // CHUNK
    nck = Tk // CHUNK
    off = nck - ncq
    cq = 4 if ncq % 4 == 0 else (2 if ncq % 2 == 0 else 1)
    kern = functools.partial(_attn_kernel, off=off, cq=cq)
    blk_row = np.arange(2 * KEY_PAD)[:, None] // KEY_PAD
    blk_lane = np.arange(LANES)[None, :] // HD
    ones_blk = jnp.asarray((blk_row == blk_lane).astype(np.float32), dtype=BF16)

    def kspec(i):
        return pl.BlockSpec((CHUNK, KV_DUP),
                            lambda b, c: (b * nck + jnp.maximum(c * cq + off - (KEY_CHUNKS - 1) + i, 0), 0))

    qspec = pl.BlockSpec((cq * CHUNK, HQ * HD), lambda b, c: (b * (ncq // cq) + c, 0))
    return pl.pallas_call(
        kern,
        grid=(B, ncq // cq),
        in_specs=[qspec] + [kspec(i) for i in range(KEY_CHUNKS + cq - 1)] + [
            pl.BlockSpec((1, HQ), lambda b, c: (0, 0)),
            pl.BlockSpec((2 * KEY_PAD, LANES), lambda b, c: (0, 0)),
        ],
        out_specs=qspec,
        out_shape=jax.ShapeDtypeStruct((B * Tq, HQ * HD), BF16),
        compiler_params=_params(("arbitrary", "arbitrary")),
        name="swa_attention",
    )(q, *([kvd] * (KEY_CHUNKS + cq - 1)), sinks, ones_blk)


def _oproj_router_kernel(x_ref, a_ref, w_ref, g_ref, wr_ref, x3_ref, h_ref, e_ref, gate_ref, cnt_ref):
    i = pl.program_id(0)
    x3 = x_ref[...] + jnp.dot(a_ref[...], w_ref[...], preferred_element_type=F32)
    x3_ref[...] = x3
    h = x3 * _rms_scale(x3) * g_ref[...]
    h_ref[...] = h
    h1, h2 = _split2(h)
    r1, r2 = _split2(wr_ref[...])
    part = jnp.dot(h1, jnp.concatenate([r1, r2], axis=1), preferred_element_type=F32)
    logits = part[:, :N_EXPERTS] + (part[:, N_EXPERTS:] + jnp.dot(h2, r1, preferred_element_type=F32))
    eidx = lax.broadcasted_iota(jnp.int32, logits.shape, 1)
    m1 = jnp.max(logits, axis=-1, keepdims=True)
    i1 = jnp.min(jnp.where(logits == m1, eidx, N_EXPERTS), axis=-1, keepdims=True)
    rest = jnp.where(eidx == i1, -jnp.inf, logits)
    m2 = jnp.max(rest, axis=-1, keepdims=True)
    i2 = jnp.min(jnp.where(rest == m2, eidx, N_EXPERTS), axis=-1, keepdims=True)
    e2 = jnp.exp(m2 - m1)
    den = 1.0 + e2
    e_ref[...] = jnp.concatenate([i1, i2], axis=1)
    gate_ref[...] = jnp.concatenate([1.0 / den, e2 / den], axis=1)
    hot = ((eidx == i1) | (eidx == i2)).astype(F32)

    @pl.when(i == 0)
    def _():
        cnt_ref[...] = jnp.zeros_like(cnt_ref)

    cnt_ref[...] += jnp.sum(hot, axis=0, keepdims=True)


def _oproj_router(x, a, w_o, g, w_r):
    R = x.shape[0]
    tm = min(ROW_TILE, R)
    row = lambda n: pl.BlockSpec((tm, n), lambda i: (i, 0))
    return pl.pallas_call(
        _oproj_router_kernel,
        grid=(R // tm,),
        in_specs=[
            row(D_MODEL), row(HQ * HD),
            pl.BlockSpec((HQ * HD, D_MODEL), lambda i: (0, 0)),
            pl.BlockSpec((1, D_MODEL), lambda i: (0, 0)),
            pl.BlockSpec((D_MODEL, N_EXPERTS), lambda i: (0, 0)),
        ],
        out_specs=[row(D_MODEL), row(D_MODEL), row(TOP_K), row(TOP_K),
                   pl.BlockSpec((1, N_EXPERTS), lambda i: (0, 0))],
        out_shape=[
            jax.ShapeDtypeStruct((R, D_MODEL), F32),
            jax.ShapeDtypeStruct((R, D_MODEL), F32),
            jax.ShapeDtypeStruct((R, TOP_K), jnp.int32),
            jax.ShapeDtypeStruct((R, TOP_K), F32),
            jax.ShapeDtypeStruct((1, N_EXPERTS), F32),
        ],
        compiler_params=_params(("arbitrary",)),
        name="swa_oproj_router",
    )(x, a, w_o, g, w_r)


def _slot_kernel(e_ref, base_ref, tri_ref, pos_ref, run_scr):
    i = pl.program_id(0)

    @pl.when(i == 0)
    def _():
        run_scr[...] = jnp.zeros_like(run_scr)

    e = e_ref[...]
    tm = e.shape[0]
    eidx = lax.broadcasted_iota(jnp.int32, (tm, N_EXPERTS), 1)
    hot0 = eidx == e[:, 0:1]
    hot1 = eidx == e[:, 1:2]
    hot = (hot0 | hot1).astype(BF16)
    before = jnp.dot(tri_ref[...], hot, preferred_element_type=F32)
    dest = before + run_scr[...] + base_ref[...]
    p0 = jnp.sum(jnp.where(hot0, dest, 0.0), axis=-1, keepdims=True)
    p1 = jnp.sum(jnp.where(hot1, dest, 0.0), axis=-1, keepdims=True)
    pos_ref[...] = jnp.concatenate([p0, p1], axis=1).astype(jnp.int32)
    run_scr[...] += jnp.sum(hot.astype(F32), axis=0, keepdims=True)


def _slots(eidx, base):
    R = eidx.shape[0]
    tm = min(ROW_TILE, R)
    idx = np.arange(tm)
    tri = jnp.asarray((idx[:, None] > idx[None, :]).astype(np.float32), dtype=BF16)
    return pl.pallas_call(
        _slot_kernel,
        grid=(R // tm,),
        in_specs=[
            pl.BlockSpec((tm, TOP_K), lambda i: (i, 0)),
            pl.BlockSpec((1, N_EXPERTS), lambda i: (0, 0)),
            pl.BlockSpec((tm, tm), lambda i: (0, 0)),
        ],
        out_specs=pl.BlockSpec((tm, TOP_K), lambda i: (i, 0)),
        out_shape=jax.ShapeDtypeStruct((R, TOP_K), jnp.int32),
        scratch_shapes=[pltpu.VMEM((1, N_EXPERTS), F32)],
        compiler_params=_params(("arbitrary",)),
        name="moe_slots",
    )(eidx, base, tri)


DMA_UNROLL = 8


def _row_copy(src, i, dst, j, sem):
    return pltpu.make_async_copy(src.at[pl.ds(i, 1)], dst.at[pl.ds(j, 1)], sem)


def _dispatch_kernel(pos_ref, cnt_ref, base_ref, nused_ref, h_ref, xs_ref, zero_scr, sem, zsem, *, tm, te, n_tiles):
    i = pl.program_id(0)

    @pl.when(i == 0)
    def _():
        zero_scr[...] = jnp.zeros_like(zero_scr)

        def tile_copy(t):
            dst = xs_ref.at[pl.ds(pl.multiple_of(t * te, te), te)]
            return pltpu.make_async_copy(zero_scr, dst, zsem)

        def issue_tile(t, carry):
            tile_copy(t).start()
            return carry

        def drain_tile(t, carry):
            tile_copy(t).wait()
            return carry

        lax.fori_loop(nused_ref[0], n_tiles, issue_tile, 0)
        lax.fori_loop(nused_ref[0], n_tiles, drain_tile, 0)
        for e in range(N_EXPERTS):
            n = cnt_ref[e]
            start = base_ref[e] + n
            npad = (te - n % te) % te

            def issue(r, carry):
                _row_copy(zero_scr, 0, xs_ref, start + r, zsem).start()
                return carry

            def drain(r, carry):
                _row_copy(zero_scr, 0, xs_ref, start + r, zsem).wait()
                return carry

            lax.fori_loop(0, npad, issue, 0)
            lax.fori_loop(0, npad, drain, 0)

    def issue(r, carry):
        _row_copy(h_ref, r, xs_ref, pos_ref[2 * r], sem).start()
        _row_copy(h_ref, r, xs_ref, pos_ref[2 * r + 1], sem).start()
        return carry

    lax.fori_loop(0, tm, issue, 0, unroll=DMA_UNROLL)
    for _ in range(TOP_K):
        pltpu.make_async_copy(h_ref, xs_ref.at[pl.ds(0, tm)], sem).wait()


def _dispatch(h, pos_flat, counts, base, n_used, n_tiles, te):
    R = h.shape[0]
    tm = min(ROW_TILE, R)
    kern = functools.partial(_dispatch_kernel, tm=tm, te=te, n_tiles=n_tiles)
    smem = pl.BlockSpec(memory_space=pltpu.SMEM)
    return pl.pallas_call(
        kern,
        grid=(R // tm,),
        in_specs=[
            pl.BlockSpec((TOP_K * tm,), lambda i: (i,), memory_space=pltpu.SMEM),
            smem, smem, smem,
            pl.BlockSpec((tm, D_MODEL), lambda i: (i, 0)),
        ],
        out_specs=pl.BlockSpec(memory_space=pl.ANY),
        out_shape=jax.ShapeDtypeStruct((n_tiles * te, D_MODEL), F32),
        scratch_shapes=[pltpu.VMEM((te, D_MODEL), F32), pltpu.SemaphoreType.DMA(()), pltpu.SemaphoreType.DMA(())],
        compiler_params=pltpu.CompilerParams(dimension_semantics=("arbitrary",), vmem_limit_bytes=VMEM_LIMIT,
                                             has_side_effects=True),
        name="moe_dispatch",
    )(pos_flat, counts, base, n_used, h)


def _expert_ffn_kernel(te_ref, nused_ref, x_ref, w1_ref, w3_ref, w2_ref, y_ref):
    i = pl.program_id(0)

    @pl.when(i < nused_ref[0])
    def _():
        y_ref[...] = _swiglu(x_ref[...].astype(BF16), w1_ref, w3_ref, w2_ref, lead=(0,))

    @pl.when(i >= nused_ref[0])
    def _():
        y_ref[...] = jnp.zeros_like(y_ref)


def _expert_ffn(xs, tile_expert, n_used, w1, w3, w2, te):
    n_rows = xs.shape[0]
    nt = n_rows // te
    wspec = lambda shape: pl.BlockSpec((1,) + shape, lambda i, te_r, nu_r: (te_r[i], 0, 0), pipeline_mode=RESIDENT)
    grid_spec = pltpu.PrefetchScalarGridSpec(
        num_scalar_prefetch=2,
        grid=(nt,),
        in_specs=[
            pl.BlockSpec((te, D_MODEL), lambda i, te_r, nu_r: (i, 0)),
            wspec((D_MODEL, D_FF)), wspec((D_MODEL, D_FF)), wspec((D_FF, D_MODEL)),
        ],
        out_specs=pl.BlockSpec((te, D_MODEL), lambda i, te_r, nu_r: (i, 0)),
    )
    return pl.pallas_call(
        _expert_ffn_kernel,
        grid_spec=grid_spec,
        out_shape=jax.ShapeDtypeStruct((n_rows, D_MODEL), F32),
        compiler_params=_params(("arbitrary",), VMEM_LIMIT_BIG),
        name="moe_expert_swiglu",
    )(tile_expert, n_used, xs, w1, w3, w2)


def _combine_kernel(pos_ref, x_ref, gate_ref, g_ref, y_ref, o_ref, buf0, buf1, sem, *, tm):
    def issue(r, carry):
        _row_copy(y_ref, pos_ref[2 * r], buf0, r, sem).start()
        _row_copy(y_ref, pos_ref[2 * r + 1], buf1, r, sem).start()
        return carry

    lax.fori_loop(0, tm, issue, 0, unroll=DMA_UNROLL)
    for buf in (buf0, buf1):
        pltpu.make_async_copy(y_ref.at[pl.ds(0, tm)], buf, sem).wait()
    gate = gate_ref[...]
    x = x_ref[...] + (buf0[...] * gate[:, 0:1] + buf1[...] * gate[:, 1:2])
    o_ref[...] = x * _rms_scale(x) * g_ref[...]


def _combine(x3, gates, pos_flat, y, g):
    R = x3.shape[0]
    tm = min(ROW_TILE, R)
    kern = functools.partial(_combine_kernel, tm=tm)
    return pl.pallas_call(
        kern,
        grid=(R // tm,),
        in_specs=[
            pl.BlockSpec((TOP_K * tm,), lambda i: (i,), memory_space=pltpu.SMEM),
            pl.BlockSpec((tm, D_MODEL), lambda i: (i, 0)),
            pl.BlockSpec((tm, TOP_K), lambda i: (i, 0)),
            pl.BlockSpec((1, D_MODEL), lambda i: (0, 0)),
            pl.BlockSpec(memory_space=pl.ANY),
        ],
        out_specs=pl.BlockSpec((tm, D_MODEL), lambda i: (i, 0)),
        out_shape=jax.ShapeDtypeStruct((R, D_MODEL), F32),
        scratch_shapes=[pltpu.VMEM((tm, D_MODEL), F32), pltpu.VMEM((tm, D_MODEL), F32),
                        pltpu.SemaphoreType.DMA(())],
        compiler_params=_params(("arbitrary",)),
        name="moe_combine_norm",
    )(pos_flat, x3, gates, g, y)


def _prep_weights(p):
    w_in = p["gdn_w_in"][0]
    w_kv = p["w_kv"]
    kcols = w_kv[:, :HKV * HD].reshape(D_MODEL, HKV, HD)
    vcols = w_kv[:, HKV * HD:].reshape(D_MODEL, HKV, HD)
    dup = lambda a: jnp.concatenate([a, a], axis=2).reshape(D_MODEL, HKV * LANES)
    w_kvd = jnp.concatenate([dup(kcols), dup(vcols)], axis=1)
    return dict(
        w_qk=w_in[:, :2 * QK_DIM].astype(BF16),
        w_v=w_in[:, 2 * QK_DIM:CONV_CH].astype(BF16),
        w_z=w_in[:, CONV_CH:CONV_CH + V_DIM].astype(BF16),
        w_ba=w_in[:, CONV_CH + V_DIM:].astype(BF16),
        w_out=p["gdn_w_out"][0].astype(BF16),
        ffn_w1=p["ffn_w1"][0].astype(BF16),
        ffn_w3=p["ffn_w3"][0].astype(BF16),
        ffn_w2=p["ffn_w2"][0].astype(BF16),
        w_kvd=w_kvd.astype(BF16),
        w_kvs=w_kv.astype(BF16),
        w_q=p["swa_w_q"][0].astype(BF16),
        w_o=p["swa_w_o"][0].astype(BF16),
        moe_w1=p["moe_w1"][0].astype(BF16),
        moe_w3=p["moe_w3"][0].astype(BF16),
        moe_w2=p["moe_w2"][0].astype(BF16),
    )


def _dup_heads(a):
    B, T = a.shape[:2]
    return jnp.concatenate([a, a], axis=3).reshape(B * T, HKV * LANES)


def _trunk(x, pos, conv0, s0, win_k, win_v, p, w):
    B, T, _ = x.shape
    R = B * T
    xf = x.reshape(R, D_MODEL)
    row = lambda a: a.reshape(1, -1)

    g_attn = row(p["norm_attn"][0])
    w_conv = p["gdn_w_conv"][0]
    nqk = 2 * QK_DIM
    z, beta, gc, gct, hn = _zbg(xf, g_attn, w["w_z"], w["w_ba"], p["gdn_a_log"][0], p["gdn_dt_bias"][0], B, T)
    qk, conv_qk = _qkv_conv(hn, w["w_qk"], w_conv[:, :nqk], conv0[:, :, :nqk], B, T, True)
    v, conv_v = _qkv_conv(hn, w["w_v"], w_conv[:, nqk:], conv0[:, :, nqk:], B, T, False)
    conv_new = jnp.concatenate([conv_qk, conv_v], axis=-1).reshape(B, -1, CONV_W - 1, CONV_CH)[:, -1]
    u, wv, att_l, qg, kd = _delta_prep(qk, v, beta, gc, gct, B, T)
    og, s_new = _delta_scan(qg, kd, u, wv, att_l, z, gc, s0, row(p["gdn_o_norm"][0]), B, T)
    x2 = _mixer_out_ffn(xf, og, w["w_out"], row(p["norm_ffn"][0]), w["ffn_w1"], w["ffn_w3"], w["ffn_w2"])

    keep = min(WINDOW, T) if win_k is None else T
    q, kvd, kvs = _qkv_rope(x2, row(p["norm_attn"][1]), row(p["kv_norm"]), w["w_q"], w["w_kvd"], w["w_kvs"],
                            _rope_tables(pos), B, T, keep)
    k_new = kvs[:, :HKV * HD].reshape(B, keep, HKV, HD)
    v_new = kvs[:, HKV * HD:].reshape(B, keep, HKV, HD)
    if win_k is None:
        keys, Tk = kvd, T
    else:
        hist = jnp.concatenate([_dup_heads(win_k), _dup_heads(win_v)], axis=1).astype(BF16)
        Tw = win_k.shape[1]
        keys = jnp.concatenate([hist.reshape(B, Tw, KV_DUP), kvd.reshape(B, T, KV_DUP)], axis=1)
        Tk = Tw + T
        keys = keys.reshape(B * Tk, KV_DUP)
    att = _attention(q, keys, row(p["swa_sinks"][0]), B, T, Tk)
    x3, h, eidx, gates, counts = _oproj_router(x2, att, w["w_o"], row(p["norm_ffn"][1]), p["moe_router"][0])

    te = min(EXPERT_TILE, max(TOP_K * R // N_EXPERTS, LANES))
    n_tiles = TOP_K * R // te + N_EXPERTS
    cnt = counts[0].astype(jnp.int32)
    tiles = (cnt + te - 1) // te
    tile_end = jnp.cumsum(tiles)
    base = (tile_end - tiles) * te
    n_used = tile_end[-1:]
    tile_ids = jnp.arange(n_tiles, dtype=jnp.int32)
    tile_expert = jnp.minimum(jnp.sum((tile_end[None, :] <= tile_ids[:, None]).astype(jnp.int32), axis=1),
                              N_EXPERTS - 1)
    pos_slot = _slots(eidx, base.astype(F32).reshape(1, N_EXPERTS)).reshape(-1)
    n_used = n_used.astype(jnp.int32)
    xs = _dispatch(h, pos_slot, cnt, base.astype(jnp.int32), n_used, n_tiles, te)
    ys = _expert_ffn(xs, tile_expert, n_used, w["moe_w1"], w["moe_w3"], w["moe_w2"], te)
    y = _combine(x3, gates, pos_slot, ys, row(p["final_norm"]))
    return (y.reshape(B, T, D_MODEL), conv_new[None], s_new[None], k_new, v_new)


def kernel(x_prompt, x_sample, cache_conv, state_delta, cache_k, cache_v, norm_attn, norm_ffn, gdn_w_in,
           gdn_w_conv, gdn_a_log, gdn_dt_bias, gdn_o_norm, gdn_w_out, kv_norm, w_kv, swa_w_q, swa_sinks, swa_w_o,
           ffn_w1, ffn_w3, ffn_w2, moe_router, moe_w1, moe_w3, moe_w2, final_norm):
    p = dict(norm_attn=norm_attn, norm_ffn=norm_ffn, gdn_w_in=gdn_w_in, gdn_w_conv=gdn_w_conv,
             gdn_a_log=gdn_a_log, gdn_dt_bias=gdn_dt_bias, gdn_o_norm=gdn_o_norm, gdn_w_out=gdn_w_out,
             kv_norm=kv_norm, w_kv=w_kv, swa_w_q=swa_w_q, swa_sinks=swa_sinks, swa_w_o=swa_w_o,
             ffn_w1=ffn_w1, ffn_w3=ffn_w3, ffn_w2=ffn_w2, moe_router=moe_router,
             moe_w1=moe_w1, moe_w3=moe_w3, moe_w2=moe_w2, final_norm=final_norm)
    w = _prep_weights(p)
    Bp, Tp, _ = x_prompt.shape
    Bs, Ts, _ = x_sample.shape
    conv0 = jnp.zeros((Bp, CONV_W - 1, CONV_CH), F32)
    s0 = jnp.zeros((Bp, HV, DK, DV), F32)
    y_p, conv_p, delta_p, k_p, v_p = _trunk(x_prompt, jnp.arange(Tp, dtype=jnp.int32), conv0, s0,
                                            None, None, p, w)
    pos_s = PAST_LEN + jnp.arange(Ts, dtype=jnp.int32)
    y_s, conv_s, delta_s, k_s, v_s = _trunk(x_sample, pos_s, cache_conv[0], state_delta[0],
                                            cache_k, cache_v, p, w)
    return (y_p, y_s, conv_p, delta_p, k_p, v_p, conv_s, delta_s, k_s, v_s)
```

```python
import functools

import jax
import jax.numpy as jnp
import numpy as np
from jax import lax
from jax.experimental import pallas as pl
from jax.experimental.pallas import tpu as pltpu

F32 = jnp.float32
BF16 = jnp.bfloat16

D_MODEL = 1024
CHUNK = 64
HK = 8
HV = 16
DK = 128
DV = 128
QK_DIM = HK * DK
V_DIM = HV * DV
CONV_CH = 2 * QK_DIM + V_DIM
CONV_W = 4
HQ = 16
HKV = 4
HD = 64
WINDOW = 128
ROT_DIM = HD // 4
ROPE_THETA = 500000.0
D_FF = 2816
N_EXPERTS = 8
TOP_K = 2
EPS = 1e-6
PAST_LEN = 2048

LANES = 128
ROW_TILE = 512
EXPERT_TILE = 512
VMEM_LIMIT = 48 * 1024 * 1024
VMEM_LIMIT_BIG = 56 * 1024 * 1024


def _params(sem, vmem=VMEM_LIMIT):
    return pltpu.CompilerParams(dimension_semantics=sem, vmem_limit_bytes=vmem)


def _rms_scale(x):
    return lax.rsqrt(jnp.mean(x * x, axis=-1, keepdims=True) + EPS)


def _bdot(a, b):
    return jnp.dot(a.astype(BF16), b.astype(BF16), preferred_element_type=F32)


def _bdot_nt(a, b):
    return lax.dot_general(a.astype(BF16), b.astype(BF16), (((1,), (1,)), ((), ())),
                           preferred_element_type=F32)


def _bdot_tn(a, b):
    return lax.dot_general(a.astype(BF16), b.astype(BF16), (((0,), (0,)), ((), ())),
                           preferred_element_type=F32)


def _split3(x):
    x1 = x.astype(BF16)
    r1 = x - x1.astype(F32)
    x2 = r1.astype(BF16)
    x3 = (r1 - x2.astype(F32)).astype(BF16)
    return x1, x2, x3


def _split2(x):
    x1 = x.astype(BF16)
    x2 = (x - x1.astype(F32)).astype(BF16)
    return x1, x2


def _dot_hi(a, b):
    a1, a2 = _split2(a)
    b1, b2 = _split2(b)
    d = functools.partial(jnp.dot, preferred_element_type=F32)
    return d(a1, b1) + (d(a1, b2) + d(a2, b1))


def _spread(x, emat, pieces):
    parts = _split3(x)[:pieces]
    out = jnp.dot(parts[0], emat, preferred_element_type=F32)
    for part in parts[1:]:
        out = out + jnp.dot(part, emat, preferred_element_type=F32)
    return out


def _head_spread_matrices():
    h = np.arange(HV)[:, None]
    full = (np.arange(HV * DV)[None, :] // DV == h)
    pair = (np.arange(HV * CHUNK)[None, :] // CHUNK == h)
    return jnp.asarray(full.astype(np.float32), dtype=BF16), jnp.asarray(pair.astype(np.float32), dtype=BF16)


def _sigmoid(x):
    return 0.5 * jnp.tanh(0.5 * x) + 0.5


def _silu(x):
    half = 0.5 * x
    return half * jnp.tanh(half) + half


CONV_SUB = 256


def _qkv_conv_kernel(h_ref, w_ref, wc_ref, cs_ref, out_ref, cst_ref, pad_scr, carry_scr, *, tm, tn, l2):
    ti = pl.program_id(1)
    j = pl.program_id(2)

    @pl.when(ti == 0)
    def _():
        carry_scr[j, 5:8, :] = cs_ref[0]

    h = h_ref[...]
    wc = wc_ref[...]
    scale = jnp.where(j == 0, DK ** -0.5, 1.0).astype(F32)
    cols = lambda k: slice(k * CONV_SUB, (k + 1) * CONV_SUB)

    def project(k):
        return jnp.dot(h, w_ref[:, cols(k)], preferred_element_type=F32)

    def finish(k, proj):
        c = cols(k)
        pad_scr[0:8, c] = carry_scr[j, :, c]
        pad_scr[8:, c] = proj
        acc = proj * wc[3:4, c]
        for s in range(1, CONV_W):
            acc = acc + pad_scr[8 - s:8 - s + tm, c] * wc[3 - s:4 - s, c]
        carry_scr[j, :, c] = pad_scr[tm:tm + 8, c]
        cst_ref[0, :, c] = pad_scr[tm + 5:tm + 8, c]
        y = _silu(acc)
        if not l2:
            out_ref[:, c] = y.astype(out_ref.dtype)
            return
        for hh in range(CONV_SUB // DK):
            seg = y[:, hh * DK:(hh + 1) * DK]
            inv = lax.rsqrt(jnp.sum(seg * seg, axis=-1, keepdims=True) + EPS) * scale
            lanes = slice(k * CONV_SUB + hh * DK, k * CONV_SUB + (hh + 1) * DK)
            out_ref[:, lanes] = (seg * inv).astype(out_ref.dtype)

    pending = project(0)
    for k in range(tn // CONV_SUB):
        upcoming = project(k + 1) if k + 1 < tn // CONV_SUB else None
        finish(k, pending)
        pending = upcoming


def _qkv_conv(h, w, w_conv, conv_state, B, T, l2):
    R = B * T
    n = w.shape[1]
    tm = min(2 * ROW_TILE, T)
    tn = QK_DIM
    nt = T // tm
    nj = n // tn
    kern = functools.partial(_qkv_conv_kernel, tm=tm, tn=tn, l2=l2)
    return pl.pallas_call(
        kern,
        grid=(B, nt, nj),
        in_specs=[
            pl.BlockSpec((tm, D_MODEL), lambda b, t, j: (b * nt + t, 0)),
            pl.BlockSpec((D_MODEL, tn), lambda b, t, j: (0, j)),
            pl.BlockSpec((CONV_W, tn), lambda b, t, j: (0, j)),
            pl.BlockSpec((1, CONV_W - 1, tn), lambda b, t, j: (b, 0, j)),
        ],
        out_specs=[
            pl.BlockSpec((tm, tn), lambda b, t, j: (b * nt + t, j)),
            pl.BlockSpec((1, CONV_W - 1, tn), lambda b, t, j: (b * nt + t, 0, j)),
        ],
        out_shape=[
            jax.ShapeDtypeStruct((R, n), BF16),
            jax.ShapeDtypeStruct((B * nt, CONV_W - 1, n), F32),
        ],
        scratch_shapes=[
            pltpu.VMEM((tm + 8, tn), F32),
            pltpu.VMEM((nj, 8, tn), F32),
        ],
        compiler_params=_params(("arbitrary", "arbitrary", "arbitrary")),
        name="gdn_qk_conv" if l2 else "gdn_v_conv",
    )(h, w, w_conv, conv_state)


def _softplus(x):
    return jnp.maximum(x, 0.0) + jnp.log(1.0 + jnp.exp(-jnp.abs(x)))


def _zbg_kernel(x_ref, g_ref, w_ref, alog_ref, dtb_ref, tri_ref, z_ref, beta_ref, gc_ref, gct_ref, h_ref, *, tm):
    x = x_ref[...]
    h = (x * _rms_scale(x) * g_ref[...]).astype(BF16)
    h_ref[...] = h
    ba = jnp.dot(h, w_ref[:, V_DIM:], preferred_element_type=F32)
    z_ref[...] = jnp.dot(h, w_ref[:, :V_DIM], preferred_element_type=F32).astype(z_ref.dtype)
    beta_ref[...] = _sigmoid(ba[:, :HV])
    gcol = -jnp.exp(alog_ref[...]) * _softplus(ba[:, HV:2 * HV] + dtb_ref[...])
    tri = tri_ref[...]
    g1, g2, g3 = _split3(gcol)
    d = functools.partial(jnp.dot, preferred_element_type=F32)
    gc = d(tri, g1) + (d(tri, g2) + d(tri, g3))
    gc_ref[...] = gc
    r = lax.broadcasted_iota(jnp.int32, (HV, HV), 0)
    c = lax.broadcasted_iota(jnp.int32, (HV, HV), 1)
    sel = (c == jnp.where(r < HV // 2, 2 * r, 2 * r - (HV - 1))).astype(BF16)
    tn = lambda a: lax.dot_general(sel, a, (((1,), (1,)), ((), ())), preferred_element_type=F32)
    c1, c2, c3 = _split3(gc)
    gct = tn(c1) + (tn(c2) + tn(c3))
    for ch in range(tm // CHUNK):
        sl = slice(ch * CHUNK, (ch + 1) * CHUNK)
        gct_ref[0, ch] = jnp.concatenate([gct[:HV // 2, sl], gct[HV // 2:, sl]], axis=1)


def _zbg(x, g, w_z, w_ba, a_log, dt_bias, B, T):
    R = B * T
    tm = min(ROW_TILE, T)
    nt = T // tm
    nc = T // CHUNK
    idx = np.arange(tm)
    tri = jnp.asarray(((idx[:, None] // CHUNK == idx[None, :] // CHUNK) & (idx[:, None] >= idx[None, :]))
                      .astype(np.float32), dtype=BF16)
    w_zba = jnp.concatenate([w_z, w_ba, jnp.zeros((D_MODEL, LANES - 2 * HV), BF16)], axis=1)
    full = lambda shape: pl.BlockSpec(shape, lambda b, t: (0,) * len(shape))
    kern = functools.partial(_zbg_kernel, tm=tm)
    return pl.pallas_call(
        kern,
        grid=(B, nt),
        in_specs=[
            pl.BlockSpec((tm, D_MODEL), lambda b, t: (b * nt + t, 0)),
            full((1, D_MODEL)),
            full((D_MODEL, V_DIM + LANES)),
            full((1, HV)), full((1, HV)),
            full((tm, tm)),
        ],
        out_specs=[
            pl.BlockSpec((tm, V_DIM), lambda b, t: (b * nt + t, 0)),
            pl.BlockSpec((tm, HV), lambda b, t: (b * nt + t, 0)),
            pl.BlockSpec((tm, HV), lambda b, t: (b * nt + t, 0)),
            pl.BlockSpec((1, tm // CHUNK, HV // 2, 2 * CHUNK), lambda b, t: (b, t, 0, 0)),
            pl.BlockSpec((tm, D_MODEL), lambda b, t: (b * nt + t, 0)),
        ],
        out_shape=[
            jax.ShapeDtypeStruct((R, V_DIM), BF16),
            jax.ShapeDtypeStruct((R, HV), F32),
            jax.ShapeDtypeStruct((R, HV), F32),
            jax.ShapeDtypeStruct((B, nc, HV // 2, 2 * CHUNK), F32),
            jax.ShapeDtypeStruct((R, D_MODEL), BF16),
        ],
        compiler_params=_params(("arbitrary", "arbitrary")),
        name="gdn_z_beta_decay",
    )(x, g, w_zba, a_log.reshape(1, HV), dt_bias.reshape(1, HV), tri)


PAIR = HV // HK


def _pair_rows(x, lo):
    zero = jnp.zeros((), x.dtype)
    return jnp.concatenate([jnp.where(lo, x, zero), jnp.where(lo, zero, x)], axis=0)


def _pair_products(lhs_list, rhs_list, lo):
    def split(a):
        hi = a.astype(BF16)
        return hi, (a - hi.astype(F32)).astype(BF16)

    ops = []
    for lhs, rhs in zip(lhs_list, rhs_list):
        lh, ll = split(lhs)
        rh, rl = split(rhs)
        rh2 = _pair_rows(rh, lo)
        ops.append((jnp.concatenate([lh, lh, ll], axis=1), jnp.concatenate([rh2, _pair_rows(rl, lo), rh2], axis=0)))
    return [jnp.dot(a, b, preferred_element_type=F32) for a, b in ops]


def _pair_products_bf16(lhs_list, rhs_list, lo):
    ops = [(lhs.astype(BF16), _pair_rows(rhs.astype(BF16), lo)) for lhs, rhs in zip(lhs_list, rhs_list)]
    return [jnp.dot(a, b, preferred_element_type=F32) for a, b in ops]


INV_BASE = 16


def _pair_unit_lower_inverse(low_list, lo, eye2, same_block):
    C = low_list[0].shape[0]
    base = same_block(INV_BASE)
    xs = [jnp.where(base, -low, 0.0) for low in low_list]
    accs = [eye2 + x for x in xs]
    pws = _pair_products(xs, xs, lo)
    n = 2
    while 2 * n < INV_BASE:
        prods = _pair_products([jnp.concatenate([a, p], axis=0) for a, p in zip(accs, pws)], pws, lo)
        accs = [a + pr[:C] for a, pr in zip(accs, prods)]
        pws = [pr[C:] for pr in prods]
        n *= 2
    prods = _pair_products(accs, pws, lo)
    tinvs = [a + pr for a, pr in zip(accs, prods)]
    size = 2 * INV_BASE
    while size <= C:
        below = same_block(size) & jnp.logical_not(same_block(size // 2))
        offs = [jnp.where(below, low, 0.0) for low in low_list]
        inner = _pair_products_bf16(offs, tinvs, lo)
        outer = _pair_products_bf16(tinvs, inner, lo)
        tinvs = [t - o for t, o in zip(tinvs, outer)]
        size *= 2
    return tinvs


def _delta_prep_kernel(q_ref, k_ref, v_ref, beta_ref, gc_ref, gct_ref, epair_ref,
                       u_ref, w_ref, a_ref, qg_ref, kd_ref, *, cb):
    C = CHUNK
    ri = lax.broadcasted_iota(jnp.int32, (C, 2 * C), 0)
    li = lax.broadcasted_iota(jnp.int32, (C, 2 * C), 1)
    lm = li & (C - 1)
    lo = li < C
    incl = ri >= lm
    strict = ri > lm
    eye2 = (ri == lm).astype(F32)
    units = [(c, p) for c in range(cb) for p in range(HK)]
    rows = lambda c: slice(c * C, (c + 1) * C)

    grams = []
    for c, p in units:
        kb = k_ref[rows(c), p * DK:(p + 1) * DK]
        qb = q_ref[rows(c), p * DK:(p + 1) * DK]
        grams.append(lax.dot_general(jnp.concatenate([kb, qb], axis=0), jnp.concatenate([kb, kb], axis=0),
                                     (((1,), (1,)), ((), ())), preferred_element_type=F32))

    epair = epair_ref[...]
    g2s, b2s, betas, begs, egs, kgs = [], [], [], [], [], []
    for c in range(cb):
        beta = beta_ref[rows(c), :]
        gc = gc_ref[rows(c), :]
        g2s.append(_spread(gc, epair, 3))
        b2s.append(_spread(beta, epair, 2))
        betas.append(beta)
        egs.append(jnp.exp(gc))
        begs.append(beta * egs[c])
        kgs.append(jnp.exp(gc[C - 1:C, :] - gc))

    lows = []
    for (c, p), gram in zip(units, grams):
        pt = slice(p * 2 * C, (p + 1) * 2 * C)
        grow2 = gct_ref[c, p:p + 1, :]
        decay2 = jnp.exp(jnp.where(incl, g2s[c][:, pt] - grow2, -jnp.inf))
        a_ref[rows(c), pt] = (gram[C:] * decay2).astype(a_ref.dtype)
        lows.append(jnp.where(strict, b2s[c][:, pt] * gram[:C] * decay2, 0.0))

    same_block = lambda s: (ri // s) == (lm // s)
    tinvs = _pair_unit_lower_inverse(lows, lo, eye2, same_block)

    ops = []
    for (c, p), tinv in zip(units, tinvs):
        kf = k_ref[rows(c), p * DK:(p + 1) * DK].astype(F32)
        qf = q_ref[rows(c), p * DK:(p + 1) * DK].astype(F32)
        rhs = []
        for h in (PAIR * p, PAIR * p + 1):
            ht = slice(h * DV, (h + 1) * DV)
            vf = v_ref[rows(c), ht].astype(F32)
            rhs.append(jnp.concatenate([vf * betas[c][:, h:h + 1], kf * begs[c][:, h:h + 1]], axis=1))
            qg_ref[rows(c), ht] = (qf * egs[c][:, h:h + 1]).astype(qg_ref.dtype)
            kd_ref[rows(c), ht] = (kf * kgs[c][:, h:h + 1]).astype(kd_ref.dtype)
        ops.append((_pair_rows(tinv.astype(BF16), lo), jnp.concatenate(rhs, axis=0).astype(BF16)))
    for (c, p), (lhs, rhs) in zip(units, ops):
        uw = jnp.dot(lhs, rhs, preferred_element_type=F32)
        for i, h in enumerate((PAIR * p, PAIR * p + 1)):
            u_ref[rows(c), h * DV:(h + 1) * DV] = uw[i * C:(i + 1) * C, :DV].astype(u_ref.dtype)
            w_ref[rows(c), h * DV:(h + 1) * DV] = uw[i * C:(i + 1) * C, DV:].astype(w_ref.dtype)


def _delta_prep(qk, v, beta, gc, gct, B, T):
    R = B * T
    cb = 4 if T % (4 * CHUNK) == 0 else (2 if T % (2 * CHUNK) == 0 else 1)
    rb = cb * CHUNK
    gct2 = gct.reshape(R // CHUNK, HK, 2 * CHUNK)
    kern = functools.partial(_delta_prep_kernel, cb=cb)
    return pl.pallas_call(
        kern,
        grid=(R // rb,),
        in_specs=[
            pl.BlockSpec((rb, QK_DIM), lambda i: (i, 0)),
            pl.BlockSpec((rb, QK_DIM), lambda i: (i, 1)),
            pl.BlockSpec((rb, V_DIM), lambda i: (i, 0)),
            pl.BlockSpec((rb, HV), lambda i: (i, 0)),
            pl.BlockSpec((rb, HV), lambda i: (i, 0)),
            pl.BlockSpec((cb, HK, 2 * CHUNK), lambda i: (i, 0, 0)),
            pl.BlockSpec((HV, HV * CHUNK), lambda i: (0, 0)),
        ],
        out_specs=[
            pl.BlockSpec((rb, V_DIM), lambda i: (i, 0)),
            pl.BlockSpec((rb, V_DIM), lambda i: (i, 0)),
            pl.BlockSpec((rb, HV * CHUNK), lambda i: (i, 0)),
            pl.BlockSpec((rb, V_DIM), lambda i: (i, 0)),
            pl.BlockSpec((rb, V_DIM), lambda i: (i, 0)),
        ],
        out_shape=[
            jax.ShapeDtypeStruct((R, V_DIM), BF16),
            jax.ShapeDtypeStruct((R, V_DIM), BF16),
            jax.ShapeDtypeStruct((R, HV * CHUNK), BF16),
            jax.ShapeDtypeStruct((R, V_DIM), BF16),
            jax.ShapeDtypeStruct((R, V_DIM), BF16),
        ],
        compiler_params=_params(("arbitrary",)),
        name="gdn_delta_prep",
    )(qk, qk, v, beta, gc, gct2, _head_spread_matrices()[1])


def _delta_scan_kernel(qg_ref, kd_ref, u_ref, w_ref, a_ref, z_ref, gc_ref, s0_ref, on_ref, efull_ref,
                       og_ref, s_ref, *, bb):
    c = pl.program_id(1)

    @pl.when(c == 0)
    def _():
        s_ref[...] = s0_ref[...]

    C = CHUNK
    lo = lax.broadcasted_iota(jnp.int32, (C, 2 * C), 1) < C
    onorm = on_ref[...]
    units = [(b, h) for b in range(bb) for h in range(HV)]
    pairs = [(b, p) for b in range(bb) for p in range(HK)]

    efull = efull_ref[...]
    dgs = [_spread(jnp.exp(gc_ref[b, C - 8:C, :]), efull, 3)[7:8, :] for b in range(bb)]

    ops = []
    kds = []
    dec = []
    for b, h in units:
        ht = slice(h * DV, (h + 1) * DV)
        kds.append(kd_ref[b, :, ht])
        dec.append(dgs[b][:, ht])
        ops.append((jnp.concatenate([w_ref[b, :, ht], qg_ref[b, :, ht]], axis=0), s_ref[b, h].astype(BF16)))
    wqs = [jnp.dot(a, s, preferred_element_type=F32) for a, s in ops]

    vns = [(u_ref[b, :, h * DV:(h + 1) * DV].astype(F32) - wq[:C]).astype(BF16) for (b, h), wq in zip(units, wqs)]

    oparts = []
    for i, (b, p) in enumerate(pairs):
        att = _pair_rows(a_ref[b, :, p * 2 * C:(p + 1) * 2 * C], lo)
        vn2 = jnp.concatenate([vns[PAIR * i], vns[PAIR * i + 1]], axis=0)
        oparts.append(jnp.dot(att, vn2, preferred_element_type=F32))
    sds = [lax.dot_general(kd, vn, (((0,), (0,)), ((), ())), preferred_element_type=F32)
           for kd, vn in zip(kds, vns)]

    for i, (b, h) in enumerate(units):
        s_ref[b, h] = s_ref[b, h] * dec[i] + sds[i]
        o = wqs[i][C:] + oparts[i // PAIR][(h % PAIR) * C:(h % PAIR + 1) * C]
        on = o * _rms_scale(o) * onorm
        zf = z_ref[b, :, h * DV:(h + 1) * DV].astype(F32)
        og_ref[b, :, h * DV:(h + 1) * DV] = (on * _silu(zf)).astype(og_ref.dtype)


def _delta_scan(qg, kd, u, w, att, z, gc, s0, o_norm, B, T):
    nc = T // CHUNK
    bb = 4 if B % 4 == 0 else (2 if B % 2 == 0 else 1)
    v3 = lambda a: a.reshape(B, T, a.shape[-1])
    blk = lambda n, j=0: pl.BlockSpec((bb, CHUNK, n), lambda b, c: (b, c, j))
    kern = functools.partial(_delta_scan_kernel, bb=bb)
    og, s_new = pl.pallas_call(
        kern,
        grid=(B // bb, nc),
        in_specs=[
            blk(V_DIM), blk(V_DIM), blk(V_DIM), blk(V_DIM), blk(HV * CHUNK), blk(V_DIM), blk(HV),
            pl.BlockSpec((bb, HV, DK, DV), lambda b, c: (b, 0, 0, 0)),
            pl.BlockSpec((1, DV), lambda b, c: (0, 0)),
            pl.BlockSpec((HV, HV * DV), lambda b, c: (0, 0)),
        ],
        out_specs=[
            blk(V_DIM),
            pl.BlockSpec((bb, HV, DK, DV), lambda b, c: (b, 0, 0, 0)),
        ],
        out_shape=[
            jax.ShapeDtypeStruct((B, T, V_DIM), BF16),
            jax.ShapeDtypeStruct((B, HV, DK, DV), F32),
        ],
        compiler_params=_params(("arbitrary", "arbitrary")),
        name="gdn_delta_scan",
    )(v3(qg), v3(kd), v3(u), v3(w), v3(att), v3(z), v3(gc), s0, o_norm, _head_spread_matrices()[0])
    return og.reshape(B * T, V_DIM), s_new


MXU_COLS = 256
FF_SPLITS = (6 * MXU_COLS, 5 * MXU_COLS)
assert sum(FF_SPLITS) == D_FF
RESIDENT = pl.Buffered(1)


def _swiglu(h, w1_ref, w3_ref, w2_ref, lead=()):
    out = None
    start = 0
    for width in FF_SPLITS:
        sl = slice(start, start + width)
        start += width
        a = jnp.dot(h, w1_ref[lead + (slice(None), sl)], preferred_element_type=F32)
        b = jnp.dot(h, w3_ref[lead + (slice(None), sl)], preferred_element_type=F32)
        mid = (_silu(a) * b).astype(BF16)
        part = jnp.dot(mid, w2_ref[lead + (sl, slice(None))], preferred_element_type=F32)
        out = part if out is None else out + part
    return out


def _mixer_out_ffn_kernel(x_ref, a_ref, wo_ref, g_ref, w1_ref, w3_ref, w2_ref, o_ref):
    x1 = x_ref[...] + jnp.dot(a_ref[...], wo_ref[...], preferred_element_type=F32)
    h = (x1 * _rms_scale(x1) * g_ref[...]).astype(BF16)
    o_ref[...] = x1 + _swiglu(h, w1_ref, w3_ref, w2_ref)


def _mixer_out_ffn(x, a, w_out, g, w1, w3, w2):
    R = x.shape[0]
    K = a.shape[1]
    tm = min(ROW_TILE, R)
    held = lambda shape: pl.BlockSpec(shape, lambda i: (0,) * len(shape), pipeline_mode=RESIDENT)
    return pl.pallas_call(
        _mixer_out_ffn_kernel,
        grid=(R // tm,),
        in_specs=[
            pl.BlockSpec((tm, D_MODEL), lambda i: (i, 0)),
            pl.BlockSpec((tm, K), lambda i: (i, 0)),
            held((K, D_MODEL)),
            held((1, D_MODEL)),
            held((D_MODEL, D_FF)), held((D_MODEL, D_FF)), held((D_FF, D_MODEL)),
        ],
        out_specs=pl.BlockSpec((tm, D_MODEL), lambda i: (i, 0)),
        out_shape=jax.ShapeDtypeStruct((R, D_MODEL), F32),
        compiler_params=_params(("arbitrary",), VMEM_LIMIT_BIG),
        name="gdn_out_dense_swiglu",
    )(x, a, w_out, g, w1, w3, w2)


KV_DUP = 2 * HKV * LANES
KV_STD = 2 * HKV * HD


def _rope(x, cos, sm, sp):
    up = pltpu.roll(x, LANES - ROT_DIM // 2, 1)
    dn = pltpu.roll(x, ROT_DIM // 2, 1)
    return x * cos + up * sm + dn * sp


def _qkv_rope_kernel(x_ref, gq_ref, gkv_ref, wq_ref, wkv_ref, wkvs_ref, cos_ref, sm_ref, sp_ref,
                     q_ref, kvd_ref, kvs_ref, *, nt, keep):
    x = x_ref[...]
    xr = x * _rms_scale(x)
    hq = (xr * gq_ref[...]).astype(BF16)
    hkv = (xr * gkv_ref[...]).astype(BF16)
    cos = cos_ref[...]
    sm = sm_ref[...]
    sp = sp_ref[...]
    q = jnp.dot(hq, wq_ref[...], preferred_element_type=F32)
    for t in range(HQ * HD // LANES):
        sl = slice(t * LANES, (t + 1) * LANES)
        q_ref[:, sl] = (_rope(q[:, sl], cos, sm, sp) * (HD ** -0.5)).astype(q_ref.dtype)
    kv = jnp.dot(hkv, wkv_ref[...], preferred_element_type=F32)
    half = KV_DUP // 2
    for t in range(HKV):
        sl = slice(t * LANES, (t + 1) * LANES)
        kvd_ref[:, sl] = _rope(kv[:, sl], cos, sm, sp).astype(kvd_ref.dtype)
    kvd_ref[:, half:] = kv[:, half:].astype(kvd_ref.dtype)

    @pl.when(pl.program_id(0) % nt == nt - 1)
    def _():
        tm = x.shape[0]
        rows = slice(tm - keep, tm)
        kvs = jnp.dot(hkv[rows], wkvs_ref[...], preferred_element_type=F32)
        for t in range(HKV * HD // LANES):
            sl = slice(t * LANES, (t + 1) * LANES)
            kvs_ref[:, sl] = _rope(kvs[:, sl], cos[rows], sm[rows], sp[rows])
        kvs_ref[:, HKV * HD:] = kvs[:, HKV * HD:]


def _rope_tables(pos):
    half = ROT_DIM // 2
    inv = jnp.power(ROPE_THETA, -jnp.arange(half, dtype=F32) * 2.0 / ROT_DIM)
    ang = pos.astype(F32)[:, None] * inv[None, :]
    cos = jnp.cos(ang)
    sin = jnp.sin(ang)
    T = pos.shape[0]
    one = jnp.ones((T, HD - ROT_DIM), F32)
    zero = jnp.zeros((T, HD - ROT_DIM), F32)
    zh = jnp.zeros((T, half), F32)
    c64 = jnp.concatenate([cos, cos, one], axis=1)
    sm64 = jnp.concatenate([-sin, zh, zero], axis=1)
    sp64 = jnp.concatenate([zh, sin, zero], axis=1)
    dup = lambda a: jnp.concatenate([a, a], axis=1)
    return dup(c64), dup(sm64), dup(sp64)


def _qkv_rope(x, gq, gkv, w_q, w_kvd, w_kvs, tables, B, T, keep):
    R = B * T
    tm = min(2 * ROW_TILE, T)
    nt = T // tm
    assert keep <= tm
    tab = pl.BlockSpec((tm, LANES), lambda i: (i % nt, 0))
    kern = functools.partial(_qkv_rope_kernel, nt=nt, keep=keep)
    return pl.pallas_call(
        kern,
        grid=(R // tm,),
        in_specs=[
            pl.BlockSpec((tm, D_MODEL), lambda i: (i, 0)),
            pl.BlockSpec((1, D_MODEL), lambda i: (0, 0)),
            pl.BlockSpec((1, D_MODEL), lambda i: (0, 0)),
            pl.BlockSpec((D_MODEL, HQ * HD), lambda i: (0, 0)),
            pl.BlockSpec((D_MODEL, KV_DUP), lambda i: (0, 0)),
            pl.BlockSpec((D_MODEL, KV_STD), lambda i: (0, 0)),
            tab, tab, tab,
        ],
        out_specs=[
            pl.BlockSpec((tm, HQ * HD), lambda i: (i, 0)),
            pl.BlockSpec((tm, KV_DUP), lambda i: (i, 0)),
            pl.BlockSpec((keep, KV_STD), lambda i: (i // nt, 0)),
        ],
        out_shape=[
            jax.ShapeDtypeStruct((R, HQ * HD), BF16),
            jax.ShapeDtypeStruct((R, KV_DUP), BF16),
            jax.ShapeDtypeStruct((B * keep, KV_STD), F32),
        ],
        compiler_params=_params(("arbitrary",)),
        name="swa_qkv_rope",
    )(x, gq, gkv, w_q, w_kvd, w_kvs, *tables)


KEY_CHUNKS = WINDOW // CHUNK + 1
KEY_PAD = 256


def _attn_kernel(*refs, off, cq):
    q_ref = refs[0]
    krefs = refs[1:KEY_CHUNKS + cq]
    sink_ref, ones_ref, o_ref = refs[KEY_CHUNKS + cq:]
    c = pl.program_id(1)
    lane = lax.broadcasted_iota(jnp.int32, (KEY_CHUNKS * CHUNK, LANES), 1)
    lo = lane < HD
    col = lax.broadcasted_iota(jnp.int32, (1, KEY_PAD), 1)
    sink_col = col == KEY_CHUNKS * CHUNK
    zpad = jnp.zeros((KEY_PAD - KEY_CHUNKS * CHUNK, LANES), BF16)
    zero = jnp.zeros((), BF16)
    tiles = HQ // HKV // 2
    ones_blk = ones_ref[...]
    units = [(u, g) for u in range(cq) for g in range(HKV)]

    vblks = []
    scores = []
    for u, g in units:
        kr = krefs[u:u + KEY_CHUNKS]
        kg = jnp.concatenate([r[:, g * LANES:(g + 1) * LANES] for r in kr], axis=0)
        vg = jnp.concatenate([r[:, KV_DUP // 2 + g * LANES:KV_DUP // 2 + (g + 1) * LANES] for r in kr], axis=0)
        kblk = jnp.concatenate([jnp.where(lo, kg, zero), zpad, jnp.where(lo, zero, kg), zpad], axis=0)
        vblk = jnp.concatenate([jnp.where(lo, vg, zero), zpad, jnp.where(lo, zero, vg), zpad], axis=0)
        vblks.append(jnp.concatenate([vblk, ones_blk], axis=1))
        qg = jnp.concatenate([q_ref[u * CHUNK:(u + 1) * CHUNK, (g * tiles + p) * LANES:(g * tiles + p + 1) * LANES]
                              for p in range(tiles)], axis=0)
        scores.append(lax.dot_general(qg, kblk, (((1,), (1,)), ((), ())), preferred_element_type=F32))

    probs = []
    for (u, g), sc in zip(units, scores):
        first = c * cq + u + off - (KEY_CHUNKS - 1)
        valid = (col < KEY_CHUNKS * CHUNK) & (col // CHUNK + first >= 0)
        base_bias = jnp.where(valid, 0.0, -jnp.inf).astype(F32)
        rows = []
        for p in range(tiles):
            t = g * tiles + p
            ps = []
            for hh in range(2):
                bias = jnp.where(sink_col, sink_ref[:, 2 * t + hh:2 * t + hh + 1], base_bias)
                sh = sc[p * CHUNK:(p + 1) * CHUNK, hh * KEY_PAD:(hh + 1) * KEY_PAD] + bias
                m = jnp.max(sh, axis=-1, keepdims=True)
                ps.append(jnp.exp(sh - m).astype(BF16))
            rows.append(jnp.concatenate(ps, axis=1))
        probs.append(jnp.concatenate(rows, axis=0))

    outs = [jnp.dot(pr, vb, preferred_element_type=F32) for pr, vb in zip(probs, vblks)]
    for (u, g), out in zip(units, outs):
        for p in range(tiles):
            t = g * tiles + p
            rows = slice(p * CHUNK, (p + 1) * CHUNK)
            o_ref[u * CHUNK:(u + 1) * CHUNK, t * LANES:(t + 1) * LANES] = (
                out[rows, :LANES] / out[rows, LANES:]).astype(o_ref.dtype)


def _attention(q, kvd, sinks, B, Tq, Tk):
    ncq = Tq // CHUNK
    nck = Tk // CHUNK
    off = nck - ncq
    cq = 4 if ncq % 4 == 0 else (2 if ncq % 2 == 0 else 1)
    kern = functools.partial(_attn_kernel, off=off, cq=cq)
    blk_row = np.arange(2 * KEY_PAD)[:, None] // KEY_PAD
    blk_lane = np.arange(LANES)[None, :] // HD
    ones_blk = jnp.asarray((blk_row == blk_lane).astype(np.float32), dtype=BF16)

    def kspec(i):
        return pl.BlockSpec((CHUNK, KV_DUP),
                            lambda b, c: (b * nck + jnp.maximum(c * cq + off - (KEY_CHUNKS - 1) + i, 0), 0))

    qspec = pl.BlockSpec((cq * CHUNK, HQ * HD), lambda b, c: (b * (ncq // cq) + c, 0))
    return pl.pallas_call(
        kern,
        grid=(B, ncq // cq),
        in_specs=[qspec] + [kspec(i) for i in range(KEY_CHUNKS + cq - 1)] + [
            pl.BlockSpec((1, HQ), lambda b, c: (0, 0)),
            pl.BlockSpec((2 * KEY_PAD, LANES), lambda b, c: (0, 0)),
        ],
        out_specs=qspec,
        out_shape=jax.ShapeDtypeStruct((B * Tq, HQ * HD), BF16),
        compiler_params=_params(("arbitrary", "arbitrary")),
        name="swa_attention",
    )(q, *([kvd] * (KEY_CHUNKS + cq - 1)), sinks, ones_blk)


def _oproj_router_kernel(x_ref, a_ref, w_ref, g_ref, wr_ref, x3_ref, h_ref, e_ref, gate_ref, cnt_ref):
    i = pl.program_id(0)
    x3 = x_ref[...] + jnp.dot(a_ref[...], w_ref[...], preferred_element_type=F32)
    x3_ref[...] = x3
    h = x3 * _rms_scale(x3) * g_ref[...]
    h_ref[...] = h
    h1, h2 = _split2(h)
    r1, r2 = _split2(wr_ref[...])
    part = jnp.dot(h1, jnp.concatenate([r1, r2], axis=1), preferred_element_type=F32)
    logits = part[:, :N_EXPERTS] + (part[:, N_EXPERTS:] + jnp.dot(h2, r1, preferred_element_type=F32))
    eidx = lax.broadcasted_iota(jnp.int32, logits.shape, 1)
    m1 = jnp.max(logits, axis=-1, keepdims=True)
    i1 = jnp.min(jnp.where(logits == m1, eidx, N_EXPERTS), axis=-1, keepdims=True)
    rest = jnp.where(eidx == i1, -jnp.inf, logits)
    m2 = jnp.max(rest, axis=-1, keepdims=True)
    i2 = jnp.min(jnp.where(rest == m2, eidx, N_EXPERTS), axis=-1, keepdims=True)
    e2 = jnp.exp(m2 - m1)
    den = 1.0 + e2
    e_ref[...] = jnp.concatenate([i1, i2], axis=1)
    gate_ref[...] = jnp.concatenate([1.0 / den, e2 / den], axis=1)
    hot = ((eidx == i1) | (eidx == i2)).astype(F32)

    @pl.when(i == 0)
    def _():
        cnt_ref[...] = jnp.zeros_like(cnt_ref)

    cnt_ref[...] += jnp.sum(hot, axis=0, keepdims=True)


def _oproj_router(x, a, w_o, g, w_r):
    R = x.shape[0]
    tm = min(ROW_TILE, R)
    row = lambda n: pl.BlockSpec((tm, n), lambda i: (i, 0))
    return pl.pallas_call(
        _oproj_router_kernel,
        grid=(R // tm,),
        in_specs=[
            row(D_MODEL), row(HQ * HD),
            pl.BlockSpec((HQ * HD, D_MODEL), lambda i: (0, 0)),
            pl.BlockSpec((1, D_MODEL), lambda i: (0, 0)),
            pl.BlockSpec((D_MODEL, N_EXPERTS), lambda i: (0, 0)),
        ],
        out_specs=[row(D_MODEL), row(D_MODEL), row(TOP_K), row(TOP_K),
                   pl.BlockSpec((1, N_EXPERTS), lambda i: (0, 0))],
        out_shape=[
            jax.ShapeDtypeStruct((R, D_MODEL), F32),
            jax.ShapeDtypeStruct((R, D_MODEL), F32),
            jax.ShapeDtypeStruct((R, TOP_K), jnp.int32),
            jax.ShapeDtypeStruct((R, TOP_K), F32),
            jax.ShapeDtypeStruct((1, N_EXPERTS), F32),
        ],
        compiler_params=_params(("arbitrary",)),
        name="swa_oproj_router",
    )(x, a, w_o, g, w_r)


def _slot_kernel(e_ref, base_ref, tri_ref, pos_ref, run_scr):
    i = pl.program_id(0)

    @pl.when(i == 0)
    def _():
        run_scr[...] = jnp.zeros_like(run_scr)

    e = e_ref[...]
    tm = e.shape[0]
    eidx = lax.broadcasted_iota(jnp.int32, (tm, N_EXPERTS), 1)
    hot0 = eidx == e[:, 0:1]
    hot1 = eidx == e[:, 1:2]
    hot = (hot0 | hot1).astype(BF16)
    before = jnp.dot(tri_ref[...], hot, preferred_element_type=F32)
    dest = before + run_scr[...] + base_ref[...]
    p0 = jnp.sum(jnp.where(hot0, dest, 0.0), axis=-1, keepdims=True)
    p1 = jnp.sum(jnp.where(hot1, dest, 0.0), axis=-1, keepdims=True)
    pos_ref[...] = jnp.concatenate([p0, p1], axis=1).astype(jnp.int32)
    run_scr[...] += jnp.sum(hot.astype(F32), axis=0, keepdims=True)


def _slots(eidx, base):
    R = eidx.shape[0]
    tm = min(ROW_TILE, R)
    idx = np.arange(tm)
    tri = jnp.asarray((idx[:, None] > idx[None, :]).astype(np.float32), dtype=BF16)
    return pl.pallas_call(
        _slot_kernel,
        grid=(R // tm,),
        in_specs=[
            pl.BlockSpec((tm, TOP_K), lambda i: (i, 0)),
            pl.BlockSpec((1, N_EXPERTS), lambda i: (0, 0)),
            pl.BlockSpec((tm, tm), lambda i: (0, 0)),
        ],
        out_specs=pl.BlockSpec((tm, TOP_K), lambda i: (i, 0)),
        out_shape=jax.ShapeDtypeStruct((R, TOP_K), jnp.int32),
        scratch_shapes=[pltpu.VMEM((1, N_EXPERTS), F32)],
        compiler_params=_params(("arbitrary",)),
        name="moe_slots",
    )(eidx, base, tri)


DMA_UNROLL = 8


def _row_copy(src, i, dst, j, sem):
    return pltpu.make_async_copy(src.at[pl.ds(i, 1)], dst.at[pl.ds(j, 1)], sem)


def _dispatch_kernel(pos_ref, cnt_ref, base_ref, nused_ref, h_ref, xs_ref, zero_scr, sem, zsem, *, tm, te, n_tiles):
    i = pl.program_id(0)

    @pl.when(i == 0)
    def _():
        zero_scr[...] = jnp.zeros_like(zero_scr)

        def tile_copy(t):
            dst = xs_ref.at[pl.ds(pl.multiple_of(t * te, te), te)]
            return pltpu.make_async_copy(zero_scr, dst, zsem)

        def issue_tile(t, carry):
            tile_copy(t).start()
            return carry

        def drain_tile(t, carry):
            tile_copy(t).wait()
            return carry

        lax.fori_loop(nused_ref[0], n_tiles, issue_tile, 0)
        lax.fori_loop(nused_ref[0], n_tiles, drain_tile, 0)
        for e in range(N_EXPERTS):
            n = cnt_ref[e]
            start = base_ref[e] + n
            npad = (te - n % te) % te

            def issue(r, carry):
                _row_copy(zero_scr, 0, xs_ref, start + r, zsem).start()
                return carry

            def drain(r, carry):
                _row_copy(zero_scr, 0, xs_ref, start + r, zsem).wait()
                return carry

            lax.fori_loop(0, npad, issue, 0)
            lax.fori_loop(0, npad, drain, 0)

    def issue(r, carry):
        _row_copy(h_ref, r, xs_ref, pos_ref[2 * r], sem).start(priority=0)
        _row_copy(h_ref, r, xs_ref, pos_ref[2 * r + 1], sem).start(priority=1)
        return carry

    lax.fori_loop(0, tm, issue, 0, unroll=DMA_UNROLL)
    for _ in range(TOP_K):
        pltpu.make_async_copy(h_ref, xs_ref.at[pl.ds(0, tm)], sem).wait()


def _dispatch(h, pos_flat, counts, base, n_used, n_tiles, te):
    R = h.shape[0]
    tm = min(ROW_TILE, R)
    kern = functools.partial(_dispatch_kernel, tm=tm, te=te, n_tiles=n_tiles)
    smem = pl.BlockSpec(memory_space=pltpu.SMEM)
    return pl.pallas_call(
        kern,
        grid=(R // tm,),
        in_specs=[
            pl.BlockSpec((TOP_K * tm,), lambda i: (i,), memory_space=pltpu.SMEM),
            smem, smem, smem,
            pl.BlockSpec((tm, D_MODEL), lambda i: (i, 0)),
        ],
        out_specs=pl.BlockSpec(memory_space=pl.ANY),
        out_shape=jax.ShapeDtypeStruct((n_tiles * te, D_MODEL), F32),
        scratch_shapes=[pltpu.VMEM((te, D_MODEL), F32), pltpu.SemaphoreType.DMA(()), pltpu.SemaphoreType.DMA(())],
        compiler_params=pltpu.CompilerParams(dimension_semantics=("arbitrary",), vmem_limit_bytes=VMEM_LIMIT,
                                             has_side_effects=True),
        name="moe_dispatch",
    )(pos_flat, counts, base, n_used, h)


def _expert_ffn_kernel(te_ref, nused_ref, x_ref, w1_ref, w3_ref, w2_ref, y_ref):
    i = pl.program_id(0)

    @pl.when(i < nused_ref[0])
    def _():
        y_ref[...] = _swiglu(x_ref[...].astype(BF16), w1_ref, w3_ref, w2_ref, lead=(0,))

    @pl.when(i >= nused_ref[0])
    def _():
        y_ref[...] = jnp.zeros_like(y_ref)


def _expert_ffn(xs, tile_expert, n_used, w1, w3, w2, te):
    n_rows = xs.shape[0]
    nt = n_rows // te
    wspec = lambda shape: pl.BlockSpec((1,) + shape, lambda i, te_r, nu_r: (te_r[i], 0, 0), pipeline_mode=RESIDENT)
    grid_spec = pltpu.PrefetchScalarGridSpec(
        num_scalar_prefetch=2,
        grid=(nt,),
        in_specs=[
            pl.BlockSpec((te, D_MODEL), lambda i, te_r, nu_r: (i, 0)),
            wspec((D_MODEL, D_FF)), wspec((D_MODEL, D_FF)), wspec((D_FF, D_MODEL)),
        ],
        out_specs=pl.BlockSpec((te, D_MODEL), lambda i, te_r, nu_r: (i, 0)),
    )
    return pl.pallas_call(
        _expert_ffn_kernel,
        grid_spec=grid_spec,
        out_shape=jax.ShapeDtypeStruct((n_rows, D_MODEL), F32),
        compiler_params=_params(("arbitrary",), VMEM_LIMIT_BIG),
        name="moe_expert_swiglu",
    )(tile_expert, n_used, xs, w1, w3, w2)


def _combine_kernel(pos_ref, x_ref, gate_ref, g_ref, y_ref, o_ref, buf0, buf1, sem, *, tm):
    def issue(r, carry):
        _row_copy(y_ref, pos_ref[2 * r], buf0, r, sem).start(priority=0)
        _row_copy(y_ref, pos_ref[2 * r + 1], buf1, r, sem).start(priority=1)
        return carry

    lax.fori_loop(0, tm, issue, 0, unroll=DMA_UNROLL)
    for buf in (buf0, buf1):
        pltpu.make_async_copy(y_ref.at[pl.ds(0, tm)], buf, sem).wait()
    gate = gate_ref[...]
    x = x_ref[...] + (buf0[...] * gate[:, 0:1] + buf1[...] * gate[:, 1:2])
    o_ref[...] = x * _rms_scale(x) * g_ref[...]


def _combine(x3, gates, pos_flat, y, g):
    R = x3.shape[0]
    tm = min(ROW_TILE, R)
    kern = functools.partial(_combine_kernel, tm=tm)
    return pl.pallas_call(
        kern,
        grid=(R // tm,),
        in_specs=[
            pl.BlockSpec((TOP_K * tm,), lambda i: (i,), memory_space=pltpu.SMEM),
            pl.BlockSpec((tm, D_MODEL), lambda i: (i, 0)),
            pl.BlockSpec((tm, TOP_K), lambda i: (i, 0)),
            pl.BlockSpec((1, D_MODEL), lambda i: (0, 0)),
            pl.BlockSpec(memory_space=pl.ANY),
        ],
        out_specs=pl.BlockSpec((tm, D_MODEL), lambda i: (i, 0)),
        out_shape=jax.ShapeDtypeStruct((R, D_MODEL), F32),
        scratch_shapes=[pltpu.VMEM((tm, D_MODEL), F32), pltpu.VMEM((tm, D_MODEL), F32),
                        pltpu.SemaphoreType.DMA(())],
        compiler_params=_params(("arbitrary",)),
        name="moe_combine_norm",
    )(pos_flat, x3, gates, g, y)


def _prep_weights(p):
    w_in = p["gdn_w_in"][0]
    w_kv = p["w_kv"]
    kcols = w_kv[:, :HKV * HD].reshape(D_MODEL, HKV, HD)
    vcols = w_kv[:, HKV * HD:].reshape(D_MODEL, HKV, HD)
    dup = lambda a: jnp.concatenate([a, a], axis=2).reshape(D_MODEL, HKV * LANES)
    w_kvd = jnp.concatenate([dup(kcols), dup(vcols)], axis=1)
    return dict(
        w_qk=w_in[:, :2 * QK_DIM].astype(BF16),
        w_v=w_in[:, 2 * QK_DIM:CONV_CH].astype(BF16),
        w_z=w_in[:, CONV_CH:CONV_CH + V_DIM].astype(BF16),
        w_ba=w_in[:, CONV_CH + V_DIM:].astype(BF16),
        w_out=p["gdn_w_out"][0].astype(BF16),
        ffn_w1=p["ffn_w1"][0].astype(BF16),
        ffn_w3=p["ffn_w3"][0].astype(BF16),
        ffn_w2=p["ffn_w2"][0].astype(BF16),
        w_kvd=w_kvd.astype(BF16),
        w_kvs=w_kv.astype(BF16),
        w_q=p["swa_w_q"][0].astype(BF16),
        w_o=p["swa_w_o"][0].astype(BF16),
        moe_w1=p["moe_w1"][0].astype(BF16),
        moe_w3=p["moe_w3"][0].astype(BF16),
        moe_w2=p["moe_w2"][0].astype(BF16),
    )


def _dup_heads(a):
    B, T = a.shape[:2]
    return jnp.concatenate([a, a], axis=3).reshape(B * T, HKV * LANES)


def _trunk(x, pos, conv0, s0, win_k, win_v, p, w):
    B, T, _ = x.shape
    R = B * T
    xf = x.reshape(R, D_MODEL)
    row = lambda a: a.reshape(1, -1)

    g_attn = row(p["norm_attn"][0])
    w_conv = p["gdn_w_conv"][0]
    nqk = 2 * QK_DIM
    z, beta, gc, gct, hn = _zbg(xf, g_attn, w["w_z"], w["w_ba"], p["gdn_a_log"][0], p["gdn_dt_bias"][0], B, T)
    qk, conv_qk = _qkv_conv(hn, w["w_qk"], w_conv[:, :nqk], conv0[:, :, :nqk], B, T, True)
    v, conv_v = _qkv_conv(hn, w["w_v"], w_conv[:, nqk:], conv0[:, :, nqk:], B, T, False)
    conv_new = jnp.concatenate([conv_qk, conv_v], axis=-1).reshape(B, -1, CONV_W - 1, CONV_CH)[:, -1]
    u, wv, att_l, qg, kd = _delta_prep(qk, v, beta, gc, gct, B, T)
    og, s_new = _delta_scan(qg, kd, u, wv, att_l, z, gc, s0, row(p["gdn_o_norm"][0]), B, T)
    x2 = _mixer_out_ffn(xf, og, w["w_out"], row(p["norm_ffn"][0]), w["ffn_w1"], w["ffn_w3"], w["ffn_w2"])

    keep = min(WINDOW, T) if win_k is None else T
    q, kvd, kvs = _qkv_rope(x2, row(p["norm_attn"][1]), row(p["kv_norm"]), w["w_q"], w["w_kvd"], w["w_kvs"],
                            _rope_tables(pos), B, T, keep)
    k_new = kvs[:, :HKV * HD].reshape(B, keep, HKV, HD)
    v_new = kvs[:, HKV * HD:].reshape(B, keep, HKV, HD)
    if win_k is None:
        keys, Tk = kvd, T
    else:
        hist = jnp.concatenate([_dup_heads(win_k), _dup_heads(win_v)], axis=1).astype(BF16)
        Tw = win_k.shape[1]
        keys = jnp.concatenate([hist.reshape(B, Tw, KV_DUP), kvd.reshape(B, T, KV_DUP)], axis=1)
        Tk = Tw + T
        keys = keys.reshape(B * Tk, KV_DUP)
    att = _attention(q, keys, row(p["swa_sinks"][0]), B, T, Tk)
    x3, h, eidx, gates, counts = _oproj_router(x2, att, w["w_o"], row(p["norm_ffn"][1]), p["moe_router"][0])

    te = min(EXPERT_TILE, max(TOP_K * R // N_EXPERTS, LANES))
    n_tiles = TOP_K * R // te + N_EXPERTS
    cnt = counts[0].astype(jnp.int32)
    tiles = (cnt + te - 1) // te
    tile_end = jnp.cumsum(tiles)
    base = (tile_end - tiles) * te
    n_used = tile_end[-1:]
    tile_ids = jnp.arange(n_tiles, dtype=jnp.int32)
    tile_expert = jnp.minimum(jnp.sum((tile_end[None, :] <= tile_ids[:, None]).astype(jnp.int32), axis=1),
                              N_EXPERTS - 1)
    pos_slot = _slots(eidx, base.astype(F32).reshape(1, N_EXPERTS)).reshape(-1)
    n_used = n_used.astype(jnp.int32)
    xs = _dispatch(h, pos_slot, cnt, base.astype(jnp.int32), n_used, n_tiles, te)
    ys = _expert_ffn(xs, tile_expert, n_used, w["moe_w1"], w["moe_w3"], w["moe_w2"], te)
    y = _combine(x3, gates, pos_slot, ys, row(p["final_norm"]))
    return (y.reshape(B, T, D_MODEL), conv_new[None], s_new[None], k_new, v_new)


def kernel(x_prompt, x_sample, cache_conv, state_delta, cache_k, cache_v, norm_attn, norm_ffn, gdn_w_in,
           gdn_w_conv, gdn_a_log, gdn_dt_bias, gdn_o_norm, gdn_w_out, kv_norm, w_kv, swa_w_q, swa_sinks, swa_w_o,
           ffn_w1, ffn_w3, ffn_w2, moe_router, moe_w1, moe_w3, moe_w2, final_norm):
    p = dict(norm_attn=norm_attn, norm_ffn=norm_ffn, gdn_w_in=gdn_w_in, gdn_w_conv=gdn_w_conv,
             gdn_a_log=gdn_a_log, gdn_dt_bias=gdn_dt_bias, gdn_o_norm=gdn_o_norm, gdn_w_out=gdn_w_out,
             kv_norm=kv_norm, w_kv=w_kv, swa_w_q=swa_w_q, swa_sinks=swa_sinks, swa_w_o=swa_w_o,
             ffn_w1=ffn_w1, ffn_w3=ffn_w3, ffn_w2=ffn_w2, moe_router=moe_router,
             moe_w1=moe_w1, moe_w3=moe_w3, moe_w2=moe_w2, final_norm=final_norm)
    w = _prep_weights(p)
    Bp, Tp, _ = x_prompt.shape
    Bs, Ts, _ = x_sample.shape
    conv0 = jnp.zeros((Bp, CONV_W - 1, CONV_CH), F32)
    s0 = jnp.zeros((Bp, HV, DK, DV), F32)
    y_p, conv_p, delta_p, k_p, v_p = _trunk(x_prompt, jnp.arange(Tp, dtype=jnp.int32), conv0, s0,
                                            None, None, p, w)
    pos_s = PAST_LEN + jnp.arange(Ts, dtype=jnp.int32)
    y_s, conv_s, delta_s, k_s, v_s = _trunk(x_sample, pos_s, cache_conv[0], state_delta[0],
                                            cache_k, cache_v, p, w)
    return (y_p, y_s, conv_p, delta_p, k_p, v_p, conv_s, delta_s, k_s, v_s)
```
